```python
import jax, jax.numpy as jnp
from jax import lax
import numpy as np

D_MODEL = 1024
BATCH = 2
SEQ = 8192
DEPTH = 2

GRID_W = 64
CTX_LEN = 256
CONV_W = 3
EPS = 1e-6
SC_CH = 512
FT_CH = 512
FT_GROUPS = 4
FT_GROUP_CH = FT_CH // FT_GROUPS
EVEN_IN = 3 * SC_CH + FT_CH
EVEN_OUT = SC_CH + FT_CH
ATT_HEADS = 8
ATT_KV_HEADS = 2
ATT_GROUP = ATT_HEADS // ATT_KV_HEADS
HEAD_DIM = 64
ATT_SCALE = HEAD_DIM ** -0.5
WINDOW = 128
BLOCK = 128
ROPE_THETA = 10000.0
ML_HEADS = 4
ML_QK_DIM = 64
ML_V_DIM = 128
ML_CHUNK = 64
ATT_Q = ATT_HEADS * HEAD_DIM
ATT_KV = ATT_KV_HEADS * HEAD_DIM
ML_QK = ML_HEADS * ML_QK_DIM
ML_V = ML_HEADS * ML_V_DIM
ODD_IN = ATT_Q + 2 * ATT_KV + 2 * ML_QK + 2 * ML_V + 4 * ML_HEADS
ODD_OUT = ATT_Q + ML_V
D_FF = 2816
N_EVEN = (DEPTH + 1) // 2
N_ODD = DEPTH // 2

kernel_name = "hybrid_conv_fourier_swa_mlstm_dit"


def rms_norm(x, g):
    xf = x.astype(jnp.float32)
    y = xf * lax.rsqrt(jnp.mean(xf * xf, axis=-1, keepdims=True) + EPS)
    return (y * g.astype(jnp.float32)).astype(x.dtype)


def modulation(cv, w, b):
    mod = (jax.nn.silu(cv) @ w + b)[..., None, :]
    return jnp.split(mod, 6, axis=-1)


def dwconv3(x, w):
    xp = jnp.pad(x, ((0, 0), (1, 1), (0, 0)))
    return xp[:, :-2] * w[0] + xp[:, 1:-1] * w[1] + xp[:, 2:] * w[2]


def short_conv_mix(u, w_conv):
    b_gate, c_gate, xv = jnp.split(u, 3, axis=-1)
    return b_gate * dwconv3(c_gate * xv, w_conv)


def fourier_mix(u):
    B_, L, _ = u.shape
    ug = u.reshape(B_, L, FT_GROUPS, FT_GROUP_CH).astype(jnp.float32)
    y = jnp.fft.fft2(ug, axes=(1, 3), norm="ortho").real
    return y.reshape(B_, L, FT_CH).astype(u.dtype)


def even_mixer(h, w_in, w_conv, w_out):
    u = h @ w_in
    y_conv = short_conv_mix(u[..., :3 * SC_CH], w_conv)
    y_four = fourier_mix(u[..., 3 * SC_CH:])
    return jnp.concatenate([y_conv, y_four], axis=-1) @ w_out


def axial_rope_tables(L):
    rows = L // GRID_W
    row = jnp.repeat(jnp.arange(rows, dtype=jnp.float32), GRID_W)
    col = jnp.tile(jnp.arange(GRID_W, dtype=jnp.float32), rows)
    axis_dim = HEAD_DIM // 2
    inv_freq = ROPE_THETA ** (-jnp.arange(0, axis_dim, 2, dtype=jnp.float32) / axis_dim)
    ang = jnp.stack([row, col])[:, :, None] * inv_freq
    return jnp.cos(ang), jnp.sin(ang)


def apply_axial_rope(x, cos, sin):
    lead = (slice(None),) + (None,) * (x.ndim - 3)
    outs = []
    for a, xa in enumerate(jnp.split(x, 2, axis=-1)):
        x1, x2 = jnp.split(xa, 2, axis=-1)
        cs = cos[a][lead].astype(x.dtype)
        sn = sin[a][lead].astype(x.dtype)
        outs += [x1 * cs - x2 * sn, x1 * sn + x2 * cs]
    return jnp.concatenate(outs, axis=-1)


def odd_project(h, w_in, gate_b, q_g, k_g):
    B_, L, _ = h.shape
    u = h @ w_in
    idx = np.cumsum([ATT_Q, ATT_KV, ATT_KV, ML_QK, ML_QK, ML_V, ML_V]).tolist()
    q_a, k_a, v_a, q_m, k_m, v_m, o_m, g = jnp.split(u, idx, axis=-1)
    q_a = rms_norm(q_a.reshape(B_, L, ATT_KV_HEADS, ATT_GROUP, HEAD_DIM), q_g)
    k_a = rms_norm(k_a.reshape(B_, L, ATT_KV_HEADS, HEAD_DIM), k_g)
    v_a = v_a.reshape(B_, L, ATT_KV_HEADS, HEAD_DIM)
    q_m = q_m.reshape(B_, L, ML_HEADS, ML_QK_DIM)
    k_m = k_m.reshape(B_, L, ML_HEADS, ML_QK_DIM) * (ML_QK_DIM ** -0.5)
    v_m = v_m.reshape(B_, L, ML_HEADS, ML_V_DIM)
    g = g.astype(jnp.float32).reshape(B_, L, 4, ML_HEADS) + gate_b
    gates = (g[:, :, 0], jax.nn.log_sigmoid(g[:, :, 1]), g[:, :, 2], jax.nn.log_sigmoid(g[:, :, 3]))
    return q_a, k_a, v_a, (q_m, k_m, v_m, gates), o_m


def windowed_sink_attention(q, k, v, k_ctx, v_ctx, sink):
    B_, L = q.shape[:2]
    nb = L // BLOCK
    qb = q.reshape(B_, nb, BLOCK, ATT_KV_HEADS, ATT_GROUP, HEAD_DIM)

    def band(a):
        ap = jnp.pad(a, ((0, 0), (BLOCK, BLOCK), (0, 0), (0, 0))).reshape(B_, nb + 2, BLOCK, ATT_KV_HEADS, HEAD_DIM)
        return jnp.concatenate([ap[:, :-2], ap[:, 1:-1], ap[:, 2:]], axis=2)

    kb, vb = band(k), band(v)
    qi = jnp.arange(BLOCK)[:, None] + BLOCK
    kj = jnp.arange(3 * BLOCK)[None, :]
    key_pos = (jnp.arange(nb)[:, None, None] - 1) * BLOCK + kj
    mask = (jnp.abs(kj - qi) <= WINDOW)[None] & (key_pos >= 0) & (key_pos < L)
    s_loc = jnp.einsum("bnqhgd,bnkhd->bnhgqk", qb, kb).astype(jnp.float32)
    s_loc = jnp.where(mask[None, :, None, None], s_loc, -jnp.inf)
    s_ctx = jnp.einsum("bnqhgd,bkhd->bnhgqk", qb, k_ctx).astype(jnp.float32)
    sk = sink.astype(jnp.float32).reshape(1, 1, ATT_KV_HEADS, ATT_GROUP, 1)
    m = jnp.maximum(jnp.maximum(jnp.max(s_loc, axis=-1), jnp.max(s_ctx, axis=-1)), sk)
    p_loc = jnp.exp(s_loc - m[..., None])
    p_ctx = jnp.exp(s_ctx - m[..., None])
    inv = (1.0 / (jnp.sum(p_loc, axis=-1) + jnp.sum(p_ctx, axis=-1) + jnp.exp(sk - m)))[..., None]
    o = (jnp.einsum("bnhgqk,bnkhd->bnqhgd", p_loc * inv, vb.astype(jnp.float32))
         + jnp.einsum("bnhgqk,bkhd->bnqhgd", p_ctx * inv, v_ctx.astype(jnp.float32)))
    return o.reshape(B_, L, ATT_Q).astype(v.dtype)


def context_sink_attention(q, k, v, sink):
    B_, C = q.shape[:2]
    s = jnp.einsum("bqhgd,bkhd->bhgqk", q, k).astype(jnp.float32)
    sk = sink.astype(jnp.float32).reshape(1, ATT_KV_HEADS, ATT_GROUP, 1)
    m = jnp.maximum(jnp.max(s, axis=-1), sk)
    p = jnp.exp(s - m[..., None])
    p = p / (jnp.sum(p, axis=-1) + jnp.exp(sk - m))[..., None]
    o = jnp.einsum("bhgqk,bkhd->bqhgd", p, v.astype(jnp.float32))
    return o.reshape(B_, C, ATT_Q).astype(v.dtype)


def mlstm_chunk_scan(q, k, v, ig, lf, state, with_output):
    B_, L = q.shape[:2]
    nc = L // ML_CHUNK

    def chunks(a):
        a = a.astype(jnp.float32)
        return jnp.moveaxis(a.reshape((B_, nc, ML_CHUNK) + a.shape[2:]), 1, 0)

    xs = (chunks(q), chunks(k), chunks(v), chunks(ig), chunks(lf))
    past_mask = jnp.tril(jnp.ones((ML_CHUNK, ML_CHUNK), dtype=bool))[None, :, :, None]

    def body(carry, xc):
        C, n, m = carry
        qc, kc, vc, ic, fc = xc
        b = jnp.cumsum(fc, axis=1)
        b_end = b[:, -1]
        w_end = b_end[:, None] - b + ic
        m_new = jnp.maximum(b_end + m, jnp.max(w_end, axis=1))
        decay = jnp.exp(b_end + m - m_new)
        wu = jnp.exp(w_end - m_new[:, None])
        C_new = decay[..., None, None] * C + jnp.einsum("bth,bthv,bthk->bhvk", wu, vc, kc)
        n_new = decay[..., None] * n + jnp.einsum("bth,bthk->bhk", wu, kc)
        new = (C_new, n_new, m_new)
        if not with_output:
            return new, None
        log_d = jnp.where(past_mask, b[:, :, None] - b[:, None] + ic[:, None], -jnp.inf)
        m_t = jnp.maximum(b + m[:, None], jnp.max(log_d, axis=2))
        inter = jnp.exp(b + m[:, None] - m_t)
        qk = jnp.einsum("bthk,bshk->btsh", qc, kc) * jnp.exp(log_d - m_t[:, :, None])
        num = jnp.einsum("btsh,bshv->bthv", qk, vc) + inter[..., None] * jnp.einsum("bhvk,bthk->bthv", C, qc)
        den = jnp.sum(qk, axis=2) + inter * jnp.einsum("bhk,bthk->bth", n, qc)
        h = num / jnp.maximum(jnp.abs(den), jnp.exp(-m_t))[..., None]
        return new, h

    state, hs = lax.scan(body, state, xs)
    if with_output:
        hs = jnp.moveaxis(hs, 0, 1).reshape(B_, L, ML_HEADS, ML_V_DIM)
    return state, hs


def bidir_mlstm(lat, ctx, need_ctx_out):
    q_l, k_l, v_l, (i_lf, f_lf, i_lb, f_lb) = lat
    q_c, k_c, v_c, (i_cf, f_cf, i_cb, f_cb) = ctx
    B_ = q_l.shape[0]
    zero = (jnp.zeros((B_, ML_HEADS, ML_V_DIM, ML_QK_DIM), jnp.float32),
            jnp.zeros((B_, ML_HEADS, ML_QK_DIM), jnp.float32),
            jnp.zeros((B_, ML_HEADS), jnp.float32))
    rev = lambda a: a[:, ::-1]
    st, hc_f = mlstm_chunk_scan(q_c, k_c, v_c, i_cf, f_cf, zero, need_ctx_out)
    _, hl_f = mlstm_chunk_scan(q_l, k_l, v_l, i_lf, f_lf, st, True)
    st, hc_b = mlstm_chunk_scan(rev(q_c), rev(k_c), rev(v_c), rev(i_cb), rev(f_cb), zero, need_ctx_out)
    _, hl_b = mlstm_chunk_scan(rev(q_l), rev(k_l), rev(v_l), rev(i_lb), rev(f_lb), st, True)
    h_lat = hl_f + rev(hl_b)
    h_ctx = hc_f + rev(hc_b) if need_ctx_out else None
    return h_lat, h_ctx


def odd_mixer(h_lat, h_ctx, rope, w_in, gate_b, q_g, k_g, sink, w_out, need_ctx_out):
    cos, sin = rope
    qa, ka, va, ml_lat, o_lat = odd_project(h_lat, w_in, gate_b, q_g, k_g)
    qc, kc, vc, ml_ctx, o_ctx = odd_project(h_ctx, w_in, gate_b, q_g, k_g)
    qa = apply_axial_rope(qa, cos, sin) * ATT_SCALE
    ka = apply_axial_rope(ka, cos, sin)
    att_lat = windowed_sink_attention(qa, ka, va, kc, vc, sink)
    hm_lat, hm_ctx = bidir_mlstm(ml_lat, ml_ctx, need_ctx_out)
    B_, L = h_lat.shape[:2]
    ml_out = hm_lat.reshape(B_, L, ML_V).astype(h_lat.dtype) * jax.nn.sigmoid(o_lat)
    y_lat = jnp.concatenate([att_lat, ml_out], axis=-1) @ w_out
    y_ctx = None
    if need_ctx_out:
        att_ctx = context_sink_attention(qc * ATT_SCALE, kc, vc, sink)
        C = h_ctx.shape[1]
        ml_ctx_out = hm_ctx.reshape(B_, C, ML_V).astype(h_ctx.dtype) * jax.nn.sigmoid(o_ctx)
        y_ctx = jnp.concatenate([att_ctx, ml_ctx_out], axis=-1) @ w_out
    return y_lat, y_ctx


def conv_ffn(h, w_up, w_conv, w_down):
    g, val = jnp.split(h @ w_up, 2, axis=-1)
    return (jax.nn.silu(dwconv3(g, w_conv)) * val) @ w_down


def setup_inputs(seed: int = 0) -> dict:
    key = jax.random.key(seed)
    ks = jax.random.split(key, 20)
    nrm = lambda k, shape, scale: jax.random.normal(k, shape, jnp.float32) * scale
    return {
        "x": nrm(ks[0], (BATCH, SEQ, D_MODEL), 1.0),
        "c": nrm(ks[1], (BATCH, D_MODEL), 1.0),
        "ctx": nrm(ks[2], (BATCH, CTX_LEN, D_MODEL), 1.0),
        "c_ctx": nrm(ks[3], (D_MODEL,), 1.0),
        "ada_w": nrm(ks[4], (DEPTH, D_MODEL, 6 * D_MODEL), 0.5 * D_MODEL ** -0.5),
        "ada_b": nrm(ks[5], (DEPTH, 6 * D_MODEL), 0.02),
        "norm_g": 1.0 + nrm(ks[6], (DEPTH, 2, D_MODEL), 0.02),
        "even_w_in": nrm(ks[7], (N_EVEN, D_MODEL, EVEN_IN), D_MODEL ** -0.5),
        "even_conv": nrm(ks[8], (N_EVEN, CONV_W, SC_CH), CONV_W ** -0.5),
        "even_w_out": nrm(ks[9], (N_EVEN, EVEN_OUT, D_MODEL), EVEN_OUT ** -0.5),
        "odd_w_in": nrm(ks[10], (N_ODD, D_MODEL, ODD_IN), D_MODEL ** -0.5),
        "odd_gate_b": jnp.array([0.0, 3.0, 0.0, 3.0], jnp.float32)[None, :, None] + nrm(ks[11], (N_ODD, 4, ML_HEADS), 0.1),
        "odd_q_g": 1.0 + nrm(ks[12], (N_ODD, HEAD_DIM), 0.02),
        "odd_k_g": 1.0 + nrm(ks[13], (N_ODD, HEAD_DIM), 0.02),
        "odd_sink": nrm(ks[14], (N_ODD, ATT_HEADS), 0.5),
        "odd_w_out": nrm(ks[15], (N_ODD, ODD_OUT, D_MODEL), ODD_OUT ** -0.5),
        "ffn_w_up": nrm(ks[16], (DEPTH, D_MODEL, 2 * D_FF), D_MODEL ** -0.5),
        "ffn_conv": nrm(ks[17], (DEPTH, CONV_W, D_FF), CONV_W ** -0.5),
        "ffn_w_down": nrm(ks[18], (DEPTH, D_FF, D_MODEL), D_FF ** -0.5),
    }


def reference(x, c, ctx, c_ctx, ada_w, ada_b, norm_g, even_w_in, even_conv, even_w_out, odd_w_in, odd_gate_b,
              odd_q_g, odd_k_g, odd_sink, odd_w_out, ffn_w_up, ffn_conv, ffn_w_down):
    L = x.shape[1]
    rope = axial_rope_tables(L)
    xl, xc = x, ctx
    for layer in range(DEPTH):
        last = layer == DEPTH - 1
        j = layer // 2
        sh1, sc1, g1, sh2, sc2, g2 = modulation(c, ada_w[layer], ada_b[layer])
        csh1, csc1, cg1, csh2, csc2, cg2 = modulation(c_ctx, ada_w[layer], ada_b[layer])
        hl = rms_norm(xl, norm_g[layer, 0]) * (1 + sc1) + sh1
        hc = rms_norm(xc, norm_g[layer, 0]) * (1 + csc1) + csh1
        if layer % 2 == 0:
            yl = even_mixer(hl, even_w_in[j], even_conv[j], even_w_out[j])
            yc = None if last else even_mixer(hc, even_w_in[j], even_conv[j], even_w_out[j])
        else:
            yl, yc = odd_mixer(hl, hc, rope, odd_w_in[j], odd_gate_b[j], odd_q_g[j], odd_k_g[j], odd_sink[j],
                               odd_w_out[j], not last)
        xl = xl + g1 * yl
        xl = xl + g2 * conv_ffn(rms_norm(xl, norm_g[layer, 1]) * (1 + sc2) + sh2,
                                ffn_w_up[layer], ffn_conv[layer], ffn_w_down[layer])
        if not last:
            xc = xc + cg1 * yc
            xc = xc + cg2 * conv_ffn(rms_norm(xc, norm_g[layer, 1]) * (1 + csc2) + csh2,
                                     ffn_w_up[layer], ffn_conv[layer], ffn_w_down[layer])
    return xl
```

```python
import functools

import numpy as np
import jax
import jax.numpy as jnp
from jax import lax
from jax.experimental import pallas as pl
from jax.experimental.pallas import tpu as pltpu

F32 = jnp.float32
BF16 = jnp.bfloat16

D_MODEL = 1024
GRID_W = 64
EPS = 1e-6
SC_CH = 512
FT_CH = 512
FT_GROUPS = 4
FT_GROUP_CH = FT_CH // FT_GROUPS
EVEN_IN = 3 * SC_CH + FT_CH
ATT_HEADS = 8
ATT_KV_HEADS = 2
HEAD_DIM = 64
ATT_SCALE = HEAD_DIM ** -0.5
WINDOW = 128
BLOCK = 128
ROPE_THETA = 10000.0
ML_HEADS = 4
ML_QK_DIM = 64
ML_V_DIM = 128
ATT_Q = ATT_HEADS * HEAD_DIM
ATT_KV = ATT_KV_HEADS * HEAD_DIM
ML_QK = ML_HEADS * ML_QK_DIM
ML_V = ML_HEADS * ML_V_DIM
D_FF = 2816

LANES = 128
SUBLANES = 8
VMEM_LIMIT_BYTES = 56 * 1024 * 1024

DFT_N1 = 128
FF_CHUNK = 256
N_FF_CHUNKS = D_FF // FF_CHUNK
ML_CHUNK = 128
NEG_BIG = -1e30

OQ, OK_, OV, OQM, OKM, OVM, OOM, OG = 0, 512, 640, 768, 1024, 1280, 1792, 2304
ODD_COLS = OG + LANES


def _cparams(sem):
    return pltpu.CompilerParams(dimension_semantics=sem, vmem_limit_bytes=VMEM_LIMIT_BYTES)


def _sigmoid(x):
    return 1.0 / (1.0 + jnp.exp(-x))


def _norm_mod(x, g, shift, scale):
    y = x * lax.rsqrt(jnp.mean(x * x, axis=-1, keepdims=True) + EPS)
    return y * g * (1.0 + scale) + shift


def _halo_rows(x_ref, xn_ref, xp_ref, mod_ref, ng_ref, row0):
    shift = mod_ref[0, row0:row0 + 1, :]
    scale = mod_ref[0, row0 + 1:row0 + 2, :]
    g = ng_ref[...]
    parts = [_norm_mod(r[0], g, shift, scale) for r in (x_ref, xn_ref, xp_ref)]
    return jnp.concatenate(parts, axis=0).astype(BF16)


def _halo_valid(tm, i, nt):
    row = lax.broadcasted_iota(jnp.int32, (tm + 2 * SUBLANES, 1), 0)
    return ((row < tm) | ((row < tm + SUBLANES) & (i < nt - 1)) | ((row >= tm + SUBLANES) & (i > 0)))


def _conv3(v, cw, tm):
    n = v.shape[0]
    vp = pltpu.roll(v, 1, 0)[:tm]
    vn = pltpu.roll(v, n - 1, 0)[:tm]
    return vp * cw[0:1] + v[:tm] * cw[1:2] + vn * cw[2:3]


def _halo_specs(tm, L):
    hb = tm // SUBLANES
    last = L // SUBLANES - 1
    return [
        pl.BlockSpec((1, tm, D_MODEL), lambda b, i: (b, i, 0)),
        pl.BlockSpec((1, SUBLANES, D_MODEL), lambda b, i: (b, jnp.minimum((i + 1) * hb, last), 0)),
        pl.BlockSpec((1, SUBLANES, D_MODEL), lambda b, i: (b, jnp.maximum(i * hb - 1, 0), 0)),
    ]


def _const_spec(shape):
    nd = len(shape)
    return pl.BlockSpec(shape, lambda *_: (0,) * nd)


def _resident_spec(shape):
    nd = len(shape)
    return pl.BlockSpec(shape, lambda *_: (0,) * nd, pipeline_mode=pl.Buffered(1))


def _mod_kernel(cv_ref, w_ref, b_ref, o_ref):
    cv = cv_ref[...]
    a = cv * _sigmoid(cv)
    o_ref[0] = jnp.dot(a, w_ref[0], preferred_element_type=F32,
                       precision=lax.Precision.HIGHEST) + b_ref[0]


def _modulation(cv, ada_w, ada_b):
    depth, _, n = ada_w.shape
    tn = 1536
    return pl.pallas_call(
        _mod_kernel,
        out_shape=jax.ShapeDtypeStruct((depth, SUBLANES, n), F32),
        grid=(depth, n // tn),
        in_specs=[
            pl.BlockSpec((SUBLANES, D_MODEL), lambda l, j: (0, 0)),
            pl.BlockSpec((1, D_MODEL, tn), lambda l, j: (l, 0, j)),
            pl.BlockSpec((1, 1, tn), lambda l, j: (l, 0, j)),
        ],
        out_specs=pl.BlockSpec((1, SUBLANES, tn), lambda l, j: (l, 0, j)),
        compiler_params=_cparams(("arbitrary", "arbitrary")),
        name="modulation",
    )(cv, ada_w, ada_b.reshape(depth, 1, n))


def _even_in_kernel(x_ref, xn_ref, xp_ref, mod_ref, ng_ref, w_ref, cw_ref, tc_ref,
                    yc_ref, zr_ref, zi_ref, *, tm, nt):
    i = pl.program_id(1)
    hh = _halo_rows(x_ref, xn_ref, xp_ref, mod_ref, ng_ref, 0)
    u = jnp.dot(hh, w_ref[...], preferred_element_type=F32)
    v = u[:, SC_CH:2 * SC_CH] * u[:, 2 * SC_CH:3 * SC_CH]
    v = jnp.where(_halo_valid(tm, i, nt), v, 0.0)
    yc = u[:tm, :SC_CH] * _conv3(v, cw_ref[...], tm)
    yc_ref[0] = yc.astype(BF16)
    uf = u[:tm, 3 * SC_CH:].astype(BF16)
    tc = tc_ref[...]
    for g in range(FT_GROUPS):
        sl = slice(g * FT_GROUP_CH, (g + 1) * FT_GROUP_CH)
        ab = jnp.dot(uf[:, sl], tc, preferred_element_type=F32)
        zr_ref[0, :, sl] = ab[:, :FT_GROUP_CH]
        zi_ref[0, :, sl] = ab[:, FT_GROUP_CH:]


def _even_in(x, mod, ng, w_in, cw, tc, tm):
    B, L, _ = x.shape
    nt = L // tm
    out = jax.ShapeDtypeStruct((B, L, FT_CH), BF16)
    zout = jax.ShapeDtypeStruct((B, L, FT_CH), F32)
    ospec = pl.BlockSpec((1, tm, FT_CH), lambda b, i: (b, i, 0))
    return pl.pallas_call(
        functools.partial(_even_in_kernel, tm=tm, nt=nt),
        out_shape=(out, zout, zout),
        grid=(B, nt),
        in_specs=_halo_specs(tm, L) + [
            pl.BlockSpec((1, SUBLANES, D_MODEL), lambda b, i: (b, 0, 0)),
            _const_spec((1, D_MODEL)),
            _const_spec((D_MODEL, EVEN_IN)),
            _const_spec((3, SC_CH)),
            _const_spec((FT_GROUP_CH, 2 * FT_GROUP_CH)),
        ],
        out_specs=(ospec, ospec, ospec),
        compiler_params=_cparams(("parallel", "arbitrary")),
        name="even_in",
    )(x, x, x, mod, ng, w_in, cw, tc)


def _seq_dft_kernel(zr_ref, zi_ref, m_ref, g_ref, y_ref, o_scr, *, n2):
    m1 = m_ref[...]
    for j in range(n2):
        z = jnp.concatenate([zr_ref[0, pl.ds(j, DFT_N1, stride=n2), :],
                             zi_ref[0, pl.ds(j, DFT_N1, stride=n2), :]], axis=0)
        o_scr[2 * DFT_N1 * j:2 * DFT_N1 * (j + 1), :] = jnp.dot(m1, z.astype(BF16), preferred_element_type=F32)
    for k1 in range(DFT_N1):
        o = jnp.concatenate([o_scr[pl.ds(k1, n2, stride=2 * DFT_N1), :],
                             o_scr[pl.ds(DFT_N1 + k1, n2, stride=2 * DFT_N1), :]], axis=0)
        y_ref[0, pl.ds(k1, n2, stride=DFT_N1), :] = jnp.dot(g_ref[k1], o.astype(BF16), preferred_element_type=F32)


def _dft_tables(L):
    n2 = L // DFT_N1
    k = np.arange(DFT_N1)
    a = 2.0 * np.pi * ((k[:, None] * k[None, :]) % DFT_N1) / DFT_N1
    er, ei = np.cos(a) / np.sqrt(DFT_N1), -np.sin(a) / np.sqrt(DFT_N1)
    m1 = np.block([[er, -ei], [ei, er]])
    k1 = np.arange(DFT_N1)[:, None, None]
    k2 = np.arange(n2)[None, :, None]
    nn = np.arange(n2)[None, None, :]
    th = 2.0 * np.pi * ((nn * (k1 + DFT_N1 * k2)) % L) / L
    g = np.concatenate([np.cos(th), np.sin(th)], axis=-1) / np.sqrt(n2)
    return jnp.asarray(m1, F32).astype(BF16), jnp.asarray(g, F32).astype(BF16)


def _channel_dft_table():
    k = np.arange(FT_GROUP_CH)
    a = 2.0 * np.pi * ((k[:, None] * k[None, :]) % FT_GROUP_CH) / FT_GROUP_CH
    t = np.concatenate([np.cos(a), -np.sin(a)], axis=1) / np.sqrt(FT_GROUP_CH)
    return jnp.asarray(t, F32).astype(BF16)


def _seq_dft(zr, zi):
    B, L, C = zr.shape
    n2 = L // DFT_N1
    m1, g = _dft_tables(L)
    zspec = pl.BlockSpec((1, L, LANES), lambda b, j: (b, 0, j))
    return pl.pallas_call(
        functools.partial(_seq_dft_kernel, n2=n2),
        out_shape=jax.ShapeDtypeStruct((B, L, C), F32),
        grid=(B, C // LANES),
        in_specs=[zspec, zspec, _const_spec((2 * DFT_N1, 2 * DFT_N1)), _const_spec((DFT_N1, n2, 2 * n2))],
        out_specs=zspec,
        scratch_shapes=[pltpu.VMEM((2 * DFT_N1 * n2, LANES), F32)],
        compiler_params=_cparams(("parallel", "arbitrary")),
        name="seq_dft",
    )(zr, zi, m1, g)


def _dense_dft_kernel(zr_ref, zi_ref, t_ref, y_ref):
    z = jnp.concatenate([zr_ref[0], zi_ref[0]], axis=0).astype(BF16)
    y_ref[0] = jnp.dot(t_ref[...], z, preferred_element_type=F32)


def _dense_seq_dft(zr, zi):
    B, L, C = zr.shape
    k = np.arange(L)
    a = 2.0 * np.pi * ((k[:, None] * k[None, :]) % L) / L
    t = jnp.asarray(np.concatenate([np.cos(a), np.sin(a)], axis=1) / np.sqrt(L), F32).astype(BF16)
    zspec = pl.BlockSpec((1, L, C), lambda b: (b, 0, 0))
    return pl.pallas_call(
        _dense_dft_kernel,
        out_shape=jax.ShapeDtypeStruct((B, L, C), F32),
        grid=(B,),
        in_specs=[zspec, zspec, _const_spec((L, 2 * L))],
        out_specs=zspec,
        compiler_params=_cparams(("arbitrary",)),
        name="dense_seq_dft",
    )(zr, zi, t)


def _even_out_kernel(x_ref, yc_ref, yf_ref, mod_ref, w_ref, o_ref):
    cat = jnp.concatenate([yc_ref[0], yf_ref[0].astype(BF16)], axis=-1)
    y = jnp.dot(cat, w_ref[...], preferred_element_type=F32)
    o_ref[0] = x_ref[0] + mod_ref[0, 2:3, :] * y


def _even_out(x, yc, yf, mod, w_out, tm):
    B, L, _ = x.shape
    xspec = pl.BlockSpec((1, tm, D_MODEL), lambda b, i: (b, i, 0))
    hspec = pl.BlockSpec((1, tm, FT_CH), lambda b, i: (b, i, 0))
    return pl.pallas_call(
        _even_out_kernel,
        out_shape=jax.ShapeDtypeStruct(x.shape, F32),
        grid=(B, L // tm),
        in_specs=[xspec, hspec, hspec,
                  pl.BlockSpec((1, SUBLANES, D_MODEL), lambda b, i: (b, 0, 0)),
                  _const_spec((D_MODEL, D_MODEL))],
        out_specs=xspec,
        compiler_params=_cparams(("parallel", "arbitrary")),
        name="even_out",
    )(x, yc, yf, mod, w_out)


def _odd_out_kernel(x_ref, att_ref, hf_ref, hb_ref, om_ref, mod_ref, w_ref, o_ref):
    ml = (hf_ref[0] + hb_ref[0]) * _sigmoid(om_ref[0].astype(F32))
    cat = jnp.concatenate([att_ref[0], ml.astype(BF16)], axis=-1)
    y = jnp.dot(cat, w_ref[...], preferred_element_type=F32)
    o_ref[0] = x_ref[0] + mod_ref[0, 2:3, :] * y


def _odd_out(x, att, hf, hb, om, mod, w_out, tm):
    B, L, _ = x.shape
    xspec = pl.BlockSpec((1, tm, D_MODEL), lambda b, i: (b, i, 0))
    hspec = pl.BlockSpec((1, tm, ML_V), lambda b, i: (b, i, 0))
    return pl.pallas_call(
        _odd_out_kernel,
        out_shape=jax.ShapeDtypeStruct(x.shape, F32),
        grid=(B, L // tm),
        in_specs=[xspec, hspec, hspec, hspec, hspec,
                  pl.BlockSpec((1, SUBLANES, D_MODEL), lambda b, i: (b, 0, 0)),
                  _const_spec((D_MODEL, D_MODEL))],
        out_specs=xspec,
        compiler_params=_cparams(("parallel", "arbitrary")),
        name="odd_out",
    )(x, att, hf, hb, om, mod, w_out)


def _ffn_kernel(x_ref, xn_ref, xp_ref, mod_ref, ng_ref, wu_ref, cw_ref, wd_ref, o_ref,
                hh_scr, act_scr, *, tm, nt):
    i = pl.program_id(1)
    hh_scr[...] = _halo_rows(x_ref, xn_ref, xp_ref, mod_ref, ng_ref, 3)
    valid = _halo_valid(tm, i, nt)
    for c in range(N_FF_CHUNKS):
        lo = c * FF_CHUNK
        g = jnp.dot(hh_scr[...], wu_ref[:, lo:lo + FF_CHUNK], preferred_element_type=F32)
        g = jnp.where(valid, g, 0.0)
        cv = _conv3(g, cw_ref[:, lo:lo + FF_CHUNK], tm)
        val = jnp.dot(hh_scr[:tm, :], wu_ref[:, D_FF + lo:D_FF + lo + FF_CHUNK], preferred_element_type=F32)
        act_scr[:, lo:lo + FF_CHUNK] = (cv * _sigmoid(cv) * val).astype(BF16)
    y = jnp.dot(act_scr[...], wd_ref[...], preferred_element_type=F32)
    o_ref[0] = x_ref[0] + mod_ref[0, 5:6, :] * y


def _ffn(x, mod, ng, w_up, cw, w_down, tm):
    B, L, _ = x.shape
    nt = L // tm
    return pl.pallas_call(
        functools.partial(_ffn_kernel, tm=tm, nt=nt),
        out_shape=jax.ShapeDtypeStruct(x.shape, F32),
        grid=(B, nt),
        in_specs=_halo_specs(tm, L) + [
            pl.BlockSpec((1, SUBLANES, D_MODEL), lambda b, i: (b, 0, 0)),
            _const_spec((1, D_MODEL)),
            _resident_spec((D_MODEL, 2 * D_FF)),
            _const_spec((3, D_FF)),
            _resident_spec((D_FF, D_MODEL)),
        ],
        out_specs=pl.BlockSpec((1, tm, D_MODEL), lambda b, i: (b, i, 0)),
        scratch_shapes=[pltpu.VMEM((tm + 2 * SUBLANES, D_MODEL), BF16),
                        pltpu.VMEM((tm, D_FF), BF16)],
        compiler_params=_cparams(("parallel", "arbitrary")),
        name="conv_ffn",
    )(x, x, x, mod, ng, w_up, cw, w_down)


def _split_dot(x, p):
    hi = x.astype(BF16)
    lo = (x - hi.astype(F32)).astype(BF16)
    return (jnp.dot(hi, p, preferred_element_type=F32) + jnp.dot(lo, p, preferred_element_type=F32))


def _odd_in_kernel(x_ref, mod_ref, ng_ref, w_ref, pm_ref, gain_ref, cos_ref, sin_ref, gb_ref,
                   q_ref, kv_ref, qm_ref, km_ref, vm_ref, om_ref, gate_ref):
    shift = mod_ref[0, 0:1, :]
    scale = mod_ref[0, 1:2, :]
    h = _norm_mod(x_ref[0], ng_ref[...], shift, scale).astype(BF16)
    u = jnp.dot(h, w_ref[...], preferred_element_type=F32)
    tm = u.shape[0]

    uqk = u[:, OQ:OV]
    ms = _split_dot(uqk * uqk, pm_ref[...])
    rn = uqk * lax.rsqrt(ms + EPS) * gain_ref[...]
    lane = lax.broadcasted_iota(jnp.int32, (1, LANES), 1)
    first = (lane % 32) < 16
    cos = cos_ref[...]
    sin = sin_ref[...]
    roped = []
    for t in range((OV - OQ) // LANES):
        xt = rn[:, t * LANES:(t + 1) * LANES]
        sw = jnp.where(first, pltpu.roll(xt, LANES - 16, 1), pltpu.roll(xt, 16, 1))
        roped.append(xt * cos + sw * sin)
    for t in range(ATT_Q // LANES):
        q_ref[0, :, t * LANES:(t + 1) * LANES] = (roped[t] * ATT_SCALE).astype(BF16)
    k = roped[ATT_Q // LANES]
    v = u[:, OV:OQM]
    half = LANES // 2
    kv_ref[0, :, 0:LANES] = k.astype(BF16)
    kv_ref[0, :, LANES:2 * LANES] = pltpu.roll(k, half, 1).astype(BF16)
    kv_ref[0, :, 2 * LANES:3 * LANES] = v.astype(BF16)
    kv_ref[0, :, 3 * LANES:4 * LANES] = pltpu.roll(v, half, 1).astype(BF16)

    qm_ref[0] = u[:, OQM:OKM].astype(BF16)
    km_ref[0] = (u[:, OKM:OVM] * (ML_QK_DIM ** -0.5)).astype(BF16)
    vm_ref[0] = u[:, OVM:OOM].astype(BF16)
    om_ref[0] = u[:, OOM:OG].astype(BF16)
    g = u[:, OG:ODD_COLS] + gb_ref[...]
    logsig = jnp.minimum(g, 0.0) - jnp.log(1.0 + jnp.exp(-jnp.abs(g)))
    is_forget = (lane % 8) >= 4
    gate_ref[0] = jnp.where(is_forget, logsig, g)


def _odd_in(x, mod, ng, w, pm, gain, cos, sin, gb, tm):
    B, L, _ = x.shape

    def out(c, dt=BF16):
        return jax.ShapeDtypeStruct((B, L, c), dt), pl.BlockSpec((1, tm, c), lambda b, i: (b, i, 0))

    outs = [out(ATT_Q), out(4 * LANES), out(ML_QK), out(ML_QK), out(ML_V), out(ML_V), out(LANES, F32)]
    nqk = OV - OQ
    return pl.pallas_call(
        _odd_in_kernel,
        out_shape=tuple(o[0] for o in outs),
        grid=(B, L // tm),
        in_specs=[
            pl.BlockSpec((1, tm, D_MODEL), lambda b, i: (b, i, 0)),
            pl.BlockSpec((1, SUBLANES, D_MODEL), lambda b, i: (b, 0, 0)),
            _const_spec((1, D_MODEL)),
            _const_spec((D_MODEL, ODD_COLS)),
            _const_spec((nqk, nqk)),
            _const_spec((1, nqk)),
            pl.BlockSpec((tm, LANES), lambda b, i: (i, 0)),
            pl.BlockSpec((tm, LANES), lambda b, i: (i, 0)),
            _const_spec((1, LANES)),
        ],
        out_specs=tuple(o[1] for o in outs),
        compiler_params=_cparams(("parallel", "arbitrary")),
        name="odd_in",
    )(x, mod, ng, w, pm, gain, cos, sin, gb)


def _rope_tables(L):
    rows = L // GRID_W
    row = jnp.repeat(jnp.arange(rows, dtype=F32), GRID_W)
    col = jnp.tile(jnp.arange(GRID_W, dtype=F32), rows)
    axis_dim = HEAD_DIM // 2
    inv_freq = ROPE_THETA ** (-jnp.arange(0, axis_dim, 2, dtype=F32) / axis_dim)
    ang = jnp.stack([row, col])[:, :, None] * inv_freq
    c, s = jnp.cos(ang), jnp.sin(ang)
    cos = jnp.concatenate([c[0], c[0], c[1], c[1]], axis=-1)
    sin = jnp.concatenate([-s[0], s[0], -s[1], s[1]], axis=-1)
    return jnp.tile(cos, (1, 2)), jnp.tile(sin, (1, 2))


def _att_kernel(sink_ref, q_ref, kvp_ref, kvc_ref, kvn_ref, kvx_ref, o_ref, *, nb):
    n = pl.program_id(1)
    kv = jnp.concatenate([kvp_ref[0], kvc_ref[0], kvn_ref[0], kvx_ref[0]], axis=0)
    nloc = 3 * BLOCK
    nkeys = kv.shape[0]
    lane = lax.broadcasted_iota(jnp.int32, (1, LANES), 1)
    lo = lane < (LANES // 2)
    zero = jnp.zeros((), BF16)
    k, ks = kv[:, 0:LANES], kv[:, LANES:2 * LANES]
    v, vs = kv[:, 2 * LANES:3 * LANES], kv[:, 3 * LANES:4 * LANES]
    kvar = ((jnp.where(lo, k, zero), jnp.where(lo, zero, ks)), (jnp.where(lo, ks, zero), jnp.where(lo, zero, k)))
    vvar = ((jnp.where(lo, v, zero), jnp.where(lo, zero, vs)), (jnp.where(lo, vs, zero), jnp.where(lo, zero, v)))

    qi = lax.broadcasted_iota(jnp.int32, (BLOCK, nkeys), 0)
    kj = lax.broadcasted_iota(jnp.int32, (BLOCK, nkeys), 1)
    band = (kj >= qi) & (kj <= qi + 2 * WINDOW)
    inside = ((kj >= BLOCK) | (n > 0)) & ((kj < 2 * BLOCK) | (n < nb - 1))
    bias = jnp.where((kj >= nloc) | (band & inside), 0.0, NEG_BIG)

    for t in range(ATT_Q // LANES):
        qt = q_ref[0, :, t * LANES:(t + 1) * LANES]
        kvh = (2 * t) // (ATT_HEADS // ATT_KV_HEADS)
        acc = jnp.zeros((BLOCK, LANES), F32)
        for par in range(2):
            s = lax.dot_general(qt, kvar[kvh][par], (((1,), (1,)), ((), ())),
                                preferred_element_type=F32) + bias
            sk = sink_ref[2 * t + par]
            m = jnp.maximum(jnp.max(s, axis=1, keepdims=True), sk)
            p = jnp.exp(s - m)
            l = jnp.sum(p, axis=1, keepdims=True) + jnp.exp(sk - m)
            r = jnp.dot(p.astype(BF16), vvar[kvh][par], preferred_element_type=F32)
            acc = acc + r * (1.0 / l)
        o_ref[0, :, t * LANES:(t + 1) * LANES] = acc.astype(BF16)


def _attention(q, kv, kvx, sink):
    B, L, _ = q.shape
    nb = L // BLOCK
    cx = kvx.shape[1]
    kw = kv.shape[2]
    return pl.pallas_call(
        functools.partial(_att_kernel, nb=nb),
        out_shape=jax.ShapeDtypeStruct((B, L, ATT_Q), BF16),
        grid=(B, nb),
        in_specs=[
            pl.BlockSpec(memory_space=pltpu.SMEM),
            pl.BlockSpec((1, BLOCK, ATT_Q), lambda b, n: (b, n, 0)),
            pl.BlockSpec((1, BLOCK, kw), lambda b, n: (b, jnp.maximum(n - 1, 0), 0)),
            pl.BlockSpec((1, BLOCK, kw), lambda b, n: (b, n, 0)),
            pl.BlockSpec((1, BLOCK, kw), lambda b, n: (b, jnp.minimum(n + 1, nb - 1), 0)),
            pl.BlockSpec((1, cx, kw), lambda b, n: (b, 0, 0)),
        ],
        out_specs=pl.BlockSpec((1, BLOCK, ATT_Q), lambda b, n: (b, n, 0)),
        compiler_params=_cparams(("parallel", "arbitrary")),
        name="window_attention",
    )(sink, q, kv, kv, kv, kvx)


def _cum_dot(a, b, split_a):
    x = a if split_a else b
    parts = []
    r = x
    for _ in range(3):
        p = r.astype(BF16)
        parts.append(p)
        r = r - p.astype(F32)
    out = None
    for p in parts:
        d = (jnp.dot(p, b, preferred_element_type=F32) if split_a
             else jnp.dot(a, p, preferred_element_type=F32))
        out = d if out is None else out + d
    return out


def _mlstm_kernel(qf_ref, kf_ref, vf_ref, gf_ref, qb_ref, kb_ref, vb_ref, gb_ref, s0_ref, m0_ref,
                  *rest, T, nb, with_output):
    if with_output:
        hf_ref, hb_ref, s_ref, m_ref = rest
    else:
        s_ref, m_ref = rest
    j = pl.program_id(0)

    @pl.when(j == 0)
    def _():
        s_ref[...] = s0_ref[...]
        m_ref[...] = m0_ref[...]

    ti = lax.broadcasted_iota(jnp.int32, (T, T), 0)
    si = lax.broadcasted_iota(jnp.int32, (T, T), 1)
    lower = si <= ti
    upper = si >= ti
    lt = jnp.where(lower, 1.0, 0.0).astype(BF16)
    ut = jnp.where(upper, 1.0, 0.0).astype(BF16)
    lane = lax.broadcasted_iota(jnp.int32, (1, LANES), 1)
    lo = lane < (LANES // 2)
    zero = jnp.zeros((), BF16)
    ones = jnp.ones((T, ML_V_DIM), BF16)

    for b in range(nb):
        for d in range(2):
            q_ref, k_ref, v_ref, g_ref = ((qf_ref, kf_ref, vf_ref, gf_ref) if d == 0
                                          else (qb_ref, kb_ref, vb_ref, gb_ref))
            gates = g_ref[b]
            gates_t = gates.T
            tri_c, tri_r, mask = (lt, ut, lower) if d == 0 else (ut, lt, upper)
            bcol = _cum_dot(tri_c, gates, split_a=False)
            brow = _cum_dot(gates_t, tri_r, split_a=True)
            end = T - 1 if d == 0 else 0
            for h in range(ML_HEADS):
                r = (b * 2 + d) * ML_HEADS + h
                li, lf = 8 * d + h, 8 * d + 4 + h
                pair = h // 2
                qp = q_ref[b, :, pair * LANES:(pair + 1) * LANES]
                kp = k_ref[b, :, pair * LANES:(pair + 1) * LANES]
                kh = jnp.where(lo, kp, zero) if h % 2 == 0 else jnp.where(lo, zero, kp)
                vext = jnp.concatenate([v_ref[b, :, h * ML_V_DIM:(h + 1) * ML_V_DIM], ones], axis=1)
                i_col = gates[:, li:li + 1]
                b_col = bcol[:, lf:lf + 1]
                m_prev = m_ref[r:r + 1, 0:1]
                s_prev = s_ref[r]
                b_end = b_col[end:end + 1, :]
                if with_output:
                    i_row = gates_t[li:li + 1, :]
                    b_row = brow[lf:lf + 1, :]
                    log_d = jnp.where(mask, b_col - b_row + i_row, NEG_BIG)
                    m_t = jnp.maximum(b_col + m_prev, jnp.max(log_d, axis=1, keepdims=True))
                    inter = jnp.exp(b_col + m_prev - m_t)
                    qk = lax.dot_general(qp, kh, (((1,), (1,)), ((), ())), preferred_element_type=F32)
                    qk = qk * jnp.exp(log_d - m_t)
                    intra = jnp.dot(qk.astype(BF16), vext, preferred_element_type=F32)
                    carry = jnp.dot(qp, s_prev.astype(BF16), preferred_element_type=F32)
                    tot = intra + inter * carry
                    hout = tot[:, :ML_V_DIM] / jnp.maximum(jnp.abs(tot[:, ML_V_DIM:]), jnp.exp(-m_t))
                    o_ref = hf_ref if d == 0 else hb_ref
                    o_ref[b, :, h * ML_V_DIM:(h + 1) * ML_V_DIM] = hout
                w_end = b_end - b_col + i_col
                m_new = jnp.maximum(b_end + m_prev, jnp.max(w_end, axis=0, keepdims=True))
                decay = jnp.exp(b_end + m_prev - m_new)
                kw = (kh.astype(F32) * jnp.exp(w_end - m_new)).astype(BF16)
                upd = lax.dot_general(kw, vext, (((0,), (0,)), ((), ())), preferred_element_type=F32)
                s_ref[r] = decay * s_prev + upd
                m_ref[r:r + 1, :] = jnp.broadcast_to(m_new, (1, LANES))


def _mlstm(qm, km, vm, gates, s0, m0, with_output):
    B, L, _ = qm.shape
    T = ML_CHUNK
    nc = L // T
    nchains = B * 2 * ML_HEADS

    def fwd(c):
        return pl.BlockSpec((B, T, c), lambda j: (0, j, 0))

    def bwd(c):
        return pl.BlockSpec((B, T, c), lambda j: (0, nc - 1 - j, 0))

    s_spec = _const_spec((nchains, LANES, 2 * ML_V_DIM))
    m_spec = _const_spec((nchains, LANES))
    out_shape = [jax.ShapeDtypeStruct((nchains, LANES, 2 * ML_V_DIM), F32),
                 jax.ShapeDtypeStruct((nchains, LANES), F32)]
    out_specs = [s_spec, m_spec]
    if with_output:
        out_shape = [jax.ShapeDtypeStruct((B, L, ML_V), F32)] * 2 + out_shape
        out_specs = [fwd(ML_V), bwd(ML_V)] + out_specs
    return pl.pallas_call(
        functools.partial(_mlstm_kernel, T=T, nb=B, with_output=with_output),
        out_shape=tuple(out_shape),
        grid=(nc,),
        in_specs=[fwd(ML_QK), fwd(ML_QK), fwd(ML_V), fwd(LANES),
                  bwd(ML_QK), bwd(ML_QK), bwd(ML_V), bwd(LANES), s_spec, m_spec],
        out_specs=tuple(out_specs),
        compiler_params=_cparams(("arbitrary",)),
        name="mlstm_scan" if with_output else "mlstm_context_state",
    )(qm, km, vm, gates, qm, km, vm, gates, s0, m0)


def _odd_weights(w_in, gate_b, q_g, k_g):
    idx = np.cumsum([ATT_Q, ATT_KV, ATT_KV, ML_QK, ML_QK, ML_V, ML_V]).tolist()
    w = jnp.pad(w_in, ((0, 0), (0, ODD_COLS - w_in.shape[1]))).astype(BF16)
    assert idx[-1] == OG
    nqk = OV - OQ
    head = np.arange(nqk) // HEAD_DIM
    pm = jnp.asarray((head[:, None] == head[None, :]) / HEAD_DIM, F32).astype(BF16)
    gain = jnp.concatenate([jnp.tile(q_g, ATT_HEADS), jnp.tile(k_g, ATT_KV_HEADS)])[None, :]
    gb = jnp.pad(gate_b.reshape(1, -1), ((0, 0), (0, LANES - gate_b.size)))
    return w, pm, gain, gb


def kernel(x, c, ctx, c_ctx, ada_w, ada_b, norm_g, even_w_in, even_conv, even_w_out, odd_w_in, odd_gate_b,
           odd_q_g, odd_k_g, odd_sink, odd_w_out, ffn_w_up, ffn_conv, ffn_w_down):
    B, L, _ = x.shape
    C = ctx.shape[1]
    depth = ada_w.shape[0]
    assert depth == 2 and L % (DFT_N1 * SUBLANES) == 0 and C % ML_CHUNK == 0

    cv = jnp.concatenate([c, c_ctx[None, :], jnp.zeros((SUBLANES - B - 1, D_MODEL), F32)], axis=0)
    mod = _modulation(cv, ada_w, ada_b).reshape(depth, SUBLANES, 6, D_MODEL)
    pad = ((0, 0), (0, SUBLANES - 6), (0, 0))
    mod_lat = [jnp.pad(mod[l, :B], pad) for l in range(depth)]
    mod_ctx = [jnp.pad(jnp.broadcast_to(mod[l, B], (B, 6, D_MODEL)), pad) for l in range(depth)]

    tm = min(512, L)
    tc = _channel_dft_table()
    ffn_w = [(ffn_w_up[l].astype(BF16), ffn_conv[l], ffn_w_down[l].astype(BF16)) for l in range(depth)]

    w_in0 = even_w_in[0].astype(BF16)
    w_out0 = even_w_out[0].astype(BF16)
    ng00, ng01 = norm_g[0, 0][None, :], norm_g[0, 1][None, :]

    def even_layer(xs, mods, tile, seq_dft):
        yc, zr, zi = _even_in(xs, mods, ng00, w_in0, even_conv[0], tc, tile)
        xs = _even_out(xs, yc, seq_dft(zr, zi), mods, w_out0, tile)
        return _ffn(xs, mods, ng01, *ffn_w[0], tile)

    xl = even_layer(x, mod_lat[0], tm, _seq_dft)
    xc = even_layer(ctx, mod_ctx[0], C, _dense_seq_dft)

    w_in1, pm, gain, gb = _odd_weights(odd_w_in[0], odd_gate_b[0], odd_q_g[0], odd_k_g[0])
    ng10, ng11 = norm_g[1, 0][None, :], norm_g[1, 1][None, :]
    cos, sin = _rope_tables(L)
    one, nil = jnp.ones((C, LANES), F32), jnp.zeros((C, LANES), F32)
    q, kv, qm, km, vm, om, gates = _odd_in(xl, mod_lat[1], ng10, w_in1, pm, gain, cos, sin, gb, tm)
    _, kvx, qmx, kmx, vmx, _, gatesx = _odd_in(xc, mod_ctx[1], ng10, w_in1, pm, gain, one, nil, gb, C)

    att = _attention(q, kv, kvx, odd_sink[0])
    nchains = B * 2 * ML_HEADS
    s0 = jnp.zeros((nchains, LANES, 2 * ML_V_DIM), F32)
    m0 = jnp.zeros((nchains, LANES), F32)
    s1, m1 = _mlstm(qmx, kmx, vmx, gatesx, s0, m0, with_output=False)
    hf, hb, _, _ = _mlstm(qm, km, vm, gates, s1, m1, with_output=True)
    xl = _odd_out(xl, att, hf, hb, om, mod_lat[1], odd_w_out[0].astype(BF16), tm)
    return _ffn(xl, mod_lat[1], ng11, *ffn_w[1], tm)
```

```python
import functools

import numpy as np
import jax
import jax.numpy as jnp
from jax import lax
from jax.experimental import pallas as pl
from jax.experimental.pallas import tpu as pltpu

F32 = jnp.float32
BF16 = jnp.bfloat16

D_MODEL = 1024
GRID_W = 64
EPS = 1e-6
SC_CH = 512
FT_CH = 512
FT_GROUPS = 4
FT_GROUP_CH = FT_CH // FT_GROUPS
EVEN_IN = 3 * SC_CH + FT_CH
ATT_HEADS = 8
ATT_KV_HEADS = 2
HEAD_DIM = 64
ATT_SCALE = HEAD_DIM ** -0.5
WINDOW = 128
BLOCK = 128
ROPE_THETA = 10000.0
ML_HEADS = 4
ML_QK_DIM = 64
ML_V_DIM = 128
ATT_Q = ATT_HEADS * HEAD_DIM
ATT_KV = ATT_KV_HEADS * HEAD_DIM
ML_QK = ML_HEADS * ML_QK_DIM
ML_V = ML_HEADS * ML_V_DIM
D_FF = 2816

LANES = 128
SUBLANES = 8
VMEM_LIMIT_BYTES = 56 * 1024 * 1024

DFT_N1 = 128
FF_CHUNK = 256
N_FF_CHUNKS = D_FF // FF_CHUNK
ML_CHUNK = 128
NEG_BIG = -1e30
LOG2E = 1.4426950408889634

OQ, OK_, OV, OQM, OKM, OVM, OOM, OG = 0, 512, 640, 768, 1024, 1280, 1792, 2304
ODD_COLS = OG + LANES


def _cparams(sem):
    return pltpu.CompilerParams(dimension_semantics=sem, vmem_limit_bytes=VMEM_LIMIT_BYTES)


def _sigmoid(x):
    return 1.0 / (1.0 + jnp.exp(-x))


def _norm_mod(x, g, shift, scale):
    y = x * lax.rsqrt(jnp.mean(x * x, axis=-1, keepdims=True) + EPS)
    return y * g * (1.0 + scale) + shift


def _halo_rows(x_ref, xn_ref, xp_ref, mod_ref, ng_ref, row0):
    shift = mod_ref[0, row0:row0 + 1, :]
    scale = mod_ref[0, row0 + 1:row0 + 2, :]
    g = ng_ref[...]
    parts = [_norm_mod(r[0], g, shift, scale) for r in (x_ref, xn_ref, xp_ref)]
    return jnp.concatenate(parts, axis=0).astype(BF16)


def _halo_valid(tm, i, nt):
    row = lax.broadcasted_iota(jnp.int32, (tm + 2 * SUBLANES, 1), 0)
    return ((row < tm) | ((row < tm + SUBLANES) & (i < nt - 1)) | ((row >= tm + SUBLANES) & (i > 0)))


def _conv3(v, cw, tm):
    n = v.shape[0]
    vp = pltpu.roll(v, 1, 0)[:tm]
    vn = pltpu.roll(v, n - 1, 0)[:tm]
    return vp * cw[0:1] + v[:tm] * cw[1:2] + vn * cw[2:3]


def _halo_specs(tm, L):
    hb = tm // SUBLANES
    last = L // SUBLANES - 1
    return [
        pl.BlockSpec((1, tm, D_MODEL), lambda b, i: (b, i, 0)),
        pl.BlockSpec((1, SUBLANES, D_MODEL), lambda b, i: (b, jnp.minimum((i + 1) * hb, last), 0)),
        pl.BlockSpec((1, SUBLANES, D_MODEL), lambda b, i: (b, jnp.maximum(i * hb - 1, 0), 0)),
    ]


def _const_spec(shape):
    nd = len(shape)
    return pl.BlockSpec(shape, lambda *_: (0,) * nd)


def _resident_spec(shape):
    nd = len(shape)
    return pl.BlockSpec(shape, lambda *_: (0,) * nd, pipeline_mode=pl.Buffered(1))


def _mod_kernel(cv_ref, w_ref, b_ref, o_ref):
    cv = cv_ref[...]
    a = cv * _sigmoid(cv)
    o_ref[0] = jnp.dot(a, w_ref[0], preferred_element_type=F32,
                       precision=lax.Precision.HIGHEST) + b_ref[0]


def _modulation(cv, ada_w, ada_b):
    depth, _, n = ada_w.shape
    tn = 1536
    return pl.pallas_call(
        _mod_kernel,
        out_shape=jax.ShapeDtypeStruct((depth, SUBLANES, n), F32),
        grid=(depth, n // tn),
        in_specs=[
            pl.BlockSpec((SUBLANES, D_MODEL), lambda l, j: (0, 0)),
            pl.BlockSpec((1, D_MODEL, tn), lambda l, j: (l, 0, j)),
            pl.BlockSpec((1, 1, tn), lambda l, j: (l, 0, j)),
        ],
        out_specs=pl.BlockSpec((1, SUBLANES, tn), lambda l, j: (l, 0, j)),
        compiler_params=_cparams(("arbitrary", "arbitrary")),
        name="modulation",
    )(cv, ada_w, ada_b.reshape(depth, 1, n))


def _even_in_kernel(x_ref, xn_ref, xp_ref, mod_ref, ng_ref, w_ref, cw_ref, tc_ref,
                    yc_ref, zr_ref, zi_ref, *, tm, nt):
    i = pl.program_id(1)
    hh = _halo_rows(x_ref, xn_ref, xp_ref, mod_ref, ng_ref, 0)
    u = jnp.dot(hh, w_ref[...], preferred_element_type=F32)
    v = u[:, SC_CH:2 * SC_CH] * u[:, 2 * SC_CH:3 * SC_CH]
    v = jnp.where(_halo_valid(tm, i, nt), v, 0.0)
    yc = u[:tm, :SC_CH] * _conv3(v, cw_ref[...], tm)
    yc_ref[0] = yc.astype(BF16)
    uf = u[:tm, 3 * SC_CH:].astype(BF16)
    tc = tc_ref[...]
    for g in range(FT_GROUPS):
        sl = slice(g * FT_GROUP_CH, (g + 1) * FT_GROUP_CH)
        ab = jnp.dot(uf[:, sl], tc, preferred_element_type=F32)
        zr_ref[0, :, sl] = ab[:, :FT_GROUP_CH]
        zi_ref[0, :, sl] = ab[:, FT_GROUP_CH:]


def _even_in(x, mod, ng, w_in, cw, tc, tm):
    B, L, _ = x.shape
    nt = L // tm
    out = jax.ShapeDtypeStruct((B, L, FT_CH), BF16)
    zout = jax.ShapeDtypeStruct((B, L, FT_CH), F32)
    ospec = pl.BlockSpec((1, tm, FT_CH), lambda b, i: (b, i, 0))
    return pl.pallas_call(
        functools.partial(_even_in_kernel, tm=tm, nt=nt),
        out_shape=(out, zout, zout),
        grid=(B, nt),
        in_specs=_halo_specs(tm, L) + [
            pl.BlockSpec((1, SUBLANES, D_MODEL), lambda b, i: (b, 0, 0)),
            _const_spec((1, D_MODEL)),
            _const_spec((D_MODEL, EVEN_IN)),
            _const_spec((3, SC_CH)),
            _const_spec((FT_GROUP_CH, 2 * FT_GROUP_CH)),
        ],
        out_specs=(ospec, ospec, ospec),
        compiler_params=_cparams(("parallel", "arbitrary")),
        name="even_in",
    )(x, x, x, mod, ng, w_in, cw, tc)


def _seq_dft_kernel(zr_ref, zi_ref, m_ref, g_ref, y_ref, o_scr, *, n2):
    m1 = m_ref[...]
    for j in range(n2):
        z = jnp.concatenate([zr_ref[0, pl.ds(j, DFT_N1, stride=n2), :],
                             zi_ref[0, pl.ds(j, DFT_N1, stride=n2), :]], axis=0)
        o_scr[2 * DFT_N1 * j:2 * DFT_N1 * (j + 1), :] = jnp.dot(m1, z.astype(BF16), preferred_element_type=F32)
    for k1 in range(DFT_N1):
        o = jnp.concatenate([o_scr[pl.ds(k1, n2, stride=2 * DFT_N1), :],
                             o_scr[pl.ds(DFT_N1 + k1, n2, stride=2 * DFT_N1), :]], axis=0)
        y_ref[0, pl.ds(k1, n2, stride=DFT_N1), :] = jnp.dot(g_ref[k1], o.astype(BF16), preferred_element_type=F32)


def _dft_tables(L):
    n2 = L // DFT_N1
    k = np.arange(DFT_N1)
    a = 2.0 * np.pi * ((k[:, None] * k[None, :]) % DFT_N1) / DFT_N1
    er, ei = np.cos(a) / np.sqrt(DFT_N1), -np.sin(a) / np.sqrt(DFT_N1)
    m1 = np.block([[er, -ei], [ei, er]])
    k1 = np.arange(DFT_N1)[:, None, None]
    k2 = np.arange(n2)[None, :, None]
    nn = np.arange(n2)[None, None, :]
    th = 2.0 * np.pi * ((nn * (k1 + DFT_N1 * k2)) % L) / L
    g = np.concatenate([np.cos(th), np.sin(th)], axis=-1) / np.sqrt(n2)
    return jnp.asarray(m1, F32).astype(BF16), jnp.asarray(g, F32).astype(BF16)


def _channel_dft_table():
    k = np.arange(FT_GROUP_CH)
    a = 2.0 * np.pi * ((k[:, None] * k[None, :]) % FT_GROUP_CH) / FT_GROUP_CH
    t = np.concatenate([np.cos(a), -np.sin(a)], axis=1) / np.sqrt(FT_GROUP_CH)
    return jnp.asarray(t, F32).astype(BF16)


def _seq_dft(zr, zi):
    B, L, C = zr.shape
    n2 = L // DFT_N1
    m1, g = _dft_tables(L)
    zspec = pl.BlockSpec((1, L, LANES), lambda b, j: (b, 0, j))
    return pl.pallas_call(
        functools.partial(_seq_dft_kernel, n2=n2),
        out_shape=jax.ShapeDtypeStruct((B, L, C), F32),
        grid=(B, C // LANES),
        in_specs=[zspec, zspec, _const_spec((2 * DFT_N1, 2 * DFT_N1)), _const_spec((DFT_N1, n2, 2 * n2))],
        out_specs=zspec,
        scratch_shapes=[pltpu.VMEM((2 * DFT_N1 * n2, LANES), F32)],
        compiler_params=_cparams(("parallel", "arbitrary")),
        name="seq_dft",
    )(zr, zi, m1, g)


def _dense_dft_kernel(zr_ref, zi_ref, t_ref, y_ref):
    z = jnp.concatenate([zr_ref[0], zi_ref[0]], axis=0).astype(BF16)
    y_ref[0] = jnp.dot(t_ref[...], z, preferred_element_type=F32)


def _dense_seq_dft(zr, zi):
    B, L, C = zr.shape
    k = np.arange(L)
    a = 2.0 * np.pi * ((k[:, None] * k[None, :]) % L) / L
    t = jnp.asarray(np.concatenate([np.cos(a), np.sin(a)], axis=1) / np.sqrt(L), F32).astype(BF16)
    zspec = pl.BlockSpec((1, L, C), lambda b: (b, 0, 0))
    return pl.pallas_call(
        _dense_dft_kernel,
        out_shape=jax.ShapeDtypeStruct((B, L, C), F32),
        grid=(B,),
        in_specs=[zspec, zspec, _const_spec((L, 2 * L))],
        out_specs=zspec,
        compiler_params=_cparams(("arbitrary",)),
        name="dense_seq_dft",
    )(zr, zi, t)


def _even_out_kernel(x_ref, yc_ref, yf_ref, mod_ref, w_ref, o_ref):
    cat = jnp.concatenate([yc_ref[0], yf_ref[0].astype(BF16)], axis=-1)
    y = jnp.dot(cat, w_ref[...], preferred_element_type=F32)
    o_ref[0] = x_ref[0] + mod_ref[0, 2:3, :] * y


def _even_out(x, yc, yf, mod, w_out, tm):
    B, L, _ = x.shape
    xspec = pl.BlockSpec((1, tm, D_MODEL), lambda b, i: (b, i, 0))
    hspec = pl.BlockSpec((1, tm, FT_CH), lambda b, i: (b, i, 0))
    return pl.pallas_call(
        _even_out_kernel,
        out_shape=jax.ShapeDtypeStruct(x.shape, F32),
        grid=(B, L // tm),
        in_specs=[xspec, hspec, hspec,
                  pl.BlockSpec((1, SUBLANES, D_MODEL), lambda b, i: (b, 0, 0)),
                  _const_spec((D_MODEL, D_MODEL))],
        out_specs=xspec,
        compiler_params=_cparams(("parallel", "arbitrary")),
        name="even_out",
    )(x, yc, yf, mod, w_out)


def _odd_out_kernel(x_ref, att_ref, hf_ref, hb_ref, om_ref, mod_ref, w_ref, o_ref):
    ml = (hf_ref[0] + hb_ref[0]) * _sigmoid(om_ref[0].astype(F32))
    cat = jnp.concatenate([att_ref[0], ml.astype(BF16)], axis=-1)
    y = jnp.dot(cat, w_ref[...], preferred_element_type=F32)
    o_ref[0] = x_ref[0] + mod_ref[0, 2:3, :] * y


def _odd_out(x, att, hf, hb, om, mod, w_out, tm):
    B, L, _ = x.shape
    xspec = pl.BlockSpec((1, tm, D_MODEL), lambda b, i: (b, i, 0))
    hspec = pl.BlockSpec((1, tm, ML_V), lambda b, i: (b, i, 0))
    return pl.pallas_call(
        _odd_out_kernel,
        out_shape=jax.ShapeDtypeStruct(x.shape, F32),
        grid=(B, L // tm),
        in_specs=[xspec, hspec, hspec, hspec, hspec,
                  pl.BlockSpec((1, SUBLANES, D_MODEL), lambda b, i: (b, 0, 0)),
                  _const_spec((D_MODEL, D_MODEL))],
        out_specs=xspec,
        compiler_params=_cparams(("parallel", "arbitrary")),
        name="odd_out",
    )(x, att, hf, hb, om, mod, w_out)


def _ffn_kernel(x_ref, xn_ref, xp_ref, mod_ref, ng_ref, wu_ref, cw_ref, wd_ref, o_ref,
                hh_scr, act_scr, *, tm, nt):
    i = pl.program_id(1)
    hh_scr[...] = _halo_rows(x_ref, xn_ref, xp_ref, mod_ref, ng_ref, 3)
    valid = _halo_valid(tm, i, nt)
    for c in range(N_FF_CHUNKS):
        lo = c * FF_CHUNK
        g = jnp.dot(hh_scr[...], wu_ref[:, lo:lo + FF_CHUNK], preferred_element_type=F32)
        g = jnp.where(valid, g, 0.0)
        cv = _conv3(g, cw_ref[:, lo:lo + FF_CHUNK], tm)
        val = jnp.dot(hh_scr[:tm, :], wu_ref[:, D_FF + lo:D_FF + lo + FF_CHUNK], preferred_element_type=F32)
        act_scr[:, lo:lo + FF_CHUNK] = (cv * _sigmoid(cv) * val).astype(BF16)
    y = jnp.dot(act_scr[...], wd_ref[...], preferred_element_type=F32)
    o_ref[0] = x_ref[0] + mod_ref[0, 5:6, :] * y


def _ffn(x, mod, ng, w_up, cw, w_down, tm):
    B, L, _ = x.shape
    nt = L // tm
    return pl.pallas_call(
        functools.partial(_ffn_kernel, tm=tm, nt=nt),
        out_shape=jax.ShapeDtypeStruct(x.shape, F32),
        grid=(B, nt),
        in_specs=_halo_specs(tm, L) + [
            pl.BlockSpec((1, SUBLANES, D_MODEL), lambda b, i: (b, 0, 0)),
            _const_spec((1, D_MODEL)),
            _resident_spec((D_MODEL, 2 * D_FF)),
            _const_spec((3, D_FF)),
            _resident_spec((D_FF, D_MODEL)),
        ],
        out_specs=pl.BlockSpec((1, tm, D_MODEL), lambda b, i: (b, i, 0)),
        scratch_shapes=[pltpu.VMEM((tm + 2 * SUBLANES, D_MODEL), BF16),
                        pltpu.VMEM((tm, D_FF), BF16)],
        compiler_params=_cparams(("parallel", "arbitrary")),
        name="conv_ffn",
    )(x, x, x, mod, ng, w_up, cw, w_down)


def _split3(x):
    parts = []
    r = x
    for _ in range(3):
        p = r.astype(BF16)
        parts.append(p)
        r = r - p.astype(F32)
    return parts


def _split_dot(x, p):
    hi = x.astype(BF16)
    lo = (x - hi.astype(F32)).astype(BF16)
    return (jnp.dot(hi, p, preferred_element_type=F32) + jnp.dot(lo, p, preferred_element_type=F32))


def _odd_in_kernel(x_ref, mod_ref, ng_ref, w_ref, pm_ref, gain_ref, cos_ref, sin_ref, gb_ref, tl_ref, tu_ref,
                   qt_ref, kv_ref, vt_ref, qm_ref, kmt_ref, vm_ref, om_ref, gc_ref, gr_ref):
    shift = mod_ref[0, 0:1, :]
    scale = mod_ref[0, 1:2, :]
    h = _norm_mod(x_ref[0], ng_ref[...], shift, scale).astype(BF16)
    u = jnp.dot(h, w_ref[...], preferred_element_type=F32)

    uqk = u[:, OQ:OV]
    ms = _split_dot(uqk * uqk, pm_ref[...])
    rn = uqk * lax.rsqrt(ms + EPS) * gain_ref[...]
    lane = lax.broadcasted_iota(jnp.int32, (1, LANES), 1)
    first = (lane % 32) < 16
    cos = cos_ref[...]
    sin = sin_ref[...]
    roped = []
    for t in range((OV - OQ) // LANES):
        xt = rn[:, t * LANES:(t + 1) * LANES]
        sw = jnp.where(first, pltpu.roll(xt, LANES - 16, 1), pltpu.roll(xt, 16, 1))
        roped.append(xt * cos + sw * sin)
    for t in range(ATT_Q // LANES):
        qt_ref[0, t * LANES:(t + 1) * LANES, :] = (roped[t] * (ATT_SCALE * LOG2E)).T.astype(BF16)
    k = roped[ATT_Q // LANES]
    v = u[:, OV:OQM]
    half = LANES // 2
    kv_ref[0, :, 0:LANES] = k.astype(BF16)
    kv_ref[0, :, LANES:2 * LANES] = pltpu.roll(k, half, 1).astype(BF16)
    vt_ref[0] = v.T.astype(BF16)

    qm_ref[0] = u[:, OQM:OKM].astype(BF16)
    for p in range(ML_QK // LANES):
        km = u[:, OKM + p * LANES:OKM + (p + 1) * LANES] * (ML_QK_DIM ** -0.5)
        kmt_ref[0, p * LANES:(p + 1) * LANES, :] = km.T.astype(BF16)
    vm_ref[0] = u[:, OVM:OOM].astype(BF16)
    om_ref[0] = u[:, OOM:OG].astype(BF16)

    g = u[:, OG:ODD_COLS] + gb_ref[...]
    logsig = jnp.minimum(g, 0.0) - jnp.log(1.0 + jnp.exp(-jnp.abs(g)))
    parts = _split3(logsig)
    tl, tu = tl_ref[...], tu_ref[...]
    cum_f = sum(jnp.dot(tl, p, preferred_element_type=F32) for p in parts)
    cum_b = sum(jnp.dot(tu, p, preferred_element_type=F32) for p in parts)
    sel = lane % 16
    gc = jnp.where((sel >= 4) & (sel < 8), cum_f, jnp.where(sel >= 12, cum_b, g))
    gc_ref[0] = gc
    gr_ref[0] = gc.T[:2 * SUBLANES, :]


def _chunk_tri(tm):
    i = np.arange(tm)
    same = (i[:, None] // ML_CHUNK) == (i[None, :] // ML_CHUNK)
    tl = same & (i[None, :] <= i[:, None])
    tu = same & (i[None, :] >= i[:, None])
    return jnp.asarray(tl, F32).astype(BF16), jnp.asarray(tu, F32).astype(BF16)


def _odd_in(x, mod, ng, w, pm, gain, cos, sin, gb, tm):
    B, L, _ = x.shape

    def rows(c, dt=BF16):
        return jax.ShapeDtypeStruct((B, L, c), dt), pl.BlockSpec((1, tm, c), lambda b, i: (b, i, 0))

    def cols(c, dt=BF16):
        return jax.ShapeDtypeStruct((B, c, L), dt), pl.BlockSpec((1, c, tm), lambda b, i: (b, 0, i))

    outs = [cols(ATT_Q), rows(2 * LANES), cols(LANES), rows(ML_QK), cols(ML_QK), rows(ML_V), rows(ML_V),
            rows(LANES, F32), cols(2 * SUBLANES, F32)]
    nqk = OV - OQ
    tl, tu = _chunk_tri(tm)
    return pl.pallas_call(
        _odd_in_kernel,
        out_shape=tuple(o[0] for o in outs),
        grid=(B, L // tm),
        in_specs=[
            pl.BlockSpec((1, tm, D_MODEL), lambda b, i: (b, i, 0)),
            pl.BlockSpec((1, SUBLANES, D_MODEL), lambda b, i: (b, 0, 0)),
            _const_spec((1, D_MODEL)),
            _const_spec((D_MODEL, ODD_COLS)),
            _const_spec((nqk, nqk)),
            _const_spec((1, nqk)),
            pl.BlockSpec((tm, LANES), lambda b, i: (i, 0)),
            pl.BlockSpec((tm, LANES), lambda b, i: (i, 0)),
            _const_spec((1, LANES)),
            _const_spec((tm, tm)),
            _const_spec((tm, tm)),
        ],
        out_specs=tuple(o[1] for o in outs),
        compiler_params=_cparams(("parallel", "arbitrary")),
        name="odd_in",
    )(x, mod, ng, w, pm, gain, cos, sin, gb, tl, tu)


def _rope_tables(L):
    rows = L // GRID_W
    row = jnp.repeat(jnp.arange(rows, dtype=F32), GRID_W)
    col = jnp.tile(jnp.arange(GRID_W, dtype=F32), rows)
    axis_dim = HEAD_DIM // 2
    inv_freq = ROPE_THETA ** (-jnp.arange(0, axis_dim, 2, dtype=F32) / axis_dim)
    ang = jnp.stack([row, col])[:, :, None] * inv_freq
    c, s = jnp.cos(ang), jnp.sin(ang)
    cos = jnp.concatenate([c[0], c[0], c[1], c[1]], axis=-1)
    sin = jnp.concatenate([-s[0], s[0], -s[1], s[1]], axis=-1)
    return jnp.tile(cos, (1, 2)), jnp.tile(sin, (1, 2))


def _att_kernel(sink_ref, qt_ref, kvp_ref, kvc_ref, kvn_ref, kvx_ref, vtp_ref, vtc_ref, vtn_ref, vtx_ref,
                o_ref, *, nb):
    n = pl.program_id(1)
    half = LANES // 2
    lo = lax.broadcasted_iota(jnp.int32, (1, LANES), 1) < half
    zero = jnp.zeros((), BF16)
    cx = kvx_ref.shape[1]
    kblocks = [kvp_ref[0], kvc_ref[0], kvn_ref[0]] + [kvx_ref[0, i:i + BLOCK] for i in range(0, cx, BLOCK)]
    vblocks = [vtp_ref[0], vtc_ref[0], vtn_ref[0]] + [vtx_ref[0, :, i:i + BLOCK] for i in range(0, cx, BLOCK)]
    ones = jnp.ones((half, BLOCK), BF16)

    kj = lax.broadcasted_iota(jnp.int32, (BLOCK, BLOCK), 0)
    qi = lax.broadcasted_iota(jnp.int32, (BLOCK, BLOCK), 1)

    def twice(x):
        return jnp.concatenate([x, x], axis=1)

    bias = [twice(jnp.where((kj >= qi) & (n > 0), 0.0, NEG_BIG)), None,
            twice(jnp.where((kj <= qi) & (n < nb - 1), 0.0, NEG_BIG))] + [None] * (cx // BLOCK)
    left = lax.broadcasted_iota(jnp.int32, (1, 2 * BLOCK), 1) < BLOCK

    group = ATT_HEADS // ATT_KV_HEADS
    for kvh in range(ATT_KV_HEADS):
        t0 = kvh * group // 2
        qt2 = jnp.concatenate([qt_ref[0, t0 * LANES:(t0 + 1) * LANES, :],
                               qt_ref[0, (t0 + 1) * LANES:(t0 + 2) * LANES, :]], axis=1)
        outs = []
        for par in range(2):
            sk = jnp.where(left, sink_ref[2 * t0 + par], sink_ref[2 * t0 + 2 + par]) * LOG2E
            m = sk
            acc = jnp.zeros((LANES, 2 * BLOCK), F32)
            for blk in range(len(kblocks)):
                k, ks = kblocks[blk][:, :LANES], kblocks[blk][:, LANES:]
                kh = ((jnp.where(lo, k, zero), jnp.where(lo, zero, ks)),
                      (jnp.where(lo, ks, zero), jnp.where(lo, zero, k)))[kvh][par]
                s = jnp.dot(kh, qt2, preferred_element_type=F32)
                if bias[blk] is not None:
                    s = s + bias[blk]
                m_new = jnp.maximum(m, jnp.max(s, axis=0, keepdims=True))
                p = jnp.exp2(s - m_new).astype(BF16)
                vh = jnp.concatenate([vblocks[blk][kvh * half:(kvh + 1) * half], ones], axis=0)
                acc = acc * jnp.exp2(m - m_new) + jnp.dot(vh, p, preferred_element_type=F32)
                m = m_new
            l = acc[half:half + 1, :] + jnp.exp2(sk - m)
            outs.append(acc[:half] * (1.0 / l))
        ot = jnp.concatenate(outs, axis=0)
        o_ref[0, :, t0 * LANES:(t0 + 1) * LANES] = ot[:, :BLOCK].T.astype(BF16)
        o_ref[0, :, (t0 + 1) * LANES:(t0 + 2) * LANES] = ot[:, BLOCK:].T.astype(BF16)


def _attention(qt, kv, kvx, vt, vtx, sink):
    B, _, L = qt.shape
    nb = L // BLOCK
    cx = kvx.shape[1]
    kw = kv.shape[2]
    vw = vt.shape[1]
    prev = lambda n: jnp.maximum(n - 1, 0)
    nxt = lambda n: jnp.minimum(n + 1, nb - 1)
    return pl.pallas_call(
        functools.partial(_att_kernel, nb=nb),
        out_shape=jax.ShapeDtypeStruct((B, L, ATT_Q), BF16),
        grid=(B, nb),
        in_specs=[
            pl.BlockSpec(memory_space=pltpu.SMEM),
            pl.BlockSpec((1, ATT_Q, BLOCK), lambda b, n: (b, 0, n)),
            pl.BlockSpec((1, BLOCK, kw), lambda b, n: (b, prev(n), 0)),
            pl.BlockSpec((1, BLOCK, kw), lambda b, n: (b, n, 0)),
            pl.BlockSpec((1, BLOCK, kw), lambda b, n: (b, nxt(n), 0)),
            pl.BlockSpec((1, cx, kw), lambda b, n: (b, 0, 0)),
            pl.BlockSpec((1, vw, BLOCK), lambda b, n: (b, 0, prev(n))),
            pl.BlockSpec((1, vw, BLOCK), lambda b, n: (b, 0, n)),
            pl.BlockSpec((1, vw, BLOCK), lambda b, n: (b, 0, nxt(n))),
            pl.BlockSpec((1, vw, cx), lambda b, n: (b, 0, 0)),
        ],
        out_specs=pl.BlockSpec((1, BLOCK, ATT_Q), lambda b, n: (b, n, 0)),
        compiler_params=_cparams(("parallel", "arbitrary")),
        name="window_attention",
    )(sink, qt, kv, kv, kv, kvx, vt, vt, vt, vtx)


ML_GROUP = 8


def _mlstm_kernel(qf_ref, ktf_ref, vf_ref, gcf_ref, grf_ref, qb_ref, ktb_ref, vb_ref, gcb_ref, grb_ref,
                  s0_ref, m0_ref, *rest, T, nb, with_output):
    if with_output:
        hf_ref, hb_ref, s_ref, m_ref = rest
    else:
        s_ref, m_ref = rest
    j = pl.program_id(0)

    @pl.when(j == 0)
    def _():
        s_ref[...] = s0_ref[...]
        m_ref[...] = m0_ref[...]

    ti = lax.broadcasted_iota(jnp.int32, (T, T), 0)
    si = lax.broadcasted_iota(jnp.int32, (T, T), 1)
    masks = (si <= ti, si >= ti)
    top = lax.broadcasted_iota(jnp.int32, (LANES, 1), 0) < (LANES // 2)
    zero = jnp.zeros((), BF16)
    ones = jnp.ones((T, ML_V_DIM), BF16)
    fwd = (qf_ref, ktf_ref, vf_ref, gcf_ref, grf_ref)
    bwd = (qb_ref, ktb_ref, vb_ref, gcb_ref, grb_ref)
    chains = [(b, d, h) for b in range(nb) for d in range(2) for h in range(ML_HEADS)]

    for g0 in range(0, len(chains), ML_GROUP):
        grp = chains[g0:g0 + ML_GROUP]
        st = {}
        for (b, d, h) in grp:
            q_ref, kt_ref, v_ref, gc_ref, gr_ref = fwd if d == 0 else bwd
            r = (b * 2 + d) * ML_HEADS + h
            li, lf = 8 * d + h, 8 * d + 4 + h
            gr = gr_ref[b]
            u_row = gr[li:li + 1, :] - gr[lf:lf + 1, :]
            m_prev = m_ref[r:r + 1, 0:1]
            e = dict(r=r, u_row=u_row, m_prev=m_prev,
                     b_end=gr[lf:lf + 1, (T - 1 if d == 0 else 0):(T if d == 0 else 1)],
                     c_end=jnp.maximum(jnp.max(u_row, axis=1, keepdims=True), m_prev))
            if with_output:
                e["mu"] = jnp.where(masks[d], u_row, NEG_BIG)
                c_col = jnp.maximum(jnp.max(e["mu"], axis=1, keepdims=True), m_prev)
                e["cb"] = jnp.broadcast_to(c_col, (T, LANES))
                e["bb"] = jnp.broadcast_to(gc_ref[b, :, lf:lf + 1], (T, LANES))
            st[(b, d, h)] = e
        for (b, d, h) in grp:
            q_ref, kt_ref, v_ref, gc_ref, gr_ref = fwd if d == 0 else bwd
            e = st[(b, d, h)]
            pair = h // 2
            ktp = kt_ref[b, pair * LANES:(pair + 1) * LANES, :]
            e["kth"] = jnp.where(top, ktp, zero) if h % 2 == 0 else jnp.where(top, zero, ktp)
            e["vext"] = jnp.concatenate([v_ref[b, :, h * ML_V_DIM:(h + 1) * ML_V_DIM], ones], axis=1)
            if with_output:
                qp = q_ref[b, :, pair * LANES:(pair + 1) * LANES]
                qk = jnp.dot(qp, e["kth"], preferred_element_type=F32) * jnp.exp(e["mu"] - e["cb"])
                qs = qp.astype(F32) * jnp.exp(e["m_prev"] - e["cb"])
                e["lhs"] = jnp.concatenate([qk.astype(BF16), qs.astype(BF16)], axis=1)
        for (b, d, h) in grp:
            e = st[(b, d, h)]
            e["s_prev"] = s_ref[e["r"]]
            if with_output:
                rhs = jnp.concatenate([e["vext"], e["s_prev"].astype(BF16)], axis=0)
                tot = jnp.dot(e["lhs"], rhs, preferred_element_type=F32)
                floor = jnp.exp(-(e["bb"] + e["cb"]))
                hout = tot[:, :ML_V_DIM] / jnp.maximum(jnp.abs(tot[:, ML_V_DIM:]), floor)
                o_ref = hf_ref if d == 0 else hb_ref
                o_ref[b, :, h * ML_V_DIM:(h + 1) * ML_V_DIM] = hout
        for (b, d, h) in grp:
            e = st[(b, d, h)]
            kw = (e["kth"].astype(F32) * jnp.exp(e["u_row"] - e["c_end"])).astype(BF16)
            upd = jnp.dot(kw, e["vext"], preferred_element_type=F32)
            s_ref[e["r"]] = jnp.exp(e["m_prev"] - e["c_end"]) * e["s_prev"] + upd
            m_ref[e["r"]:e["r"] + 1, :] = jnp.broadcast_to(e["b_end"] + e["c_end"], (1, LANES))


def _mlstm(qm, kmt, vm, gc, gr, s0, m0, with_output):
    B, L, _ = qm.shape
    T = ML_CHUNK
    nc = L // T
    nchains = B * 2 * ML_HEADS
    up = lambda j: j
    down = lambda j: nc - 1 - j

    def specs(o):
        return [pl.BlockSpec((B, T, ML_QK), lambda j: (0, o(j), 0)),
                pl.BlockSpec((B, ML_QK, T), lambda j: (0, 0, o(j))),
                pl.BlockSpec((B, T, ML_V), lambda j: (0, o(j), 0)),
                pl.BlockSpec((B, T, LANES), lambda j: (0, o(j), 0)),
                pl.BlockSpec((B, 2 * SUBLANES, T), lambda j: (0, 0, o(j)))]

    s_spec = _const_spec((nchains, LANES, 2 * ML_V_DIM))
    m_spec = _const_spec((nchains, LANES))
    out_shape = [jax.ShapeDtypeStruct((nchains, LANES, 2 * ML_V_DIM), F32),
                 jax.ShapeDtypeStruct((nchains, LANES), F32)]
    out_specs = [s_spec, m_spec]
    if with_output:
        out_shape = [jax.ShapeDtypeStruct((B, L, ML_V), F32)] * 2 + out_shape
        out_specs = [pl.BlockSpec((B, T, ML_V), lambda j: (0, up(j), 0)),
                     pl.BlockSpec((B, T, ML_V), lambda j: (0, down(j), 0))] + out_specs
    return pl.pallas_call(
        functools.partial(_mlstm_kernel, T=T, nb=B, with_output=with_output),
        out_shape=tuple(out_shape),
        grid=(nc,),
        in_specs=specs(up) + specs(down) + [s_spec, m_spec],
        out_specs=tuple(out_specs),
        compiler_params=_cparams(("arbitrary",)),
        name="mlstm_scan" if with_output else "mlstm_context_state",
    )(qm, kmt, vm, gc, gr, qm, kmt, vm, gc, gr, s0, m0)


def _odd_weights(w_in, gate_b, q_g, k_g):
    idx = np.cumsum([ATT_Q, ATT_KV, ATT_KV, ML_QK, ML_QK, ML_V, ML_V]).tolist()
    w = jnp.pad(w_in, ((0, 0), (0, ODD_COLS - w_in.shape[1]))).astype(BF16)
    assert idx[-1] == OG
    nqk = OV - OQ
    head = np.arange(nqk) // HEAD_DIM
    pm = jnp.asarray((head[:, None] == head[None, :]) / HEAD_DIM, F32).astype(BF16)
    gain = jnp.concatenate([jnp.tile(q_g, ATT_HEADS), jnp.tile(k_g, ATT_KV_HEADS)])[None, :]
    gb = jnp.pad(gate_b.reshape(1, -1), ((0, 0), (0, LANES - gate_b.size)))
    return w, pm, gain, gb


def kernel(x, c, ctx, c_ctx, ada_w, ada_b, norm_g, even_w_in, even_conv, even_w_out, odd_w_in, odd_gate_b,
           odd_q_g, odd_k_g, odd_sink, odd_w_out, ffn_w_up, ffn_conv, ffn_w_down):
    B, L, _ = x.shape
    C = ctx.shape[1]
    depth = ada_w.shape[0]
    assert depth == 2 and L % (DFT_N1 * SUBLANES) == 0 and C % ML_CHUNK == 0

    cv = jnp.concatenate([c, c_ctx[None, :], jnp.zeros((SUBLANES - B - 1, D_MODEL), F32)], axis=0)
    mod = _modulation(cv, ada_w, ada_b).reshape(depth, SUBLANES, 6, D_MODEL)
    pad = ((0, 0), (0, SUBLANES - 6), (0, 0))
    mod_lat = [jnp.pad(mod[l, :B], pad) for l in range(depth)]
    mod_ctx = [jnp.pad(jnp.broadcast_to(mod[l, B], (B, 6, D_MODEL)), pad) for l in range(depth)]

    tm = min(512, L)
    tc = _channel_dft_table()
    ffn_w = [(ffn_w_up[l].astype(BF16), ffn_conv[l], ffn_w_down[l].astype(BF16)) for l in range(depth)]

    w_in0 = even_w_in[0].astype(BF16)
    w_out0 = even_w_out[0].astype(BF16)
    ng00, ng01 = norm_g[0, 0][None, :], norm_g[0, 1][None, :]

    def even_layer(xs, mods, tile, seq_dft):
        yc, zr, zi = _even_in(xs, mods, ng00, w_in0, even_conv[0], tc, tile)
        xs = _even_out(xs, yc, seq_dft(zr, zi), mods, w_out0, tile)
        return _ffn(xs, mods, ng01, *ffn_w[0], tile)

    xl = even_layer(x, mod_lat[0], tm, _seq_dft)
    xc = even_layer(ctx, mod_ctx[0], C, _dense_seq_dft)

    w_in1, pm, gain, gb = _odd_weights(odd_w_in[0], odd_gate_b[0], odd_q_g[0], odd_k_g[0])
    ng10, ng11 = norm_g[1, 0][None, :], norm_g[1, 1][None, :]
    cos, sin = _rope_tables(L)
    one, nil = jnp.ones((C, LANES), F32), jnp.zeros((C, LANES), F32)
    qt, kv, vt, qm, kmt, vm, om, gc, gr = _odd_in(xl, mod_lat[1], ng10, w_in1, pm, gain, cos, sin, gb, tm)
    _, kvx, vtx, qmx, kmtx, vmx, _, gcx, grx = _odd_in(xc, mod_ctx[1], ng10, w_in1, pm, gain, one, nil, gb, C)

    att = _attention(qt, kv, kvx, vt, vtx, odd_sink[0])
    nchains = B * 2 * ML_HEADS
    s0 = jnp.zeros((nchains, LANES, 2 * ML_V_DIM), F32)
    m0 = jnp.zeros((nchains, LANES), F32)
    s1, m1 = _mlstm(qmx, kmtx, vmx, gcx, grx, s0, m0, with_output=False)
    hf, hb, _, _ = _mlstm(qm, kmt, vm, gc, gr, s1, m1, with_output=True)
    xl = _odd_out(xl, att, hf, hb, om, mod_lat[1], odd_w_out[0].astype(BF16), tm)
    return _ffn(xl, mod_lat[1], ng11, *ffn_w[1], tm)
```

```python
import functools

import numpy as np
import jax
import jax.numpy as jnp
from jax import lax
from jax.experimental import pallas as pl
from jax.experimental.pallas import tpu as pltpu

F32 = jnp.float32
BF16 = jnp.bfloat16

D_MODEL = 1024
GRID_W = 64
EPS = 1e-6
SC_CH = 512
FT_CH = 512
FT_GROUPS = 4
FT_GROUP_CH = FT_CH // FT_GROUPS
EVEN_IN = 3 * SC_CH + FT_CH
ATT_HEADS = 8
ATT_KV_HEADS = 2
HEAD_DIM = 64
ATT_SCALE = HEAD_DIM ** -0.5
WINDOW = 128
BLOCK = 128
ROPE_THETA = 10000.0
ML_HEADS = 4
ML_QK_DIM = 64
ML_V_DIM = 128
ATT_Q = ATT_HEADS * HEAD_DIM
ATT_KV = ATT_KV_HEADS * HEAD_DIM
ML_QK = ML_HEADS * ML_QK_DIM
ML_V = ML_HEADS * ML_V_DIM
D_FF = 2816

LANES = 128
SUBLANES = 8
VMEM_LIMIT_BYTES = 56 * 1024 * 1024

DFT_N1 = 128
FF_CHUNK = 256
N_FF_CHUNKS = D_FF // FF_CHUNK
ML_CHUNK = 128
NEG_BIG = -1e30
LOG2E = 1.4426950408889634

OQ, OK_, OV, OQM, OKM, OVM, OOM, OG = 0, 512, 640, 768, 1024, 1280, 1792, 2304
ODD_COLS = OG + LANES


def _cparams(sem):
    return pltpu.CompilerParams(dimension_semantics=sem, vmem_limit_bytes=VMEM_LIMIT_BYTES)


def _sigmoid(x):
    return 1.0 / (1.0 + jnp.exp(-x))


def _norm_mod(x, g, shift, scale):
    y = x * lax.rsqrt(jnp.mean(x * x, axis=-1, keepdims=True) + EPS)
    return y * g * (1.0 + scale) + shift


def _halo_rows(x_ref, xn_ref, xp_ref, mod_ref, ng_ref, row0):
    shift = mod_ref[0, row0:row0 + 1, :]
    scale = mod_ref[0, row0 + 1:row0 + 2, :]
    g = ng_ref[...]
    parts = [_norm_mod(r[0], g, shift, scale) for r in (x_ref, xn_ref, xp_ref)]
    return jnp.concatenate(parts, axis=0).astype(BF16)


def _halo_valid(tm, i, nt):
    row = lax.broadcasted_iota(jnp.int32, (tm + 2 * SUBLANES, 1), 0)
    return ((row < tm) | ((row < tm + SUBLANES) & (i < nt - 1)) | ((row >= tm + SUBLANES) & (i > 0)))


def _conv3(v, cw, tm):
    n = v.shape[0]
    vp = pltpu.roll(v, 1, 0)[:tm]
    vn = pltpu.roll(v, n - 1, 0)[:tm]
    return vp * cw[0:1] + v[:tm] * cw[1:2] + vn * cw[2:3]


def _halo_specs(tm, L):
    hb = tm // SUBLANES
    last = L // SUBLANES - 1
    return [
        pl.BlockSpec((1, tm, D_MODEL), lambda b, i: (b, i, 0)),
        pl.BlockSpec((1, SUBLANES, D_MODEL), lambda b, i: (b, jnp.minimum((i + 1) * hb, last), 0)),
        pl.BlockSpec((1, SUBLANES, D_MODEL), lambda b, i: (b, jnp.maximum(i * hb - 1, 0), 0)),
    ]


def _const_spec(shape):
    nd = len(shape)
    return pl.BlockSpec(shape, lambda *_: (0,) * nd)


def _resident_spec(shape):
    nd = len(shape)
    return pl.BlockSpec(shape, lambda *_: (0,) * nd, pipeline_mode=pl.Buffered(1))


def _mod_kernel(cv_ref, w_ref, b_ref, o_ref):
    cv = cv_ref[...]
    a = cv * _sigmoid(cv)
    o_ref[0] = jnp.dot(a, w_ref[0], preferred_element_type=F32,
                       precision=lax.Precision.HIGHEST) + b_ref[0]


def _modulation(cv, ada_w, ada_b):
    depth, _, n = ada_w.shape
    tn = 1536
    return pl.pallas_call(
        _mod_kernel,
        out_shape=jax.ShapeDtypeStruct((depth, SUBLANES, n), F32),
        grid=(depth, n // tn),
        in_specs=[
            pl.BlockSpec((SUBLANES, D_MODEL), lambda l, j: (0, 0)),
            pl.BlockSpec((1, D_MODEL, tn), lambda l, j: (l, 0, j)),
            pl.BlockSpec((1, 1, tn), lambda l, j: (l, 0, j)),
        ],
        out_specs=pl.BlockSpec((1, SUBLANES, tn), lambda l, j: (l, 0, j)),
        compiler_params=_cparams(("arbitrary", "arbitrary")),
        name="modulation",
    )(cv, ada_w, ada_b.reshape(depth, 1, n))


def _even_in_kernel(x_ref, xn_ref, xp_ref, mod_ref, ng_ref, w_ref, cw_ref, tc_ref,
                    yc_ref, zr_ref, zi_ref, *, tm, nt):
    i = pl.program_id(1)
    hh = _halo_rows(x_ref, xn_ref, xp_ref, mod_ref, ng_ref, 0)
    u = jnp.dot(hh, w_ref[...], preferred_element_type=F32)
    v = u[:, SC_CH:2 * SC_CH] * u[:, 2 * SC_CH:3 * SC_CH]
    v = jnp.where(_halo_valid(tm, i, nt), v, 0.0)
    yc = u[:tm, :SC_CH] * _conv3(v, cw_ref[...], tm)
    yc_ref[0] = yc.astype(BF16)
    uf = u[:tm, 3 * SC_CH:].astype(BF16)
    tc = tc_ref[...]
    for g in range(FT_GROUPS):
        sl = slice(g * FT_GROUP_CH, (g + 1) * FT_GROUP_CH)
        ab = jnp.dot(uf[:, sl], tc, preferred_element_type=F32)
        zr_ref[0, :, sl] = ab[:, :FT_GROUP_CH]
        zi_ref[0, :, sl] = ab[:, FT_GROUP_CH:]


def _even_in(x, mod, ng, w_in, cw, tc, tm):
    B, L, _ = x.shape
    nt = L // tm
    out = jax.ShapeDtypeStruct((B, L, FT_CH), BF16)
    zout = jax.ShapeDtypeStruct((B, L, FT_CH), F32)
    ospec = pl.BlockSpec((1, tm, FT_CH), lambda b, i: (b, i, 0))
    return pl.pallas_call(
        functools.partial(_even_in_kernel, tm=tm, nt=nt),
        out_shape=(out, zout, zout),
        grid=(B, nt),
        in_specs=_halo_specs(tm, L) + [
            pl.BlockSpec((1, SUBLANES, D_MODEL), lambda b, i: (b, 0, 0)),
            _const_spec((1, D_MODEL)),
            _const_spec((D_MODEL, EVEN_IN)),
            _const_spec((3, SC_CH)),
            _const_spec((FT_GROUP_CH, 2 * FT_GROUP_CH)),
        ],
        out_specs=(ospec, ospec, ospec),
        compiler_params=_cparams(("parallel", "arbitrary")),
        name="even_in",
    )(x, x, x, mod, ng, w_in, cw, tc)


def _seq_dft_kernel(zr_ref, zi_ref, m_ref, g_ref, y_ref, o_scr, *, n2):
    m1 = m_ref[...]
    for j in range(n2):
        z = jnp.concatenate([zr_ref[0, pl.ds(j, DFT_N1, stride=n2), :],
                             zi_ref[0, pl.ds(j, DFT_N1, stride=n2), :]], axis=0)
        o_scr[2 * DFT_N1 * j:2 * DFT_N1 * (j + 1), :] = jnp.dot(m1, z.astype(BF16), preferred_element_type=F32)
    for k1 in range(DFT_N1):
        o = jnp.concatenate([o_scr[pl.ds(k1, n2, stride=2 * DFT_N1), :],
                             o_scr[pl.ds(DFT_N1 + k1, n2, stride=2 * DFT_N1), :]], axis=0)
        y_ref[0, pl.ds(k1, n2, stride=DFT_N1), :] = jnp.dot(g_ref[k1], o.astype(BF16), preferred_element_type=F32)


def _dft_tables(L):
    n2 = L // DFT_N1
    k = np.arange(DFT_N1)
    a = 2.0 * np.pi * ((k[:, None] * k[None, :]) % DFT_N1) / DFT_N1
    er, ei = np.cos(a) / np.sqrt(DFT_N1), -np.sin(a) / np.sqrt(DFT_N1)
    m1 = np.block([[er, -ei], [ei, er]])
    k1 = np.arange(DFT_N1)[:, None, None]
    k2 = np.arange(n2)[None, :, None]
    nn = np.arange(n2)[None, None, :]
    th = 2.0 * np.pi * ((nn * (k1 + DFT_N1 * k2)) % L) / L
    g = np.concatenate([np.cos(th), np.sin(th)], axis=-1) / np.sqrt(n2)
    return jnp.asarray(m1, F32).astype(BF16), jnp.asarray(g, F32).astype(BF16)


def _channel_dft_table():
    k = np.arange(FT_GROUP_CH)
    a = 2.0 * np.pi * ((k[:, None] * k[None, :]) % FT_GROUP_CH) / FT_GROUP_CH
    t = np.concatenate([np.cos(a), -np.sin(a)], axis=1) / np.sqrt(FT_GROUP_CH)
    return jnp.asarray(t, F32).astype(BF16)


def _seq_dft(zr, zi):
    B, L, C = zr.shape
    n2 = L // DFT_N1
    m1, g = _dft_tables(L)
    zspec = pl.BlockSpec((1, L, LANES), lambda b, j: (b, 0, j))
    return pl.pallas_call(
        functools.partial(_seq_dft_kernel, n2=n2),
        out_shape=jax.ShapeDtypeStruct((B, L, C), F32),
        grid=(B, C // LANES),
        in_specs=[zspec, zspec, _const_spec((2 * DFT_N1, 2 * DFT_N1)), _const_spec((DFT_N1, n2, 2 * n2))],
        out_specs=zspec,
        scratch_shapes=[pltpu.VMEM((2 * DFT_N1 * n2, LANES), F32)],
        compiler_params=_cparams(("parallel", "arbitrary")),
        name="seq_dft",
    )(zr, zi, m1, g)


def _dense_dft_kernel(zr_ref, zi_ref, t_ref, y_ref):
    z = jnp.concatenate([zr_ref[0], zi_ref[0]], axis=0).astype(BF16)
    y_ref[0] = jnp.dot(t_ref[...], z, preferred_element_type=F32)


def _dense_seq_dft(zr, zi):
    B, L, C = zr.shape
    k = np.arange(L)
    a = 2.0 * np.pi * ((k[:, None] * k[None, :]) % L) / L
    t = jnp.asarray(np.concatenate([np.cos(a), np.sin(a)], axis=1) / np.sqrt(L), F32).astype(BF16)
    zspec = pl.BlockSpec((1, L, C), lambda b: (b, 0, 0))
    return pl.pallas_call(
        _dense_dft_kernel,
        out_shape=jax.ShapeDtypeStruct((B, L, C), F32),
        grid=(B,),
        in_specs=[zspec, zspec, _const_spec((L, 2 * L))],
        out_specs=zspec,
        compiler_params=_cparams(("arbitrary",)),
        name="dense_seq_dft",
    )(zr, zi, t)


HALO = 16


def _wide_halo_specs(tm, L, width):
    hb = tm // HALO
    last = L // HALO - 1
    return [
        pl.BlockSpec((1, tm, width), lambda b, i: (b, i, 0)),
        pl.BlockSpec((1, HALO, width), lambda b, i: (b, jnp.minimum((i + 1) * hb, last), 0)),
        pl.BlockSpec((1, HALO, width), lambda b, i: (b, jnp.maximum(i * hb - 1, 0), 0)),
    ]


def _circ(t_ref, n_ref, p_ref):
    return jnp.concatenate([t_ref[0], n_ref[0], p_ref[0]], axis=0)


def _mix_ffn_kernel(*refs, tm, nt, odd):
    n_in = 15 if odd else 9
    x3, rest = refs[:3], refs[3:n_in]
    mod_ref, ng_ref, wo_ref, wu_ref, cw_ref, wd_ref, o_ref, hh_scr, act_scr = refs[n_in:]
    i = pl.program_id(1)
    if odd:
        att, hf, hb, om = (_circ(*rest[k:k + 3]) for k in range(0, 12, 3))
        lhs = jnp.concatenate([att, ((hf + hb) * _sigmoid(om.astype(F32))).astype(BF16)], axis=-1)
    else:
        yc, yf = _circ(*rest[0:3]), _circ(*rest[3:6])
        lhs = jnp.concatenate([yc, yf.astype(BF16)], axis=-1)
    x1 = _circ(*x3) + mod_ref[0, 2:3, :] * jnp.dot(lhs, wo_ref[...], preferred_element_type=F32)
    hh_scr[...] = _norm_mod(x1, ng_ref[...], mod_ref[0, 3:4, :], mod_ref[0, 4:5, :]).astype(BF16)
    row = lax.broadcasted_iota(jnp.int32, (tm + 2 * HALO, 1), 0)
    valid = (row < tm) | ((row < tm + HALO) & (i < nt - 1)) | ((row >= tm + HALO) & (i > 0))
    for c in range(N_FF_CHUNKS):
        lo = c * FF_CHUNK
        g = jnp.dot(hh_scr[...], wu_ref[:, lo:lo + FF_CHUNK], preferred_element_type=F32)
        g = jnp.where(valid, g, 0.0)
        cv = _conv3(g, cw_ref[:, lo:lo + FF_CHUNK], tm)
        val = jnp.dot(hh_scr[:tm, :], wu_ref[:, D_FF + lo:D_FF + lo + FF_CHUNK], preferred_element_type=F32)
        act_scr[:, lo:lo + FF_CHUNK] = (cv * _sigmoid(cv) * val).astype(BF16)
    y = jnp.dot(act_scr[...], wd_ref[...], preferred_element_type=F32)
    o_ref[0] = x1[:tm] + mod_ref[0, 5:6, :] * y


def _mix_ffn(x, mixed, mod, ng, w_out, w_up, cw, w_down, tm, odd):
    B, L, _ = x.shape
    nt = L // tm
    specs = _wide_halo_specs(tm, L, D_MODEL)
    args = [x, x, x]
    for a in mixed:
        specs += _wide_halo_specs(tm, L, a.shape[-1])
        args += [a, a, a]
    return pl.pallas_call(
        functools.partial(_mix_ffn_kernel, tm=tm, nt=nt, odd=odd),
        out_shape=jax.ShapeDtypeStruct(x.shape, F32),
        grid=(B, nt),
        in_specs=specs + [
            pl.BlockSpec((1, SUBLANES, D_MODEL), lambda b, i: (b, 0, 0)),
            _const_spec((1, D_MODEL)),
            _resident_spec((D_MODEL, D_MODEL)),
            _resident_spec((D_MODEL, 2 * D_FF)),
            _const_spec((3, D_FF)),
            _resident_spec((D_FF, D_MODEL)),
        ],
        out_specs=pl.BlockSpec((1, tm, D_MODEL), lambda b, i: (b, i, 0)),
        scratch_shapes=[pltpu.VMEM((tm + 2 * HALO, D_MODEL), BF16),
                        pltpu.VMEM((tm, D_FF), BF16)],
        compiler_params=_cparams(("parallel", "arbitrary")),
        name="odd_mix_ffn" if odd else "even_mix_ffn",
    )(*args, mod, ng, w_out, w_up, cw, w_down)


def _split3(x):
    parts = []
    r = x
    for _ in range(3):
        p = r.astype(BF16)
        parts.append(p)
        r = r - p.astype(F32)
    return parts


def _split_dot(x, p):
    hi = x.astype(BF16)
    lo = (x - hi.astype(F32)).astype(BF16)
    return (jnp.dot(hi, p, preferred_element_type=F32) + jnp.dot(lo, p, preferred_element_type=F32))


def _odd_in_kernel(x_ref, mod_ref, ng_ref, w_ref, pm_ref, gain_ref, cos_ref, sin_ref, gb_ref, tl_ref, tu_ref,
                   qt_ref, kv_ref, vt_ref, qm_ref, kmt_ref, vm_ref, om_ref, gc_ref, gr_ref):
    shift = mod_ref[0, 0:1, :]
    scale = mod_ref[0, 1:2, :]
    h = _norm_mod(x_ref[0], ng_ref[...], shift, scale).astype(BF16)
    u = jnp.dot(h, w_ref[...], preferred_element_type=F32)

    uqk = u[:, OQ:OV]
    ms = _split_dot(uqk * uqk, pm_ref[...])
    rn = uqk * lax.rsqrt(ms + EPS) * gain_ref[...]
    lane = lax.broadcasted_iota(jnp.int32, (1, LANES), 1)
    first = (lane % 32) < 16
    cos = cos_ref[...]
    sin = sin_ref[...]
    roped = []
    for t in range((OV - OQ) // LANES):
        xt = rn[:, t * LANES:(t + 1) * LANES]
        sw = jnp.where(first, pltpu.roll(xt, LANES - 16, 1), pltpu.roll(xt, 16, 1))
        roped.append(xt * cos + sw * sin)
    for t in range(ATT_Q // LANES):
        qt_ref[0, t * LANES:(t + 1) * LANES, :] = (roped[t] * (ATT_SCALE * LOG2E)).T.astype(BF16)
    k = roped[ATT_Q // LANES]
    v = u[:, OV:OQM]
    half = LANES // 2
    kv_ref[0, :, 0:LANES] = k.astype(BF16)
    kv_ref[0, :, LANES:2 * LANES] = pltpu.roll(k, half, 1).astype(BF16)
    vt_ref[0] = v.T.astype(BF16)

    qm_ref[0] = u[:, OQM:OKM].astype(BF16)
    for p in range(ML_QK // LANES):
        km = u[:, OKM + p * LANES:OKM + (p + 1) * LANES] * (ML_QK_DIM ** -0.5)
        kmt_ref[0, p * LANES:(p + 1) * LANES, :] = km.T.astype(BF16)
    vm_ref[0] = u[:, OVM:OOM].astype(BF16)
    om_ref[0] = u[:, OOM:OG].astype(BF16)

    gt = (u[:, OG:ODD_COLS] + gb_ref[...]).T[:2 * SUBLANES, :]
    row = lax.broadcasted_iota(jnp.int32, (2 * SUBLANES, 1), 0)
    logsig = jnp.minimum(gt, 0.0) - jnp.log(1.0 + jnp.exp(-jnp.abs(gt)))
    parts = _split3(logsig)
    tl, tu = tl_ref[...], tu_ref[...]
    cum_f = sum(jnp.dot(p, tu, preferred_element_type=F32) for p in parts)
    cum_b = sum(jnp.dot(p, tl, preferred_element_type=F32) for p in parts)
    sel = row % 8
    gr = jnp.where(sel < 4, gt, jnp.where(row < SUBLANES, cum_f, cum_b))
    gr_ref[0] = gr
    gc_ref[0] = jnp.concatenate([gr, jnp.zeros((LANES - 2 * SUBLANES, gr.shape[1]), F32)], axis=0).T


def _chunk_tri(tm):
    i = np.arange(tm)
    same = (i[:, None] // ML_CHUNK) == (i[None, :] // ML_CHUNK)
    tl = same & (i[None, :] <= i[:, None])
    tu = same & (i[None, :] >= i[:, None])
    return jnp.asarray(tl, F32).astype(BF16), jnp.asarray(tu, F32).astype(BF16)


def _odd_in(x, mod, ng, w, pm, gain, cos, sin, gb, tm):
    B, L, _ = x.shape

    def rows(c, dt=BF16):
        return jax.ShapeDtypeStruct((B, L, c), dt), pl.BlockSpec((1, tm, c), lambda b, i: (b, i, 0))

    def cols(c, dt=BF16):
        return jax.ShapeDtypeStruct((B, c, L), dt), pl.BlockSpec((1, c, tm), lambda b, i: (b, 0, i))

    outs = [cols(ATT_Q), rows(2 * LANES), cols(LANES), rows(ML_QK), cols(ML_QK), rows(ML_V), rows(ML_V),
            rows(LANES, F32), cols(2 * SUBLANES, F32)]
    nqk = OV - OQ
    tl, tu = _chunk_tri(tm)
    return pl.pallas_call(
        _odd_in_kernel,
        out_shape=tuple(o[0] for o in outs),
        grid=(B, L // tm),
        in_specs=[
            pl.BlockSpec((1, tm, D_MODEL), lambda b, i: (b, i, 0)),
            pl.BlockSpec((1, SUBLANES, D_MODEL), lambda b, i: (b, 0, 0)),
            _const_spec((1, D_MODEL)),
            _const_spec((D_MODEL, ODD_COLS)),
            _const_spec((nqk, nqk)),
            _const_spec((1, nqk)),
            pl.BlockSpec((tm, LANES), lambda b, i: (i, 0)),
            pl.BlockSpec((tm, LANES), lambda b, i: (i, 0)),
            _const_spec((1, LANES)),
            _const_spec((tm, tm)),
            _const_spec((tm, tm)),
        ],
        out_specs=tuple(o[1] for o in outs),
        compiler_params=_cparams(("parallel", "arbitrary")),
        name="odd_in",
    )(x, mod, ng, w, pm, gain, cos, sin, gb, tl, tu)


def _rope_tables(L):
    rows = L // GRID_W
    row = jnp.repeat(jnp.arange(rows, dtype=F32), GRID_W)
    col = jnp.tile(jnp.arange(GRID_W, dtype=F32), rows)
    axis_dim = HEAD_DIM // 2
    inv_freq = ROPE_THETA ** (-jnp.arange(0, axis_dim, 2, dtype=F32) / axis_dim)
    ang = jnp.stack([row, col])[:, :, None] * inv_freq
    c, s = jnp.cos(ang), jnp.sin(ang)
    cos = jnp.concatenate([c[0], c[0], c[1], c[1]], axis=-1)
    sin = jnp.concatenate([-s[0], s[0], -s[1], s[1]], axis=-1)
    return jnp.tile(cos, (1, 2)), jnp.tile(sin, (1, 2))


def _att_kernel(sink_ref, qt_ref, kvp_ref, kvc_ref, kvn_ref, kvx_ref, vtp_ref, vtc_ref, vtn_ref, vtx_ref,
                o_ref, *, nb):
    n = pl.program_id(1)
    half = LANES // 2
    lo = lax.broadcasted_iota(jnp.int32, (1, LANES), 1) < half
    zero = jnp.zeros((), BF16)
    cx = kvx_ref.shape[1]
    kblocks = [kvp_ref[0], kvc_ref[0], kvn_ref[0]] + [kvx_ref[0, i:i + BLOCK] for i in range(0, cx, BLOCK)]
    vblocks = [vtp_ref[0], vtc_ref[0], vtn_ref[0]] + [vtx_ref[0, :, i:i + BLOCK] for i in range(0, cx, BLOCK)]
    ones = jnp.ones((half, BLOCK), BF16)

    kj = lax.broadcasted_iota(jnp.int32, (BLOCK, BLOCK), 0)
    qi = lax.broadcasted_iota(jnp.int32, (BLOCK, BLOCK), 1)

    def twice(x):
        return jnp.concatenate([x, x], axis=1)

    bias = [twice(jnp.where((kj >= qi) & (n > 0), 0.0, NEG_BIG)), None,
            twice(jnp.where((kj <= qi) & (n < nb - 1), 0.0, NEG_BIG))] + [None] * (cx // BLOCK)
    left = lax.broadcasted_iota(jnp.int32, (1, 2 * BLOCK), 1) < BLOCK

    group = ATT_HEADS // ATT_KV_HEADS
    for kvh in range(ATT_KV_HEADS):
        t0 = kvh * group // 2
        qt2 = jnp.concatenate([qt_ref[0, t0 * LANES:(t0 + 1) * LANES, :],
                               qt_ref[0, (t0 + 1) * LANES:(t0 + 2) * LANES, :]], axis=1)
        outs = []
        for par in range(2):
            sk = jnp.where(left, sink_ref[2 * t0 + par], sink_ref[2 * t0 + 2 + par]) * LOG2E
            m = sk
            acc = jnp.zeros((LANES, 2 * BLOCK), F32)
            for blk in range(len(kblocks)):
                k, ks = kblocks[blk][:, :LANES], kblocks[blk][:, LANES:]
                kh = ((jnp.where(lo, k, zero), jnp.where(lo, zero, ks)),
                      (jnp.where(lo, ks, zero), jnp.where(lo, zero, k)))[kvh][par]
                s = jnp.dot(kh, qt2, preferred_element_type=F32)
                if bias[blk] is not None:
                    s = s + bias[blk]
                m_new = jnp.maximum(m, jnp.max(s, axis=0, keepdims=True))
                p = jnp.exp2(s - m_new).astype(BF16)
                vh = jnp.concatenate([vblocks[blk][kvh * half:(kvh + 1) * half], ones], axis=0)
                acc = acc * jnp.exp2(m - m_new) + jnp.dot(vh, p, preferred_element_type=F32)
                m = m_new
            l = acc[half:half + 1, :] + jnp.exp2(sk - m)
            outs.append(acc[:half] * (1.0 / l))
        ot = jnp.concatenate(outs, axis=0)
        o_ref[0, :, t0 * LANES:(t0 + 1) * LANES] = ot[:, :BLOCK].T.astype(BF16)
        o_ref[0, :, (t0 + 1) * LANES:(t0 + 2) * LANES] = ot[:, BLOCK:].T.astype(BF16)


def _attention(qt, kv, kvx, vt, vtx, sink):
    B, _, L = qt.shape
    nb = L // BLOCK
    cx = kvx.shape[1]
    kw = kv.shape[2]
    vw = vt.shape[1]
    prev = lambda n: jnp.maximum(n - 1, 0)
    nxt = lambda n: jnp.minimum(n + 1, nb - 1)
    return pl.pallas_call(
        functools.partial(_att_kernel, nb=nb),
        out_shape=jax.ShapeDtypeStruct((B, L, ATT_Q), BF16),
        grid=(B, nb),
        in_specs=[
            pl.BlockSpec(memory_space=pltpu.SMEM),
            pl.BlockSpec((1, ATT_Q, BLOCK), lambda b, n: (b, 0, n)),
            pl.BlockSpec((1, BLOCK, kw), lambda b, n: (b, prev(n), 0)),
            pl.BlockSpec((1, BLOCK, kw), lambda b, n: (b, n, 0)),
            pl.BlockSpec((1, BLOCK, kw), lambda b, n: (b, nxt(n), 0)),
            pl.BlockSpec((1, cx, kw), lambda b, n: (b, 0, 0)),
            pl.BlockSpec((1, vw, BLOCK), lambda b, n: (b, 0, prev(n))),
            pl.BlockSpec((1, vw, BLOCK), lambda b, n: (b, 0, n)),
            pl.BlockSpec((1, vw, BLOCK), lambda b, n: (b, 0, nxt(n))),
            pl.BlockSpec((1, vw, cx), lambda b, n: (b, 0, 0)),
        ],
        out_specs=pl.BlockSpec((1, BLOCK, ATT_Q), lambda b, n: (b, n, 0)),
        compiler_params=_cparams(("parallel", "arbitrary")),
        name="window_attention",
    )(sink, qt, kv, kv, kv, kvx, vt, vt, vt, vtx)


ML_GROUP = 8


def _mlstm_kernel(qf_ref, ktf_ref, vf_ref, gcf_ref, grf_ref, qb_ref, ktb_ref, vb_ref, gcb_ref, grb_ref,
                  s0_ref, m0_ref, *rest, T, nb, with_output):
    if with_output:
        hf_ref, hb_ref, s_ref, m_ref = rest
    else:
        s_ref, m_ref = rest
    j = pl.program_id(0)

    @pl.when(j == 0)
    def _():
        s_ref[...] = s0_ref[...]
        m_ref[...] = m0_ref[...]

    ti = lax.broadcasted_iota(jnp.int32, (T, T), 0)
    si = lax.broadcasted_iota(jnp.int32, (T, T), 1)
    masks = (si <= ti, si >= ti)
    top = lax.broadcasted_iota(jnp.int32, (LANES, 1), 0) < (LANES // 2)
    zero = jnp.zeros((), BF16)
    ones = jnp.ones((T, ML_V_DIM), BF16)
    fwd = (qf_ref, ktf_ref, vf_ref, gcf_ref, grf_ref)
    bwd = (qb_ref, ktb_ref, vb_ref, gcb_ref, grb_ref)
    chains = [(b, d, h) for b in range(nb) for d in range(2) for h in range(ML_HEADS)]

    for g0 in range(0, len(chains), ML_GROUP):
        grp = chains[g0:g0 + ML_GROUP]
        st = {}
        for (b, d, h) in grp:
            q_ref, kt_ref, v_ref, gc_ref, gr_ref = fwd if d == 0 else bwd
            r = (b * 2 + d) * ML_HEADS + h
            li, lf = 8 * d + h, 8 * d + 4 + h
            gr = gr_ref[b]
            u_row = gr[li:li + 1, :] - gr[lf:lf + 1, :]
            m_prev = m_ref[r:r + 1, 0:1]
            e = dict(r=r, u_row=u_row, m_prev=m_prev,
                     b_end=gr[lf:lf + 1, (T - 1 if d == 0 else 0):(T if d == 0 else 1)],
                     c_end=jnp.maximum(jnp.max(u_row, axis=1, keepdims=True), m_prev))
            if with_output:
                e["mu"] = jnp.where(masks[d], u_row, NEG_BIG)
                c_col = jnp.maximum(jnp.max(e["mu"], axis=1, keepdims=True), m_prev)
                e["cb"] = jnp.broadcast_to(c_col, (T, LANES))
                e["bb"] = jnp.broadcast_to(gc_ref[b, :, lf:lf + 1], (T, LANES))
            st[(b, d, h)] = e
        for (b, d, h) in grp:
            q_ref, kt_ref, v_ref, gc_ref, gr_ref = fwd if d == 0 else bwd
            e = st[(b, d, h)]
            pair = h // 2
            ktp = kt_ref[b, pair * LANES:(pair + 1) * LANES, :]
            e["kth"] = jnp.where(top, ktp, zero) if h % 2 == 0 else jnp.where(top, zero, ktp)
            e["vext"] = jnp.concatenate([v_ref[b, :, h * ML_V_DIM:(h + 1) * ML_V_DIM], ones], axis=1)
            if with_output:
                qp = q_ref[b, :, pair * LANES:(pair + 1) * LANES]
                qk = jnp.dot(qp, e["kth"], preferred_element_type=F32) * jnp.exp(e["mu"] - e["cb"])
                qs = qp.astype(F32) * jnp.exp(e["m_prev"] - e["cb"])
                e["lhs"] = jnp.concatenate([qk.astype(BF16), qs.astype(BF16)], axis=1)
        for (b, d, h) in grp:
            e = st[(b, d, h)]
            e["s_prev"] = s_ref[e["r"]]
            if with_output:
                rhs = jnp.concatenate([e["vext"], e["s_prev"].astype(BF16)], axis=0)
                tot = jnp.dot(e["lhs"], rhs, preferred_element_type=F32)
                floor = jnp.exp(-(e["bb"] + e["cb"]))
                hout = tot[:, :ML_V_DIM] / jnp.maximum(jnp.abs(tot[:, ML_V_DIM:]), floor)
                o_ref = hf_ref if d == 0 else hb_ref
                o_ref[b, :, h * ML_V_DIM:(h + 1) * ML_V_DIM] = hout
        for (b, d, h) in grp:
            e = st[(b, d, h)]
            kw = (e["kth"].astype(F32) * jnp.exp(e["u_row"] - e["c_end"])).astype(BF16)
            upd = jnp.dot(kw, e["vext"], preferred_element_type=F32)
            s_ref[e["r"]] = jnp.exp(e["m_prev"] - e["c_end"]) * e["s_prev"] + upd
            m_ref[e["r"]:e["r"] + 1, :] = jnp.broadcast_to(e["b_end"] + e["c_end"], (1, LANES))


def _mlstm(qm, kmt, vm, gc, gr, s0, m0, with_output):
    B, L, _ = qm.shape
    T = ML_CHUNK
    nc = L // T
    nchains = B * 2 * ML_HEADS
    up = lambda j: j
    down = lambda j: nc - 1 - j

    def specs(o):
        return [pl.BlockSpec((B, T, ML_QK), lambda j: (0, o(j), 0)),
                pl.BlockSpec((B, ML_QK, T), lambda j: (0, 0, o(j))),
                pl.BlockSpec((B, T, ML_V), lambda j: (0, o(j), 0)),
                pl.BlockSpec((B, T, LANES), lambda j: (0, o(j), 0)),
                pl.BlockSpec((B, 2 * SUBLANES, T), lambda j: (0, 0, o(j)))]

    s_spec = _const_spec((nchains, LANES, 2 * ML_V_DIM))
    m_spec = _const_spec((nchains, LANES))
    out_shape = [jax.ShapeDtypeStruct((nchains, LANES, 2 * ML_V_DIM), F32),
                 jax.ShapeDtypeStruct((nchains, LANES), F32)]
    out_specs = [s_spec, m_spec]
    if with_output:
        out_shape = [jax.ShapeDtypeStruct((B, L, ML_V), F32)] * 2 + out_shape
        out_specs = [pl.BlockSpec((B, T, ML_V), lambda j: (0, up(j), 0)),
                     pl.BlockSpec((B, T, ML_V), lambda j: (0, down(j), 0))] + out_specs
    return pl.pallas_call(
        functools.partial(_mlstm_kernel, T=T, nb=B, with_output=with_output),
        out_shape=tuple(out_shape),
        grid=(nc,),
        in_specs=specs(up) + specs(down) + [s_spec, m_spec],
        out_specs=tuple(out_specs),
        compiler_params=_cparams(("arbitrary",)),
        name="mlstm_scan" if with_output else "mlstm_context_state",
    )(qm, kmt, vm, gc, gr, qm, kmt, vm, gc, gr, s0, m0)


def _odd_weights(w_in, gate_b, q_g, k_g):
    idx = np.cumsum([ATT_Q, ATT_KV, ATT_KV, ML_QK, ML_QK, ML_V, ML_V]).tolist()
    w = jnp.pad(w_in, ((0, 0), (0, ODD_COLS - w_in.shape[1]))).astype(BF16)
    assert idx[-1] == OG
    nqk = OV - OQ
    head = np.arange(nqk) // HEAD_DIM
    pm = jnp.asarray((head[:, None] == head[None, :]) / HEAD_DIM, F32).astype(BF16)
    gain = jnp.concatenate([jnp.tile(q_g, ATT_HEADS), jnp.tile(k_g, ATT_KV_HEADS)])[None, :]
    gb = jnp.pad(gate_b.reshape(1, -1), ((0, 0), (0, LANES - gate_b.size)))
    return w, pm, gain, gb


def kernel(x, c, ctx, c_ctx, ada_w, ada_b, norm_g, even_w_in, even_conv, even_w_out, odd_w_in, odd_gate_b,
           odd_q_g, odd_k_g, odd_sink, odd_w_out, ffn_w_up, ffn_conv, ffn_w_down):
    B, L, _ = x.shape
    C = ctx.shape[1]
    depth = ada_w.shape[0]
    assert depth == 2 and L % (DFT_N1 * SUBLANES) == 0 and C % ML_CHUNK == 0

    cv = jnp.concatenate([c, c_ctx[None, :], jnp.zeros((SUBLANES - B - 1, D_MODEL), F32)], axis=0)
    mod = _modulation(cv, ada_w, ada_b).reshape(depth, SUBLANES, 6, D_MODEL)
    pad = ((0, 0), (0, SUBLANES - 6), (0, 0))
    mod_lat = [jnp.pad(mod[l, :B], pad) for l in range(depth)]
    mod_ctx = [jnp.pad(jnp.broadcast_to(mod[l, B], (B, 6, D_MODEL)), pad) for l in range(depth)]

    tm = min(512, L)
    tc = _channel_dft_table()
    ffn_w = [(ffn_w_up[l].astype(BF16), ffn_conv[l], ffn_w_down[l].astype(BF16)) for l in range(depth)]

    w_in0 = even_w_in[0].astype(BF16)
    w_out0 = even_w_out[0].astype(BF16)
    ng00, ng01 = norm_g[0, 0][None, :], norm_g[0, 1][None, :]

    def even_layer(xs, mods, tile, seq_dft):
        yc, zr, zi = _even_in(xs, mods, ng00, w_in0, even_conv[0], tc, tile)
        return _mix_ffn(xs, (yc, seq_dft(zr, zi)), mods, ng01, w_out0, *ffn_w[0], tile, odd=False)

    xl = even_layer(x, mod_lat[0], tm, _seq_dft)
    xc = even_layer(ctx, mod_ctx[0], C, _dense_seq_dft)

    w_in1, pm, gain, gb = _odd_weights(odd_w_in[0], odd_gate_b[0], odd_q_g[0], odd_k_g[0])
    ng10, ng11 = norm_g[1, 0][None, :], norm_g[1, 1][None, :]
    cos, sin = _rope_tables(L)
    one, nil = jnp.ones((C, LANES), F32), jnp.zeros((C, LANES), F32)
    qt, kv, vt, qm, kmt, vm, om, gc, gr = _odd_in(xl, mod_lat[1], ng10, w_in1, pm, gain, cos, sin, gb, tm)
    _, kvx, vtx, qmx, kmtx, vmx, _, gcx, grx = _odd_in(xc, mod_ctx[1], ng10, w_in1, pm, gain, one, nil, gb, C)

    att = _attention(qt, kv, kvx, vt, vtx, odd_sink[0])
    nchains = B * 2 * ML_HEADS
    s0 = jnp.zeros((nchains, LANES, 2 * ML_V_DIM), F32)
    m0 = jnp.zeros((nchains, LANES), F32)
    s1, m1 = _mlstm(qmx, kmtx, vmx, gcx, grx, s0, m0, with_output=False)
    hf, hb, _, _ = _mlstm(qm, kmt, vm, gc, gr, s1, m1, with_output=True)
    return _mix_ffn(xl, (att, hf, hb, om), mod_lat[1], ng11, odd_w_out[0].astype(BF16), *ffn_w[1], tm, odd=True)
```

```python
import functools

import numpy as np
import jax
import jax.numpy as jnp
from jax import lax
from jax.experimental import pallas as pl
from jax.experimental.pallas import tpu as pltpu

F32 = jnp.float32
BF16 = jnp.bfloat16

D_MODEL = 1024
GRID_W = 64
EPS = 1e-6
SC_CH = 512
FT_CH = 512
FT_GROUPS = 4
FT_GROUP_CH = FT_CH // FT_GROUPS
EVEN_IN = 3 * SC_CH + FT_CH
ATT_HEADS = 8
ATT_KV_HEADS = 2
HEAD_DIM = 64
ATT_SCALE = HEAD_DIM ** -0.5
WINDOW = 128
BLOCK = 128
ROPE_THETA = 10000.0
ML_HEADS = 4
ML_QK_DIM = 64
ML_V_DIM = 128
ATT_Q = ATT_HEADS * HEAD_DIM
ATT_KV = ATT_KV_HEADS * HEAD_DIM
ML_QK = ML_HEADS * ML_QK_DIM
ML_V = ML_HEADS * ML_V_DIM
D_FF = 2816

LANES = 128
SUBLANES = 8
VMEM_LIMIT_BYTES = 56 * 1024 * 1024

DFT_N1 = 128
FF_CHUNK = 256
N_FF_CHUNKS = D_FF // FF_CHUNK
ML_CHUNK = 128
ATT_QB = 2
NEG_BIG = -1e30
LOG2E = 1.4426950408889634

OQ, OK_, OV, OQM, OKM, OVM, OOM, OG = 0, 512, 640, 768, 1024, 1280, 1792, 2304
ODD_COLS = OG + LANES


def _cparams(sem):
    return pltpu.CompilerParams(dimension_semantics=sem, vmem_limit_bytes=VMEM_LIMIT_BYTES)


def _sigmoid(x):
    return 1.0 / (1.0 + jnp.exp(-x))


def _norm_mod(x, g, shift, scale):
    y = x * lax.rsqrt(jnp.mean(x * x, axis=-1, keepdims=True) + EPS)
    return y * g * (1.0 + scale) + shift


def _halo_rows(x_ref, xn_ref, xp_ref, mod_ref, ng_ref, row0):
    shift = mod_ref[0, row0:row0 + 1, :]
    scale = mod_ref[0, row0 + 1:row0 + 2, :]
    g = ng_ref[...]
    parts = [_norm_mod(r[0], g, shift, scale) for r in (x_ref, xn_ref, xp_ref)]
    return jnp.concatenate(parts, axis=0).astype(BF16)


def _halo_valid(tm, i, nt):
    row = lax.broadcasted_iota(jnp.int32, (tm + 2 * SUBLANES, 1), 0)
    return ((row < tm) | ((row < tm + SUBLANES) & (i < nt - 1)) | ((row >= tm + SUBLANES) & (i > 0)))


def _conv3(v, cw, tm):
    n = v.shape[0]
    vp = pltpu.roll(v, 1, 0)[:tm]
    vn = pltpu.roll(v, n - 1, 0)[:tm]
    return vp * cw[0:1] + v[:tm] * cw[1:2] + vn * cw[2:3]


def _halo_specs(tm, L):
    hb = tm // SUBLANES
    last = L // SUBLANES - 1
    return [
        pl.BlockSpec((1, tm, D_MODEL), lambda b, i: (b, i, 0)),
        pl.BlockSpec((1, SUBLANES, D_MODEL), lambda b, i: (b, jnp.minimum((i + 1) * hb, last), 0)),
        pl.BlockSpec((1, SUBLANES, D_MODEL), lambda b, i: (b, jnp.maximum(i * hb - 1, 0), 0)),
    ]


def _const_spec(shape):
    nd = len(shape)
    return pl.BlockSpec(shape, lambda *_: (0,) * nd)


def _resident_spec(shape):
    nd = len(shape)
    return pl.BlockSpec(shape, lambda *_: (0,) * nd, pipeline_mode=pl.Buffered(1))


def _split_dot(x, p):
    hi = x.astype(BF16)
    lo = (x - hi.astype(F32)).astype(BF16)
    return (jnp.dot(hi, p, preferred_element_type=F32) + jnp.dot(lo, p, preferred_element_type=F32))


def _mod_kernel(cv_ref, w_ref, b_ref, o_ref):
    cv = cv_ref[...]
    o_ref[0] = _split_dot(cv * _sigmoid(cv), w_ref[0].astype(BF16)) + b_ref[0]


def _modulation(cv, ada_w, ada_b):
    depth, _, n = ada_w.shape
    tn = 1536
    return pl.pallas_call(
        _mod_kernel,
        out_shape=jax.ShapeDtypeStruct((depth, SUBLANES, n), F32),
        grid=(depth, n // tn),
        in_specs=[
            pl.BlockSpec((SUBLANES, D_MODEL), lambda l, j: (0, 0)),
            pl.BlockSpec((1, D_MODEL, tn), lambda l, j: (l, 0, j)),
            pl.BlockSpec((1, 1, tn), lambda l, j: (l, 0, j)),
        ],
        out_specs=pl.BlockSpec((1, SUBLANES, tn), lambda l, j: (l, 0, j)),
        compiler_params=_cparams(("arbitrary", "arbitrary")),
        name="modulation",
    )(cv, ada_w, ada_b.reshape(depth, 1, n))


CAST_STEPS = 8


def _cast_kernel(*refs):
    n = len(refs) // 2
    for src, dst in zip(refs[:n], refs[n:]):
        w = src[...].astype(BF16)
        pad = dst.shape[-1] - src.shape[-1]
        if pad:
            w = jnp.concatenate([w, jnp.zeros(w.shape[:-1] + (pad,), BF16)], axis=-1)
        dst[...] = w


def _cast_weights(ws, widths):
    in_specs, out_specs, out_shape = [], [], []
    for w, wd in zip(ws, widths):
        s, r, c = w.shape
        in_specs.append(pl.BlockSpec((s, r // CAST_STEPS, c), lambda i: (0, i, 0)))
        out_specs.append(pl.BlockSpec((s, r // CAST_STEPS, wd), lambda i: (0, i, 0)))
        out_shape.append(jax.ShapeDtypeStruct((s, r, wd), BF16))
    return pl.pallas_call(
        _cast_kernel,
        out_shape=tuple(out_shape),
        grid=(CAST_STEPS,),
        in_specs=in_specs,
        out_specs=tuple(out_specs),
        compiler_params=_cparams(("arbitrary",)),
        name="cast_weights",
    )(*ws)


def _even_in_kernel(x_ref, xn_ref, xp_ref, mod_ref, ng_ref, w_ref, cw_ref, tc_ref,
                    yc_ref, zr_ref, zi_ref, *, tm, nt, n2):
    i = pl.program_id(1)
    hh = _halo_rows(x_ref, xn_ref, xp_ref, mod_ref, ng_ref, 0)
    u = jnp.dot(hh, w_ref[...], preferred_element_type=F32)
    v = u[:, SC_CH:2 * SC_CH] * u[:, 2 * SC_CH:3 * SC_CH]
    v = jnp.where(_halo_valid(tm, i, nt), v, 0.0)
    yc = u[:tm, :SC_CH] * _conv3(v, cw_ref[...], tm)
    yc_ref[0] = yc.astype(BF16)
    uf = u[:tm, 3 * SC_CH:].astype(BF16)
    tc = tc_ref[...]
    for g in range(FT_GROUPS):
        sl = slice(g * FT_GROUP_CH, (g + 1) * FT_GROUP_CH)
        ab = jnp.dot(uf[:, sl], tc, preferred_element_type=F32)
        if n2 is None:
            zr_ref[g, 0] = ab[:, :FT_GROUP_CH]
            zi_ref[g, 0] = ab[:, FT_GROUP_CH:]
        else:
            for a in range(tm // n2):
                dst = pl.ds(a, n2, stride=tm // n2)
                zr_ref[g, 0, dst, :] = ab[n2 * a:n2 * (a + 1), :FT_GROUP_CH]
                zi_ref[g, 0, dst, :] = ab[n2 * a:n2 * (a + 1), FT_GROUP_CH:]


def _even_in(x, mod, ng, w_in, cw, tc, tm, n2):
    B, L, _ = x.shape
    nt = L // tm
    out = jax.ShapeDtypeStruct((B, L, FT_CH), BF16)
    zout = jax.ShapeDtypeStruct((FT_GROUPS, B, L, FT_GROUP_CH), F32)
    ospec = pl.BlockSpec((1, tm, FT_CH), lambda b, i: (b, i, 0))
    zspec = pl.BlockSpec((FT_GROUPS, 1, tm, FT_GROUP_CH), lambda b, i: (0, b, i, 0))
    return pl.pallas_call(
        functools.partial(_even_in_kernel, tm=tm, nt=nt, n2=n2),
        out_shape=(out, zout, zout),
        grid=(B, nt),
        in_specs=_halo_specs(tm, L) + [
            pl.BlockSpec((1, SUBLANES, D_MODEL), lambda b, i: (b, 0, 0)),
            _const_spec((1, D_MODEL)),
            _const_spec((D_MODEL, EVEN_IN)),
            _const_spec((3, SC_CH)),
            _const_spec((FT_GROUP_CH, 2 * FT_GROUP_CH)),
        ],
        out_specs=(ospec, zspec, zspec),
        compiler_params=_cparams(("parallel", "arbitrary")),
        name="even_in",
    )(x, x, x, mod, ng, w_in, cw, tc)


def _seq_dft_kernel(zr_ref, zi_ref, m_ref, g_ref, y_ref, o_scr, *, n2, tm):
    m1 = m_ref[...]
    chunk = tm // n2
    ntile = DFT_N1 // chunk

    def rows(ref, j):
        return [ref[0, 0, t * tm + j * chunk:t * tm + (j + 1) * chunk, :] for t in range(ntile)]

    for j in range(n2):
        z = jnp.concatenate(rows(zr_ref, j) + rows(zi_ref, j), axis=0)
        o_scr[2 * DFT_N1 * j:2 * DFT_N1 * (j + 1), :] = jnp.dot(m1, z.astype(BF16), preferred_element_type=F32)
    for k1 in range(DFT_N1):
        o = jnp.concatenate([o_scr[pl.ds(k1, n2, stride=2 * DFT_N1), :],
                             o_scr[pl.ds(DFT_N1 + k1, n2, stride=2 * DFT_N1), :]], axis=0)
        y_ref[0, pl.ds(k1, n2, stride=DFT_N1), :] = jnp.dot(g_ref[k1], o.astype(BF16), preferred_element_type=F32)


def _dft_tables(L):
    n2 = L // DFT_N1
    k = np.arange(DFT_N1)
    a = 2.0 * np.pi * ((k[:, None] * k[None, :]) % DFT_N1) / DFT_N1
    er, ei = np.cos(a) / np.sqrt(DFT_N1), -np.sin(a) / np.sqrt(DFT_N1)
    m1 = np.block([[er, -ei], [ei, er]])
    k1 = np.arange(DFT_N1)[:, None, None]
    k2 = np.arange(n2)[None, :, None]
    nn = np.arange(n2)[None, None, :]
    th = 2.0 * np.pi * ((nn * (k1 + DFT_N1 * k2)) % L) / L
    g = np.concatenate([np.cos(th), np.sin(th)], axis=-1) / np.sqrt(n2)
    return jnp.asarray(m1, F32).astype(BF16), jnp.asarray(g, F32).astype(BF16)


def _channel_dft_table():
    k = np.arange(FT_GROUP_CH)
    a = 2.0 * np.pi * ((k[:, None] * k[None, :]) % FT_GROUP_CH) / FT_GROUP_CH
    t = np.concatenate([np.cos(a), -np.sin(a)], axis=1) / np.sqrt(FT_GROUP_CH)
    return jnp.asarray(t, F32).astype(BF16)


def _seq_dft(zr, zi, tm):
    G, B, L, C = zr.shape
    n2 = L // DFT_N1
    m1, g = _dft_tables(L)
    zspec = pl.BlockSpec((1, 1, L, C), lambda b, j: (j, b, 0, 0))
    return pl.pallas_call(
        functools.partial(_seq_dft_kernel, n2=n2, tm=tm),
        out_shape=jax.ShapeDtypeStruct((B, L, G * C), F32),
        grid=(B, G),
        in_specs=[zspec, zspec, _const_spec((2 * DFT_N1, 2 * DFT_N1)), _const_spec((DFT_N1, n2, 2 * n2))],
        out_specs=pl.BlockSpec((1, L, C), lambda b, j: (b, 0, j)),
        scratch_shapes=[pltpu.VMEM((2 * DFT_N1 * n2, LANES), F32)],
        compiler_params=_cparams(("parallel", "arbitrary")),
        name="seq_dft",
    )(zr, zi, m1, g)


def _dense_dft_kernel(zr_ref, zi_ref, t_ref, y_ref):
    z = jnp.concatenate([zr_ref[0, 0], zi_ref[0, 0]], axis=0).astype(BF16)
    y_ref[0] = jnp.dot(t_ref[...], z, preferred_element_type=F32)


def _dense_seq_dft(zr, zi):
    G, B, L, C = zr.shape
    k = np.arange(L)
    a = 2.0 * np.pi * ((k[:, None] * k[None, :]) % L) / L
    t = jnp.asarray(np.concatenate([np.cos(a), np.sin(a)], axis=1) / np.sqrt(L), F32).astype(BF16)
    zspec = pl.BlockSpec((1, 1, L, C), lambda b, j: (j, b, 0, 0))
    return pl.pallas_call(
        _dense_dft_kernel,
        out_shape=jax.ShapeDtypeStruct((B, L, G * C), F32),
        grid=(B, G),
        in_specs=[zspec, zspec, _const_spec((L, 2 * L))],
        out_specs=pl.BlockSpec((1, L, C), lambda b, j: (b, 0, j)),
        compiler_params=_cparams(("arbitrary", "arbitrary")),
        name="dense_seq_dft",
    )(zr, zi, t)


HALO = 16


def _wide_halo_specs(tm, L, width):
    hb = tm // HALO
    last = L // HALO - 1
    return [
        pl.BlockSpec((1, tm, width), lambda b, i: (b, i, 0)),
        pl.BlockSpec((1, HALO, width), lambda b, i: (b, jnp.minimum((i + 1) * hb, last), 0)),
        pl.BlockSpec((1, HALO, width), lambda b, i: (b, jnp.maximum(i * hb - 1, 0), 0)),
    ]


def _circ(t_ref, n_ref, p_ref):
    return jnp.concatenate([t_ref[0], n_ref[0], p_ref[0]], axis=0)


def _mix_ffn_kernel(*refs, tm, nt, odd):
    n_in = 15 if odd else 9
    x3, rest = refs[:3], refs[3:n_in]
    mod_ref, ng_ref, wo_ref, wu_ref, cw_ref, wd_ref, o_ref, hh_scr, act_scr = refs[n_in:]
    i = pl.program_id(1)
    if odd:
        att, hf, hb, om = (_circ(*rest[k:k + 3]) for k in range(0, 12, 3))
        lhs = jnp.concatenate([att, ((hf + hb) * _sigmoid(om.astype(F32))).astype(BF16)], axis=-1)
    else:
        yc, yf = _circ(*rest[0:3]), _circ(*rest[3:6])
        lhs = jnp.concatenate([yc, yf.astype(BF16)], axis=-1)
    x1 = _circ(*x3) + mod_ref[0, 2:3, :] * jnp.dot(lhs, wo_ref[...], preferred_element_type=F32)
    hh_scr[...] = _norm_mod(x1, ng_ref[...], mod_ref[0, 3:4, :], mod_ref[0, 4:5, :]).astype(BF16)
    row = lax.broadcasted_iota(jnp.int32, (tm + 2 * HALO, 1), 0)
    valid = (row < tm) | ((row < tm + HALO) & (i < nt - 1)) | ((row >= tm + HALO) & (i > 0))
    for c in range(N_FF_CHUNKS):
        lo = c * FF_CHUNK
        g = jnp.dot(hh_scr[...], wu_ref[:, lo:lo + FF_CHUNK], preferred_element_type=F32)
        g = jnp.where(valid, g, 0.0)
        cv = _conv3(g, cw_ref[:, lo:lo + FF_CHUNK], tm)
        val = jnp.dot(hh_scr[:tm, :], wu_ref[:, D_FF + lo:D_FF + lo + FF_CHUNK], preferred_element_type=F32)
        act_scr[:, lo:lo + FF_CHUNK] = (cv * _sigmoid(cv) * val).astype(BF16)
    y = jnp.dot(act_scr[...], wd_ref[...], preferred_element_type=F32)
    o_ref[0] = x1[:tm] + mod_ref[0, 5:6, :] * y


def _mix_ffn(x, mixed, mod, ng, w_out, w_up, cw, w_down, tm, odd):
    B, L, _ = x.shape
    nt = L // tm
    specs = _wide_halo_specs(tm, L, D_MODEL)
    args = [x, x, x]
    for a in mixed:
        specs += _wide_halo_specs(tm, L, a.shape[-1])
        args += [a, a, a]
    return pl.pallas_call(
        functools.partial(_mix_ffn_kernel, tm=tm, nt=nt, odd=odd),
        out_shape=jax.ShapeDtypeStruct(x.shape, F32),
        grid=(B, nt),
        in_specs=specs + [
            pl.BlockSpec((1, SUBLANES, D_MODEL), lambda b, i: (b, 0, 0)),
            _const_spec((1, D_MODEL)),
            _resident_spec((D_MODEL, D_MODEL)),
            _resident_spec((D_MODEL, 2 * D_FF)),
            _const_spec((3, D_FF)),
            _resident_spec((D_FF, D_MODEL)),
        ],
        out_specs=pl.BlockSpec((1, tm, D_MODEL), lambda b, i: (b, i, 0)),
        scratch_shapes=[pltpu.VMEM((tm + 2 * HALO, D_MODEL), BF16),
                        pltpu.VMEM((tm, D_FF), BF16)],
        compiler_params=_cparams(("parallel", "arbitrary")),
        name="odd_mix_ffn" if odd else "even_mix_ffn",
    )(*args, mod, ng, w_out, w_up, cw, w_down)


def _split3(x):
    parts = []
    r = x
    for _ in range(3):
        p = r.astype(BF16)
        parts.append(p)
        r = r - p.astype(F32)
    return parts


def _odd_in_kernel(x_ref, mod_ref, ng_ref, w_ref, pm_ref, gain_ref, cos_ref, sin_ref, gb_ref, tl_ref, tu_ref,
                   qt_ref, kv_ref, vt_ref, qm_ref, kmt_ref, vm_ref, om_ref, gc_ref, gr_ref):
    shift = mod_ref[0, 0:1, :]
    scale = mod_ref[0, 1:2, :]
    h = _norm_mod(x_ref[0], ng_ref[...], shift, scale).astype(BF16)
    u = jnp.dot(h, w_ref[...], preferred_element_type=F32)

    uqk = u[:, OQ:OV]
    ms = _split_dot(uqk * uqk, pm_ref[...])
    rn = uqk * lax.rsqrt(ms + EPS) * gain_ref[...]
    lane = lax.broadcasted_iota(jnp.int32, (1, LANES), 1)
    first = (lane % 32) < 16
    cos = cos_ref[...]
    sin = sin_ref[...]
    roped = []
    for t in range((OV - OQ) // LANES):
        xt = rn[:, t * LANES:(t + 1) * LANES]
        sw = jnp.where(first, pltpu.roll(xt, LANES - 16, 1), pltpu.roll(xt, 16, 1))
        roped.append(xt * cos + sw * sin)
    for t in range(ATT_Q // LANES):
        qt_ref[0, t * LANES:(t + 1) * LANES, :] = (roped[t] * (ATT_SCALE * LOG2E)).T.astype(BF16)
    k = roped[ATT_Q // LANES]
    v = u[:, OV:OQM]
    half = LANES // 2
    kv_ref[0, :, 0:LANES] = k.astype(BF16)
    kv_ref[0, :, LANES:2 * LANES] = pltpu.roll(k, half, 1).astype(BF16)
    vt_ref[0] = v.T.astype(BF16)

    qm_ref[0] = u[:, OQM:OKM].astype(BF16)
    for p in range(ML_QK // LANES):
        km = u[:, OKM + p * LANES:OKM + (p + 1) * LANES] * (ML_QK_DIM ** -0.5)
        kmt_ref[0, p * LANES:(p + 1) * LANES, :] = km.T.astype(BF16)
    vm_ref[0] = u[:, OVM:OOM].astype(BF16)
    om_ref[0] = u[:, OOM:OG].astype(BF16)

    gt = (u[:, OG:ODD_COLS] + gb_ref[...]).T[:2 * SUBLANES, :]
    row = lax.broadcasted_iota(jnp.int32, (2 * SUBLANES, 1), 0)
    logsig = jnp.minimum(gt, 0.0) - jnp.log(1.0 + jnp.exp(-jnp.abs(gt)))
    parts = _split3(logsig)
    tl, tu = tl_ref[...], tu_ref[...]
    cum_f = sum(jnp.dot(p, tu, preferred_element_type=F32) for p in parts)
    cum_b = sum(jnp.dot(p, tl, preferred_element_type=F32) for p in parts)
    sel = row % 8
    gr = jnp.where(sel < 4, gt, jnp.where(row < SUBLANES, cum_f, cum_b))
    gr_ref[0] = gr
    gc_ref[0] = jnp.concatenate([gr, jnp.zeros((LANES - 2 * SUBLANES, gr.shape[1]), F32)], axis=0).T


def _chunk_tri(tm):
    i = np.arange(tm)
    same = (i[:, None] // ML_CHUNK) == (i[None, :] // ML_CHUNK)
    tl = same & (i[None, :] <= i[:, None])
    tu = same & (i[None, :] >= i[:, None])
    return jnp.asarray(tl, F32).astype(BF16), jnp.asarray(tu, F32).astype(BF16)


def _odd_in(x, mod, ng, w, pm, gain, cos, sin, gb, tm):
    B, L, _ = x.shape

    def rows(c, dt=BF16):
        return jax.ShapeDtypeStruct((B, L, c), dt), pl.BlockSpec((1, tm, c), lambda b, i: (b, i, 0))

    def cols(c, dt=BF16):
        return jax.ShapeDtypeStruct((B, c, L), dt), pl.BlockSpec((1, c, tm), lambda b, i: (b, 0, i))

    outs = [cols(ATT_Q), rows(2 * LANES), cols(LANES), rows(ML_QK), cols(ML_QK), rows(ML_V), rows(ML_V),
            rows(LANES, F32), cols(2 * SUBLANES, F32)]
    nqk = OV - OQ
    tl, tu = _chunk_tri(tm)
    return pl.pallas_call(
        _odd_in_kernel,
        out_shape=tuple(o[0] for o in outs),
        grid=(B, L // tm),
        in_specs=[
            pl.BlockSpec((1, tm, D_MODEL), lambda b, i: (b, i, 0)),
            pl.BlockSpec((1, SUBLANES, D_MODEL), lambda b, i: (b, 0, 0)),
            _const_spec((1, D_MODEL)),
            _const_spec((D_MODEL, ODD_COLS)),
            _const_spec((nqk, nqk)),
            _const_spec((1, nqk)),
            pl.BlockSpec((tm, LANES), lambda b, i: (i, 0)),
            pl.BlockSpec((tm, LANES), lambda b, i: (i, 0)),
            _const_spec((1, LANES)),
            _const_spec((tm, tm)),
            _const_spec((tm, tm)),
        ],
        out_specs=tuple(o[1] for o in outs),
        compiler_params=_cparams(("parallel", "arbitrary")),
        name="odd_in",
    )(x, mod, ng, w, pm, gain, cos, sin, gb, tl, tu)


def _rope_tables(L):
    rows = L // GRID_W
    row = jnp.repeat(jnp.arange(rows, dtype=F32), GRID_W)
    col = jnp.tile(jnp.arange(GRID_W, dtype=F32), rows)
    axis_dim = HEAD_DIM // 2
    inv_freq = ROPE_THETA ** (-jnp.arange(0, axis_dim, 2, dtype=F32) / axis_dim)
    ang = jnp.stack([row, col])[:, :, None] * inv_freq
    c, s = jnp.cos(ang), jnp.sin(ang)
    cos = jnp.concatenate([c[0], c[0], c[1], c[1]], axis=-1)
    sin = jnp.concatenate([-s[0], s[0], -s[1], s[1]], axis=-1)
    return jnp.tile(cos, (1, 2)), jnp.tile(sin, (1, 2))


def _att_kernel(sink_ref, qt_ref, kvp_ref, kvc_ref, kvn_ref, kvx_ref, vtp_ref, vtc_ref, vtn_ref, vtx_ref,
                o_ref, *, nb):
    n = pl.program_id(1)
    half = LANES // 2
    lo = lax.broadcasted_iota(jnp.int32, (1, LANES), 1) < half
    zero = jnp.zeros((), BF16)
    cx = kvx_ref.shape[1]
    klocal = ([kvp_ref[0]] + [kvc_ref[0, i * BLOCK:(i + 1) * BLOCK] for i in range(ATT_QB)] + [kvn_ref[0]])
    vlocal = ([vtp_ref[0]] + [vtc_ref[0, :, i * BLOCK:(i + 1) * BLOCK] for i in range(ATT_QB)] + [vtn_ref[0]])
    kctx = [kvx_ref[0, i:i + BLOCK] for i in range(0, cx, BLOCK)]
    vctx = [vtx_ref[0, :, i:i + BLOCK] for i in range(0, cx, BLOCK)]
    ones = jnp.ones((half, BLOCK), BF16)

    kj = lax.broadcasted_iota(jnp.int32, (BLOCK, BLOCK), 0)
    qi = lax.broadcasted_iota(jnp.int32, (BLOCK, BLOCK), 1)

    def twice(x):
        return jnp.concatenate([x, x], axis=1)

    left = lax.broadcasted_iota(jnp.int32, (1, 2 * BLOCK), 1) < BLOCK
    group = ATT_HEADS // ATT_KV_HEADS
    for qb in range(ATT_QB):
        blk_id = n * ATT_QB + qb
        kblocks = klocal[qb:qb + 3] + kctx
        vblocks = vlocal[qb:qb + 3] + vctx
        bias = [twice(jnp.where((kj >= qi) & (blk_id > 0), 0.0, NEG_BIG)), None,
                twice(jnp.where((kj <= qi) & (blk_id < nb - 1), 0.0, NEG_BIG))] + [None] * len(kctx)
        for kvh in range(ATT_KV_HEADS):
            t0 = kvh * group // 2
            cols = slice(qb * BLOCK, (qb + 1) * BLOCK)
            qt2 = jnp.concatenate([qt_ref[0, t0 * LANES:(t0 + 1) * LANES, cols],
                                   qt_ref[0, (t0 + 1) * LANES:(t0 + 2) * LANES, cols]], axis=1)
            outs = []
            for par in range(2):
                sk = jnp.where(left, sink_ref[2 * t0 + par], sink_ref[2 * t0 + 2 + par]) * LOG2E
                m = sk
                acc = jnp.zeros((LANES, 2 * BLOCK), F32)
                for blk in range(len(kblocks)):
                    k, ks = kblocks[blk][:, :LANES], kblocks[blk][:, LANES:]
                    kh = ((jnp.where(lo, k, zero), jnp.where(lo, zero, ks)),
                          (jnp.where(lo, ks, zero), jnp.where(lo, zero, k)))[kvh][par]
                    s = jnp.dot(kh, qt2, preferred_element_type=F32)
                    if bias[blk] is not None:
                        s = s + bias[blk]
                    m_new = jnp.maximum(m, jnp.max(s, axis=0, keepdims=True))
                    p = jnp.exp2(s - m_new).astype(BF16)
                    vh = jnp.concatenate([vblocks[blk][kvh * half:(kvh + 1) * half], ones], axis=0)
                    acc = acc * jnp.exp2(m - m_new) + jnp.dot(vh, p, preferred_element_type=F32)
                    m = m_new
                l = acc[half:half + 1, :] + jnp.exp2(sk - m)
                outs.append(acc[:half] * (1.0 / l))
            ot = jnp.concatenate(outs, axis=0)
            rows = slice(qb * BLOCK, (qb + 1) * BLOCK)
            o_ref[0, rows, t0 * LANES:(t0 + 1) * LANES] = ot[:, :BLOCK].T.astype(BF16)
            o_ref[0, rows, (t0 + 1) * LANES:(t0 + 2) * LANES] = ot[:, BLOCK:].T.astype(BF16)


def _attention(qt, kv, kvx, vt, vtx, sink):
    B, _, L = qt.shape
    nb = L // BLOCK
    ns = nb // ATT_QB
    cx = kvx.shape[1]
    kw = kv.shape[2]
    vw = vt.shape[1]
    wide = ATT_QB * BLOCK
    prev = lambda n: jnp.maximum(n * ATT_QB - 1, 0)
    nxt = lambda n: jnp.minimum((n + 1) * ATT_QB, nb - 1)
    return pl.pallas_call(
        functools.partial(_att_kernel, nb=nb),
        out_shape=jax.ShapeDtypeStruct((B, L, ATT_Q), BF16),
        grid=(B, ns),
        in_specs=[
            pl.BlockSpec(memory_space=pltpu.SMEM),
            pl.BlockSpec((1, ATT_Q, wide), lambda b, n: (b, 0, n)),
            pl.BlockSpec((1, BLOCK, kw), lambda b, n: (b, prev(n), 0)),
            pl.BlockSpec((1, wide, kw), lambda b, n: (b, n, 0)),
            pl.BlockSpec((1, BLOCK, kw), lambda b, n: (b, nxt(n), 0)),
            pl.BlockSpec((1, cx, kw), lambda b, n: (b, 0, 0)),
            pl.BlockSpec((1, vw, BLOCK), lambda b, n: (b, 0, prev(n))),
            pl.BlockSpec((1, vw, wide), lambda b, n: (b, 0, n)),
            pl.BlockSpec((1, vw, BLOCK), lambda b, n: (b, 0, nxt(n))),
            pl.BlockSpec((1, vw, cx), lambda b, n: (b, 0, 0)),
        ],
        out_specs=pl.BlockSpec((1, wide, ATT_Q), lambda b, n: (b, n, 0)),
        compiler_params=_cparams(("parallel", "arbitrary")),
        name="window_attention",
    )(sink, qt, kv, kv, kv, kvx, vt, vt, vt, vtx)


ML_GROUP = 8


def _mlstm_kernel(qf_ref, ktf_ref, vf_ref, gcf_ref, grf_ref, qb_ref, ktb_ref, vb_ref, gcb_ref, grb_ref,
                  s0_ref, m0_ref, *rest, T, nb, with_output):
    if with_output:
        hf_ref, hb_ref, s_ref, m_ref = rest
    else:
        s_ref, m_ref = rest
    j = pl.program_id(0)

    @pl.when(j == 0)
    def _():
        s_ref[...] = s0_ref[...]
        m_ref[...] = m0_ref[...]

    ti = lax.broadcasted_iota(jnp.int32, (T, T), 0)
    si = lax.broadcasted_iota(jnp.int32, (T, T), 1)
    masks = (si <= ti, si >= ti)
    top = lax.broadcasted_iota(jnp.int32, (LANES, 1), 0) < (LANES // 2)
    zero = jnp.zeros((), BF16)
    ones = jnp.ones((T, ML_V_DIM), BF16)
    fwd = (qf_ref, ktf_ref, vf_ref, gcf_ref, grf_ref)
    bwd = (qb_ref, ktb_ref, vb_ref, gcb_ref, grb_ref)
    chains = [(b, d, h) for b in range(nb) for d in range(2) for h in range(ML_HEADS)]

    for g0 in range(0, len(chains), ML_GROUP):
        grp = chains[g0:g0 + ML_GROUP]
        st = {}
        for (b, d, h) in grp:
            q_ref, kt_ref, v_ref, gc_ref, gr_ref = fwd if d == 0 else bwd
            r = (b * 2 + d) * ML_HEADS + h
            li, lf = 8 * d + h, 8 * d + 4 + h
            gr = gr_ref[b]
            u_row = gr[li:li + 1, :] - gr[lf:lf + 1, :]
            m_prev = m_ref[r:r + 1, 0:1]
            e = dict(r=r, u_row=u_row, m_prev=m_prev,
                     b_end=gr[lf:lf + 1, (T - 1 if d == 0 else 0):(T if d == 0 else 1)],
                     c_end=jnp.maximum(jnp.max(u_row, axis=1, keepdims=True), m_prev))
            if with_output:
                e["mu"] = jnp.where(masks[d], u_row, NEG_BIG)
                c_col = jnp.maximum(jnp.max(e["mu"], axis=1, keepdims=True), m_prev)
                e["cb"] = jnp.broadcast_to(c_col, (T, LANES))
                e["bb"] = jnp.broadcast_to(gc_ref[b, :, lf:lf + 1], (T, LANES))
            st[(b, d, h)] = e
        for (b, d, h) in grp:
            q_ref, kt_ref, v_ref, gc_ref, gr_ref = fwd if d == 0 else bwd
            e = st[(b, d, h)]
            pair = h // 2
            ktp = kt_ref[b, pair * LANES:(pair + 1) * LANES, :]
            e["kth"] = jnp.where(top, ktp, zero) if h % 2 == 0 else jnp.where(top, zero, ktp)
            e["vext"] = jnp.concatenate([v_ref[b, :, h * ML_V_DIM:(h + 1) * ML_V_DIM], ones], axis=1)
            if with_output:
                qp = q_ref[b, :, pair * LANES:(pair + 1) * LANES]
                qk = jnp.dot(qp, e["kth"], preferred_element_type=F32) * jnp.exp(e["mu"] - e["cb"])
                qs = qp.astype(F32) * jnp.exp(e["m_prev"] - e["cb"])
                e["lhs"] = jnp.concatenate([qk.astype(BF16), qs.astype(BF16)], axis=1)
        for (b, d, h) in grp:
            e = st[(b, d, h)]
            e["s_prev"] = s_ref[e["r"]]
            if with_output:
                rhs = jnp.concatenate([e["vext"], e["s_prev"].astype(BF16)], axis=0)
                tot = jnp.dot(e["lhs"], rhs, preferred_element_type=F32)
                floor = jnp.exp(-(e["bb"] + e["cb"]))
                hout = tot[:, :ML_V_DIM] / jnp.maximum(jnp.abs(tot[:, ML_V_DIM:]), floor)
                o_ref = hf_ref if d == 0 else hb_ref
                o_ref[b, :, h * ML_V_DIM:(h + 1) * ML_V_DIM] = hout
        for (b, d, h) in grp:
            e = st[(b, d, h)]
            kw = (e["kth"].astype(F32) * jnp.exp(e["u_row"] - e["c_end"])).astype(BF16)
            upd = jnp.dot(kw, e["vext"], preferred_element_type=F32)
            s_ref[e["r"]] = jnp.exp(e["m_prev"] - e["c_end"]) * e["s_prev"] + upd
            m_ref[e["r"]:e["r"] + 1, :] = jnp.broadcast_to(e["b_end"] + e["c_end"], (1, LANES))


def _mlstm(qm, kmt, vm, gc, gr, s0, m0, with_output):
    B, L, _ = qm.shape
    T = ML_CHUNK
    nc = L // T
    nchains = B * 2 * ML_HEADS
    up = lambda j: j
    down = lambda j: nc - 1 - j

    def specs(o):
        return [pl.BlockSpec((B, T, ML_QK), lambda j: (0, o(j), 0)),
                pl.BlockSpec((B, ML_QK, T), lambda j: (0, 0, o(j))),
                pl.BlockSpec((B, T, ML_V), lambda j: (0, o(j), 0)),
                pl.BlockSpec((B, T, LANES), lambda j: (0, o(j), 0)),
                pl.BlockSpec((B, 2 * SUBLANES, T), lambda j: (0, 0, o(j)))]

    s_spec = _const_spec((nchains, LANES, 2 * ML_V_DIM))
    m_spec = _const_spec((nchains, LANES))
    out_shape = [jax.ShapeDtypeStruct((nchains, LANES, 2 * ML_V_DIM), F32),
                 jax.ShapeDtypeStruct((nchains, LANES), F32)]
    out_specs = [s_spec, m_spec]
    if with_output:
        out_shape = [jax.ShapeDtypeStruct((B, L, ML_V), F32)] * 2 + out_shape
        out_specs = [pl.BlockSpec((B, T, ML_V), lambda j: (0, up(j), 0)),
                     pl.BlockSpec((B, T, ML_V), lambda j: (0, down(j), 0))] + out_specs
    return pl.pallas_call(
        functools.partial(_mlstm_kernel, T=T, nb=B, with_output=with_output),
        out_shape=tuple(out_shape),
        grid=(nc,),
        in_specs=specs(up) + specs(down) + [s_spec, m_spec],
        out_specs=tuple(out_specs),
        compiler_params=_cparams(("arbitrary",)),
        name="mlstm_scan" if with_output else "mlstm_context_state",
    )(qm, kmt, vm, gc, gr, qm, kmt, vm, gc, gr, s0, m0)


def _odd_tables(gate_b, q_g, k_g):
    assert sum([ATT_Q, ATT_KV, ATT_KV, ML_QK, ML_QK, ML_V, ML_V]) == OG
    nqk = OV - OQ
    head = np.arange(nqk) // HEAD_DIM
    pm = jnp.asarray((head[:, None] == head[None, :]) / HEAD_DIM, F32).astype(BF16)
    gain = jnp.concatenate([jnp.tile(q_g, ATT_HEADS), jnp.tile(k_g, ATT_KV_HEADS)])[None, :]
    gb = jnp.pad(gate_b.reshape(1, -1), ((0, 0), (0, LANES - gate_b.size)))
    return pm, gain, gb


def kernel(x, c, ctx, c_ctx, ada_w, ada_b, norm_g, even_w_in, even_conv, even_w_out, odd_w_in, odd_gate_b,
           odd_q_g, odd_k_g, odd_sink, odd_w_out, ffn_w_up, ffn_conv, ffn_w_down):
    B, L, _ = x.shape
    C = ctx.shape[1]
    depth = ada_w.shape[0]
    assert depth == 2 and L % (DFT_N1 * SUBLANES) == 0 and C % ML_CHUNK == 0

    cv = jnp.concatenate([c, c_ctx[None, :], jnp.zeros((SUBLANES - B - 1, D_MODEL), F32)], axis=0)
    mod = _modulation(cv, ada_w, ada_b).reshape(depth, SUBLANES, 6, D_MODEL)
    pad = ((0, 0), (0, SUBLANES - 6), (0, 0))
    mod_lat = [jnp.pad(mod[l, :B], pad) for l in range(depth)]
    mod_ctx = [jnp.pad(jnp.broadcast_to(mod[l, B], (B, 6, D_MODEL)), pad) for l in range(depth)]

    tm = min(512, L)
    tc = _channel_dft_table()
    w_up, w_down, w_in0, w_out0, w_in1, w_out1 = _cast_weights(
        (ffn_w_up, ffn_w_down, even_w_in, even_w_out, odd_w_in, odd_w_out),
        (2 * D_FF, D_MODEL, EVEN_IN, D_MODEL, ODD_COLS, D_MODEL))
    ffn_w = [(w_up[l], ffn_conv[l], w_down[l]) for l in range(depth)]

    w_in0, w_out0, w_in1, w_out1 = w_in0[0], w_out0[0], w_in1[0], w_out1[0]
    ng00, ng01 = norm_g[0, 0][None, :], norm_g[0, 1][None, :]

    def even_layer(xs, mods, tile, n2):
        yc, zr, zi = _even_in(xs, mods, ng00, w_in0, even_conv[0], tc, tile, n2)
        yf = _dense_seq_dft(zr, zi) if n2 is None else _seq_dft(zr, zi, tile)
        return _mix_ffn(xs, (yc, yf), mods, ng01, w_out0, *ffn_w[0], tile, odd=False)

    xl = even_layer(x, mod_lat[0], tm, L // DFT_N1)
    xc = even_layer(ctx, mod_ctx[0], C, None)

    pm, gain, gb = _odd_tables(odd_gate_b[0], odd_q_g[0], odd_k_g[0])
    ng10, ng11 = norm_g[1, 0][None, :], norm_g[1, 1][None, :]
    cos, sin = _rope_tables(L)
    one, nil = jnp.ones((C, LANES), F32), jnp.zeros((C, LANES), F32)
    qt, kv, vt, qm, kmt, vm, om, gc, gr = _odd_in(xl, mod_lat[1], ng10, w_in1, pm, gain, cos, sin, gb, tm)
    _, kvx, vtx, qmx, kmtx, vmx, _, gcx, grx = _odd_in(xc, mod_ctx[1], ng10, w_in1, pm, gain, one, nil, gb, C)

    att = _attention(qt, kv, kvx, vt, vtx, odd_sink[0])
    nchains = B * 2 * ML_HEADS
    s0 = jnp.zeros((nchains, LANES, 2 * ML_V_DIM), F32)
    m0 = jnp.zeros((nchains, LANES), F32)
    s1, m1 = _mlstm(qmx, kmtx, vmx, gcx, grx, s0, m0, with_output=False)
    hf, hb, _, _ = _mlstm(qm, kmt, vm, gc, gr, s1, m1, with_output=True)
    return _mix_ffn(xl, (att, hf, hb, om), mod_lat[1], ng11, w_out1, *ffn_w[1], tm, odd=True)
```

```python
import functools

import numpy as np
import jax
import jax.numpy as jnp
from jax import lax
from jax.experimental import pallas as pl
from jax.experimental.pallas import tpu as pltpu

F32 = jnp.float32
BF16 = jnp.bfloat16

D_MODEL = 1024
GRID_W = 64
EPS = 1e-6
SC_CH = 512
FT_CH = 512
FT_GROUPS = 4
FT_GROUP_CH = FT_CH // FT_GROUPS
EVEN_IN = 3 * SC_CH + FT_CH
ATT_HEADS = 8
ATT_KV_HEADS = 2
HEAD_DIM = 64
ATT_SCALE = HEAD_DIM ** -0.5
WINDOW = 128
BLOCK = 128
ROPE_THETA = 10000.0
ML_HEADS = 4
ML_QK_DIM = 64
ML_V_DIM = 128
ATT_Q = ATT_HEADS * HEAD_DIM
ATT_KV = ATT_KV_HEADS * HEAD_DIM
ML_QK = ML_HEADS * ML_QK_DIM
ML_V = ML_HEADS * ML_V_DIM
D_FF = 2816

LANES = 128
SUBLANES = 8
VMEM_LIMIT_BYTES = 56 * 1024 * 1024

DFT_N1 = 128
FF_CHUNK = 256
N_FF_CHUNKS = D_FF // FF_CHUNK
ML_CHUNK = 128
ATT_QB = 2
NEG_BIG = -1e30
LOG2E = 1.4426950408889634

OQ, OK_, OV, OQM, OKM, OVM, OOM, OG = 0, 512, 640, 768, 1024, 1280, 1792, 2304
ODD_COLS = OG + LANES


def _cparams(sem):
    return pltpu.CompilerParams(dimension_semantics=sem, vmem_limit_bytes=VMEM_LIMIT_BYTES)


def _sigmoid(x):
    return 1.0 / (1.0 + jnp.exp(-x))


def _norm_mod(x, g, shift, scale):
    y = x * lax.rsqrt(jnp.mean(x * x, axis=-1, keepdims=True) + EPS)
    return y * g * (1.0 + scale) + shift


def _mod_vec(mod_ref, k, mrow):
    r = pl.program_id(0) if mrow is None else mrow
    return mod_ref[k, pl.ds(r, 1), :]


def _mod_spec(layer):
    return pl.BlockSpec((None, 6, SUBLANES, D_MODEL), lambda *_: (layer, 0, 0, 0))


def _halo_rows(x_ref, xn_ref, xp_ref, shift, scale, ng_ref):
    g = ng_ref[...]
    parts = [_norm_mod(r[0], g, shift, scale) for r in (x_ref, xn_ref, xp_ref)]
    return jnp.concatenate(parts, axis=0).astype(BF16)


def _halo_valid(tm, i, nt):
    row = lax.broadcasted_iota(jnp.int32, (tm + 2 * SUBLANES, 1), 0)
    return ((row < tm) | ((row < tm + SUBLANES) & (i < nt - 1)) | ((row >= tm + SUBLANES) & (i > 0)))


def _conv3(v, cw, tm):
    n = v.shape[0]
    vp = pltpu.roll(v, 1, 0)[:tm]
    vn = pltpu.roll(v, n - 1, 0)[:tm]
    return vp * cw[0:1] + v[:tm] * cw[1:2] + vn * cw[2:3]


def _halo_specs(tm, L):
    hb = tm // SUBLANES
    last = L // SUBLANES - 1
    return [
        pl.BlockSpec((1, tm, D_MODEL), lambda b, i: (b, i, 0)),
        pl.BlockSpec((1, SUBLANES, D_MODEL), lambda b, i: (b, jnp.minimum((i + 1) * hb, last), 0)),
        pl.BlockSpec((1, SUBLANES, D_MODEL), lambda b, i: (b, jnp.maximum(i * hb - 1, 0), 0)),
    ]


def _const_spec(shape):
    nd = len(shape)
    return pl.BlockSpec(shape, lambda *_: (0,) * nd)


def _resident_spec(shape):
    nd = len(shape)
    return pl.BlockSpec(shape, lambda *_: (0,) * nd, pipeline_mode=pl.Buffered(1))


def _layer_spec(shape, layer):
    nd = len(shape)
    return pl.BlockSpec((None,) + tuple(shape), lambda *_: (layer,) + (0,) * nd, pipeline_mode=pl.Buffered(1))


def _split_dot(x, p):
    hi = x.astype(BF16)
    lo = (x - hi.astype(F32)).astype(BF16)
    return (jnp.dot(hi, p, preferred_element_type=F32) + jnp.dot(lo, p, preferred_element_type=F32))


def _mod_kernel(cv_ref, w_ref, b_ref, o_ref):
    cv = cv_ref[...]
    o_ref[0, 0] = _split_dot(cv * _sigmoid(cv), w_ref[0].astype(BF16)) + b_ref[0, 0]


def _modulation(cv, ada_w, ada_b):
    depth, _, n = ada_w.shape
    nv = n // D_MODEL
    return pl.pallas_call(
        _mod_kernel,
        out_shape=jax.ShapeDtypeStruct((depth, nv, SUBLANES, D_MODEL), F32),
        grid=(depth, nv),
        in_specs=[
            pl.BlockSpec((SUBLANES, D_MODEL), lambda l, j: (0, 0)),
            pl.BlockSpec((1, D_MODEL, D_MODEL), lambda l, j: (l, 0, j)),
            pl.BlockSpec((1, 1, 1, D_MODEL), lambda l, j: (l, j, 0, 0)),
        ],
        out_specs=pl.BlockSpec((1, 1, SUBLANES, D_MODEL), lambda l, j: (l, j, 0, 0)),
        compiler_params=_cparams(("arbitrary", "arbitrary")),
        name="modulation",
    )(cv, ada_w, ada_b.reshape(depth, nv, 1, D_MODEL))


CAST_STEPS = 8


def _cast_kernel(*refs):
    n = len(refs) // 2
    for src, dst in zip(refs[:n], refs[n:]):
        w = src[...].astype(BF16)
        pad = dst.shape[-1] - src.shape[-1]
        if pad:
            w = jnp.concatenate([w, jnp.zeros(w.shape[:-1] + (pad,), BF16)], axis=-1)
        dst[...] = w


def _cast_weights(ws, widths):
    in_specs, out_specs, out_shape = [], [], []
    for w, wd in zip(ws, widths):
        lead, (r, c) = w.shape[:-2], w.shape[-2:]
        imap = (lambda i: (0, i, 0)) if lead else (lambda i: (i, 0))
        in_specs.append(pl.BlockSpec(lead + (r // CAST_STEPS, c), imap))
        out_specs.append(pl.BlockSpec(lead + (r // CAST_STEPS, wd), imap))
        out_shape.append(jax.ShapeDtypeStruct(lead + (r, wd), BF16))
    return pl.pallas_call(
        _cast_kernel,
        out_shape=tuple(out_shape),
        grid=(CAST_STEPS,),
        in_specs=in_specs,
        out_specs=tuple(out_specs),
        compiler_params=_cparams(("arbitrary",)),
        name="cast_weights",
    )(*ws)


def _even_in_kernel(x_ref, xn_ref, xp_ref, mod_ref, ng_ref, w_ref, cw_ref, tc_ref,
                    yc_ref, zr_ref, zi_ref, *, tm, nt, n2, mrow):
    i = pl.program_id(1)
    hh = _halo_rows(x_ref, xn_ref, xp_ref, _mod_vec(mod_ref, 0, mrow), _mod_vec(mod_ref, 1, mrow), ng_ref)
    u = jnp.dot(hh, w_ref[...], preferred_element_type=F32)
    v = u[:, SC_CH:2 * SC_CH] * u[:, 2 * SC_CH:3 * SC_CH]
    v = jnp.where(_halo_valid(tm, i, nt), v, 0.0)
    yc = u[:tm, :SC_CH] * _conv3(v, cw_ref[...], tm)
    yc_ref[0] = yc.astype(BF16)
    uf = u[:tm, 3 * SC_CH:].astype(BF16)
    tc = tc_ref[...].astype(BF16)
    for g in range(FT_GROUPS):
        sl = slice(g * FT_GROUP_CH, (g + 1) * FT_GROUP_CH)
        ab = jnp.dot(uf[:, sl], tc, preferred_element_type=F32)
        if n2 is None:
            zr_ref[g, 0] = ab[:, :FT_GROUP_CH]
            zi_ref[g, 0] = ab[:, FT_GROUP_CH:]
        else:
            for a in range(tm // n2):
                dst = pl.ds(a, n2, stride=tm // n2)
                zr_ref[g, 0, dst, :] = ab[n2 * a:n2 * (a + 1), :FT_GROUP_CH]
                zi_ref[g, 0, dst, :] = ab[n2 * a:n2 * (a + 1), FT_GROUP_CH:]


def _even_in(x, mod, layer, mrow, ng, w_in, cw, tc, tm, n2):
    B, L, _ = x.shape
    nt = L // tm
    out = jax.ShapeDtypeStruct((B, L, FT_CH), BF16)
    zout = jax.ShapeDtypeStruct((FT_GROUPS, B, L, FT_GROUP_CH), F32)
    ospec = pl.BlockSpec((1, tm, FT_CH), lambda b, i: (b, i, 0))
    zspec = pl.BlockSpec((FT_GROUPS, 1, tm, FT_GROUP_CH), lambda b, i: (0, b, i, 0))
    return pl.pallas_call(
        functools.partial(_even_in_kernel, tm=tm, nt=nt, n2=n2, mrow=mrow),
        out_shape=(out, zout, zout),
        grid=(B, nt),
        in_specs=_halo_specs(tm, L) + [
            _mod_spec(layer),
            _const_spec((1, D_MODEL)),
            _const_spec((D_MODEL, EVEN_IN)),
            _const_spec((3, SC_CH)),
            _const_spec((FT_GROUP_CH, 2 * FT_GROUP_CH)),
        ],
        out_specs=(ospec, zspec, zspec),
        compiler_params=_cparams(("parallel", "arbitrary")),
        name="even_in",
    )(x, x, x, mod, ng, w_in, cw, tc)


def _seq_dft_kernel(zr_ref, zi_ref, m_ref, g_ref, y_ref, o_scr, *, n2, tm):
    m1 = m_ref[...].astype(BF16)
    chunk = tm // n2
    ntile = DFT_N1 // chunk

    def rows(ref, j):
        return [ref[0, 0, t * tm + j * chunk:t * tm + (j + 1) * chunk, :] for t in range(ntile)]

    for j in range(n2):
        z = jnp.concatenate(rows(zr_ref, j) + rows(zi_ref, j), axis=0)
        o_scr[2 * DFT_N1 * j:2 * DFT_N1 * (j + 1), :] = jnp.dot(m1, z.astype(BF16), preferred_element_type=F32)
    for k1 in range(DFT_N1):
        o = jnp.concatenate([o_scr[pl.ds(k1, n2, stride=2 * DFT_N1), :],
                             o_scr[pl.ds(DFT_N1 + k1, n2, stride=2 * DFT_N1), :]], axis=0)
        y_ref[0, pl.ds(k1, n2, stride=DFT_N1), :] = jnp.dot(g_ref[k1].astype(BF16), o.astype(BF16), preferred_element_type=F32)


def _dft_tables(L):
    n2 = L // DFT_N1
    k = np.arange(DFT_N1)
    a = 2.0 * np.pi * ((k[:, None] * k[None, :]) % DFT_N1) / DFT_N1
    er, ei = np.cos(a) / np.sqrt(DFT_N1), -np.sin(a) / np.sqrt(DFT_N1)
    m1 = np.block([[er, -ei], [ei, er]])
    k1 = np.arange(DFT_N1)[:, None, None]
    k2 = np.arange(n2)[None, :, None]
    nn = np.arange(n2)[None, None, :]
    th = 2.0 * np.pi * ((nn * (k1 + DFT_N1 * k2)) % L) / L
    g = np.concatenate([np.cos(th), np.sin(th)], axis=-1) / np.sqrt(n2)
    return jnp.asarray(m1, F32), jnp.asarray(g, F32)


def _channel_dft_table():
    k = np.arange(FT_GROUP_CH)
    a = 2.0 * np.pi * ((k[:, None] * k[None, :]) % FT_GROUP_CH) / FT_GROUP_CH
    t = np.concatenate([np.cos(a), -np.sin(a)], axis=1) / np.sqrt(FT_GROUP_CH)
    return jnp.asarray(t, F32)


def _seq_dft(zr, zi, tm):
    G, B, L, C = zr.shape
    n2 = L // DFT_N1
    m1, g = _dft_tables(L)
    zspec = pl.BlockSpec((1, 1, L, C), lambda b, j: (j, b, 0, 0))
    return pl.pallas_call(
        functools.partial(_seq_dft_kernel, n2=n2, tm=tm),
        out_shape=jax.ShapeDtypeStruct((B, L, G * C), F32),
        grid=(B, G),
        in_specs=[zspec, zspec, _const_spec((2 * DFT_N1, 2 * DFT_N1)), _const_spec((DFT_N1, n2, 2 * n2))],
        out_specs=pl.BlockSpec((1, L, C), lambda b, j: (b, 0, j)),
        scratch_shapes=[pltpu.VMEM((2 * DFT_N1 * n2, LANES), F32)],
        compiler_params=_cparams(("parallel", "arbitrary")),
        name="seq_dft",
    )(zr, zi, m1, g)


def _dense_dft_kernel(zr_ref, zi_ref, t_ref, y_ref):
    z = jnp.concatenate([zr_ref[0, 0], zi_ref[0, 0]], axis=0).astype(BF16)
    y_ref[0] = jnp.dot(t_ref[...].astype(BF16), z, preferred_element_type=F32)


def _dense_seq_dft(zr, zi):
    G, B, L, C = zr.shape
    k = np.arange(L)
    a = 2.0 * np.pi * ((k[:, None] * k[None, :]) % L) / L
    t = jnp.asarray(np.concatenate([np.cos(a), np.sin(a)], axis=1) / np.sqrt(L), F32)
    zspec = pl.BlockSpec((1, 1, L, C), lambda b, j: (j, b, 0, 0))
    return pl.pallas_call(
        _dense_dft_kernel,
        out_shape=jax.ShapeDtypeStruct((B, L, G * C), F32),
        grid=(B, G),
        in_specs=[zspec, zspec, _const_spec((L, 2 * L))],
        out_specs=pl.BlockSpec((1, L, C), lambda b, j: (b, 0, j)),
        compiler_params=_cparams(("arbitrary", "arbitrary")),
        name="dense_seq_dft",
    )(zr, zi, t)


HALO = 16


def _wide_halo_specs(tm, L, width):
    hb = tm // HALO
    last = L // HALO - 1
    return [
        pl.BlockSpec((1, tm, width), lambda b, i: (b, i, 0)),
        pl.BlockSpec((1, HALO, width), lambda b, i: (b, jnp.minimum((i + 1) * hb, last), 0)),
        pl.BlockSpec((1, HALO, width), lambda b, i: (b, jnp.maximum(i * hb - 1, 0), 0)),
    ]


def _circ(t_ref, n_ref, p_ref):
    return jnp.concatenate([t_ref[0], n_ref[0], p_ref[0]], axis=0)


def _mix_ffn_kernel(*refs, tm, nt, odd, mrow):
    n_in = 15 if odd else 9
    x3, rest = refs[:3], refs[3:n_in]
    mod_ref, ng_ref, wo_ref, wu_ref, cw_ref, wd_ref, o_ref, hh_scr, act_scr = refs[n_in:]
    i = pl.program_id(1)
    if odd:
        att, hf, hb, om = (_circ(*rest[k:k + 3]) for k in range(0, 12, 3))
        lhs = jnp.concatenate([att, ((hf + hb) * _sigmoid(om.astype(F32))).astype(BF16)], axis=-1)
    else:
        yc, yf = _circ(*rest[0:3]), _circ(*rest[3:6])
        lhs = jnp.concatenate([yc, yf.astype(BF16)], axis=-1)
    mv = [_mod_vec(mod_ref, k, mrow) for k in range(6)]
    x1 = _circ(*x3) + mv[2] * jnp.dot(lhs, wo_ref[...], preferred_element_type=F32)
    hh_scr[...] = _norm_mod(x1, ng_ref[...], mv[3], mv[4]).astype(BF16)
    row = lax.broadcasted_iota(jnp.int32, (tm + 2 * HALO, 1), 0)
    valid = (row < tm) | ((row < tm + HALO) & (i < nt - 1)) | ((row >= tm + HALO) & (i > 0))
    for c in range(N_FF_CHUNKS):
        lo = c * FF_CHUNK
        g = jnp.dot(hh_scr[...], wu_ref[:, lo:lo + FF_CHUNK], preferred_element_type=F32)
        g = jnp.where(valid, g, 0.0)
        cv = _conv3(g, cw_ref[:, lo:lo + FF_CHUNK], tm)
        val = jnp.dot(hh_scr[:tm, :], wu_ref[:, D_FF + lo:D_FF + lo + FF_CHUNK], preferred_element_type=F32)
        act_scr[:, lo:lo + FF_CHUNK] = (cv * _sigmoid(cv) * val).astype(BF16)
    y = jnp.dot(act_scr[...], wd_ref[...], preferred_element_type=F32)
    o_ref[0] = x1[:tm] + mv[5] * y


def _mix_ffn(x, mixed, mod, layer, mrow, ng, w_out, w_up, cw, w_down, tm, odd):
    B, L, _ = x.shape
    nt = L // tm
    specs = _wide_halo_specs(tm, L, D_MODEL)
    args = [x, x, x]
    for a in mixed:
        specs += _wide_halo_specs(tm, L, a.shape[-1])
        args += [a, a, a]
    return pl.pallas_call(
        functools.partial(_mix_ffn_kernel, tm=tm, nt=nt, odd=odd, mrow=mrow),
        out_shape=jax.ShapeDtypeStruct(x.shape, F32),
        grid=(B, nt),
        in_specs=specs + [
            _mod_spec(layer),
            _const_spec((1, D_MODEL)),
            _resident_spec((D_MODEL, D_MODEL)),
            _layer_spec((D_MODEL, 2 * D_FF), layer),
            _layer_spec((3, D_FF), layer),
            _layer_spec((D_FF, D_MODEL), layer),
        ],
        out_specs=pl.BlockSpec((1, tm, D_MODEL), lambda b, i: (b, i, 0)),
        scratch_shapes=[pltpu.VMEM((tm + 2 * HALO, D_MODEL), BF16),
                        pltpu.VMEM((tm, D_FF), BF16)],
        compiler_params=_cparams(("parallel", "arbitrary")),
        name="odd_mix_ffn" if odd else "even_mix_ffn",
    )(*args, mod, ng, w_out, w_up, cw, w_down)


def _split3(x):
    parts = []
    r = x
    for _ in range(3):
        p = r.astype(BF16)
        parts.append(p)
        r = r - p.astype(F32)
    return parts


def _odd_in_kernel(x_ref, mod_ref, ng_ref, w_ref, pm_ref, gain_ref, cos_ref, sin_ref, gb_ref, tl_ref, tu_ref,
                   qt_ref, kv_ref, vt_ref, qm_ref, kmt_ref, vm_ref, om_ref, gc_ref, gr_ref, *, mrow):
    h = _norm_mod(x_ref[0], ng_ref[...], _mod_vec(mod_ref, 0, mrow), _mod_vec(mod_ref, 1, mrow)).astype(BF16)
    u = jnp.dot(h, w_ref[...], preferred_element_type=F32)

    uqk = u[:, OQ:OV]
    ms = jnp.dot((uqk * uqk).astype(BF16), pm_ref[...].astype(BF16), preferred_element_type=F32)
    rn = uqk * lax.rsqrt(ms + EPS) * gain_ref[...]
    lane = lax.broadcasted_iota(jnp.int32, (1, LANES), 1)
    first = (lane % 32) < 16
    cos = cos_ref[...]
    sin = sin_ref[...]
    roped = []
    for t in range((OV - OQ) // LANES):
        xt = rn[:, t * LANES:(t + 1) * LANES]
        sw = jnp.where(first, pltpu.roll(xt, LANES - 16, 1), pltpu.roll(xt, 16, 1))
        roped.append(xt * cos + sw * sin)
    for t in range(ATT_Q // LANES):
        qt_ref[0, t * LANES:(t + 1) * LANES, :] = (roped[t] * (ATT_SCALE * LOG2E)).T.astype(BF16)
    k = roped[ATT_Q // LANES]
    v = u[:, OV:OQM]
    half = LANES // 2
    kv_ref[0, :, 0:LANES] = k.astype(BF16)
    kv_ref[0, :, LANES:2 * LANES] = pltpu.roll(k, half, 1).astype(BF16)
    vt_ref[0] = v.T.astype(BF16)

    qm_ref[0] = u[:, OQM:OKM].astype(BF16)
    for p in range(ML_QK // LANES):
        km = u[:, OKM + p * LANES:OKM + (p + 1) * LANES] * (ML_QK_DIM ** -0.5)
        kmt_ref[0, p * LANES:(p + 1) * LANES, :] = km.T.astype(BF16)
    vm_ref[0] = u[:, OVM:OOM].astype(BF16)
    om_ref[0] = u[:, OOM:OG].astype(BF16)

    gt = (u[:, OG:ODD_COLS] + gb_ref[...]).T[:2 * SUBLANES, :]
    row = lax.broadcasted_iota(jnp.int32, (2 * SUBLANES, 1), 0)
    logsig = jnp.minimum(gt, 0.0) - jnp.log(1.0 + jnp.exp(-jnp.abs(gt)))
    parts = _split3(logsig)
    tl, tu = tl_ref[...].astype(BF16), tu_ref[...].astype(BF16)
    cum_f = sum(jnp.dot(p, tu, preferred_element_type=F32) for p in parts)
    cum_b = sum(jnp.dot(p, tl, preferred_element_type=F32) for p in parts)
    sel = row % 8
    gr = jnp.where(sel < 4, gt, jnp.where(row < SUBLANES, cum_f, cum_b))
    gr_ref[0] = gr
    gc_ref[0] = jnp.concatenate([gr, jnp.zeros((LANES - 2 * SUBLANES, gr.shape[1]), F32)], axis=0).T


def _chunk_tri(tm):
    i = np.arange(tm)
    same = (i[:, None] // ML_CHUNK) == (i[None, :] // ML_CHUNK)
    tl = same & (i[None, :] <= i[:, None])
    tu = same & (i[None, :] >= i[:, None])
    return jnp.asarray(tl, F32), jnp.asarray(tu, F32)


def _odd_in(x, mod, layer, mrow, ng, w, pm, gain, cos, sin, gb, tm):
    B, L, _ = x.shape

    def rows(c, dt=BF16):
        return jax.ShapeDtypeStruct((B, L, c), dt), pl.BlockSpec((1, tm, c), lambda b, i: (b, i, 0))

    def cols(c, dt=BF16):
        return jax.ShapeDtypeStruct((B, c, L), dt), pl.BlockSpec((1, c, tm), lambda b, i: (b, 0, i))

    outs = [cols(ATT_Q), rows(2 * LANES), cols(LANES), rows(ML_QK), cols(ML_QK), rows(ML_V), rows(ML_V),
            rows(LANES, F32), cols(2 * SUBLANES, F32)]
    nqk = OV - OQ
    tl, tu = _chunk_tri(tm)
    return pl.pallas_call(
        functools.partial(_odd_in_kernel, mrow=mrow),
        out_shape=tuple(o[0] for o in outs),
        grid=(B, L // tm),
        in_specs=[
            pl.BlockSpec((1, tm, D_MODEL), lambda b, i: (b, i, 0)),
            _mod_spec(layer),
            _const_spec((1, D_MODEL)),
            _const_spec((D_MODEL, ODD_COLS)),
            _const_spec((nqk, nqk)),
            _const_spec((1, nqk)),
            pl.BlockSpec((tm, LANES), lambda b, i: (i, 0)),
            pl.BlockSpec((tm, LANES), lambda b, i: (i, 0)),
            _const_spec((1, LANES)),
            _const_spec((tm, tm)),
            _const_spec((tm, tm)),
        ],
        out_specs=tuple(o[1] for o in outs),
        compiler_params=_cparams(("parallel", "arbitrary")),
        name="odd_in",
    )(x, mod, ng, w, pm, gain, cos, sin, gb, tl, tu)


def _rope_tables(L):
    rows = L // GRID_W
    pos = np.stack([np.repeat(np.arange(rows), GRID_W), np.tile(np.arange(GRID_W), rows)]).astype(np.float64)
    axis_dim = HEAD_DIM // 2
    inv_freq = ROPE_THETA ** (-np.arange(0, axis_dim, 2, dtype=np.float64) / axis_dim)
    ang = pos[:, :, None] * inv_freq
    c, sn = np.cos(ang), np.sin(ang)
    cos = np.concatenate([c[0], c[0], c[1], c[1]], axis=-1)
    sin = np.concatenate([-sn[0], sn[0], -sn[1], sn[1]], axis=-1)
    return jnp.asarray(np.tile(cos, (1, 2)), F32), jnp.asarray(np.tile(sin, (1, 2)), F32)


def _att_kernel(sink_ref, qt_ref, kvp_ref, kvc_ref, kvn_ref, kvx_ref, vtp_ref, vtc_ref, vtn_ref, vtx_ref,
                o_ref, *, nb):
    n = pl.program_id(1)
    half = LANES // 2
    lo = lax.broadcasted_iota(jnp.int32, (1, LANES), 1) < half
    zero = jnp.zeros((), BF16)
    cx = kvx_ref.shape[1]
    klocal = ([kvp_ref[0]] + [kvc_ref[0, i * BLOCK:(i + 1) * BLOCK] for i in range(ATT_QB)] + [kvn_ref[0]])
    vlocal = ([vtp_ref[0]] + [vtc_ref[0, :, i * BLOCK:(i + 1) * BLOCK] for i in range(ATT_QB)] + [vtn_ref[0]])
    kctx = [kvx_ref[0, i:i + BLOCK] for i in range(0, cx, BLOCK)]
    vctx = [vtx_ref[0, :, i:i + BLOCK] for i in range(0, cx, BLOCK)]
    ones = jnp.ones((half, BLOCK), BF16)

    kj = lax.broadcasted_iota(jnp.int32, (BLOCK, BLOCK), 0)
    qi = lax.broadcasted_iota(jnp.int32, (BLOCK, BLOCK), 1)

    def twice(x):
        return jnp.concatenate([x, x], axis=1)

    left = lax.broadcasted_iota(jnp.int32, (1, 2 * BLOCK), 1) < BLOCK
    group = ATT_HEADS // ATT_KV_HEADS
    for qb in range(ATT_QB):
        blk_id = n * ATT_QB + qb
        kblocks = klocal[qb:qb + 3] + kctx
        vblocks = vlocal[qb:qb + 3] + vctx
        bias = [twice(jnp.where((kj >= qi) & (blk_id > 0), 0.0, NEG_BIG)), None,
                twice(jnp.where((kj <= qi) & (blk_id < nb - 1), 0.0, NEG_BIG))] + [None] * len(kctx)
        for kvh in range(ATT_KV_HEADS):
            t0 = kvh * group // 2
            cols = slice(qb * BLOCK, (qb + 1) * BLOCK)
            qt2 = jnp.concatenate([qt_ref[0, t0 * LANES:(t0 + 1) * LANES, cols],
                                   qt_ref[0, (t0 + 1) * LANES:(t0 + 2) * LANES, cols]], axis=1)
            outs = []
            for par in range(2):
                sk = jnp.where(left, sink_ref[2 * t0 + par], sink_ref[2 * t0 + 2 + par]) * LOG2E
                m = sk
                acc = jnp.zeros((LANES, 2 * BLOCK), F32)
                for blk in range(len(kblocks)):
                    k, ks = kblocks[blk][:, :LANES], kblocks[blk][:, LANES:]
                    kh = ((jnp.where(lo, k, zero), jnp.where(lo, zero, ks)),
                          (jnp.where(lo, ks, zero), jnp.where(lo, zero, k)))[kvh][par]
                    s = jnp.dot(kh, qt2, preferred_element_type=F32)
                    if bias[blk] is not None:
                        s = s + bias[blk]
                    m_new = jnp.maximum(m, jnp.max(s, axis=0, keepdims=True))
                    p = jnp.exp2(s - m_new).astype(BF16)
                    vh = jnp.concatenate([vblocks[blk][kvh * half:(kvh + 1) * half], ones], axis=0)
                    acc = acc * jnp.exp2(m - m_new) + jnp.dot(vh, p, preferred_element_type=F32)
                    m = m_new
                l = acc[half:half + 1, :] + jnp.exp2(sk - m)
                outs.append(acc[:half] * (1.0 / l))
            ot = jnp.concatenate(outs, axis=0)
            rows = slice(qb * BLOCK, (qb + 1) * BLOCK)
            o_ref[0, rows, t0 * LANES:(t0 + 1) * LANES] = ot[:, :BLOCK].T.astype(BF16)
            o_ref[0, rows, (t0 + 1) * LANES:(t0 + 2) * LANES] = ot[:, BLOCK:].T.astype(BF16)


def _attention(qt, kv, kvx, vt, vtx, sink):
    B, _, L = qt.shape
    nb = L // BLOCK
    ns = nb // ATT_QB
    cx = kvx.shape[1]
    kw = kv.shape[2]
    vw = vt.shape[1]
    wide = ATT_QB * BLOCK
    prev = lambda n: jnp.maximum(n * ATT_QB - 1, 0)
    nxt = lambda n: jnp.minimum((n + 1) * ATT_QB, nb - 1)
    return pl.pallas_call(
        functools.partial(_att_kernel, nb=nb),
        out_shape=jax.ShapeDtypeStruct((B, L, ATT_Q), BF16),
        grid=(B, ns),
        in_specs=[
            pl.BlockSpec(memory_space=pltpu.SMEM),
            pl.BlockSpec((1, ATT_Q, wide), lambda b, n: (b, 0, n)),
            pl.BlockSpec((1, BLOCK, kw), lambda b, n: (b, prev(n), 0)),
            pl.BlockSpec((1, wide, kw), lambda b, n: (b, n, 0)),
            pl.BlockSpec((1, BLOCK, kw), lambda b, n: (b, nxt(n), 0)),
            pl.BlockSpec((1, cx, kw), lambda b, n: (b, 0, 0)),
            pl.BlockSpec((1, vw, BLOCK), lambda b, n: (b, 0, prev(n))),
            pl.BlockSpec((1, vw, wide), lambda b, n: (b, 0, n)),
            pl.BlockSpec((1, vw, BLOCK), lambda b, n: (b, 0, nxt(n))),
            pl.BlockSpec((1, vw, cx), lambda b, n: (b, 0, 0)),
        ],
        out_specs=pl.BlockSpec((1, wide, ATT_Q), lambda b, n: (b, n, 0)),
        compiler_params=_cparams(("parallel", "arbitrary")),
        name="window_attention",
    )(sink, qt, kv, kv, kv, kvx, vt, vt, vt, vtx)


ML_GROUP = 8


def _mlstm_kernel(qf_ref, ktf_ref, vf_ref, gcf_ref, grf_ref, qb_ref, ktb_ref, vb_ref, gcb_ref, grb_ref,
                  s0_ref, m0_ref, *rest, T, nb, with_output):
    if with_output:
        hf_ref, hb_ref, s_ref, m_ref = rest
    else:
        s_ref, m_ref = rest
    j = pl.program_id(0)

    @pl.when(j == 0)
    def _():
        s_ref[...] = s0_ref[...]
        m_ref[...] = m0_ref[...]

    ti = lax.broadcasted_iota(jnp.int32, (T, T), 0)
    si = lax.broadcasted_iota(jnp.int32, (T, T), 1)
    masks = (si <= ti, si >= ti)
    top = lax.broadcasted_iota(jnp.int32, (LANES, 1), 0) < (LANES // 2)
    zero = jnp.zeros((), BF16)
    ones = jnp.ones((T, ML_V_DIM), BF16)
    fwd = (qf_ref, ktf_ref, vf_ref, gcf_ref, grf_ref)
    bwd = (qb_ref, ktb_ref, vb_ref, gcb_ref, grb_ref)
    chains = [(b, d, h) for b in range(nb) for d in range(2) for h in range(ML_HEADS)]

    for g0 in range(0, len(chains), ML_GROUP):
        grp = chains[g0:g0 + ML_GROUP]
        st = {}
        for (b, d, h) in grp:
            q_ref, kt_ref, v_ref, gc_ref, gr_ref = fwd if d == 0 else bwd
            r = (b * 2 + d) * ML_HEADS + h
            li, lf = 8 * d + h, 8 * d + 4 + h
            gr = gr_ref[b]
            u_row = gr[li:li + 1, :] - gr[lf:lf + 1, :]
            m_prev = m_ref[r:r + 1, 0:1]
            e = dict(r=r, u_row=u_row, m_prev=m_prev,
                     b_end=gr[lf:lf + 1, (T - 1 if d == 0 else 0):(T if d == 0 else 1)],
                     c_end=jnp.maximum(jnp.max(u_row, axis=1, keepdims=True), m_prev))
            if with_output:
                e["mu"] = jnp.where(masks[d], u_row, NEG_BIG)
                c_col = jnp.maximum(jnp.max(e["mu"], axis=1, keepdims=True), m_prev)
                e["cb"] = jnp.broadcast_to(c_col, (T, LANES))
                e["bb"] = jnp.broadcast_to(gc_ref[b, :, lf:lf + 1], (T, LANES))
            st[(b, d, h)] = e
        for (b, d, h) in grp:
            q_ref, kt_ref, v_ref, gc_ref, gr_ref = fwd if d == 0 else bwd
            e = st[(b, d, h)]
            pair = h // 2
            ktp = kt_ref[b, pair * LANES:(pair + 1) * LANES, :]
            e["kth"] = jnp.where(top, ktp, zero) if h % 2 == 0 else jnp.where(top, zero, ktp)
            e["vext"] = jnp.concatenate([v_ref[b, :, h * ML_V_DIM:(h + 1) * ML_V_DIM], ones], axis=1)
            if with_output:
                qp = q_ref[b, :, pair * LANES:(pair + 1) * LANES]
                qk = jnp.dot(qp, e["kth"], preferred_element_type=F32) * jnp.exp(e["mu"] - e["cb"])
                qs = qp.astype(F32) * jnp.exp(e["m_prev"] - e["cb"])
                e["lhs"] = jnp.concatenate([qk.astype(BF16), qs.astype(BF16)], axis=1)
        for (b, d, h) in grp:
            e = st[(b, d, h)]
            e["s_prev"] = s_ref[e["r"]]
            if with_output:
                rhs = jnp.concatenate([e["vext"], e["s_prev"].astype(BF16)], axis=0)
                tot = jnp.dot(e["lhs"], rhs, preferred_element_type=F32)
                floor = jnp.exp(-(e["bb"] + e["cb"]))
                hout = tot[:, :ML_V_DIM] / jnp.maximum(jnp.abs(tot[:, ML_V_DIM:]), floor)
                o_ref = hf_ref if d == 0 else hb_ref
                o_ref[b, :, h * ML_V_DIM:(h + 1) * ML_V_DIM] = hout
        for (b, d, h) in grp:
            e = st[(b, d, h)]
            kw = (e["kth"].astype(F32) * jnp.exp(e["u_row"] - e["c_end"])).astype(BF16)
            upd = jnp.dot(kw, e["vext"], preferred_element_type=F32)
            s_ref[e["r"]] = jnp.exp(e["m_prev"] - e["c_end"]) * e["s_prev"] + upd
            m_ref[e["r"]:e["r"] + 1, :] = jnp.broadcast_to(e["b_end"] + e["c_end"], (1, LANES))


def _mlstm(qm, kmt, vm, gc, gr, s0, m0, with_output):
    B, L, _ = qm.shape
    T = ML_CHUNK
    nc = L // T
    nchains = B * 2 * ML_HEADS
    up = lambda j: j
    down = lambda j: nc - 1 - j

    def specs(o):
        return [pl.BlockSpec((B, T, ML_QK), lambda j: (0, o(j), 0)),
                pl.BlockSpec((B, ML_QK, T), lambda j: (0, 0, o(j))),
                pl.BlockSpec((B, T, ML_V), lambda j: (0, o(j), 0)),
                pl.BlockSpec((B, T, LANES), lambda j: (0, o(j), 0)),
                pl.BlockSpec((B, 2 * SUBLANES, T), lambda j: (0, 0, o(j)))]

    s_spec = _const_spec((nchains, LANES, 2 * ML_V_DIM))
    m_spec = _const_spec((nchains, LANES))
    out_shape = [jax.ShapeDtypeStruct((nchains, LANES, 2 * ML_V_DIM), F32),
                 jax.ShapeDtypeStruct((nchains, LANES), F32)]
    out_specs = [s_spec, m_spec]
    if with_output:
        out_shape = [jax.ShapeDtypeStruct((B, L, ML_V), F32)] * 2 + out_shape
        out_specs = [pl.BlockSpec((B, T, ML_V), lambda j: (0, up(j), 0)),
                     pl.BlockSpec((B, T, ML_V), lambda j: (0, down(j), 0))] + out_specs
    return pl.pallas_call(
        functools.partial(_mlstm_kernel, T=T, nb=B, with_output=with_output),
        out_shape=tuple(out_shape),
        grid=(nc,),
        in_specs=specs(up) + specs(down) + [s_spec, m_spec],
        out_specs=tuple(out_specs),
        compiler_params=_cparams(("arbitrary",)),
        name="mlstm_scan" if with_output else "mlstm_context_state",
    )(qm, kmt, vm, gc, gr, qm, kmt, vm, gc, gr, s0, m0)


def _odd_tables(gate_b, q_g, k_g):
    assert sum([ATT_Q, ATT_KV, ATT_KV, ML_QK, ML_QK, ML_V, ML_V]) == OG
    nqk = OV - OQ
    head = np.arange(nqk) // HEAD_DIM
    pm = jnp.asarray((head[:, None] == head[None, :]) / HEAD_DIM, F32)
    gain = jnp.concatenate([jnp.tile(q_g, ATT_HEADS), jnp.tile(k_g, ATT_KV_HEADS)])[None, :]
    gb = jnp.pad(gate_b.reshape(1, -1), ((0, 0), (0, LANES - gate_b.size)))
    return pm, gain, gb


def kernel(x, c, ctx, c_ctx, ada_w, ada_b, norm_g, even_w_in, even_conv, even_w_out, odd_w_in, odd_gate_b,
           odd_q_g, odd_k_g, odd_sink, odd_w_out, ffn_w_up, ffn_conv, ffn_w_down):
    B, L, _ = x.shape
    C = ctx.shape[1]
    depth = ada_w.shape[0]
    assert depth == 2 and L % (DFT_N1 * SUBLANES) == 0 and C % ML_CHUNK == 0

    cv = jnp.concatenate([c, c_ctx[None, :], jnp.zeros((SUBLANES - B - 1, D_MODEL), F32)], axis=0)
    mod = _modulation(cv, ada_w, ada_b)
    lat, cx = None, B

    tm = min(512, L)
    tc = _channel_dft_table()
    w_up, w_down, w_in0, w_out0, w_in1, w_out1 = _cast_weights(
        (ffn_w_up, ffn_w_down, even_w_in[0], even_w_out[0], odd_w_in[0], odd_w_out[0]),
        (2 * D_FF, D_MODEL, EVEN_IN, D_MODEL, ODD_COLS, D_MODEL))
    ffn_w = (w_up, ffn_conv, w_down)

    ng00, ng01 = norm_g[0, 0][None, :], norm_g[0, 1][None, :]

    def even_layer(xs, mrow, tile, n2):
        yc, zr, zi = _even_in(xs, mod, 0, mrow, ng00, w_in0, even_conv[0], tc, tile, n2)
        yf = _dense_seq_dft(zr, zi) if n2 is None else _seq_dft(zr, zi, tile)
        return _mix_ffn(xs, (yc, yf), mod, 0, mrow, ng01, w_out0, *ffn_w, tile, odd=False)

    xl = even_layer(x, lat, tm, L // DFT_N1)
    xc = even_layer(ctx, cx, C, None)

    pm, gain, gb = _odd_tables(odd_gate_b[0], odd_q_g[0], odd_k_g[0])
    ng10, ng11 = norm_g[1, 0][None, :], norm_g[1, 1][None, :]
    cos, sin = _rope_tables(L)
    one, nil = jnp.ones((C, LANES), F32), jnp.zeros((C, LANES), F32)
    qt, kv, vt, qm, kmt, vm, om, gc, gr = _odd_in(xl, mod, 1, lat, ng10, w_in1, pm, gain, cos, sin, gb, tm)
    _, kvx, vtx, qmx, kmtx, vmx, _, gcx, grx = _odd_in(xc, mod, 1, cx, ng10, w_in1, pm, gain, one, nil, gb, C)

    att = _attention(qt, kv, kvx, vt, vtx, odd_sink[0])
    nchains = B * 2 * ML_HEADS
    s0 = jnp.zeros((nchains, LANES, 2 * ML_V_DIM), F32)
    m0 = jnp.zeros((nchains, LANES), F32)
    s1, m1 = _mlstm(qmx, kmtx, vmx, gcx, grx, s0, m0, with_output=False)
    hf, hb, _, _ = _mlstm(qm, kmt, vm, gc, gr, s1, m1, with_output=True)
    return _mix_ffn(xl, (att, hf, hb, om), mod, 1, lat, ng11, w_out1, *ffn_w, tm, odd=True)
```

```python
import functools

import numpy as np
import jax
import jax.numpy as jnp
from jax import lax
from jax.experimental import pallas as pl
from jax.experimental.pallas import tpu as pltpu

F32 = jnp.float32
BF16 = jnp.bfloat16

D_MODEL = 1024
GRID_W = 64
EPS = 1e-6
SC_CH = 512
FT_CH = 512
FT_GROUPS = 4
FT_GROUP_CH = FT_CH // FT_GROUPS
EVEN_IN = 3 * SC_CH + FT_CH
ATT_HEADS = 8
ATT_KV_HEADS = 2
HEAD_DIM = 64
ATT_SCALE = HEAD_DIM ** -0.5
WINDOW = 128
BLOCK = 128
ROPE_THETA = 10000.0
ML_HEADS = 4
ML_QK_DIM = 64
ML_V_DIM = 128
ATT_Q = ATT_HEADS * HEAD_DIM
ATT_KV = ATT_KV_HEADS * HEAD_DIM
ML_QK = ML_HEADS * ML_QK_DIM
ML_V = ML_HEADS * ML_V_DIM
D_FF = 2816

LANES = 128
SUBLANES = 8
VMEM_LIMIT_BYTES = 56 * 1024 * 1024

DFT_N1 = 128
FF_CHUNK = 256
N_FF_CHUNKS = D_FF // FF_CHUNK
ML_CHUNK = 128
ATT_QB = 2
NEG_BIG = -1e30
LOG2E = 1.4426950408889634

OQ, OK_, OV, OQM, OKM, OVM, OOM, OG = 0, 512, 640, 768, 1024, 1280, 1792, 2304
ODD_COLS = OG + LANES


def _cparams(sem):
    return pltpu.CompilerParams(dimension_semantics=sem, vmem_limit_bytes=VMEM_LIMIT_BYTES)


def _sigmoid(x):
    return 1.0 / (1.0 + jnp.exp(-x))


def _norm_mod(x, g, shift, scale):
    y = x * lax.rsqrt(jnp.mean(x * x, axis=-1, keepdims=True) + EPS)
    return y * g * (1.0 + scale) + shift


def _mod_vec(mod_ref, k, mrow):
    r = pl.program_id(0) if mrow is None else mrow
    return mod_ref[k, pl.ds(r, 1), :]


def _mod_spec(layer):
    return pl.BlockSpec((None, 6, SUBLANES, D_MODEL), lambda *_: (layer, 0, 0, 0))


def _halo_rows(x_ref, xn_ref, xp_ref, shift, scale, ng_ref):
    g = ng_ref[...]
    parts = [_norm_mod(r[0], g, shift, scale) for r in (x_ref, xn_ref, xp_ref)]
    return jnp.concatenate(parts, axis=0).astype(BF16)


def _halo_valid(tm, i, nt):
    row = lax.broadcasted_iota(jnp.int32, (tm + 2 * SUBLANES, 1), 0)
    return ((row < tm) | ((row < tm + SUBLANES) & (i < nt - 1)) | ((row >= tm + SUBLANES) & (i > 0)))


def _conv3(v, cw, tm):
    n = v.shape[0]
    vp = pltpu.roll(v, 1, 0)[:tm]
    vn = pltpu.roll(v, n - 1, 0)[:tm]
    return vp * cw[0:1] + v[:tm] * cw[1:2] + vn * cw[2:3]


def _halo_specs(tm, L):
    hb = tm // SUBLANES
    last = L // SUBLANES - 1
    return [
        pl.BlockSpec((1, tm, D_MODEL), lambda b, i: (b, i, 0)),
        pl.BlockSpec((1, SUBLANES, D_MODEL), lambda b, i: (b, jnp.minimum((i + 1) * hb, last), 0)),
        pl.BlockSpec((1, SUBLANES, D_MODEL), lambda b, i: (b, jnp.maximum(i * hb - 1, 0), 0)),
    ]


def _const_spec(shape):
    nd = len(shape)
    return pl.BlockSpec(shape, lambda *_: (0,) * nd)


def _resident_spec(shape):
    nd = len(shape)
    return pl.BlockSpec(shape, lambda *_: (0,) * nd, pipeline_mode=pl.Buffered(1))


def _layer_spec(shape, layer):
    nd = len(shape)
    return pl.BlockSpec((None,) + tuple(shape), lambda *_: (layer,) + (0,) * nd, pipeline_mode=pl.Buffered(1))


def _split_dot(x, p):
    hi = x.astype(BF16)
    lo = (x - hi.astype(F32)).astype(BF16)
    return (jnp.dot(hi, p, preferred_element_type=F32) + jnp.dot(lo, p, preferred_element_type=F32))


def _mod_kernel(cv_ref, w_ref, b_ref, o_ref):
    cv = cv_ref[...]
    o_ref[0, 0] = _split_dot(cv * _sigmoid(cv), w_ref[0].astype(BF16)) + b_ref[0, 0]


def _modulation(cv, ada_w, ada_b):
    depth, _, n = ada_w.shape
    nv = n // D_MODEL
    return pl.pallas_call(
        _mod_kernel,
        out_shape=jax.ShapeDtypeStruct((depth, nv, SUBLANES, D_MODEL), F32),
        grid=(depth, nv),
        in_specs=[
            pl.BlockSpec((SUBLANES, D_MODEL), lambda l, j: (0, 0)),
            pl.BlockSpec((1, D_MODEL, D_MODEL), lambda l, j: (l, 0, j)),
            pl.BlockSpec((1, 1, 1, D_MODEL), lambda l, j: (l, j, 0, 0)),
        ],
        out_specs=pl.BlockSpec((1, 1, SUBLANES, D_MODEL), lambda l, j: (l, j, 0, 0)),
        compiler_params=_cparams(("arbitrary", "arbitrary")),
        name="modulation",
    )(cv, ada_w, ada_b.reshape(depth, nv, 1, D_MODEL))


CAST_STEPS = 8


def _cast_kernel(*refs):
    n = len(refs) // 2
    for src, dst in zip(refs[:n], refs[n:]):
        w = src[...].astype(BF16)
        pad = dst.shape[-1] - src.shape[-1]
        if pad:
            w = jnp.concatenate([w, jnp.zeros(w.shape[:-1] + (pad,), BF16)], axis=-1)
        dst[...] = w


def _cast_weights(ws, widths):
    in_specs, out_specs, out_shape = [], [], []
    for w, wd in zip(ws, widths):
        lead, (r, c) = w.shape[:-2], w.shape[-2:]
        imap = (lambda i: (0, i, 0)) if lead else (lambda i: (i, 0))
        in_specs.append(pl.BlockSpec(lead + (r // CAST_STEPS, c), imap))
        out_specs.append(pl.BlockSpec(lead + (r // CAST_STEPS, wd), imap))
        out_shape.append(jax.ShapeDtypeStruct(lead + (r, wd), BF16))
    return pl.pallas_call(
        _cast_kernel,
        out_shape=tuple(out_shape),
        grid=(CAST_STEPS,),
        in_specs=in_specs,
        out_specs=tuple(out_specs),
        compiler_params=_cparams(("arbitrary",)),
        name="cast_weights",
    )(*ws)


def _even_in_kernel(x_ref, xn_ref, xp_ref, mod_ref, ng_ref, w_ref, cw_ref, tc_ref,
                    yc_ref, zr_ref, zi_ref, *, tm, nt, n2, mrow):
    i = pl.program_id(1)
    hh = _halo_rows(x_ref, xn_ref, xp_ref, _mod_vec(mod_ref, 0, mrow), _mod_vec(mod_ref, 1, mrow), ng_ref)
    u = jnp.dot(hh, w_ref[...], preferred_element_type=F32)
    v = u[:, SC_CH:2 * SC_CH] * u[:, 2 * SC_CH:3 * SC_CH]
    v = jnp.where(_halo_valid(tm, i, nt), v, 0.0)
    yc = u[:tm, :SC_CH] * _conv3(v, cw_ref[...], tm)
    yc_ref[0] = yc.astype(BF16)
    uf = u[:tm, 3 * SC_CH:].astype(BF16)
    tc = tc_ref[...].astype(BF16)
    for g in range(FT_GROUPS):
        sl = slice(g * FT_GROUP_CH, (g + 1) * FT_GROUP_CH)
        ab = jnp.dot(uf[:, sl], tc, preferred_element_type=F32)
        if n2 is None:
            zr_ref[g, 0] = ab[:, :FT_GROUP_CH]
            zi_ref[g, 0] = ab[:, FT_GROUP_CH:]
        else:
            for a in range(tm // n2):
                dst = pl.ds(a, n2, stride=tm // n2)
                zr_ref[g, 0, dst, :] = ab[n2 * a:n2 * (a + 1), :FT_GROUP_CH]
                zi_ref[g, 0, dst, :] = ab[n2 * a:n2 * (a + 1), FT_GROUP_CH:]


def _even_in(x, mod, layer, mrow, ng, w_in, cw, tc, tm, n2):
    B, L, _ = x.shape
    nt = L // tm
    out = jax.ShapeDtypeStruct((B, L, FT_CH), BF16)
    zout = jax.ShapeDtypeStruct((FT_GROUPS, B, L, FT_GROUP_CH), F32)
    ospec = pl.BlockSpec((1, tm, FT_CH), lambda b, i: (b, i, 0))
    zspec = pl.BlockSpec((FT_GROUPS, 1, tm, FT_GROUP_CH), lambda b, i: (0, b, i, 0))
    return pl.pallas_call(
        functools.partial(_even_in_kernel, tm=tm, nt=nt, n2=n2, mrow=mrow),
        out_shape=(out, zout, zout),
        grid=(B, nt),
        in_specs=_halo_specs(tm, L) + [
            _mod_spec(layer),
            _const_spec((1, D_MODEL)),
            _const_spec((D_MODEL, EVEN_IN)),
            _const_spec((3, SC_CH)),
            _const_spec((FT_GROUP_CH, 2 * FT_GROUP_CH)),
        ],
        out_specs=(ospec, zspec, zspec),
        compiler_params=_cparams(("parallel", "arbitrary")),
        name="even_in",
    )(x, x, x, mod, ng, w_in, cw, tc)


def _seq_dft_kernel(zr_ref, zi_ref, m_ref, g_ref, y_ref, o_scr, *, n2, tm):
    m1 = m_ref[...].astype(BF16)
    chunk = tm // n2
    ntile = DFT_N1 // chunk

    def rows(ref, j):
        return [ref[0, 0, t * tm + j * chunk:t * tm + (j + 1) * chunk, :] for t in range(ntile)]

    for j in range(n2):
        z = jnp.concatenate(rows(zr_ref, j) + rows(zi_ref, j), axis=0)
        o_scr[2 * DFT_N1 * j:2 * DFT_N1 * (j + 1), :] = jnp.dot(m1, z.astype(BF16), preferred_element_type=F32)
    for k1 in range(DFT_N1):
        o = jnp.concatenate([o_scr[pl.ds(k1, n2, stride=2 * DFT_N1), :],
                             o_scr[pl.ds(DFT_N1 + k1, n2, stride=2 * DFT_N1), :]], axis=0)
        y_ref[0, pl.ds(k1, n2, stride=DFT_N1), :] = jnp.dot(g_ref[k1].astype(BF16), o.astype(BF16), preferred_element_type=F32)


def _dft_tables(L):
    n2 = L // DFT_N1
    k = np.arange(DFT_N1)
    a = 2.0 * np.pi * ((k[:, None] * k[None, :]) % DFT_N1) / DFT_N1
    er, ei = np.cos(a) / np.sqrt(DFT_N1), -np.sin(a) / np.sqrt(DFT_N1)
    m1 = np.block([[er, -ei], [ei, er]])
    k1 = np.arange(DFT_N1)[:, None, None]
    k2 = np.arange(n2)[None, :, None]
    nn = np.arange(n2)[None, None, :]
    th = 2.0 * np.pi * ((nn * (k1 + DFT_N1 * k2)) % L) / L
    g = np.concatenate([np.cos(th), np.sin(th)], axis=-1) / np.sqrt(n2)
    return jnp.asarray(m1, F32), jnp.asarray(g, F32)


def _channel_dft_table():
    k = np.arange(FT_GROUP_CH)
    a = 2.0 * np.pi * ((k[:, None] * k[None, :]) % FT_GROUP_CH) / FT_GROUP_CH
    t = np.concatenate([np.cos(a), -np.sin(a)], axis=1) / np.sqrt(FT_GROUP_CH)
    return jnp.asarray(t, F32)


def _seq_dft(zr, zi, tm):
    G, B, L, C = zr.shape
    n2 = L // DFT_N1
    m1, g = _dft_tables(L)
    zspec = pl.BlockSpec((1, 1, L, C), lambda b, j: (j, b, 0, 0))
    return pl.pallas_call(
        functools.partial(_seq_dft_kernel, n2=n2, tm=tm),
        out_shape=jax.ShapeDtypeStruct((B, L, G * C), F32),
        grid=(B, G),
        in_specs=[zspec, zspec, _const_spec((2 * DFT_N1, 2 * DFT_N1)), _const_spec((DFT_N1, n2, 2 * n2))],
        out_specs=pl.BlockSpec((1, L, C), lambda b, j: (b, 0, j)),
        scratch_shapes=[pltpu.VMEM((2 * DFT_N1 * n2, LANES), F32)],
        compiler_params=_cparams(("parallel", "arbitrary")),
        name="seq_dft",
    )(zr, zi, m1, g)


def _dense_dft_kernel(zr_ref, zi_ref, t_ref, y_ref):
    z = jnp.concatenate([zr_ref[0, 0], zi_ref[0, 0]], axis=0).astype(BF16)
    y_ref[0] = jnp.dot(t_ref[...].astype(BF16), z, preferred_element_type=F32)


def _dense_seq_dft(zr, zi):
    G, B, L, C = zr.shape
    k = np.arange(L)
    a = 2.0 * np.pi * ((k[:, None] * k[None, :]) % L) / L
    t = jnp.asarray(np.concatenate([np.cos(a), np.sin(a)], axis=1) / np.sqrt(L), F32)
    zspec = pl.BlockSpec((1, 1, L, C), lambda b, j: (j, b, 0, 0))
    return pl.pallas_call(
        _dense_dft_kernel,
        out_shape=jax.ShapeDtypeStruct((B, L, G * C), F32),
        grid=(B, G),
        in_specs=[zspec, zspec, _const_spec((L, 2 * L))],
        out_specs=pl.BlockSpec((1, L, C), lambda b, j: (b, 0, j)),
        compiler_params=_cparams(("arbitrary", "arbitrary")),
        name="dense_seq_dft",
    )(zr, zi, t)


HALO = 16


def _wide_halo_specs(tm, L, width):
    hb = tm // HALO
    last = L // HALO - 1
    return [
        pl.BlockSpec((1, tm, width), lambda b, i: (b, i, 0)),
        pl.BlockSpec((1, HALO, width), lambda b, i: (b, jnp.minimum((i + 1) * hb, last), 0)),
        pl.BlockSpec((1, HALO, width), lambda b, i: (b, jnp.maximum(i * hb - 1, 0), 0)),
    ]


def _circ(t_ref, n_ref, p_ref):
    return jnp.concatenate([t_ref[0], n_ref[0], p_ref[0]], axis=0)


def _mix_ffn_kernel(*refs, tm, nt, odd, mrow):
    n_in = 15 if odd else 9
    x3, rest = refs[:3], refs[3:n_in]
    mod_ref, ng_ref, wo_ref, wu_ref, cw_ref, wd_ref, o_ref, hh_scr, act_scr = refs[n_in:]
    i = pl.program_id(1)
    if odd:
        att, hf, hb, om = (_circ(*rest[k:k + 3]) for k in range(0, 12, 3))
        lhs = jnp.concatenate([att, ((hf + hb) * _sigmoid(om.astype(F32))).astype(BF16)], axis=-1)
    else:
        yc, yf = _circ(*rest[0:3]), _circ(*rest[3:6])
        lhs = jnp.concatenate([yc, yf.astype(BF16)], axis=-1)
    mv = [_mod_vec(mod_ref, k, mrow) for k in range(6)]
    x1 = _circ(*x3) + mv[2] * jnp.dot(lhs, wo_ref[...], preferred_element_type=F32)
    hh_scr[...] = _norm_mod(x1, ng_ref[...], mv[3], mv[4]).astype(BF16)
    row = lax.broadcasted_iota(jnp.int32, (tm + 2 * HALO, 1), 0)
    valid = (row < tm) | ((row < tm + HALO) & (i < nt - 1)) | ((row >= tm + HALO) & (i > 0))
    for c in range(N_FF_CHUNKS):
        lo = c * FF_CHUNK
        g = jnp.dot(hh_scr[...], wu_ref[:, lo:lo + FF_CHUNK], preferred_element_type=F32)
        g = jnp.where(valid, g, 0.0)
        cv = _conv3(g, cw_ref[:, lo:lo + FF_CHUNK], tm)
        val = jnp.dot(hh_scr[:tm, :], wu_ref[:, D_FF + lo:D_FF + lo + FF_CHUNK], preferred_element_type=F32)
        act_scr[:, lo:lo + FF_CHUNK] = (cv * _sigmoid(cv) * val).astype(BF16)
    y = jnp.dot(act_scr[...], wd_ref[...], preferred_element_type=F32)
    o_ref[0] = x1[:tm] + mv[5] * y


def _mix_ffn(x, mixed, mod, layer, mrow, ng, w_out, w_up, cw, w_down, tm, odd):
    B, L, _ = x.shape
    nt = L // tm
    specs = _wide_halo_specs(tm, L, D_MODEL)
    args = [x, x, x]
    for a in mixed:
        specs += _wide_halo_specs(tm, L, a.shape[-1])
        args += [a, a, a]
    return pl.pallas_call(
        functools.partial(_mix_ffn_kernel, tm=tm, nt=nt, odd=odd, mrow=mrow),
        out_shape=jax.ShapeDtypeStruct(x.shape, F32),
        grid=(B, nt),
        in_specs=specs + [
            _mod_spec(layer),
            _const_spec((1, D_MODEL)),
            _resident_spec((D_MODEL, D_MODEL)),
            _layer_spec((D_MODEL, 2 * D_FF), layer),
            _layer_spec((3, D_FF), layer),
            _layer_spec((D_FF, D_MODEL), layer),
        ],
        out_specs=pl.BlockSpec((1, tm, D_MODEL), lambda b, i: (b, i, 0)),
        scratch_shapes=[pltpu.VMEM((tm + 2 * HALO, D_MODEL), BF16),
                        pltpu.VMEM((tm, D_FF), BF16)],
        compiler_params=_cparams(("parallel", "arbitrary")),
        name="odd_mix_ffn" if odd else "even_mix_ffn",
    )(*args, mod, ng, w_out, w_up, cw, w_down)


def _split3(x):
    parts = []
    r = x
    for _ in range(3):
        p = r.astype(BF16)
        parts.append(p)
        r = r - p.astype(F32)
    return parts


def _odd_in_kernel(x_ref, mod_ref, ng_ref, w_ref, pm_ref, gain_ref, cos_ref, sin_ref, gb_ref, tl_ref, tu_ref,
                   qt_ref, kv_ref, vt_ref, qm_ref, kmt_ref, vm_ref, om_ref, gc_ref, gr_ref, *, mrow, tm, nsub):
    for s in range(nsub):
        rs = slice(s * tm, (s + 1) * tm)
        h = _norm_mod(x_ref[0, rs], ng_ref[...], _mod_vec(mod_ref, 0, mrow), _mod_vec(mod_ref, 1, mrow))
        u = jnp.dot(h.astype(BF16), w_ref[...], preferred_element_type=F32)

        uqk = u[:, OQ:OV]
        ms = jnp.dot((uqk * uqk).astype(BF16), pm_ref[...].astype(BF16), preferred_element_type=F32)
        rn = uqk * lax.rsqrt(ms + EPS) * gain_ref[...]
        lane = lax.broadcasted_iota(jnp.int32, (1, LANES), 1)
        first = (lane % 32) < 16
        cos = cos_ref[rs, :]
        sin = sin_ref[rs, :]
        roped = []
        for t in range((OV - OQ) // LANES):
            xt = rn[:, t * LANES:(t + 1) * LANES]
            sw = jnp.where(first, pltpu.roll(xt, LANES - 16, 1), pltpu.roll(xt, 16, 1))
            roped.append(xt * cos + sw * sin)
        for t in range(ATT_Q // LANES):
            qt_ref[0, t * LANES:(t + 1) * LANES, rs] = (roped[t] * (ATT_SCALE * LOG2E)).T.astype(BF16)
        k = roped[ATT_Q // LANES]
        v = u[:, OV:OQM]
        half = LANES // 2
        kv_ref[0, rs, 0:LANES] = k.astype(BF16)
        kv_ref[0, rs, LANES:2 * LANES] = pltpu.roll(k, half, 1).astype(BF16)
        vt_ref[0, :, rs] = v.T.astype(BF16)

        qm_ref[0, rs] = u[:, OQM:OKM].astype(BF16)
        for p in range(ML_QK // LANES):
            km = u[:, OKM + p * LANES:OKM + (p + 1) * LANES] * (ML_QK_DIM ** -0.5)
            kmt_ref[0, p * LANES:(p + 1) * LANES, rs] = km.T.astype(BF16)
        vm_ref[0, rs] = u[:, OVM:OOM].astype(BF16)
        om_ref[0, rs] = u[:, OOM:OG].astype(BF16)

        gt = (u[:, OG:ODD_COLS] + gb_ref[...]).T[:2 * SUBLANES, :]
        row = lax.broadcasted_iota(jnp.int32, (2 * SUBLANES, 1), 0)
        logsig = jnp.minimum(gt, 0.0) - jnp.log(1.0 + jnp.exp(-jnp.abs(gt)))
        parts = _split3(logsig)
        tl, tu = tl_ref[...].astype(BF16), tu_ref[...].astype(BF16)
        cum_f = sum(jnp.dot(p, tu, preferred_element_type=F32) for p in parts)
        cum_b = sum(jnp.dot(p, tl, preferred_element_type=F32) for p in parts)
        sel = row % 8
        gr = jnp.where(sel < 4, gt, jnp.where(row < SUBLANES, cum_f, cum_b))
        gr_ref[0, :, rs] = gr
        gc_ref[0, rs] = jnp.concatenate([gr, jnp.zeros((LANES - 2 * SUBLANES, tm), F32)], axis=0).T


def _chunk_tri(tm):
    i = np.arange(tm)
    same = (i[:, None] // ML_CHUNK) == (i[None, :] // ML_CHUNK)
    tl = same & (i[None, :] <= i[:, None])
    tu = same & (i[None, :] >= i[:, None])
    return jnp.asarray(tl, F32), jnp.asarray(tu, F32)


def _odd_in(x, mod, layer, mrow, ng, w, pm, gain, cos, sin, gb, tm):
    B, L, _ = x.shape
    nsub = 2 if L % (2 * tm) == 0 else 1
    bm = nsub * tm

    def rows(c, dt=BF16):
        return jax.ShapeDtypeStruct((B, L, c), dt), pl.BlockSpec((1, bm, c), lambda b, i: (b, i, 0))

    def cols(c, dt=BF16):
        return jax.ShapeDtypeStruct((B, c, L), dt), pl.BlockSpec((1, c, bm), lambda b, i: (b, 0, i))

    outs = [cols(ATT_Q), rows(2 * LANES), cols(LANES), rows(ML_QK), cols(ML_QK), rows(ML_V), rows(ML_V),
            rows(LANES, F32), cols(2 * SUBLANES, F32)]
    nqk = OV - OQ
    tl, tu = _chunk_tri(tm)
    return pl.pallas_call(
        functools.partial(_odd_in_kernel, mrow=mrow, tm=tm, nsub=nsub),
        out_shape=tuple(o[0] for o in outs),
        grid=(B, L // bm),
        in_specs=[
            pl.BlockSpec((1, bm, D_MODEL), lambda b, i: (b, i, 0)),
            _mod_spec(layer),
            _const_spec((1, D_MODEL)),
            _const_spec((D_MODEL, ODD_COLS)),
            _const_spec((nqk, nqk)),
            _const_spec((1, nqk)),
            pl.BlockSpec((bm, LANES), lambda b, i: (i, 0)),
            pl.BlockSpec((bm, LANES), lambda b, i: (i, 0)),
            _const_spec((1, LANES)),
            _const_spec((tm, tm)),
            _const_spec((tm, tm)),
        ],
        out_specs=tuple(o[1] for o in outs),
        compiler_params=_cparams(("parallel", "arbitrary")),
        name="odd_in",
    )(x, mod, ng, w, pm, gain, cos, sin, gb, tl, tu)


def _rope_tables(L):
    rows = L // GRID_W
    pos = np.stack([np.repeat(np.arange(rows), GRID_W), np.tile(np.arange(GRID_W), rows)]).astype(np.float64)
    axis_dim = HEAD_DIM // 2
    inv_freq = ROPE_THETA ** (-np.arange(0, axis_dim, 2, dtype=np.float64) / axis_dim)
    ang = pos[:, :, None] * inv_freq
    c, sn = np.cos(ang), np.sin(ang)
    cos = np.concatenate([c[0], c[0], c[1], c[1]], axis=-1)
    sin = np.concatenate([-sn[0], sn[0], -sn[1], sn[1]], axis=-1)
    return jnp.asarray(np.tile(cos, (1, 2)), F32), jnp.asarray(np.tile(sin, (1, 2)), F32)


def _att_steps(n, sink_ref, qt_ref, kvp_ref, kvc_ref, kvn_ref, kvx_ref, vtp_ref, vtc_ref, vtn_ref, vtx_ref,
               o_ref, *, nb):
    half = LANES // 2
    lo = lax.broadcasted_iota(jnp.int32, (1, LANES), 1) < half
    zero = jnp.zeros((), BF16)
    cx = kvx_ref.shape[1]
    klocal = ([kvp_ref[0]] + [kvc_ref[0, i * BLOCK:(i + 1) * BLOCK] for i in range(ATT_QB)] + [kvn_ref[0]])
    vlocal = ([vtp_ref[0]] + [vtc_ref[0, :, i * BLOCK:(i + 1) * BLOCK] for i in range(ATT_QB)] + [vtn_ref[0]])
    kctx = [kvx_ref[0, i:i + BLOCK] for i in range(0, cx, BLOCK)]
    vctx = [vtx_ref[0, :, i:i + BLOCK] for i in range(0, cx, BLOCK)]
    ones = jnp.ones((half, BLOCK), BF16)

    kj = lax.broadcasted_iota(jnp.int32, (BLOCK, BLOCK), 0)
    qi = lax.broadcasted_iota(jnp.int32, (BLOCK, BLOCK), 1)

    def twice(x):
        return jnp.concatenate([x, x], axis=1)

    left = lax.broadcasted_iota(jnp.int32, (1, 2 * BLOCK), 1) < BLOCK
    group = ATT_HEADS // ATT_KV_HEADS

    def block(qb):
        blk_id = n * ATT_QB + qb
        kblocks = klocal[qb:qb + 3] + kctx
        vblocks = vlocal[qb:qb + 3] + vctx
        bias = [twice(jnp.where((kj >= qi) & (blk_id > 0), 0.0, NEG_BIG)), None,
                twice(jnp.where((kj <= qi) & (blk_id < nb - 1), 0.0, NEG_BIG))] + [None] * len(kctx)
        for kvh in range(ATT_KV_HEADS):
            t0 = kvh * group // 2
            cols = slice(qb * BLOCK, (qb + 1) * BLOCK)
            qt2 = jnp.concatenate([qt_ref[0, t0 * LANES:(t0 + 1) * LANES, cols],
                                   qt_ref[0, (t0 + 1) * LANES:(t0 + 2) * LANES, cols]], axis=1)
            outs = []
            for par in range(2):
                sk = jnp.where(left, sink_ref[2 * t0 + par], sink_ref[2 * t0 + 2 + par]) * LOG2E
                m = sk
                acc = jnp.zeros((LANES, 2 * BLOCK), F32)
                for blk in range(len(kblocks)):
                    k, ks = kblocks[blk][:, :LANES], kblocks[blk][:, LANES:]
                    kh = ((jnp.where(lo, k, zero), jnp.where(lo, zero, ks)),
                          (jnp.where(lo, ks, zero), jnp.where(lo, zero, k)))[kvh][par]
                    s = jnp.dot(kh, qt2, preferred_element_type=F32)
                    if bias[blk] is not None:
                        s = s + bias[blk]
                    m_new = jnp.maximum(m, jnp.max(s, axis=0, keepdims=True))
                    p = jnp.exp2(s - m_new).astype(BF16)
                    vh = jnp.concatenate([vblocks[blk][kvh * half:(kvh + 1) * half], ones], axis=0)
                    acc = acc * jnp.exp2(m - m_new) + jnp.dot(vh, p, preferred_element_type=F32)
                    m = m_new
                l = acc[half:half + 1, :] + jnp.exp2(sk - m)
                outs.append(acc[:half] * (1.0 / l))
            ot = jnp.concatenate(outs, axis=0)
            rows = slice(qb * BLOCK, (qb + 1) * BLOCK)
            o_ref[0, rows, t0 * LANES:(t0 + 1) * LANES] = ot[:, :BLOCK].T.astype(BF16)
            o_ref[0, rows, (t0 + 1) * LANES:(t0 + 2) * LANES] = ot[:, BLOCK:].T.astype(BF16)

    return [functools.partial(block, qb) for qb in range(ATT_QB)]


ML_GROUP = 8


def _mlstm_steps(j, qf_ref, ktf_ref, vf_ref, gcf_ref, grf_ref, qb_ref, ktb_ref, vb_ref, gcb_ref, grb_ref,
                 s0_ref, m0_ref, *rest, T, nb, with_output):
    if with_output:
        hf_ref, hb_ref, s_ref, m_ref = rest
    else:
        s_ref, m_ref = rest

    @pl.when(j == 0)
    def _():
        s_ref[...] = s0_ref[...]
        m_ref[...] = m0_ref[...]

    ti = lax.broadcasted_iota(jnp.int32, (T, T), 0)
    si = lax.broadcasted_iota(jnp.int32, (T, T), 1)
    masks = (si <= ti, si >= ti)
    top = lax.broadcasted_iota(jnp.int32, (LANES, 1), 0) < (LANES // 2)
    zero = jnp.zeros((), BF16)
    ones = jnp.ones((T, ML_V_DIM), BF16)
    fwd = (qf_ref, ktf_ref, vf_ref, gcf_ref, grf_ref)
    bwd = (qb_ref, ktb_ref, vb_ref, gcb_ref, grb_ref)
    chains = [(b, d, h) for b in range(nb) for d in range(2) for h in range(ML_HEADS)]

    def group(g0):
        grp = chains[g0:g0 + ML_GROUP]
        st = {}
        for (b, d, h) in grp:
            q_ref, kt_ref, v_ref, gc_ref, gr_ref = fwd if d == 0 else bwd
            r = (b * 2 + d) * ML_HEADS + h
            li, lf = 8 * d + h, 8 * d + 4 + h
            gr = gr_ref[b]
            u_row = gr[li:li + 1, :] - gr[lf:lf + 1, :]
            m_prev = m_ref[r:r + 1, 0:1]
            e = dict(r=r, u_row=u_row, m_prev=m_prev,
                     b_end=gr[lf:lf + 1, (T - 1 if d == 0 else 0):(T if d == 0 else 1)],
                     c_end=jnp.maximum(jnp.max(u_row, axis=1, keepdims=True), m_prev))
            if with_output:
                e["mu"] = jnp.where(masks[d], u_row, NEG_BIG)
                c_col = jnp.maximum(jnp.max(e["mu"], axis=1, keepdims=True), m_prev)
                e["cb"] = jnp.broadcast_to(c_col, (T, LANES))
                e["bb"] = jnp.broadcast_to(gc_ref[b, :, lf:lf + 1], (T, LANES))
            st[(b, d, h)] = e
        for (b, d, h) in grp:
            q_ref, kt_ref, v_ref, gc_ref, gr_ref = fwd if d == 0 else bwd
            e = st[(b, d, h)]
            pair = h // 2
            ktp = kt_ref[b, pair * LANES:(pair + 1) * LANES, :]
            e["kth"] = jnp.where(top, ktp, zero) if h % 2 == 0 else jnp.where(top, zero, ktp)
            e["vext"] = jnp.concatenate([v_ref[b, :, h * ML_V_DIM:(h + 1) * ML_V_DIM], ones], axis=1)
            if with_output:
                qp = q_ref[b, :, pair * LANES:(pair + 1) * LANES]
                qk = jnp.dot(qp, e["kth"], preferred_element_type=F32) * jnp.exp(e["mu"] - e["cb"])
                qs = qp.astype(F32) * jnp.exp(e["m_prev"] - e["cb"])
                e["lhs"] = jnp.concatenate([qk.astype(BF16), qs.astype(BF16)], axis=1)
        for (b, d, h) in grp:
            e = st[(b, d, h)]
            e["s_prev"] = s_ref[e["r"]]
            if with_output:
                rhs = jnp.concatenate([e["vext"], e["s_prev"].astype(BF16)], axis=0)
                tot = jnp.dot(e["lhs"], rhs, preferred_element_type=F32)
                floor = jnp.exp(-(e["bb"] + e["cb"]))
                hout = tot[:, :ML_V_DIM] / jnp.maximum(jnp.abs(tot[:, ML_V_DIM:]), floor)
                o_ref = hf_ref if d == 0 else hb_ref
                o_ref[b, :, h * ML_V_DIM:(h + 1) * ML_V_DIM] = hout
        for (b, d, h) in grp:
            e = st[(b, d, h)]
            kw = (e["kth"].astype(F32) * jnp.exp(e["u_row"] - e["c_end"])).astype(BF16)
            upd = jnp.dot(kw, e["vext"], preferred_element_type=F32)
            s_ref[e["r"]] = jnp.exp(e["m_prev"] - e["c_end"]) * e["s_prev"] + upd
            m_ref[e["r"]:e["r"] + 1, :] = jnp.broadcast_to(e["b_end"] + e["c_end"], (1, LANES))

    return [functools.partial(group, g0) for g0 in range(0, len(chains), ML_GROUP)]


def _mlstm_kernel(*refs, T, nb, with_output):
    for step in _mlstm_steps(pl.program_id(0), *refs, T=T, nb=nb, with_output=with_output):
        step()


def _att_mlstm_kernel(*refs, T, nb, nblk, ns):
    j = pl.program_id(0)
    sink_ref, att_in, ml_in = refs[0], refs[1:10], refs[10:22]
    o_ref, ml_out = refs[22], refs[23:]
    att = _att_steps(lax.rem(j, ns), sink_ref, *att_in, o_ref, nb=nblk)
    ml = _mlstm_steps(j, *ml_in, *ml_out, T=T, nb=nb, with_output=True)
    for k in range(max(len(att), len(ml))):
        if k < len(att):
            att[k]()
        if k < len(ml):
            ml[k]()


def _mlstm_specs(B, T, nc):
    up = lambda j: j
    down = lambda j: nc - 1 - j

    def specs(o):
        return [pl.BlockSpec((B, T, ML_QK), lambda j: (0, o(j), 0)),
                pl.BlockSpec((B, ML_QK, T), lambda j: (0, 0, o(j))),
                pl.BlockSpec((B, T, ML_V), lambda j: (0, o(j), 0)),
                pl.BlockSpec((B, T, LANES), lambda j: (0, o(j), 0)),
                pl.BlockSpec((B, 2 * SUBLANES, T), lambda j: (0, 0, o(j)))]

    nchains = B * 2 * ML_HEADS
    s_spec = _const_spec((nchains, LANES, 2 * ML_V_DIM))
    m_spec = _const_spec((nchains, LANES))
    state_shape = [jax.ShapeDtypeStruct((nchains, LANES, 2 * ML_V_DIM), F32),
                   jax.ShapeDtypeStruct((nchains, LANES), F32)]
    h_specs = [pl.BlockSpec((B, T, ML_V), lambda j: (0, up(j), 0)),
               pl.BlockSpec((B, T, ML_V), lambda j: (0, down(j), 0))]
    return specs(up) + specs(down) + [s_spec, m_spec], h_specs, [s_spec, m_spec], state_shape


def _att_mlstm(qt, kv, kvx, vt, vtx, sink, qm, kmt, vm, gc, gr, s0, m0):
    B, _, L = qt.shape
    T = ML_CHUNK
    nc = L // T
    nblk = L // BLOCK
    ns = nblk // ATT_QB
    assert nc == B * ns
    cx = kvx.shape[1]
    kw = kv.shape[2]
    vw = vt.shape[1]
    wide = ATT_QB * BLOCK
    smp = lambda j: j // ns
    cur = lambda j: j % ns
    prev = lambda j: jnp.maximum(cur(j) * ATT_QB - 1, 0)
    nxt = lambda j: jnp.minimum((cur(j) + 1) * ATT_QB, nblk - 1)
    att_specs = [
        pl.BlockSpec(memory_space=pltpu.SMEM),
        pl.BlockSpec((1, ATT_Q, wide), lambda j: (smp(j), 0, cur(j))),
        pl.BlockSpec((1, BLOCK, kw), lambda j: (smp(j), prev(j), 0)),
        pl.BlockSpec((1, wide, kw), lambda j: (smp(j), cur(j), 0)),
        pl.BlockSpec((1, BLOCK, kw), lambda j: (smp(j), nxt(j), 0)),
        pl.BlockSpec((1, cx, kw), lambda j: (smp(j), 0, 0)),
        pl.BlockSpec((1, vw, BLOCK), lambda j: (smp(j), 0, prev(j))),
        pl.BlockSpec((1, vw, wide), lambda j: (smp(j), 0, cur(j))),
        pl.BlockSpec((1, vw, BLOCK), lambda j: (smp(j), 0, nxt(j))),
        pl.BlockSpec((1, vw, cx), lambda j: (smp(j), 0, 0)),
    ]
    ml_in, h_specs, st_specs, st_shape = _mlstm_specs(B, T, nc)
    h_shape = jax.ShapeDtypeStruct((B, L, ML_V), F32)
    att, hf, hb, _, _ = pl.pallas_call(
        functools.partial(_att_mlstm_kernel, T=T, nb=B, nblk=nblk, ns=ns),
        out_shape=(jax.ShapeDtypeStruct((B, L, ATT_Q), BF16), h_shape, h_shape, *st_shape),
        grid=(nc,),
        in_specs=att_specs + ml_in,
        out_specs=(pl.BlockSpec((1, wide, ATT_Q), lambda j: (smp(j), cur(j), 0)), *h_specs, *st_specs),
        compiler_params=_cparams(("arbitrary",)),
        name="attention_mlstm",
    )(sink, qt, kv, kv, kv, kvx, vt, vt, vt, vtx, qm, kmt, vm, gc, gr, qm, kmt, vm, gc, gr, s0, m0)
    return att, hf, hb


def _mlstm(qm, kmt, vm, gc, gr, s0, m0, with_output):
    B, L, _ = qm.shape
    T = ML_CHUNK
    nc = L // T
    ml_in, h_specs, st_specs, st_shape = _mlstm_specs(B, T, nc)
    out_shape, out_specs = st_shape, st_specs
    if with_output:
        out_shape = [jax.ShapeDtypeStruct((B, L, ML_V), F32)] * 2 + out_shape
        out_specs = h_specs + out_specs
    return pl.pallas_call(
        functools.partial(_mlstm_kernel, T=T, nb=B, with_output=with_output),
        out_shape=tuple(out_shape),
        grid=(nc,),
        in_specs=ml_in,
        out_specs=tuple(out_specs),
        compiler_params=_cparams(("arbitrary",)),
        name="mlstm_scan" if with_output else "mlstm_context_state",
    )(qm, kmt, vm, gc, gr, qm, kmt, vm, gc, gr, s0, m0)


def _odd_tables(gate_b, q_g, k_g):
    assert sum([ATT_Q, ATT_KV, ATT_KV, ML_QK, ML_QK, ML_V, ML_V]) == OG
    nqk = OV - OQ
    head = np.arange(nqk) // HEAD_DIM
    pm = jnp.asarray((head[:, None] == head[None, :]) / HEAD_DIM, F32)
    gain = jnp.concatenate([jnp.tile(q_g, ATT_HEADS), jnp.tile(k_g, ATT_KV_HEADS)])[None, :]
    gb = jnp.pad(gate_b.reshape(1, -1), ((0, 0), (0, LANES - gate_b.size)))
    return pm, gain, gb


def kernel(x, c, ctx, c_ctx, ada_w, ada_b, norm_g, even_w_in, even_conv, even_w_out, odd_w_in, odd_gate_b,
           odd_q_g, odd_k_g, odd_sink, odd_w_out, ffn_w_up, ffn_conv, ffn_w_down):
    B, L, _ = x.shape
    C = ctx.shape[1]
    depth = ada_w.shape[0]
    assert depth == 2 and L % (DFT_N1 * SUBLANES) == 0 and C % ML_CHUNK == 0

    cv = jnp.concatenate([c, c_ctx[None, :], jnp.zeros((SUBLANES - B - 1, D_MODEL), F32)], axis=0)
    mod = _modulation(cv, ada_w, ada_b)
    lat, cx = None, B

    tm = min(512, L)
    tc = _channel_dft_table()
    w_up, w_down, w_in0, w_out0, w_in1, w_out1 = _cast_weights(
        (ffn_w_up, ffn_w_down, even_w_in[0], even_w_out[0], odd_w_in[0], odd_w_out[0]),
        (2 * D_FF, D_MODEL, EVEN_IN, D_MODEL, ODD_COLS, D_MODEL))
    ffn_w = (w_up, ffn_conv, w_down)

    ng00, ng01 = norm_g[0, 0][None, :], norm_g[0, 1][None, :]

    def even_layer(xs, mrow, tile, n2):
        yc, zr, zi = _even_in(xs, mod, 0, mrow, ng00, w_in0, even_conv[0], tc, tile, n2)
        yf = _dense_seq_dft(zr, zi) if n2 is None else _seq_dft(zr, zi, tile)
        return _mix_ffn(xs, (yc, yf), mod, 0, mrow, ng01, w_out0, *ffn_w, tile, odd=False)

    xl = even_layer(x, lat, tm, L // DFT_N1)
    xc = even_layer(ctx, cx, C, None)

    pm, gain, gb = _odd_tables(odd_gate_b[0], odd_q_g[0], odd_k_g[0])
    ng10, ng11 = norm_g[1, 0][None, :], norm_g[1, 1][None, :]
    cos, sin = _rope_tables(L)
    one, nil = jnp.ones((C, LANES), F32), jnp.zeros((C, LANES), F32)
    qt, kv, vt, qm, kmt, vm, om, gc, gr = _odd_in(xl, mod, 1, lat, ng10, w_in1, pm, gain, cos, sin, gb, tm)
    _, kvx, vtx, qmx, kmtx, vmx, _, gcx, grx = _odd_in(xc, mod, 1, cx, ng10, w_in1, pm, gain, one, nil, gb, C)

    nchains = B * 2 * ML_HEADS
    s0 = jnp.zeros((nchains, LANES, 2 * ML_V_DIM), F32)
    m0 = jnp.zeros((nchains, LANES), F32)
    s1, m1 = _mlstm(qmx, kmtx, vmx, gcx, grx, s0, m0, with_output=False)
    att, hf, hb = _att_mlstm(qt, kv, kvx, vt, vtx, odd_sink[0], qm, kmt, vm, gc, gr, s1, m1)
    return _mix_ffn(xl, (att, hf, hb, om), mod, 1, lat, ng11, w_out1, *ffn_w, tm, odd=True)
```

```python
import functools

import numpy as np
import jax
import jax.numpy as jnp
from jax import lax
from jax.experimental import pallas as pl
from jax.experimental.pallas import tpu as pltpu

F32 = jnp.float32
BF16 = jnp.bfloat16

D_MODEL = 1024
GRID_W = 64
EPS = 1e-6
SC_CH = 512
FT_CH = 512
FT_GROUPS = 4
FT_GROUP_CH = FT_CH // FT_GROUPS
EVEN_IN = 3 * SC_CH + FT_CH
ATT_HEADS = 8
ATT_KV_HEADS = 2
HEAD_DIM = 64
ATT_SCALE = HEAD_DIM ** -0.5
WINDOW = 128
BLOCK = 128
ROPE_THETA = 10000.0
ML_HEADS = 4
ML_QK_DIM = 64
ML_V_DIM = 128
ATT_Q = ATT_HEADS * HEAD_DIM
ATT_KV = ATT_KV_HEADS * HEAD_DIM
ML_QK = ML_HEADS * ML_QK_DIM
ML_V = ML_HEADS * ML_V_DIM
D_FF = 2816

LANES = 128
SUBLANES = 8
VMEM_LIMIT_BYTES = 56 * 1024 * 1024

DFT_N1 = 128
FF_CHUNK = 256
N_FF_CHUNKS = D_FF // FF_CHUNK
ML_CHUNK = 128
ATT_QB = 2
NEG_BIG = -1e30
LOG2E = 1.4426950408889634

OQ, OK_, OV, OQM, OKM, OVM, OOM, OG = 0, 512, 640, 768, 1024, 1280, 1792, 2304
ODD_COLS = OG + LANES


def _cparams(sem):
    return pltpu.CompilerParams(dimension_semantics=sem, vmem_limit_bytes=VMEM_LIMIT_BYTES)


def _sigmoid(x):
    return 1.0 / (1.0 + jnp.exp(-x))


def _norm_mod(x, g, shift, scale):
    y = x * lax.rsqrt(jnp.mean(x * x, axis=-1, keepdims=True) + EPS)
    return y * g * (1.0 + scale) + shift


def _mod_vec(mod_ref, k, mrow):
    r = pl.program_id(0) if mrow is None else mrow
    return mod_ref[k, pl.ds(r, 1), :]


def _mod_spec(layer):
    return pl.BlockSpec((None, 6, SUBLANES, D_MODEL), lambda *_: (layer, 0, 0, 0))


def _halo_rows(x, xn, xp, shift, scale, ng_ref):
    g = ng_ref[...]
    parts = [_norm_mod(r, g, shift, scale) for r in (x, xn, xp)]
    return jnp.concatenate(parts, axis=0).astype(BF16)


def _halo_valid(tm, i, nt):
    row = lax.broadcasted_iota(jnp.int32, (tm + 2 * SUBLANES, 1), 0)
    return ((row < tm) | ((row < tm + SUBLANES) & (i < nt - 1)) | ((row >= tm + SUBLANES) & (i > 0)))


def _conv3(v, cw, tm):
    n = v.shape[0]
    vp = pltpu.roll(v, 1, 0)[:tm]
    vn = pltpu.roll(v, n - 1, 0)[:tm]
    return vp * cw[0:1] + v[:tm] * cw[1:2] + vn * cw[2:3]


def _halo_specs(tm, L):
    hb = tm // SUBLANES
    last = L // SUBLANES - 1
    return [
        pl.BlockSpec((1, tm, D_MODEL), lambda b, i: (b, i, 0)),
        pl.BlockSpec((1, SUBLANES, D_MODEL), lambda b, i: (b, jnp.minimum((i + 1) * hb, last), 0)),
        pl.BlockSpec((1, SUBLANES, D_MODEL), lambda b, i: (b, jnp.maximum(i * hb - 1, 0), 0)),
    ]


def _const_spec(shape):
    nd = len(shape)
    return pl.BlockSpec(shape, lambda *_: (0,) * nd)


def _resident_spec(shape):
    nd = len(shape)
    return pl.BlockSpec(shape, lambda *_: (0,) * nd, pipeline_mode=pl.Buffered(1))


def _layer_spec(shape, layer):
    nd = len(shape)
    return pl.BlockSpec((None,) + tuple(shape), lambda *_: (layer,) + (0,) * nd, pipeline_mode=pl.Buffered(1))


def _split_dot(x, p):
    hi = x.astype(BF16)
    lo = (x - hi.astype(F32)).astype(BF16)
    return (jnp.dot(hi, p, preferred_element_type=F32) + jnp.dot(lo, p, preferred_element_type=F32))


def _mod_kernel(cv_ref, w_ref, b_ref, o_ref):
    cv = cv_ref[...]
    o_ref[0, 0] = _split_dot(cv * _sigmoid(cv), w_ref[0].astype(BF16)) + b_ref[0, 0]


def _modulation(cv, ada_w, ada_b):
    depth, _, n = ada_w.shape
    nv = n // D_MODEL
    return pl.pallas_call(
        _mod_kernel,
        out_shape=jax.ShapeDtypeStruct((depth, nv, SUBLANES, D_MODEL), F32),
        grid=(depth, nv),
        in_specs=[
            pl.BlockSpec((SUBLANES, D_MODEL), lambda l, j: (0, 0)),
            pl.BlockSpec((1, D_MODEL, D_MODEL), lambda l, j: (l, 0, j)),
            pl.BlockSpec((1, 1, 1, D_MODEL), lambda l, j: (l, j, 0, 0)),
        ],
        out_specs=pl.BlockSpec((1, 1, SUBLANES, D_MODEL), lambda l, j: (l, j, 0, 0)),
        compiler_params=_cparams(("arbitrary", "arbitrary")),
        name="modulation",
    )(cv, ada_w, ada_b.reshape(depth, nv, 1, D_MODEL))


CAST_STEPS = 8


def _cast_kernel(*refs):
    n = len(refs) // 2
    for src, dst in zip(refs[:n], refs[n:]):
        w = src[...].astype(BF16)
        pad = dst.shape[-1] - src.shape[-1]
        if pad:
            w = jnp.concatenate([w, jnp.zeros(w.shape[:-1] + (pad,), BF16)], axis=-1)
        dst[...] = w


def _cast_weights(ws, widths):
    in_specs, out_specs, out_shape = [], [], []
    for w, wd in zip(ws, widths):
        lead, (r, c) = w.shape[:-2], w.shape[-2:]
        imap = (lambda i: (0, i, 0)) if lead else (lambda i: (i, 0))
        in_specs.append(pl.BlockSpec(lead + (r // CAST_STEPS, c), imap))
        out_specs.append(pl.BlockSpec(lead + (r // CAST_STEPS, wd), imap))
        out_shape.append(jax.ShapeDtypeStruct(lead + (r, wd), BF16))
    return pl.pallas_call(
        _cast_kernel,
        out_shape=tuple(out_shape),
        grid=(CAST_STEPS,),
        in_specs=in_specs,
        out_specs=tuple(out_specs),
        compiler_params=_cparams(("arbitrary",)),
        name="cast_weights",
    )(*ws)


def _even_in_kernel(x_ref, xn_ref, xp_ref, mod_ref, ng_ref, w_ref, cw_ref, tc_ref,
                    yc_ref, zr_ref, zi_ref, *, tm, nt, n2, mrow, nsub):
    i = pl.program_id(1)
    shift, scale = _mod_vec(mod_ref, 0, mrow), _mod_vec(mod_ref, 1, mrow)
    tc = tc_ref[...].astype(BF16)
    for s in range(nsub):
        lo = s * tm
        x = x_ref[0, lo:lo + tm]
        xn = xn_ref[0] if s == nsub - 1 else x_ref[0, lo + tm:lo + tm + SUBLANES]
        xp = xp_ref[0] if s == 0 else x_ref[0, lo - SUBLANES:lo]
        hh = _halo_rows(x, xn, xp, shift, scale, ng_ref)
        u = jnp.dot(hh, w_ref[...], preferred_element_type=F32)
        v = u[:, SC_CH:2 * SC_CH] * u[:, 2 * SC_CH:3 * SC_CH]
        v = jnp.where(_halo_valid(tm, i * nsub + s, nt), v, 0.0)
        yc = u[:tm, :SC_CH] * _conv3(v, cw_ref[...], tm)
        yc_ref[0, lo:lo + tm] = yc.astype(BF16)
        uf = u[:tm, 3 * SC_CH:].astype(BF16)
        for g in range(FT_GROUPS):
            sl = slice(g * FT_GROUP_CH, (g + 1) * FT_GROUP_CH)
            ab = jnp.dot(uf[:, sl], tc, preferred_element_type=F32)
            if n2 is None:
                zr_ref[g, 0, lo:lo + tm] = ab[:, :FT_GROUP_CH]
                zi_ref[g, 0, lo:lo + tm] = ab[:, FT_GROUP_CH:]
            else:
                for a in range(tm // n2):
                    dst = pl.ds(lo + a, n2, stride=tm // n2)
                    zr_ref[g, 0, dst, :] = ab[n2 * a:n2 * (a + 1), :FT_GROUP_CH]
                    zi_ref[g, 0, dst, :] = ab[n2 * a:n2 * (a + 1), FT_GROUP_CH:]


def _even_in(x, mod, layer, mrow, ng, w_in, cw, tc, tm, n2):
    B, L, _ = x.shape
    nt = L // tm
    nsub = 2 if nt % 2 == 0 else 1
    bm = nsub * tm
    out = jax.ShapeDtypeStruct((B, L, FT_CH), BF16)
    zout = jax.ShapeDtypeStruct((FT_GROUPS, B, L, FT_GROUP_CH), F32)
    ospec = pl.BlockSpec((1, bm, FT_CH), lambda b, i: (b, i, 0))
    zspec = pl.BlockSpec((FT_GROUPS, 1, bm, FT_GROUP_CH), lambda b, i: (0, b, i, 0))
    return pl.pallas_call(
        functools.partial(_even_in_kernel, tm=tm, nt=nt, n2=n2, mrow=mrow, nsub=nsub),
        out_shape=(out, zout, zout),
        grid=(B, nt // nsub),
        in_specs=_halo_specs(bm, L) + [
            _mod_spec(layer),
            _const_spec((1, D_MODEL)),
            _const_spec((D_MODEL, EVEN_IN)),
            _const_spec((3, SC_CH)),
            _const_spec((FT_GROUP_CH, 2 * FT_GROUP_CH)),
        ],
        out_specs=(ospec, zspec, zspec),
        compiler_params=_cparams(("parallel", "arbitrary")),
        name="even_in",
    )(x, x, x, mod, ng, w_in, cw, tc)


def _seq_dft_kernel(zr_ref, zi_ref, m_ref, g_ref, y_ref, o_scr, *, n2, tm):
    m1 = m_ref[...].astype(BF16)
    chunk = tm // n2
    ntile = DFT_N1 // chunk

    def rows(ref, j):
        return [ref[0, 0, t * tm + j * chunk:t * tm + (j + 1) * chunk, :] for t in range(ntile)]

    for j in range(n2):
        z = jnp.concatenate(rows(zr_ref, j) + rows(zi_ref, j), axis=0)
        o_scr[2 * DFT_N1 * j:2 * DFT_N1 * (j + 1), :] = jnp.dot(m1, z.astype(BF16), preferred_element_type=F32)
    for k1 in range(DFT_N1):
        o = jnp.concatenate([o_scr[pl.ds(k1, n2, stride=2 * DFT_N1), :],
                             o_scr[pl.ds(DFT_N1 + k1, n2, stride=2 * DFT_N1), :]], axis=0)
        y_ref[0, pl.ds(k1, n2, stride=DFT_N1), :] = jnp.dot(g_ref[k1].astype(BF16), o.astype(BF16), preferred_element_type=F32)


def _dft_tables(L):
    n2 = L // DFT_N1
    k = np.arange(DFT_N1)
    a = 2.0 * np.pi * ((k[:, None] * k[None, :]) % DFT_N1) / DFT_N1
    er, ei = np.cos(a) / np.sqrt(DFT_N1), -np.sin(a) / np.sqrt(DFT_N1)
    m1 = np.block([[er, -ei], [ei, er]])
    k1 = np.arange(DFT_N1)[:, None, None]
    k2 = np.arange(n2)[None, :, None]
    nn = np.arange(n2)[None, None, :]
    th = 2.0 * np.pi * ((nn * (k1 + DFT_N1 * k2)) % L) / L
    g = np.concatenate([np.cos(th), np.sin(th)], axis=-1) / np.sqrt(n2)
    return jnp.asarray(m1, F32), jnp.asarray(g, F32)


def _channel_dft_table():
    k = np.arange(FT_GROUP_CH)
    a = 2.0 * np.pi * ((k[:, None] * k[None, :]) % FT_GROUP_CH) / FT_GROUP_CH
    t = np.concatenate([np.cos(a), -np.sin(a)], axis=1) / np.sqrt(FT_GROUP_CH)
    return jnp.asarray(t, F32)


def _seq_dft(zr, zi, tm):
    G, B, L, C = zr.shape
    n2 = L // DFT_N1
    m1, g = _dft_tables(L)
    zspec = pl.BlockSpec((1, 1, L, C), lambda b, j: (j, b, 0, 0))
    return pl.pallas_call(
        functools.partial(_seq_dft_kernel, n2=n2, tm=tm),
        out_shape=jax.ShapeDtypeStruct((B, L, G * C), F32),
        grid=(B, G),
        in_specs=[zspec, zspec, _const_spec((2 * DFT_N1, 2 * DFT_N1)), _const_spec((DFT_N1, n2, 2 * n2))],
        out_specs=pl.BlockSpec((1, L, C), lambda b, j: (b, 0, j)),
        scratch_shapes=[pltpu.VMEM((2 * DFT_N1 * n2, LANES), F32)],
        compiler_params=_cparams(("parallel", "arbitrary")),
        name="seq_dft",
    )(zr, zi, m1, g)


def _dense_dft_kernel(zr_ref, zi_ref, t_ref, y_ref):
    z = jnp.concatenate([zr_ref[0, 0], zi_ref[0, 0]], axis=0).astype(BF16)
    y_ref[0] = jnp.dot(t_ref[...].astype(BF16), z, preferred_element_type=F32)


def _dense_seq_dft(zr, zi):
    G, B, L, C = zr.shape
    k = np.arange(L)
    a = 2.0 * np.pi * ((k[:, None] * k[None, :]) % L) / L
    t = jnp.asarray(np.concatenate([np.cos(a), np.sin(a)], axis=1) / np.sqrt(L), F32)
    zspec = pl.BlockSpec((1, 1, L, C), lambda b, j: (j, b, 0, 0))
    return pl.pallas_call(
        _dense_dft_kernel,
        out_shape=jax.ShapeDtypeStruct((B, L, G * C), F32),
        grid=(B, G),
        in_specs=[zspec, zspec, _const_spec((L, 2 * L))],
        out_specs=pl.BlockSpec((1, L, C), lambda b, j: (b, 0, j)),
        compiler_params=_cparams(("arbitrary", "arbitrary")),
        name="dense_seq_dft",
    )(zr, zi, t)


HALO = 16


def _wide_halo_specs(tm, L, width):
    hb = tm // HALO
    last = L // HALO - 1
    return [
        pl.BlockSpec((1, tm, width), lambda b, i: (b, i, 0)),
        pl.BlockSpec((1, HALO, width), lambda b, i: (b, jnp.minimum((i + 1) * hb, last), 0)),
        pl.BlockSpec((1, HALO, width), lambda b, i: (b, jnp.maximum(i * hb - 1, 0), 0)),
    ]


def _circ(t_ref, n_ref, p_ref):
    return jnp.concatenate([t_ref[0], n_ref[0], p_ref[0]], axis=0)


def _mix_ffn_kernel(*refs, tm, nt, odd, mrow):
    n_in = 15 if odd else 9
    x3, rest = refs[:3], refs[3:n_in]
    mod_ref, ng_ref, wo_ref, wu_ref, cw_ref, wd_ref, o_ref, hh_scr, act_scr = refs[n_in:]
    i = pl.program_id(1)
    if odd:
        att, hf, hb, om = (_circ(*rest[k:k + 3]) for k in range(0, 12, 3))
        lhs = jnp.concatenate([att, ((hf + hb) * _sigmoid(om.astype(F32))).astype(BF16)], axis=-1)
    else:
        yc, yf = _circ(*rest[0:3]), _circ(*rest[3:6])
        lhs = jnp.concatenate([yc, yf.astype(BF16)], axis=-1)
    mv = [_mod_vec(mod_ref, k, mrow) for k in range(6)]
    x1 = _circ(*x3) + mv[2] * jnp.dot(lhs, wo_ref[...], preferred_element_type=F32)
    hh_scr[...] = _norm_mod(x1, ng_ref[...], mv[3], mv[4]).astype(BF16)
    row = lax.broadcasted_iota(jnp.int32, (tm + 2 * HALO, 1), 0)
    valid = (row < tm) | ((row < tm + HALO) & (i < nt - 1)) | ((row >= tm + HALO) & (i > 0))
    for c in range(N_FF_CHUNKS):
        lo = c * FF_CHUNK
        g = jnp.dot(hh_scr[...], wu_ref[:, lo:lo + FF_CHUNK], preferred_element_type=F32)
        g = jnp.where(valid, g, 0.0)
        cv = _conv3(g, cw_ref[:, lo:lo + FF_CHUNK], tm)
        val = jnp.dot(hh_scr[:tm, :], wu_ref[:, D_FF + lo:D_FF + lo + FF_CHUNK], preferred_element_type=F32)
        act_scr[:, lo:lo + FF_CHUNK] = (cv * _sigmoid(cv) * val).astype(BF16)
    y = jnp.dot(act_scr[...], wd_ref[...], preferred_element_type=F32)
    o_ref[0] = x1[:tm] + mv[5] * y


def _mix_ffn(x, mixed, mod, layer, mrow, ng, w_out, w_up, cw, w_down, tm, odd):
    B, L, _ = x.shape
    nt = L // tm
    specs = _wide_halo_specs(tm, L, D_MODEL)
    args = [x, x, x]
    for a in mixed:
        specs += _wide_halo_specs(tm, L, a.shape[-1])
        args += [a, a, a]
    return pl.pallas_call(
        functools.partial(_mix_ffn_kernel, tm=tm, nt=nt, odd=odd, mrow=mrow),
        out_shape=jax.ShapeDtypeStruct(x.shape, F32),
        grid=(B, nt),
        in_specs=specs + [
            _mod_spec(layer),
            _const_spec((1, D_MODEL)),
            _resident_spec((D_MODEL, D_MODEL)),
            _layer_spec((D_MODEL, 2 * D_FF), layer),
            _layer_spec((3, D_FF), layer),
            _layer_spec((D_FF, D_MODEL), layer),
        ],
        out_specs=pl.BlockSpec((1, tm, D_MODEL), lambda b, i: (b, i, 0)),
        scratch_shapes=[pltpu.VMEM((tm + 2 * HALO, D_MODEL), BF16),
                        pltpu.VMEM((tm, D_FF), BF16)],
        compiler_params=_cparams(("parallel", "arbitrary")),
        name="odd_mix_ffn" if odd else "even_mix_ffn",
    )(*args, mod, ng, w_out, w_up, cw, w_down)


def _split3(x):
    parts = []
    r = x
    for _ in range(3):
        p = r.astype(BF16)
        parts.append(p)
        r = r - p.astype(F32)
    return parts


def _odd_in_kernel(x_ref, mod_ref, ng_ref, w_ref, pm_ref, gain_ref, cos_ref, sin_ref, gb_ref, tl_ref, tu_ref,
                   qt_ref, kv_ref, vt_ref, qm_ref, kmt_ref, vm_ref, om_ref, gc_ref, gr_ref, *, mrow, tm, nsub):
    for s in range(nsub):
        rs = slice(s * tm, (s + 1) * tm)
        h = _norm_mod(x_ref[0, rs], ng_ref[...], _mod_vec(mod_ref, 0, mrow), _mod_vec(mod_ref, 1, mrow))
        u = jnp.dot(h.astype(BF16), w_ref[...], preferred_element_type=F32)

        uqk = u[:, OQ:OV]
        ms = jnp.dot((uqk * uqk).astype(BF16), pm_ref[...].astype(BF16), preferred_element_type=F32)
        rn = uqk * lax.rsqrt(ms + EPS) * gain_ref[...]
        lane = lax.broadcasted_iota(jnp.int32, (1, LANES), 1)
        first = (lane % 32) < 16
        cos = cos_ref[rs, :]
        sin = sin_ref[rs, :]
        roped = []
        for t in range((OV - OQ) // LANES):
            xt = rn[:, t * LANES:(t + 1) * LANES]
            sw = jnp.where(first, pltpu.roll(xt, LANES - 16, 1), pltpu.roll(xt, 16, 1))
            roped.append(xt * cos + sw * sin)
        for t in range(ATT_Q // LANES):
            qt_ref[0, t * LANES:(t + 1) * LANES, rs] = (roped[t] * (ATT_SCALE * LOG2E)).T.astype(BF16)
        k = roped[ATT_Q // LANES]
        v = u[:, OV:OQM]
        half = LANES // 2
        kv_ref[0, rs, 0:LANES] = k.astype(BF16)
        kv_ref[0, rs, LANES:2 * LANES] = pltpu.roll(k, half, 1).astype(BF16)
        vt_ref[0, :, rs] = v.T.astype(BF16)

        qm_ref[0, rs] = u[:, OQM:OKM].astype(BF16)
        for p in range(ML_QK // LANES):
            km = u[:, OKM + p * LANES:OKM + (p + 1) * LANES] * (ML_QK_DIM ** -0.5)
            kmt_ref[0, p * LANES:(p + 1) * LANES, rs] = km.T.astype(BF16)
        vm_ref[0, rs] = u[:, OVM:OOM].astype(BF16)
        om_ref[0, rs] = u[:, OOM:OG].astype(BF16)

        gt = (u[:, OG:ODD_COLS] + gb_ref[...]).T[:2 * SUBLANES, :]
        row = lax.broadcasted_iota(jnp.int32, (2 * SUBLANES, 1), 0)
        logsig = jnp.minimum(gt, 0.0) - jnp.log(1.0 + jnp.exp(-jnp.abs(gt)))
        parts = _split3(logsig)
        tl, tu = tl_ref[...].astype(BF16), tu_ref[...].astype(BF16)
        cum_f = sum(jnp.dot(p, tu, preferred_element_type=F32) for p in parts)
        cum_b = sum(jnp.dot(p, tl, preferred_element_type=F32) for p in parts)
        sel = row % 8
        gr = jnp.where(sel < 4, gt, jnp.where(row < SUBLANES, cum_f, cum_b)) * LOG2E
        gr_ref[0, :, rs] = gr
        gc_ref[0, rs] = jnp.concatenate([gr, jnp.zeros((LANES - 2 * SUBLANES, tm), F32)], axis=0).T


def _chunk_tri(tm):
    i = np.arange(tm)
    same = (i[:, None] // ML_CHUNK) == (i[None, :] // ML_CHUNK)
    tl = same & (i[None, :] <= i[:, None])
    tu = same & (i[None, :] >= i[:, None])
    return jnp.asarray(tl, F32), jnp.asarray(tu, F32)


def _odd_in(x, mod, layer, mrow, ng, w, pm, gain, cos, sin, gb, tm):
    B, L, _ = x.shape
    nsub = 2 if L % (2 * tm) == 0 else 1
    bm = nsub * tm

    def rows(c, dt=BF16):
        return jax.ShapeDtypeStruct((B, L, c), dt), pl.BlockSpec((1, bm, c), lambda b, i: (b, i, 0))

    def cols(c, dt=BF16):
        return jax.ShapeDtypeStruct((B, c, L), dt), pl.BlockSpec((1, c, bm), lambda b, i: (b, 0, i))

    outs = [cols(ATT_Q), rows(2 * LANES), cols(LANES), rows(ML_QK), cols(ML_QK), rows(ML_V), rows(ML_V),
            rows(LANES, F32), cols(2 * SUBLANES, F32)]
    nqk = OV - OQ
    tl, tu = _chunk_tri(tm)
    return pl.pallas_call(
        functools.partial(_odd_in_kernel, mrow=mrow, tm=tm, nsub=nsub),
        out_shape=tuple(o[0] for o in outs),
        grid=(B, L // bm),
        in_specs=[
            pl.BlockSpec((1, bm, D_MODEL), lambda b, i: (b, i, 0)),
            _mod_spec(layer),
            _const_spec((1, D_MODEL)),
            _const_spec((D_MODEL, ODD_COLS)),
            _const_spec((nqk, nqk)),
            _const_spec((1, nqk)),
            pl.BlockSpec((bm, LANES), lambda b, i: (i, 0)),
            pl.BlockSpec((bm, LANES), lambda b, i: (i, 0)),
            _const_spec((1, LANES)),
            _const_spec((tm, tm)),
            _const_spec((tm, tm)),
        ],
        out_specs=tuple(o[1] for o in outs),
        compiler_params=_cparams(("parallel", "arbitrary")),
        name="odd_in",
    )(x, mod, ng, w, pm, gain, cos, sin, gb, tl, tu)


def _rope_tables(L):
    rows = L // GRID_W
    pos = np.stack([np.repeat(np.arange(rows), GRID_W), np.tile(np.arange(GRID_W), rows)]).astype(np.float64)
    axis_dim = HEAD_DIM // 2
    inv_freq = ROPE_THETA ** (-np.arange(0, axis_dim, 2, dtype=np.float64) / axis_dim)
    ang = pos[:, :, None] * inv_freq
    c, sn = np.cos(ang), np.sin(ang)
    cos = np.concatenate([c[0], c[0], c[1], c[1]], axis=-1)
    sin = np.concatenate([-sn[0], sn[0], -sn[1], sn[1]], axis=-1)
    return jnp.asarray(np.tile(cos, (1, 2)), F32), jnp.asarray(np.tile(sin, (1, 2)), F32)


def _att_steps(n, sink_ref, qt_ref, kvp_ref, kvc_ref, kvn_ref, kvx_ref, vtp_ref, vtc_ref, vtn_ref, vtx_ref,
               o_ref, *, nb):
    half = LANES // 2
    lo = lax.broadcasted_iota(jnp.int32, (1, LANES), 1) < half
    zero = jnp.zeros((), BF16)
    cx = kvx_ref.shape[1]
    klocal = ([kvp_ref[0]] + [kvc_ref[0, i * BLOCK:(i + 1) * BLOCK] for i in range(ATT_QB)] + [kvn_ref[0]])
    vlocal = ([vtp_ref[0]] + [vtc_ref[0, :, i * BLOCK:(i + 1) * BLOCK] for i in range(ATT_QB)] + [vtn_ref[0]])
    kctx = [kvx_ref[0, i:i + BLOCK] for i in range(0, cx, BLOCK)]
    vctx = [vtx_ref[0, :, i:i + BLOCK] for i in range(0, cx, BLOCK)]
    ones = jnp.ones((half, BLOCK), BF16)

    kj = lax.broadcasted_iota(jnp.int32, (BLOCK, BLOCK), 0)
    qi = lax.broadcasted_iota(jnp.int32, (BLOCK, BLOCK), 1)

    def twice(x):
        return jnp.concatenate([x, x], axis=1)

    left = lax.broadcasted_iota(jnp.int32, (1, 2 * BLOCK), 1) < BLOCK
    group = ATT_HEADS // ATT_KV_HEADS

    def block(qb):
        blk_id = n * ATT_QB + qb
        kblocks = klocal[qb:qb + 3] + kctx
        vblocks = vlocal[qb:qb + 3] + vctx
        bias = [twice(jnp.where((kj >= qi) & (blk_id > 0), 0.0, NEG_BIG)).astype(BF16), None,
                twice(jnp.where((kj <= qi) & (blk_id < nb - 1), 0.0, NEG_BIG)).astype(BF16)] + [None] * len(kctx)
        for kvh in range(ATT_KV_HEADS):
            t0 = kvh * group // 2
            cols = slice(qb * BLOCK, (qb + 1) * BLOCK)
            qt2 = jnp.concatenate([qt_ref[0, t0 * LANES:(t0 + 1) * LANES, cols],
                                   qt_ref[0, (t0 + 1) * LANES:(t0 + 2) * LANES, cols]], axis=1)
            outs = []
            for par in range(2):
                sk = jnp.where(left, sink_ref[2 * t0 + par], sink_ref[2 * t0 + 2 + par]) * LOG2E
                m = sk.astype(BF16).astype(F32)
                acc = jnp.zeros((LANES, 2 * BLOCK), F32)
                for blk in range(len(kblocks)):
                    k, ks = kblocks[blk][:, :LANES], kblocks[blk][:, LANES:]
                    kh = ((jnp.where(lo, k, zero), jnp.where(lo, zero, ks)),
                          (jnp.where(lo, ks, zero), jnp.where(lo, zero, k)))[kvh][par]
                    s = jnp.dot(kh, qt2, preferred_element_type=F32)
                    s = s.astype(BF16)
                    if bias[blk] is not None:
                        s = s + bias[blk]
                    m_new = jnp.maximum(m, jnp.max(s, axis=0, keepdims=True).astype(F32))
                    p = jnp.exp2(s - m_new.astype(BF16))
                    vh = jnp.concatenate([vblocks[blk][kvh * half:(kvh + 1) * half], ones], axis=0)
                    acc = acc * jnp.exp2(m - m_new) + jnp.dot(vh, p, preferred_element_type=F32)
                    m = m_new
                l = acc[half:half + 1, :] + jnp.exp2(sk - m)
                outs.append(acc[:half] * (1.0 / l))
            ot = jnp.concatenate(outs, axis=0)
            rows = slice(qb * BLOCK, (qb + 1) * BLOCK)
            o_ref[0, rows, t0 * LANES:(t0 + 1) * LANES] = ot[:, :BLOCK].T.astype(BF16)
            o_ref[0, rows, (t0 + 1) * LANES:(t0 + 2) * LANES] = ot[:, BLOCK:].T.astype(BF16)

    return [functools.partial(block, qb) for qb in range(ATT_QB)]


ML_GROUP = 8


def _mlstm_steps(j, qf_ref, ktf_ref, vf_ref, gcf_ref, grf_ref, qb_ref, ktb_ref, vb_ref, gcb_ref, grb_ref,
                 s0_ref, m0_ref, *rest, T, nb, with_output):
    if with_output:
        hf_ref, hb_ref, s_ref, m_ref = rest
    else:
        s_ref, m_ref = rest

    @pl.when(j == 0)
    def _():
        s_ref[...] = s0_ref[...]
        m_ref[...] = m0_ref[...]

    ti = lax.broadcasted_iota(jnp.int32, (T, T), 0)
    si = lax.broadcasted_iota(jnp.int32, (T, T), 1)
    masks = (si <= ti, si >= ti)
    top = lax.broadcasted_iota(jnp.int32, (LANES, 1), 0) < (LANES // 2)
    zero = jnp.zeros((), BF16)
    ones = jnp.ones((T, ML_V_DIM), BF16)
    zpad = jnp.zeros((ML_QK_DIM, 2 * ML_V_DIM), BF16)
    fwd = (qf_ref, ktf_ref, vf_ref, gcf_ref, grf_ref)
    bwd = (qb_ref, ktb_ref, vb_ref, gcb_ref, grb_ref)
    chains = [(b, d, h) for b in range(nb) for d in range(2) for h in range(ML_HEADS)]

    def group(g0):
        grp = chains[g0:g0 + ML_GROUP]
        st = {}
        for (b, d, h) in grp:
            q_ref, kt_ref, v_ref, gc_ref, gr_ref = fwd if d == 0 else bwd
            r = (b * 2 + d) * ML_HEADS + h
            li, lf = 8 * d + h, 8 * d + 4 + h
            gr = gr_ref[b]
            u_row = gr[li:li + 1, :] - gr[lf:lf + 1, :]
            m_prev = m_ref[r:r + 1, 0:1]
            e = dict(r=r, u_row=u_row, m_prev=m_prev,
                     b_end=gr[lf:lf + 1, (T - 1 if d == 0 else 0):(T if d == 0 else 1)],
                     c_end=jnp.maximum(jnp.max(u_row, axis=1, keepdims=True), m_prev))
            if with_output:
                e["mu"] = jnp.where(masks[d], u_row, NEG_BIG)
                c_col = jnp.maximum(jnp.max(e["mu"], axis=1, keepdims=True), m_prev)
                e["cb"] = jnp.broadcast_to(c_col, (T, LANES))
                e["bb"] = jnp.broadcast_to(gc_ref[b, :, lf:lf + 1], (T, LANES))
            st[(b, d, h)] = e
        for (b, d, h) in grp:
            q_ref, kt_ref, v_ref, gc_ref, gr_ref = fwd if d == 0 else bwd
            e = st[(b, d, h)]
            pair = h // 2
            ktp = kt_ref[b, pair * LANES:(pair + 1) * LANES, :]
            e["kth"] = ktp[(h % 2) * ML_QK_DIM:(h % 2 + 1) * ML_QK_DIM, :]
            e["vext"] = jnp.concatenate([v_ref[b, :, h * ML_V_DIM:(h + 1) * ML_V_DIM], ones], axis=1)
            if with_output:
                qp = q_ref[b, :, pair * LANES:(pair + 1) * LANES]
                kpad = jnp.where(top, ktp, zero) if h % 2 == 0 else jnp.where(top, zero, ktp)
                qk = jnp.dot(qp, kpad, preferred_element_type=F32) * jnp.exp2(e["mu"] - e["cb"])
                qs = qp.astype(F32) * jnp.exp2(e["m_prev"] - e["cb"])
                e["lhs"] = jnp.concatenate([qk.astype(BF16), qs.astype(BF16)], axis=1)
        for (b, d, h) in grp:
            e = st[(b, d, h)]
            e["s_prev"] = s_ref[e["r"]]
            if with_output:
                sb = e["s_prev"].astype(BF16)
                rhs = jnp.concatenate([e["vext"]] + ([sb, zpad] if h % 2 == 0 else [zpad, sb]), axis=0)
                tot = jnp.dot(e["lhs"], rhs, preferred_element_type=F32)
                floor = jnp.exp2(-(e["bb"] + e["cb"]))
                hout = tot[:, :ML_V_DIM] / jnp.maximum(jnp.abs(tot[:, ML_V_DIM:]), floor)
                o_ref = hf_ref if d == 0 else hb_ref
                o_ref[b, :, h * ML_V_DIM:(h + 1) * ML_V_DIM] = hout
        for (b, d, h) in grp:
            e = st[(b, d, h)]
            kw = (e["kth"].astype(F32) * jnp.exp2(e["u_row"] - e["c_end"])).astype(BF16)
            upd = jnp.dot(kw, e["vext"], preferred_element_type=F32)
            s_ref[e["r"]] = jnp.exp2(e["m_prev"] - e["c_end"]) * e["s_prev"] + upd
            m_ref[e["r"]:e["r"] + 1, :] = jnp.broadcast_to(e["b_end"] + e["c_end"], (1, LANES))

    return [functools.partial(group, g0) for g0 in range(0, len(chains), ML_GROUP)]


def _mlstm_kernel(*refs, T, nb, with_output):
    for step in _mlstm_steps(pl.program_id(0), *refs, T=T, nb=nb, with_output=with_output):
        step()


def _att_mlstm_kernel(*refs, T, nb, nblk, ns):
    j = pl.program_id(0)
    sink_ref, att_in, ml_in = refs[0], refs[1:10], refs[10:22]
    o_ref, ml_out = refs[22], refs[23:]
    att = _att_steps(lax.rem(j, ns), sink_ref, *att_in, o_ref, nb=nblk)
    ml = _mlstm_steps(j, *ml_in, *ml_out, T=T, nb=nb, with_output=True)
    for k in range(max(len(att), len(ml))):
        if k < len(att):
            att[k]()
        if k < len(ml):
            ml[k]()


def _mlstm_specs(B, T, nc):
    up = lambda j: j
    down = lambda j: nc - 1 - j

    def specs(o):
        return [pl.BlockSpec((B, T, ML_QK), lambda j: (0, o(j), 0)),
                pl.BlockSpec((B, ML_QK, T), lambda j: (0, 0, o(j))),
                pl.BlockSpec((B, T, ML_V), lambda j: (0, o(j), 0)),
                pl.BlockSpec((B, T, LANES), lambda j: (0, o(j), 0)),
                pl.BlockSpec((B, 2 * SUBLANES, T), lambda j: (0, 0, o(j)))]

    nchains = B * 2 * ML_HEADS
    s_spec = _const_spec((nchains, ML_QK_DIM, 2 * ML_V_DIM))
    m_spec = _const_spec((nchains, LANES))
    state_shape = [jax.ShapeDtypeStruct((nchains, ML_QK_DIM, 2 * ML_V_DIM), F32),
                   jax.ShapeDtypeStruct((nchains, LANES), F32)]
    h_specs = [pl.BlockSpec((B, T, ML_V), lambda j: (0, up(j), 0)),
               pl.BlockSpec((B, T, ML_V), lambda j: (0, down(j), 0))]
    return specs(up) + specs(down) + [s_spec, m_spec], h_specs, [s_spec, m_spec], state_shape


def _att_mlstm(qt, kv, kvx, vt, vtx, sink, qm, kmt, vm, gc, gr, s0, m0):
    B, _, L = qt.shape
    T = ML_CHUNK
    nc = L // T
    nblk = L // BLOCK
    ns = nblk // ATT_QB
    assert nc == B * ns
    cx = kvx.shape[1]
    kw = kv.shape[2]
    vw = vt.shape[1]
    wide = ATT_QB * BLOCK
    smp = lambda j: j // ns
    cur = lambda j: j % ns
    prev = lambda j: jnp.maximum(cur(j) * ATT_QB - 1, 0)
    nxt = lambda j: jnp.minimum((cur(j) + 1) * ATT_QB, nblk - 1)
    att_specs = [
        pl.BlockSpec(memory_space=pltpu.SMEM),
        pl.BlockSpec((1, ATT_Q, wide), lambda j: (smp(j), 0, cur(j))),
        pl.BlockSpec((1, BLOCK, kw), lambda j: (smp(j), prev(j), 0)),
        pl.BlockSpec((1, wide, kw), lambda j: (smp(j), cur(j), 0)),
        pl.BlockSpec((1, BLOCK, kw), lambda j: (smp(j), nxt(j), 0)),
        pl.BlockSpec((1, cx, kw), lambda j: (smp(j), 0, 0)),
        pl.BlockSpec((1, vw, BLOCK), lambda j: (smp(j), 0, prev(j))),
        pl.BlockSpec((1, vw, wide), lambda j: (smp(j), 0, cur(j))),
        pl.BlockSpec((1, vw, BLOCK), lambda j: (smp(j), 0, nxt(j))),
        pl.BlockSpec((1, vw, cx), lambda j: (smp(j), 0, 0)),
    ]
    ml_in, h_specs, st_specs, st_shape = _mlstm_specs(B, T, nc)
    h_shape = jax.ShapeDtypeStruct((B, L, ML_V), F32)
    att, hf, hb, _, _ = pl.pallas_call(
        functools.partial(_att_mlstm_kernel, T=T, nb=B, nblk=nblk, ns=ns),
        out_shape=(jax.ShapeDtypeStruct((B, L, ATT_Q), BF16), h_shape, h_shape, *st_shape),
        grid=(nc,),
        in_specs=att_specs + ml_in,
        out_specs=(pl.BlockSpec((1, wide, ATT_Q), lambda j: (smp(j), cur(j), 0)), *h_specs, *st_specs),
        compiler_params=_cparams(("arbitrary",)),
        name="attention_mlstm",
    )(sink, qt, kv, kv, kv, kvx, vt, vt, vt, vtx, qm, kmt, vm, gc, gr, qm, kmt, vm, gc, gr, s0, m0)
    return att, hf, hb


def _mlstm(qm, kmt, vm, gc, gr, s0, m0, with_output):
    B, L, _ = qm.shape
    T = ML_CHUNK
    nc = L // T
    ml_in, h_specs, st_specs, st_shape = _mlstm_specs(B, T, nc)
    out_shape, out_specs = st_shape, st_specs
    if with_output:
        out_shape = [jax.ShapeDtypeStruct((B, L, ML_V), F32)] * 2 + out_shape
        out_specs = h_specs + out_specs
    return pl.pallas_call(
        functools.partial(_mlstm_kernel, T=T, nb=B, with_output=with_output),
        out_shape=tuple(out_shape),
        grid=(nc,),
        in_specs=ml_in,
        out_specs=tuple(out_specs),
        compiler_params=_cparams(("arbitrary",)),
        name="mlstm_scan" if with_output else "mlstm_context_state",
    )(qm, kmt, vm, gc, gr, qm, kmt, vm, gc, gr, s0, m0)


def _odd_tables(gate_b, q_g, k_g):
    assert sum([ATT_Q, ATT_KV, ATT_KV, ML_QK, ML_QK, ML_V, ML_V]) == OG
    nqk = OV - OQ
    head = np.arange(nqk) // HEAD_DIM
    pm = jnp.asarray((head[:, None] == head[None, :]) / HEAD_DIM, F32)
    gain = jnp.concatenate([jnp.tile(q_g, ATT_HEADS), jnp.tile(k_g, ATT_KV_HEADS)])[None, :]
    gb = jnp.pad(gate_b.reshape(1, -1), ((0, 0), (0, LANES - gate_b.size)))
    return pm, gain, gb


def kernel(x, c, ctx, c_ctx, ada_w, ada_b, norm_g, even_w_in, even_conv, even_w_out, odd_w_in, odd_gate_b,
           odd_q_g, odd_k_g, odd_sink, odd_w_out, ffn_w_up, ffn_conv, ffn_w_down):
    B, L, _ = x.shape
    C = ctx.shape[1]
    depth = ada_w.shape[0]
    assert depth == 2 and L % (DFT_N1 * SUBLANES) == 0 and C % ML_CHUNK == 0

    cv = jnp.concatenate([c, c_ctx[None, :], jnp.zeros((SUBLANES - B - 1, D_MODEL), F32)], axis=0)
    mod = _modulation(cv, ada_w, ada_b)
    lat, cx = None, B

    tm = min(512, L)
    tc = _channel_dft_table()
    w_up, w_down, w_in0, w_out0, w_in1, w_out1 = _cast_weights(
        (ffn_w_up, ffn_w_down, even_w_in[0], even_w_out[0], odd_w_in[0], odd_w_out[0]),
        (2 * D_FF, D_MODEL, EVEN_IN, D_MODEL, ODD_COLS, D_MODEL))
    ffn_w = (w_up, ffn_conv, w_down)

    ng00, ng01 = norm_g[0, 0][None, :], norm_g[0, 1][None, :]

    def even_layer(xs, mrow, tile, n2):
        yc, zr, zi = _even_in(xs, mod, 0, mrow, ng00, w_in0, even_conv[0], tc, tile, n2)
        yf = _dense_seq_dft(zr, zi) if n2 is None else _seq_dft(zr, zi, tile)
        return _mix_ffn(xs, (yc, yf), mod, 0, mrow, ng01, w_out0, *ffn_w, tile, odd=False)

    xl = even_layer(x, lat, tm, L // DFT_N1)
    xc = even_layer(ctx, cx, C, None)

    pm, gain, gb = _odd_tables(odd_gate_b[0], odd_q_g[0], odd_k_g[0])
    ng10, ng11 = norm_g[1, 0][None, :], norm_g[1, 1][None, :]
    cos, sin = _rope_tables(L)
    one, nil = jnp.ones((C, LANES), F32), jnp.zeros((C, LANES), F32)
    qt, kv, vt, qm, kmt, vm, om, gc, gr = _odd_in(xl, mod, 1, lat, ng10, w_in1, pm, gain, cos, sin, gb, tm)
    _, kvx, vtx, qmx, kmtx, vmx, _, gcx, grx = _odd_in(xc, mod, 1, cx, ng10, w_in1, pm, gain, one, nil, gb, C)

    nchains = B * 2 * ML_HEADS
    s0 = jnp.zeros((nchains, ML_QK_DIM, 2 * ML_V_DIM), F32)
    m0 = jnp.zeros((nchains, LANES), F32)
    s1, m1 = _mlstm(qmx, kmtx, vmx, gcx, grx, s0, m0, with_output=False)
    att, hf, hb = _att_mlstm(qt, kv, kvx, vt, vtx, odd_sink[0], qm, kmt, vm, gc, gr, s1, m1)
    return _mix_ffn(xl, (att, hf, hb, om), mod, 1, lat, ng11, w_out1, *ffn_w, tm, odd=True)
```

```python
import functools

import numpy as np
import jax
import jax.numpy as jnp
from jax import lax
from jax.experimental import pallas as pl
from jax.experimental.pallas import tpu as pltpu

F32 = jnp.float32
BF16 = jnp.bfloat16

D_MODEL = 1024
GRID_W = 64
EPS = 1e-6
SC_CH = 512
FT_CH = 512
FT_GROUPS = 4
FT_GROUP_CH = FT_CH // FT_GROUPS
EVEN_IN = 3 * SC_CH + FT_CH
ATT_HEADS = 8
ATT_KV_HEADS = 2
HEAD_DIM = 64
ATT_SCALE = HEAD_DIM ** -0.5
WINDOW = 128
BLOCK = 128
ROPE_THETA = 10000.0
ML_HEADS = 4
ML_QK_DIM = 64
ML_V_DIM = 128
ATT_Q = ATT_HEADS * HEAD_DIM
ATT_KV = ATT_KV_HEADS * HEAD_DIM
ML_QK = ML_HEADS * ML_QK_DIM
ML_V = ML_HEADS * ML_V_DIM
D_FF = 2816

LANES = 128
SUBLANES = 8
VMEM_LIMIT_BYTES = 56 * 1024 * 1024

DFT_N1 = 128
FF_CHUNK = 256
N_FF_CHUNKS = D_FF // FF_CHUNK
ML_CHUNK = 128
ATT_QB = 2
NEG_BIG = -1e30
LOG2E = 1.4426950408889634

OQ, OK_, OV, OQM, OKM, OVM, OOM, OG = 0, 512, 640, 768, 1024, 1280, 1792, 2304
ODD_COLS = OG + LANES


def _cparams(sem):
    return pltpu.CompilerParams(dimension_semantics=sem, vmem_limit_bytes=VMEM_LIMIT_BYTES)


def _sigmoid(x):
    return 1.0 / (1.0 + jnp.exp(-x))


def _norm_mod(x, g, shift, scale):
    y = x * lax.rsqrt(jnp.mean(x * x, axis=-1, keepdims=True) + EPS)
    return y * g * (1.0 + scale) + shift


def _mod_vec(mod_ref, k, mrow):
    r = pl.program_id(0) if mrow is None else mrow
    return mod_ref[k, pl.ds(r, 1), :]


def _mod_spec(layer):
    return pl.BlockSpec((None, 6, SUBLANES, D_MODEL), lambda *_: (layer, 0, 0, 0))


def _halo_rows(x, xn, xp, shift, scale, ng_ref):
    g = ng_ref[...]
    parts = [_norm_mod(r, g, shift, scale) for r in (x, xn, xp)]
    return jnp.concatenate(parts, axis=0).astype(BF16)


def _halo_valid(tm, i, nt):
    row = lax.broadcasted_iota(jnp.int32, (tm + 2 * SUBLANES, 1), 0)
    return ((row < tm) | ((row < tm + SUBLANES) & (i < nt - 1)) | ((row >= tm + SUBLANES) & (i > 0)))


def _conv3(v, cw, tm):
    n = v.shape[0]
    vp = pltpu.roll(v, 1, 0)[:tm]
    vn = pltpu.roll(v, n - 1, 0)[:tm]
    return vp * cw[0:1] + v[:tm] * cw[1:2] + vn * cw[2:3]


def _halo_specs(tm, L):
    hb = tm // SUBLANES
    last = L // SUBLANES - 1
    return [
        pl.BlockSpec((1, tm, D_MODEL), lambda b, i: (b, i, 0)),
        pl.BlockSpec((1, SUBLANES, D_MODEL), lambda b, i: (b, jnp.minimum((i + 1) * hb, last), 0)),
        pl.BlockSpec((1, SUBLANES, D_MODEL), lambda b, i: (b, jnp.maximum(i * hb - 1, 0), 0)),
    ]


def _const_spec(shape):
    nd = len(shape)
    return pl.BlockSpec(shape, lambda *_: (0,) * nd)


def _resident_spec(shape):
    nd = len(shape)
    return pl.BlockSpec(shape, lambda *_: (0,) * nd, pipeline_mode=pl.Buffered(1))


def _weight_spec(w, shape, layer):
    return _resident_spec(shape) if w.ndim == len(shape) else _layer_spec(shape, layer)


def _layer_spec(shape, layer):
    nd = len(shape)
    return pl.BlockSpec((None,) + tuple(shape), lambda *_: (layer,) + (0,) * nd, pipeline_mode=pl.Buffered(1))


def _split_dot(x, p):
    hi = x.astype(BF16)
    lo = (x - hi.astype(F32)).astype(BF16)
    return (jnp.dot(hi, p, preferred_element_type=F32) + jnp.dot(lo, p, preferred_element_type=F32))


def _mod_kernel(cv_ref, w_ref, b_ref, o_ref):
    cv = cv_ref[...]
    o_ref[0, 0] = _split_dot(cv * _sigmoid(cv), w_ref[0].astype(BF16)) + b_ref[0, 0]


def _modulation(cv, ada_w, ada_b):
    depth, _, n = ada_w.shape
    nv = n // D_MODEL
    return pl.pallas_call(
        _mod_kernel,
        out_shape=jax.ShapeDtypeStruct((depth, nv, SUBLANES, D_MODEL), F32),
        grid=(depth, nv),
        in_specs=[
            pl.BlockSpec((SUBLANES, D_MODEL), lambda l, j: (0, 0)),
            pl.BlockSpec((1, D_MODEL, D_MODEL), lambda l, j: (l, 0, j)),
            pl.BlockSpec((1, 1, 1, D_MODEL), lambda l, j: (l, j, 0, 0)),
        ],
        out_specs=pl.BlockSpec((1, 1, SUBLANES, D_MODEL), lambda l, j: (l, j, 0, 0)),
        compiler_params=_cparams(("arbitrary", "arbitrary")),
        name="modulation",
    )(cv, ada_w, ada_b.reshape(depth, nv, 1, D_MODEL))


CAST_STEPS = 8


def _cast_refs(srcs, dsts):
    for src, dst in zip(srcs, dsts):
        w = src[...].astype(BF16)
        pad = dst.shape[-1] - src.shape[-1]
        if pad:
            w = jnp.concatenate([w, jnp.zeros(w.shape[:-1] + (pad,), BF16)], axis=-1)
        dst[...] = w


def _cast_kernel(*refs):
    n = len(refs) // 2
    _cast_refs(refs[:n], refs[n:])


def _cast_weights(ws, widths):
    in_specs, out_specs, out_shape = [], [], []
    for w, wd in zip(ws, widths):
        lead, (r, c) = w.shape[:-2], w.shape[-2:]
        imap = (lambda i: (0, i, 0)) if lead else (lambda i: (i, 0))
        in_specs.append(pl.BlockSpec(lead + (r // CAST_STEPS, c), imap))
        out_specs.append(pl.BlockSpec(lead + (r // CAST_STEPS, wd), imap))
        out_shape.append(jax.ShapeDtypeStruct(lead + (r, wd), BF16))
    return pl.pallas_call(
        _cast_kernel,
        out_shape=tuple(out_shape),
        grid=(CAST_STEPS,),
        in_specs=in_specs,
        out_specs=tuple(out_specs),
        compiler_params=_cparams(("arbitrary",)),
        name="cast_weights",
    )(*ws)


def _even_in_kernel(x_ref, xn_ref, xp_ref, mod_ref, ng_ref, w_ref, cw_ref, tc_ref,
                    yc_ref, zr_ref, zi_ref, *, tm, nt, n2, mrow, nsub):
    i = pl.program_id(1)
    shift, scale = _mod_vec(mod_ref, 0, mrow), _mod_vec(mod_ref, 1, mrow)
    tc = tc_ref[...].astype(BF16)
    for s in range(nsub):
        lo = s * tm
        x = x_ref[0, lo:lo + tm]
        xn = xn_ref[0] if s == nsub - 1 else x_ref[0, lo + tm:lo + tm + SUBLANES]
        xp = xp_ref[0] if s == 0 else x_ref[0, lo - SUBLANES:lo]
        hh = _halo_rows(x, xn, xp, shift, scale, ng_ref)
        u = jnp.dot(hh, w_ref[...], preferred_element_type=F32)
        v = u[:, SC_CH:2 * SC_CH] * u[:, 2 * SC_CH:3 * SC_CH]
        v = jnp.where(_halo_valid(tm, i * nsub + s, nt), v, 0.0)
        yc = u[:tm, :SC_CH] * _conv3(v, cw_ref[...], tm)
        yc_ref[0, lo:lo + tm] = yc.astype(BF16)
        uf = u[:tm, 3 * SC_CH:].astype(BF16)
        for g in range(FT_GROUPS):
            sl = slice(g * FT_GROUP_CH, (g + 1) * FT_GROUP_CH)
            ab = jnp.dot(uf[:, sl], tc, preferred_element_type=F32)
            if n2 is None:
                zr_ref[g, 0, lo:lo + tm] = ab[:, :FT_GROUP_CH]
                zi_ref[g, 0, lo:lo + tm] = ab[:, FT_GROUP_CH:]
            else:
                for a in range(tm // n2):
                    dst = pl.ds(lo + a, n2, stride=tm // n2)
                    zr_ref[g, 0, dst, :] = ab[n2 * a:n2 * (a + 1), :FT_GROUP_CH]
                    zi_ref[g, 0, dst, :] = ab[n2 * a:n2 * (a + 1), FT_GROUP_CH:]


def _even_in(x, mod, layer, mrow, ng, w_in, cw, tc, tm, n2):
    B, L, _ = x.shape
    nt = L // tm
    nsub = 2 if nt % 2 == 0 else 1
    bm = nsub * tm
    out = jax.ShapeDtypeStruct((B, L, FT_CH), BF16)
    zout = jax.ShapeDtypeStruct((FT_GROUPS, B, L, FT_GROUP_CH), F32)
    ospec = pl.BlockSpec((1, bm, FT_CH), lambda b, i: (b, i, 0))
    zspec = pl.BlockSpec((FT_GROUPS, 1, bm, FT_GROUP_CH), lambda b, i: (0, b, i, 0))
    return pl.pallas_call(
        functools.partial(_even_in_kernel, tm=tm, nt=nt, n2=n2, mrow=mrow, nsub=nsub),
        out_shape=(out, zout, zout),
        grid=(B, nt // nsub),
        in_specs=_halo_specs(bm, L) + [
            _mod_spec(layer),
            _const_spec((1, D_MODEL)),
            _const_spec((D_MODEL, EVEN_IN)),
            _const_spec((3, SC_CH)),
            _const_spec((FT_GROUP_CH, 2 * FT_GROUP_CH)),
        ],
        out_specs=(ospec, zspec, zspec),
        compiler_params=_cparams(("parallel", "arbitrary")),
        name="even_in",
    )(x, x, x, mod, ng, w_in, cw, tc)


def _seq_dft_kernel(zr_ref, zi_ref, m_ref, g_ref, y_ref, o_scr, *, n2, tm):
    m1 = m_ref[...].astype(BF16)
    chunk = tm // n2
    ntile = DFT_N1 // chunk

    def rows(ref, j):
        return [ref[0, 0, t * tm + j * chunk:t * tm + (j + 1) * chunk, :] for t in range(ntile)]

    for j in range(n2):
        z = jnp.concatenate(rows(zr_ref, j) + rows(zi_ref, j), axis=0)
        o_scr[2 * DFT_N1 * j:2 * DFT_N1 * (j + 1), :] = jnp.dot(m1, z.astype(BF16), preferred_element_type=F32)
    for k1 in range(DFT_N1):
        o = jnp.concatenate([o_scr[pl.ds(k1, n2, stride=2 * DFT_N1), :],
                             o_scr[pl.ds(DFT_N1 + k1, n2, stride=2 * DFT_N1), :]], axis=0)
        y_ref[0, pl.ds(k1, n2, stride=DFT_N1), :] = jnp.dot(g_ref[k1].astype(BF16), o.astype(BF16), preferred_element_type=F32)


def _dft_tables(L):
    n2 = L // DFT_N1
    k = np.arange(DFT_N1)
    a = 2.0 * np.pi * ((k[:, None] * k[None, :]) % DFT_N1) / DFT_N1
    er, ei = np.cos(a) / np.sqrt(DFT_N1), -np.sin(a) / np.sqrt(DFT_N1)
    m1 = np.block([[er, -ei], [ei, er]])
    k1 = np.arange(DFT_N1)[:, None, None]
    k2 = np.arange(n2)[None, :, None]
    nn = np.arange(n2)[None, None, :]
    th = 2.0 * np.pi * ((nn * (k1 + DFT_N1 * k2)) % L) / L
    g = np.concatenate([np.cos(th), np.sin(th)], axis=-1) / np.sqrt(n2)
    return jnp.asarray(m1, F32), jnp.asarray(g, F32)


def _channel_dft_table():
    k = np.arange(FT_GROUP_CH)
    a = 2.0 * np.pi * ((k[:, None] * k[None, :]) % FT_GROUP_CH) / FT_GROUP_CH
    t = np.concatenate([np.cos(a), -np.sin(a)], axis=1) / np.sqrt(FT_GROUP_CH)
    return jnp.asarray(t, F32)


def _seq_dft(zr, zi, tm):
    G, B, L, C = zr.shape
    n2 = L // DFT_N1
    m1, g = _dft_tables(L)
    zspec = pl.BlockSpec((1, 1, L, C), lambda b, j: (j, b, 0, 0))
    return pl.pallas_call(
        functools.partial(_seq_dft_kernel, n2=n2, tm=tm),
        out_shape=jax.ShapeDtypeStruct((B, L, G * C), F32),
        grid=(B, G),
        in_specs=[zspec, zspec, _const_spec((2 * DFT_N1, 2 * DFT_N1)), _const_spec((DFT_N1, n2, 2 * n2))],
        out_specs=pl.BlockSpec((1, L, C), lambda b, j: (b, 0, j)),
        scratch_shapes=[pltpu.VMEM((2 * DFT_N1 * n2, LANES), F32)],
        compiler_params=_cparams(("parallel", "arbitrary")),
        name="seq_dft",
    )(zr, zi, m1, g)


def _dense_dft_kernel(zr_ref, zi_ref, t_ref, y_ref):
    z = jnp.concatenate([zr_ref[0, 0], zi_ref[0, 0]], axis=0).astype(BF16)
    y_ref[0] = jnp.dot(t_ref[...].astype(BF16), z, preferred_element_type=F32)


def _dense_seq_dft(zr, zi):
    G, B, L, C = zr.shape
    k = np.arange(L)
    a = 2.0 * np.pi * ((k[:, None] * k[None, :]) % L) / L
    t = jnp.asarray(np.concatenate([np.cos(a), np.sin(a)], axis=1) / np.sqrt(L), F32)
    zspec = pl.BlockSpec((1, 1, L, C), lambda b, j: (j, b, 0, 0))
    return pl.pallas_call(
        _dense_dft_kernel,
        out_shape=jax.ShapeDtypeStruct((B, L, G * C), F32),
        grid=(B, G),
        in_specs=[zspec, zspec, _const_spec((L, 2 * L))],
        out_specs=pl.BlockSpec((1, L, C), lambda b, j: (b, 0, j)),
        compiler_params=_cparams(("arbitrary", "arbitrary")),
        name="dense_seq_dft",
    )(zr, zi, t)


HALO = 16


def _wide_halo_specs(tm, L, width):
    hb = tm // HALO
    last = L // HALO - 1
    return [
        pl.BlockSpec((1, tm, width), lambda b, i: (b, i, 0)),
        pl.BlockSpec((1, HALO, width), lambda b, i: (b, jnp.minimum((i + 1) * hb, last), 0)),
        pl.BlockSpec((1, HALO, width), lambda b, i: (b, jnp.maximum(i * hb - 1, 0), 0)),
    ]


def _circ(t_ref, n_ref, p_ref):
    return jnp.concatenate([t_ref[0], n_ref[0], p_ref[0]], axis=0)


def _mix_ffn_kernel(*refs, tm, nt, odd, mrow, nside):
    n_in = 15 if odd else 9
    x3, rest = refs[:3], refs[3:n_in]
    mod_ref, ng_ref, wo_ref, wu_ref, cw_ref, wd_ref = refs[n_in:n_in + 6]
    side_src = refs[n_in + 6:n_in + 6 + nside]
    o_ref = refs[n_in + 6 + nside]
    side_dst = refs[n_in + 7 + nside:n_in + 7 + 2 * nside]
    hh_scr, act_scr = refs[n_in + 7 + 2 * nside:]
    _cast_refs(side_src, side_dst)
    i = pl.program_id(1)
    if odd:
        att, hf, hb, om = (_circ(*rest[k:k + 3]) for k in range(0, 12, 3))
        lhs = jnp.concatenate([att, ((hf + hb) * _sigmoid(om.astype(F32))).astype(BF16)], axis=-1)
    else:
        yc, yf = _circ(*rest[0:3]), _circ(*rest[3:6])
        lhs = jnp.concatenate([yc, yf.astype(BF16)], axis=-1)
    mv = [_mod_vec(mod_ref, k, mrow) for k in range(6)]
    x1 = _circ(*x3) + mv[2] * jnp.dot(lhs, wo_ref[...], preferred_element_type=F32)
    hh_scr[...] = _norm_mod(x1, ng_ref[...], mv[3], mv[4]).astype(BF16)
    row = lax.broadcasted_iota(jnp.int32, (tm + 2 * HALO, 1), 0)
    valid = (row < tm) | ((row < tm + HALO) & (i < nt - 1)) | ((row >= tm + HALO) & (i > 0))
    for c in range(N_FF_CHUNKS):
        lo = c * FF_CHUNK
        g = jnp.dot(hh_scr[...], wu_ref[:, lo:lo + FF_CHUNK], preferred_element_type=F32)
        g = jnp.where(valid, g, 0.0)
        cv = _conv3(g, cw_ref[:, lo:lo + FF_CHUNK], tm)
        val = jnp.dot(hh_scr[:tm, :], wu_ref[:, D_FF + lo:D_FF + lo + FF_CHUNK], preferred_element_type=F32)
        act_scr[:, lo:lo + FF_CHUNK] = (cv * _sigmoid(cv) * val).astype(BF16)
    y = jnp.dot(act_scr[...], wd_ref[...], preferred_element_type=F32)
    o_ref[0] = x1[:tm] + mv[5] * y


def _mix_ffn(x, mixed, mod, layer, mrow, ng, w_out, w_up, cw, w_down, tm, odd, side=()):
    B, L, _ = x.shape
    nt = L // tm
    specs = _wide_halo_specs(tm, L, D_MODEL)
    args = [x, x, x]
    for a in mixed:
        specs += _wide_halo_specs(tm, L, a.shape[-1])
        args += [a, a, a]
    side_in, side_out, side_shape = [], [], []
    for w, wd, per in side:
        r, c = w.shape
        rb = r * per // (B * nt)
        imap = lambda b, i, per=per: ((b * nt + i) // per, 0)
        side_in.append(pl.BlockSpec((rb, c), imap))
        side_out.append(pl.BlockSpec((rb, wd), imap))
        side_shape.append(jax.ShapeDtypeStruct((r, wd), BF16))
    out = pl.pallas_call(
        functools.partial(_mix_ffn_kernel, tm=tm, nt=nt, odd=odd, mrow=mrow, nside=len(side)),
        out_shape=(jax.ShapeDtypeStruct(x.shape, F32), *side_shape),
        grid=(B, nt),
        in_specs=specs + [
            _mod_spec(layer),
            _const_spec((1, D_MODEL)),
            _resident_spec((D_MODEL, D_MODEL)),
            _weight_spec(w_up, (D_MODEL, 2 * D_FF), layer),
            _layer_spec((3, D_FF), layer),
            _weight_spec(w_down, (D_FF, D_MODEL), layer),
        ] + side_in,
        out_specs=(pl.BlockSpec((1, tm, D_MODEL), lambda b, i: (b, i, 0)), *side_out),
        scratch_shapes=[pltpu.VMEM((tm + 2 * HALO, D_MODEL), BF16),
                        pltpu.VMEM((tm, D_FF), BF16)],
        compiler_params=_cparams(("parallel", "arbitrary")),
        name="odd_mix_ffn" if odd else "even_mix_ffn",
    )(*args, mod, ng, w_out, w_up, cw, w_down, *[w for w, _, _ in side])
    return out if side else out[0]


def _split3(x):
    parts = []
    r = x
    for _ in range(3):
        p = r.astype(BF16)
        parts.append(p)
        r = r - p.astype(F32)
    return parts


def _odd_in_kernel(x_ref, mod_ref, ng_ref, w_ref, pm_ref, gain_ref, cos_ref, sin_ref, gb_ref, tl_ref, tu_ref,
                   qt_ref, kv_ref, vt_ref, qm_ref, kmt_ref, vm_ref, om_ref, gc_ref, gr_ref, *, mrow, tm, nsub):
    for s in range(nsub):
        rs = slice(s * tm, (s + 1) * tm)
        h = _norm_mod(x_ref[0, rs], ng_ref[...], _mod_vec(mod_ref, 0, mrow), _mod_vec(mod_ref, 1, mrow))
        u = jnp.dot(h.astype(BF16), w_ref[...], preferred_element_type=F32)

        uqk = u[:, OQ:OV]
        ms = jnp.dot((uqk * uqk).astype(BF16), pm_ref[...].astype(BF16), preferred_element_type=F32)
        rn = uqk * lax.rsqrt(ms + EPS) * gain_ref[...]
        lane = lax.broadcasted_iota(jnp.int32, (1, LANES), 1)
        first = (lane % 32) < 16
        cos = cos_ref[rs, :]
        sin = sin_ref[rs, :]
        roped = []
        for t in range((OV - OQ) // LANES):
            xt = rn[:, t * LANES:(t + 1) * LANES]
            sw = jnp.where(first, pltpu.roll(xt, LANES - 16, 1), pltpu.roll(xt, 16, 1))
            roped.append(xt * cos + sw * sin)
        for t in range(ATT_Q // LANES):
            qt_ref[0, t * LANES:(t + 1) * LANES, rs] = (roped[t] * (ATT_SCALE * LOG2E)).T.astype(BF16)
        k = roped[ATT_Q // LANES]
        v = u[:, OV:OQM]
        half = LANES // 2
        kv_ref[0, rs, 0:LANES] = k.astype(BF16)
        kv_ref[0, rs, LANES:2 * LANES] = pltpu.roll(k, half, 1).astype(BF16)
        vt_ref[0, :, rs] = v.T.astype(BF16)

        qm_ref[0, rs] = u[:, OQM:OKM].astype(BF16)
        for p in range(ML_QK // LANES):
            km = u[:, OKM + p * LANES:OKM + (p + 1) * LANES] * (ML_QK_DIM ** -0.5)
            kmt_ref[0, p * LANES:(p + 1) * LANES, rs] = km.T.astype(BF16)
        vm_ref[0, rs] = u[:, OVM:OOM].astype(BF16)
        om_ref[0, rs] = u[:, OOM:OG].astype(BF16)

        gt = (u[:, OG:ODD_COLS] + gb_ref[...]).T[:2 * SUBLANES, :]
        row = lax.broadcasted_iota(jnp.int32, (2 * SUBLANES, 1), 0)
        logsig = jnp.minimum(gt, 0.0) - jnp.log(1.0 + jnp.exp(-jnp.abs(gt)))
        parts = _split3(logsig)
        tl, tu = tl_ref[...].astype(BF16), tu_ref[...].astype(BF16)
        cum_f = sum(jnp.dot(p, tu, preferred_element_type=F32) for p in parts)
        cum_b = sum(jnp.dot(p, tl, preferred_element_type=F32) for p in parts)
        sel = row % 8
        gr = jnp.where(sel < 4, gt, jnp.where(row < SUBLANES, cum_f, cum_b)) * LOG2E
        gr_ref[0, :, rs] = gr
        gc_ref[0, rs] = jnp.concatenate([gr, jnp.zeros((LANES - 2 * SUBLANES, tm), F32)], axis=0).T


def _chunk_tri(tm):
    i = np.arange(tm)
    same = (i[:, None] // ML_CHUNK) == (i[None, :] // ML_CHUNK)
    tl = same & (i[None, :] <= i[:, None])
    tu = same & (i[None, :] >= i[:, None])
    return jnp.asarray(tl, F32), jnp.asarray(tu, F32)


def _odd_in(x, mod, layer, mrow, ng, w, pm, gain, cos, sin, gb, tm):
    B, L, _ = x.shape
    nsub = 2 if L % (2 * tm) == 0 else 1
    bm = nsub * tm

    def rows(c, dt=BF16):
        return jax.ShapeDtypeStruct((B, L, c), dt), pl.BlockSpec((1, bm, c), lambda b, i: (b, i, 0))

    def cols(c, dt=BF16):
        return jax.ShapeDtypeStruct((B, c, L), dt), pl.BlockSpec((1, c, bm), lambda b, i: (b, 0, i))

    outs = [cols(ATT_Q), rows(2 * LANES), cols(LANES), rows(ML_QK), cols(ML_QK), rows(ML_V), rows(ML_V),
            rows(LANES, F32), cols(2 * SUBLANES, F32)]
    nqk = OV - OQ
    tl, tu = _chunk_tri(tm)
    return pl.pallas_call(
        functools.partial(_odd_in_kernel, mrow=mrow, tm=tm, nsub=nsub),
        out_shape=tuple(o[0] for o in outs),
        grid=(B, L // bm),
        in_specs=[
            pl.BlockSpec((1, bm, D_MODEL), lambda b, i: (b, i, 0)),
            _mod_spec(layer),
            _const_spec((1, D_MODEL)),
            _const_spec((D_MODEL, ODD_COLS)),
            _const_spec((nqk, nqk)),
            _const_spec((1, nqk)),
            pl.BlockSpec((bm, LANES), lambda b, i: (i, 0)),
            pl.BlockSpec((bm, LANES), lambda b, i: (i, 0)),
            _const_spec((1, LANES)),
            _const_spec((tm, tm)),
            _const_spec((tm, tm)),
        ],
        out_specs=tuple(o[1] for o in outs),
        compiler_params=_cparams(("parallel", "arbitrary")),
        name="odd_in",
    )(x, mod, ng, w, pm, gain, cos, sin, gb, tl, tu)


def _rope_tables(L):
    rows = L // GRID_W
    pos = np.stack([np.repeat(np.arange(rows), GRID_W), np.tile(np.arange(GRID_W), rows)]).astype(np.float64)
    axis_dim = HEAD_DIM // 2
    inv_freq = ROPE_THETA ** (-np.arange(0, axis_dim, 2, dtype=np.float64) / axis_dim)
    ang = pos[:, :, None] * inv_freq
    c, sn = np.cos(ang), np.sin(ang)
    cos = np.concatenate([c[0], c[0], c[1], c[1]], axis=-1)
    sin = np.concatenate([-sn[0], sn[0], -sn[1], sn[1]], axis=-1)
    return jnp.asarray(np.tile(cos, (1, 2)), F32), jnp.asarray(np.tile(sin, (1, 2)), F32)


def _att_steps(n, sink_ref, qt_ref, kvp_ref, kvc_ref, kvn_ref, kvx_ref, vtp_ref, vtc_ref, vtn_ref, vtx_ref,
               o_ref, *, nb):
    half = LANES // 2
    lo = lax.broadcasted_iota(jnp.int32, (1, LANES), 1) < half
    zero = jnp.zeros((), BF16)
    cx = kvx_ref.shape[1]
    klocal = ([kvp_ref[0]] + [kvc_ref[0, i * BLOCK:(i + 1) * BLOCK] for i in range(ATT_QB)] + [kvn_ref[0]])
    vlocal = ([vtp_ref[0]] + [vtc_ref[0, :, i * BLOCK:(i + 1) * BLOCK] for i in range(ATT_QB)] + [vtn_ref[0]])
    kctx = [kvx_ref[0, i:i + BLOCK] for i in range(0, cx, BLOCK)]
    vctx = [vtx_ref[0, :, i:i + BLOCK] for i in range(0, cx, BLOCK)]
    ones = jnp.ones((half, BLOCK), BF16)

    kj = lax.broadcasted_iota(jnp.int32, (BLOCK, BLOCK), 0)
    qi = lax.broadcasted_iota(jnp.int32, (BLOCK, BLOCK), 1)

    def twice(x):
        return jnp.concatenate([x, x], axis=1)

    left = lax.broadcasted_iota(jnp.int32, (1, 2 * BLOCK), 1) < BLOCK
    group = ATT_HEADS // ATT_KV_HEADS

    def block(qb):
        blk_id = n * ATT_QB + qb
        kblocks = klocal[qb:qb + 3] + kctx
        vblocks = vlocal[qb:qb + 3] + vctx
        bias = [twice(jnp.where((kj >= qi) & (blk_id > 0), 0.0, NEG_BIG)).astype(BF16), None,
                twice(jnp.where((kj <= qi) & (blk_id < nb - 1), 0.0, NEG_BIG)).astype(BF16)] + [None] * len(kctx)
        for kvh in range(ATT_KV_HEADS):
            t0 = kvh * group // 2
            cols = slice(qb * BLOCK, (qb + 1) * BLOCK)
            qt2 = jnp.concatenate([qt_ref[0, t0 * LANES:(t0 + 1) * LANES, cols],
                                   qt_ref[0, (t0 + 1) * LANES:(t0 + 2) * LANES, cols]], axis=1)
            outs = []
            for par in range(2):
                sk = jnp.where(left, sink_ref[2 * t0 + par], sink_ref[2 * t0 + 2 + par]) * LOG2E
                m = sk.astype(BF16).astype(F32)
                acc = jnp.zeros((LANES, 2 * BLOCK), F32)
                for blk in range(len(kblocks)):
                    k, ks = kblocks[blk][:, :LANES], kblocks[blk][:, LANES:]
                    kh = ((jnp.where(lo, k, zero), jnp.where(lo, zero, ks)),
                          (jnp.where(lo, ks, zero), jnp.where(lo, zero, k)))[kvh][par]
                    s = jnp.dot(kh, qt2, preferred_element_type=F32)
                    s = s.astype(BF16)
                    if bias[blk] is not None:
                        s = s + bias[blk]
                    m_new = jnp.maximum(m, jnp.max(s, axis=0, keepdims=True).astype(F32))
                    p = jnp.exp2(s - m_new.astype(BF16))
                    vh = jnp.concatenate([vblocks[blk][kvh * half:(kvh + 1) * half], ones], axis=0)
                    acc = acc * jnp.exp2(m - m_new) + jnp.dot(vh, p, preferred_element_type=F32)
                    m = m_new
                l = acc[half:half + 1, :] + jnp.exp2(sk - m)
                outs.append(acc[:half] * (1.0 / l))
            ot = jnp.concatenate(outs, axis=0)
            rows = slice(qb * BLOCK, (qb + 1) * BLOCK)
            o_ref[0, rows, t0 * LANES:(t0 + 1) * LANES] = ot[:, :BLOCK].T.astype(BF16)
            o_ref[0, rows, (t0 + 1) * LANES:(t0 + 2) * LANES] = ot[:, BLOCK:].T.astype(BF16)

    return [functools.partial(block, qb) for qb in range(ATT_QB)]


ML_GROUP = 8


def _mlstm_steps(j, qf_ref, ktf_ref, vf_ref, gcf_ref, grf_ref, qb_ref, ktb_ref, vb_ref, gcb_ref, grb_ref,
                 s0_ref, m0_ref, *rest, T, nb, with_output):
    if with_output:
        hf_ref, hb_ref, s_ref, m_ref = rest
    else:
        s_ref, m_ref = rest

    @pl.when(j == 0)
    def _():
        s_ref[...] = s0_ref[...]
        m_ref[...] = m0_ref[...]

    ti = lax.broadcasted_iota(jnp.int32, (T, T), 0)
    si = lax.broadcasted_iota(jnp.int32, (T, T), 1)
    masks = (si <= ti, si >= ti)
    top = lax.broadcasted_iota(jnp.int32, (LANES, 1), 0) < (LANES // 2)
    zero = jnp.zeros((), BF16)
    ones = jnp.ones((T, ML_V_DIM), BF16)
    zpad = jnp.zeros((ML_QK_DIM, 2 * ML_V_DIM), BF16)
    fwd = (qf_ref, ktf_ref, vf_ref, gcf_ref, grf_ref)
    bwd = (qb_ref, ktb_ref, vb_ref, gcb_ref, grb_ref)
    chains = [(b, d, h) for b in range(nb) for d in range(2) for h in range(ML_HEADS)]

    def group(g0):
        grp = chains[g0:g0 + ML_GROUP]
        st = {}
        for (b, d, h) in grp:
            q_ref, kt_ref, v_ref, gc_ref, gr_ref = fwd if d == 0 else bwd
            r = (b * 2 + d) * ML_HEADS + h
            li, lf = 8 * d + h, 8 * d + 4 + h
            gr = gr_ref[b]
            u_row = gr[li:li + 1, :] - gr[lf:lf + 1, :]
            m_prev = m_ref[r:r + 1, 0:1]
            e = dict(r=r, u_row=u_row, m_prev=m_prev,
                     b_end=gr[lf:lf + 1, (T - 1 if d == 0 else 0):(T if d == 0 else 1)],
                     c_end=jnp.maximum(jnp.max(u_row, axis=1, keepdims=True), m_prev))
            if with_output:
                e["mu"] = jnp.where(masks[d], u_row, NEG_BIG)
                c_col = jnp.maximum(jnp.max(e["mu"], axis=1, keepdims=True), m_prev)
                e["cb"] = jnp.broadcast_to(c_col, (T, LANES))
                e["bb"] = jnp.broadcast_to(gc_ref[b, :, lf:lf + 1], (T, LANES))
            st[(b, d, h)] = e
        for (b, d, h) in grp:
            q_ref, kt_ref, v_ref, gc_ref, gr_ref = fwd if d == 0 else bwd
            e = st[(b, d, h)]
            pair = h // 2
            ktp = kt_ref[b, pair * LANES:(pair + 1) * LANES, :]
            e["kth"] = ktp[(h % 2) * ML_QK_DIM:(h % 2 + 1) * ML_QK_DIM, :]
            e["vext"] = jnp.concatenate([v_ref[b, :, h * ML_V_DIM:(h + 1) * ML_V_DIM], ones], axis=1)
            if with_output:
                qp = q_ref[b, :, pair * LANES:(pair + 1) * LANES]
                kpad = jnp.where(top, ktp, zero) if h % 2 == 0 else jnp.where(top, zero, ktp)
                qk = jnp.dot(qp, kpad, preferred_element_type=F32) * jnp.exp2(e["mu"] - e["cb"])
                qs = qp.astype(F32) * jnp.exp2(e["m_prev"] - e["cb"])
                e["lhs"] = jnp.concatenate([qk.astype(BF16), qs.astype(BF16)], axis=1)
        for (b, d, h) in grp:
            e = st[(b, d, h)]
            e["s_prev"] = s_ref[e["r"]]
            if with_output:
                sb = e["s_prev"].astype(BF16)
                rhs = jnp.concatenate([e["vext"]] + ([sb, zpad] if h % 2 == 0 else [zpad, sb]), axis=0)
                tot = jnp.dot(e["lhs"], rhs, preferred_element_type=F32)
                floor = jnp.exp2(-(e["bb"] + e["cb"]))
                hout = tot[:, :ML_V_DIM] / jnp.maximum(jnp.abs(tot[:, ML_V_DIM:]), floor)
                o_ref = hf_ref if d == 0 else hb_ref
                o_ref[b, :, h * ML_V_DIM:(h + 1) * ML_V_DIM] = hout
        for (b, d, h) in grp:
            e = st[(b, d, h)]
            kw = (e["kth"].astype(F32) * jnp.exp2(e["u_row"] - e["c_end"])).astype(BF16)
            upd = jnp.dot(kw, e["vext"], preferred_element_type=F32)
            s_ref[e["r"]] = jnp.exp2(e["m_prev"] - e["c_end"]) * e["s_prev"] + upd
            m_ref[e["r"]:e["r"] + 1, :] = jnp.broadcast_to(e["b_end"] + e["c_end"], (1, LANES))

    return [functools.partial(group, g0) for g0 in range(0, len(chains), ML_GROUP)]


def _mlstm_kernel(*refs, T, nb, with_output):
    for step in _mlstm_steps(pl.program_id(0), *refs, T=T, nb=nb, with_output=with_output):
        step()


def _att_mlstm_kernel(*refs, T, nb, nblk, ns):
    j = pl.program_id(0)
    sink_ref, att_in, ml_in = refs[0], refs[1:10], refs[10:22]
    o_ref, ml_out = refs[22], refs[23:]
    att = _att_steps(lax.rem(j, ns), sink_ref, *att_in, o_ref, nb=nblk)
    ml = _mlstm_steps(j, *ml_in, *ml_out, T=T, nb=nb, with_output=True)
    for k in range(max(len(att), len(ml))):
        if k < len(att):
            att[k]()
        if k < len(ml):
            ml[k]()


def _mlstm_specs(B, T, nc):
    up = lambda j: j
    down = lambda j: nc - 1 - j

    def specs(o):
        return [pl.BlockSpec((B, T, ML_QK), lambda j: (0, o(j), 0)),
                pl.BlockSpec((B, ML_QK, T), lambda j: (0, 0, o(j))),
                pl.BlockSpec((B, T, ML_V), lambda j: (0, o(j), 0)),
                pl.BlockSpec((B, T, LANES), lambda j: (0, o(j), 0)),
                pl.BlockSpec((B, 2 * SUBLANES, T), lambda j: (0, 0, o(j)))]

    nchains = B * 2 * ML_HEADS
    s_spec = _const_spec((nchains, ML_QK_DIM, 2 * ML_V_DIM))
    m_spec = _const_spec((nchains, LANES))
    state_shape = [jax.ShapeDtypeStruct((nchains, ML_QK_DIM, 2 * ML_V_DIM), F32),
                   jax.ShapeDtypeStruct((nchains, LANES), F32)]
    h_specs = [pl.BlockSpec((B, T, ML_V), lambda j: (0, up(j), 0)),
               pl.BlockSpec((B, T, ML_V), lambda j: (0, down(j), 0))]
    return specs(up) + specs(down) + [s_spec, m_spec], h_specs, [s_spec, m_spec], state_shape


def _att_mlstm(qt, kv, kvx, vt, vtx, sink, qm, kmt, vm, gc, gr, s0, m0):
    B, _, L = qt.shape
    T = ML_CHUNK
    nc = L // T
    nblk = L // BLOCK
    ns = nblk // ATT_QB
    assert nc == B * ns
    cx = kvx.shape[1]
    kw = kv.shape[2]
    vw = vt.shape[1]
    wide = ATT_QB * BLOCK
    smp = lambda j: j // ns
    cur = lambda j: j % ns
    prev = lambda j: jnp.maximum(cur(j) * ATT_QB - 1, 0)
    nxt = lambda j: jnp.minimum((cur(j) + 1) * ATT_QB, nblk - 1)
    att_specs = [
        pl.BlockSpec(memory_space=pltpu.SMEM),
        pl.BlockSpec((1, ATT_Q, wide), lambda j: (smp(j), 0, cur(j))),
        pl.BlockSpec((1, BLOCK, kw), lambda j: (smp(j), prev(j), 0)),
        pl.BlockSpec((1, wide, kw), lambda j: (smp(j), cur(j), 0)),
        pl.BlockSpec((1, BLOCK, kw), lambda j: (smp(j), nxt(j), 0)),
        pl.BlockSpec((1, cx, kw), lambda j: (smp(j), 0, 0)),
        pl.BlockSpec((1, vw, BLOCK), lambda j: (smp(j), 0, prev(j))),
        pl.BlockSpec((1, vw, wide), lambda j: (smp(j), 0, cur(j))),
        pl.BlockSpec((1, vw, BLOCK), lambda j: (smp(j), 0, nxt(j))),
        pl.BlockSpec((1, vw, cx), lambda j: (smp(j), 0, 0)),
    ]
    ml_in, h_specs, st_specs, st_shape = _mlstm_specs(B, T, nc)
    h_shape = jax.ShapeDtypeStruct((B, L, ML_V), F32)
    att, hf, hb, _, _ = pl.pallas_call(
        functools.partial(_att_mlstm_kernel, T=T, nb=B, nblk=nblk, ns=ns),
        out_shape=(jax.ShapeDtypeStruct((B, L, ATT_Q), BF16), h_shape, h_shape, *st_shape),
        grid=(nc,),
        in_specs=att_specs + ml_in,
        out_specs=(pl.BlockSpec((1, wide, ATT_Q), lambda j: (smp(j), cur(j), 0)), *h_specs, *st_specs),
        compiler_params=_cparams(("arbitrary",)),
        name="attention_mlstm",
    )(sink, qt, kv, kv, kv, kvx, vt, vt, vt, vtx, qm, kmt, vm, gc, gr, qm, kmt, vm, gc, gr, s0, m0)
    return att, hf, hb


def _mlstm(qm, kmt, vm, gc, gr, s0, m0, with_output):
    B, L, _ = qm.shape
    T = ML_CHUNK
    nc = L // T
    ml_in, h_specs, st_specs, st_shape = _mlstm_specs(B, T, nc)
    out_shape, out_specs = st_shape, st_specs
    if with_output:
        out_shape = [jax.ShapeDtypeStruct((B, L, ML_V), F32)] * 2 + out_shape
        out_specs = h_specs + out_specs
    return pl.pallas_call(
        functools.partial(_mlstm_kernel, T=T, nb=B, with_output=with_output),
        out_shape=tuple(out_shape),
        grid=(nc,),
        in_specs=ml_in,
        out_specs=tuple(out_specs),
        compiler_params=_cparams(("arbitrary",)),
        name="mlstm_scan" if with_output else "mlstm_context_state",
    )(qm, kmt, vm, gc, gr, qm, kmt, vm, gc, gr, s0, m0)


def _odd_tables(gate_b, q_g, k_g):
    assert sum([ATT_Q, ATT_KV, ATT_KV, ML_QK, ML_QK, ML_V, ML_V]) == OG
    nqk = OV - OQ
    head = np.arange(nqk) // HEAD_DIM
    pm = jnp.asarray((head[:, None] == head[None, :]) / HEAD_DIM, F32)
    gain = jnp.concatenate([jnp.tile(q_g, ATT_HEADS), jnp.tile(k_g, ATT_KV_HEADS)])[None, :]
    gb = jnp.pad(gate_b.reshape(1, -1), ((0, 0), (0, LANES - gate_b.size)))
    return pm, gain, gb


def kernel(x, c, ctx, c_ctx, ada_w, ada_b, norm_g, even_w_in, even_conv, even_w_out, odd_w_in, odd_gate_b,
           odd_q_g, odd_k_g, odd_sink, odd_w_out, ffn_w_up, ffn_conv, ffn_w_down):
    B, L, _ = x.shape
    C = ctx.shape[1]
    depth = ada_w.shape[0]
    assert depth == 2 and L % (DFT_N1 * SUBLANES) == 0 and C % ML_CHUNK == 0

    cv = jnp.concatenate([c, c_ctx[None, :], jnp.zeros((SUBLANES - B - 1, D_MODEL), F32)], axis=0)
    mod = _modulation(cv, ada_w, ada_b)
    lat, cx = None, B

    tm = min(512, L)
    tc = _channel_dft_table()
    w_up0, w_down0, w_in0, w_out0 = _cast_weights(
        (ffn_w_up[0], ffn_w_down[0], even_w_in[0], even_w_out[0]), (2 * D_FF, D_MODEL, EVEN_IN, D_MODEL))
    later = ((ffn_w_up[1], 2 * D_FF, 1), (ffn_w_down[1], D_MODEL, 2), (odd_w_in[0], ODD_COLS, 1),
             (odd_w_out[0], D_MODEL, 1))

    ng00, ng01 = norm_g[0, 0][None, :], norm_g[0, 1][None, :]

    def even_layer(xs, mrow, tile, n2, side=()):
        yc, zr, zi = _even_in(xs, mod, 0, mrow, ng00, w_in0, even_conv[0], tc, tile, n2)
        yf = _dense_seq_dft(zr, zi) if n2 is None else _seq_dft(zr, zi, tile)
        return _mix_ffn(xs, (yc, yf), mod, 0, mrow, ng01, w_out0, w_up0, ffn_conv, w_down0, tile, odd=False,
                        side=side)

    xl, w_up1, w_down1, w_in1, w_out1 = even_layer(x, lat, tm, L // DFT_N1, later)
    xc = even_layer(ctx, cx, C, None)

    pm, gain, gb = _odd_tables(odd_gate_b[0], odd_q_g[0], odd_k_g[0])
    ng10, ng11 = norm_g[1, 0][None, :], norm_g[1, 1][None, :]
    cos, sin = _rope_tables(L)
    one, nil = jnp.ones((C, LANES), F32), jnp.zeros((C, LANES), F32)
    qt, kv, vt, qm, kmt, vm, om, gc, gr = _odd_in(xl, mod, 1, lat, ng10, w_in1, pm, gain, cos, sin, gb, tm)
    _, kvx, vtx, qmx, kmtx, vmx, _, gcx, grx = _odd_in(xc, mod, 1, cx, ng10, w_in1, pm, gain, one, nil, gb, C)

    nchains = B * 2 * ML_HEADS
    s0 = jnp.zeros((nchains, ML_QK_DIM, 2 * ML_V_DIM), F32)
    m0 = jnp.zeros((nchains, LANES), F32)
    s1, m1 = _mlstm(qmx, kmtx, vmx, gcx, grx, s0, m0, with_output=False)
    att, hf, hb = _att_mlstm(qt, kv, kvx, vt, vtx, odd_sink[0], qm, kmt, vm, gc, gr, s1, m1)
    return _mix_ffn(xl, (att, hf, hb, om), mod, 1, lat, ng11, w_out1, w_up1, ffn_conv, w_down1, tm, odd=True)
```

```python
import functools

import numpy as np
import jax
import jax.numpy as jnp
from jax import lax
from jax.experimental import pallas as pl
from jax.experimental.pallas import tpu as pltpu

F32 = jnp.float32
BF16 = jnp.bfloat16

D_MODEL = 1024
GRID_W = 64
EPS = 1e-6
SC_CH = 512
FT_CH = 512
FT_GROUPS = 4
FT_GROUP_CH = FT_CH // FT_GROUPS
EVEN_IN = 3 * SC_CH + FT_CH
ATT_HEADS = 8
ATT_KV_HEADS = 2
HEAD_DIM = 64
ATT_SCALE = HEAD_DIM ** -0.5
WINDOW = 128
BLOCK = 128
ROPE_THETA = 10000.0
ML_HEADS = 4
ML_QK_DIM = 64
ML_V_DIM = 128
ATT_Q = ATT_HEADS * HEAD_DIM
ATT_KV = ATT_KV_HEADS * HEAD_DIM
ML_QK = ML_HEADS * ML_QK_DIM
ML_V = ML_HEADS * ML_V_DIM
D_FF = 2816

LANES = 128
SUBLANES = 8
VMEM_LIMIT_BYTES = 56 * 1024 * 1024

DFT_N1 = 128
FF_CHUNK = 256
N_FF_CHUNKS = D_FF // FF_CHUNK
ML_CHUNK = 128
ATT_QB = 2
NEG_BIG = -1e30
LOG2E = 1.4426950408889634

OQ, OK_, OV, OQM, OKM, OVM, OOM, OG = 0, 512, 640, 768, 1024, 1280, 1792, 2304
ODD_COLS = OG + LANES


def _cparams(sem):
    return pltpu.CompilerParams(dimension_semantics=sem, vmem_limit_bytes=VMEM_LIMIT_BYTES)


def _sigmoid(x):
    return 1.0 / (1.0 + jnp.exp(-x))


def _norm_mod(x, g, shift, scale):
    y = x * lax.rsqrt(jnp.mean(x * x, axis=-1, keepdims=True) + EPS)
    return y * g * (1.0 + scale) + shift


def _mod_vec(mod_ref, k, mrow):
    r = pl.program_id(0) if mrow is None else mrow
    return mod_ref[k, pl.ds(r, 1), :]


def _mod_spec(layer):
    return pl.BlockSpec((None, 6, SUBLANES, D_MODEL), lambda *_: (layer, 0, 0, 0))


def _halo_rows(x, xn, xp, shift, scale, ng_ref):
    g = ng_ref[...]
    parts = [_norm_mod(r, g, shift, scale) for r in (x, xn, xp)]
    return jnp.concatenate(parts, axis=0).astype(BF16)


def _halo_valid(tm, i, nt):
    row = lax.broadcasted_iota(jnp.int32, (tm + 2 * SUBLANES, 1), 0)
    return ((row < tm) | ((row < tm + SUBLANES) & (i < nt - 1)) | ((row >= tm + SUBLANES) & (i > 0)))


def _conv3(v, cw, tm):
    n = v.shape[0]
    vp = pltpu.roll(v, 1, 0)[:tm]
    vn = pltpu.roll(v, n - 1, 0)[:tm]
    return vp * cw[0:1] + v[:tm] * cw[1:2] + vn * cw[2:3]


def _halo_specs(tm, L):
    hb = tm // SUBLANES
    last = L // SUBLANES - 1
    return [
        pl.BlockSpec((1, tm, D_MODEL), lambda b, i: (b, i, 0)),
        pl.BlockSpec((1, SUBLANES, D_MODEL), lambda b, i: (b, jnp.minimum((i + 1) * hb, last), 0)),
        pl.BlockSpec((1, SUBLANES, D_MODEL), lambda b, i: (b, jnp.maximum(i * hb - 1, 0), 0)),
    ]


def _const_spec(shape):
    nd = len(shape)
    return pl.BlockSpec(shape, lambda *_: (0,) * nd)


def _resident_spec(shape):
    nd = len(shape)
    return pl.BlockSpec(shape, lambda *_: (0,) * nd, pipeline_mode=pl.Buffered(1))


def _weight_spec(w, shape, layer):
    return _resident_spec(shape) if w.ndim == len(shape) else _layer_spec(shape, layer)


def _layer_spec(shape, layer):
    nd = len(shape)
    return pl.BlockSpec((None,) + tuple(shape), lambda *_: (layer,) + (0,) * nd, pipeline_mode=pl.Buffered(1))


def _split_dot(x, p):
    hi = x.astype(BF16)
    lo = (x - hi.astype(F32)).astype(BF16)
    return (jnp.dot(hi, p, preferred_element_type=F32) + jnp.dot(lo, p, preferred_element_type=F32))


def _mod_kernel(cv_ref, w_ref, b_ref, o_ref):
    cv = cv_ref[...]
    o_ref[0, 0] = _split_dot(cv * _sigmoid(cv), w_ref[0].astype(BF16)) + b_ref[0, 0]


def _modulation(cv, ada_w, ada_b):
    depth, _, n = ada_w.shape
    nv = n // D_MODEL
    return pl.pallas_call(
        _mod_kernel,
        out_shape=jax.ShapeDtypeStruct((depth, nv, SUBLANES, D_MODEL), F32),
        grid=(depth, nv),
        in_specs=[
            pl.BlockSpec((SUBLANES, D_MODEL), lambda l, j: (0, 0)),
            pl.BlockSpec((1, D_MODEL, D_MODEL), lambda l, j: (l, 0, j)),
            pl.BlockSpec((1, 1, 1, D_MODEL), lambda l, j: (l, j, 0, 0)),
        ],
        out_specs=pl.BlockSpec((1, 1, SUBLANES, D_MODEL), lambda l, j: (l, j, 0, 0)),
        compiler_params=_cparams(("arbitrary", "arbitrary")),
        name="modulation",
    )(cv, ada_w, ada_b.reshape(depth, nv, 1, D_MODEL))


CAST_STEPS = 8


def _cast_refs(srcs, dsts):
    for src, dst in zip(srcs, dsts):
        w = src[...].astype(BF16)
        pad = dst.shape[-1] - src.shape[-1]
        if pad:
            w = jnp.concatenate([w, jnp.zeros(w.shape[:-1] + (pad,), BF16)], axis=-1)
        dst[...] = w


def _cast_kernel(*refs):
    n = len(refs) // 2
    _cast_refs(refs[:n], refs[n:])


def _layer_rows_spec(w, layer, rb, imap):
    if w.ndim == 2:
        return pl.BlockSpec((rb, w.shape[1]), lambda *a: (imap(*a), 0))
    return pl.BlockSpec((None, rb, w.shape[2]), lambda *a: (layer, imap(*a), 0))


def _cast_weights(ws, layer, widths):
    in_specs, out_specs, out_shape = [], [], []
    for w, wd in zip(ws, widths):
        r = w.shape[-2]
        in_specs.append(_layer_rows_spec(w, layer, r // CAST_STEPS, lambda i: i))
        out_specs.append(pl.BlockSpec((r // CAST_STEPS, wd), lambda i: (i, 0)))
        out_shape.append(jax.ShapeDtypeStruct((r, wd), BF16))
    return pl.pallas_call(
        _cast_kernel,
        out_shape=tuple(out_shape),
        grid=(CAST_STEPS,),
        in_specs=in_specs,
        out_specs=tuple(out_specs),
        compiler_params=_cparams(("arbitrary",)),
        name="cast_weights",
    )(*ws)


def _even_in_kernel(x_ref, xn_ref, xp_ref, mod_ref, ng_ref, w_ref, cw_ref, tc_ref,
                    yc_ref, zr_ref, zi_ref, *, tm, nt, n2, mrow, nsub):
    i = pl.program_id(1)
    shift, scale = _mod_vec(mod_ref, 0, mrow), _mod_vec(mod_ref, 1, mrow)
    tc = tc_ref[...].astype(BF16)
    for s in range(nsub):
        lo = s * tm
        x = x_ref[0, lo:lo + tm]
        xn = xn_ref[0] if s == nsub - 1 else x_ref[0, lo + tm:lo + tm + SUBLANES]
        xp = xp_ref[0] if s == 0 else x_ref[0, lo - SUBLANES:lo]
        hh = _halo_rows(x, xn, xp, shift, scale, ng_ref)
        u = jnp.dot(hh, w_ref[...], preferred_element_type=F32)
        v = u[:, SC_CH:2 * SC_CH] * u[:, 2 * SC_CH:3 * SC_CH]
        v = jnp.where(_halo_valid(tm, i * nsub + s, nt), v, 0.0)
        yc = u[:tm, :SC_CH] * _conv3(v, cw_ref[...], tm)
        yc_ref[0, lo:lo + tm] = yc.astype(BF16)
        uf = u[:tm, 3 * SC_CH:].astype(BF16)
        for g in range(FT_GROUPS):
            sl = slice(g * FT_GROUP_CH, (g + 1) * FT_GROUP_CH)
            ab = jnp.dot(uf[:, sl], tc, preferred_element_type=F32)
            if n2 is None:
                zr_ref[g, 0, lo:lo + tm] = ab[:, :FT_GROUP_CH]
                zi_ref[g, 0, lo:lo + tm] = ab[:, FT_GROUP_CH:]
            else:
                for a in range(tm // n2):
                    dst = pl.ds(lo + a, n2, stride=tm // n2)
                    zr_ref[g, 0, dst, :] = ab[n2 * a:n2 * (a + 1), :FT_GROUP_CH]
                    zi_ref[g, 0, dst, :] = ab[n2 * a:n2 * (a + 1), FT_GROUP_CH:]


def _even_in(x, mod, layer, mrow, ng, w_in, cw, tc, tm, n2):
    B, L, _ = x.shape
    nt = L // tm
    nsub = 2 if nt % 2 == 0 else 1
    bm = nsub * tm
    out = jax.ShapeDtypeStruct((B, L, FT_CH), BF16)
    zout = jax.ShapeDtypeStruct((FT_GROUPS, B, L, FT_GROUP_CH), F32)
    ospec = pl.BlockSpec((1, bm, FT_CH), lambda b, i: (b, i, 0))
    zspec = pl.BlockSpec((FT_GROUPS, 1, bm, FT_GROUP_CH), lambda b, i: (0, b, i, 0))
    return pl.pallas_call(
        functools.partial(_even_in_kernel, tm=tm, nt=nt, n2=n2, mrow=mrow, nsub=nsub),
        out_shape=(out, zout, zout),
        grid=(B, nt // nsub),
        in_specs=_halo_specs(bm, L) + [
            _mod_spec(layer),
            _const_spec((1, D_MODEL)),
            _const_spec((D_MODEL, EVEN_IN)),
            _const_spec((3, SC_CH)),
            _const_spec((FT_GROUP_CH, 2 * FT_GROUP_CH)),
        ],
        out_specs=(ospec, zspec, zspec),
        compiler_params=_cparams(("parallel", "arbitrary")),
        name="even_in",
    )(x, x, x, mod, ng, w_in, cw, tc)


def _seq_dft_kernel(zr_ref, zi_ref, m_ref, g_ref, y_ref, o_scr, *, n2, tm):
    m1 = m_ref[...].astype(BF16)
    chunk = tm // n2
    ntile = DFT_N1 // chunk

    def rows(ref, j):
        return [ref[0, 0, t * tm + j * chunk:t * tm + (j + 1) * chunk, :] for t in range(ntile)]

    for j in range(n2):
        z = jnp.concatenate(rows(zr_ref, j) + rows(zi_ref, j), axis=0)
        o_scr[2 * DFT_N1 * j:2 * DFT_N1 * (j + 1), :] = jnp.dot(m1, z.astype(BF16), preferred_element_type=F32)
    for k1 in range(DFT_N1):
        o = jnp.concatenate([o_scr[pl.ds(k1, n2, stride=2 * DFT_N1), :],
                             o_scr[pl.ds(DFT_N1 + k1, n2, stride=2 * DFT_N1), :]], axis=0)
        y_ref[0, pl.ds(k1, n2, stride=DFT_N1), :] = jnp.dot(g_ref[k1].astype(BF16), o.astype(BF16), preferred_element_type=F32)


def _dft_tables(L):
    n2 = L // DFT_N1
    k = np.arange(DFT_N1)
    a = 2.0 * np.pi * ((k[:, None] * k[None, :]) % DFT_N1) / DFT_N1
    er, ei = np.cos(a) / np.sqrt(DFT_N1), -np.sin(a) / np.sqrt(DFT_N1)
    m1 = np.block([[er, -ei], [ei, er]])
    k1 = np.arange(DFT_N1)[:, None, None]
    k2 = np.arange(n2)[None, :, None]
    nn = np.arange(n2)[None, None, :]
    th = 2.0 * np.pi * ((nn * (k1 + DFT_N1 * k2)) % L) / L
    g = np.concatenate([np.cos(th), np.sin(th)], axis=-1) / np.sqrt(n2)
    return jnp.asarray(m1, F32), jnp.asarray(g, F32)


def _channel_dft_table():
    k = np.arange(FT_GROUP_CH)
    a = 2.0 * np.pi * ((k[:, None] * k[None, :]) % FT_GROUP_CH) / FT_GROUP_CH
    t = np.concatenate([np.cos(a), -np.sin(a)], axis=1) / np.sqrt(FT_GROUP_CH)
    return jnp.asarray(t, F32)


def _seq_dft(zr, zi, tm):
    G, B, L, C = zr.shape
    n2 = L // DFT_N1
    m1, g = _dft_tables(L)
    zspec = pl.BlockSpec((1, 1, L, C), lambda b, j: (j, b, 0, 0))
    return pl.pallas_call(
        functools.partial(_seq_dft_kernel, n2=n2, tm=tm),
        out_shape=jax.ShapeDtypeStruct((B, L, G * C), F32),
        grid=(B, G),
        in_specs=[zspec, zspec, _const_spec((2 * DFT_N1, 2 * DFT_N1)), _const_spec((DFT_N1, n2, 2 * n2))],
        out_specs=pl.BlockSpec((1, L, C), lambda b, j: (b, 0, j)),
        scratch_shapes=[pltpu.VMEM((2 * DFT_N1 * n2, LANES), F32)],
        compiler_params=_cparams(("parallel", "arbitrary")),
        name="seq_dft",
    )(zr, zi, m1, g)


def _dense_dft_kernel(zr_ref, zi_ref, t_ref, y_ref):
    z = jnp.concatenate([zr_ref[0, 0], zi_ref[0, 0]], axis=0).astype(BF16)
    y_ref[0] = jnp.dot(t_ref[...].astype(BF16), z, preferred_element_type=F32)


def _dense_seq_dft(zr, zi):
    G, B, L, C = zr.shape
    k = np.arange(L)
    a = 2.0 * np.pi * ((k[:, None] * k[None, :]) % L) / L
    t = jnp.asarray(np.concatenate([np.cos(a), np.sin(a)], axis=1) / np.sqrt(L), F32)
    zspec = pl.BlockSpec((1, 1, L, C), lambda b, j: (j, b, 0, 0))
    return pl.pallas_call(
        _dense_dft_kernel,
        out_shape=jax.ShapeDtypeStruct((B, L, G * C), F32),
        grid=(B, G),
        in_specs=[zspec, zspec, _const_spec((L, 2 * L))],
        out_specs=pl.BlockSpec((1, L, C), lambda b, j: (b, 0, j)),
        compiler_params=_cparams(("arbitrary", "arbitrary")),
        name="dense_seq_dft",
    )(zr, zi, t)


HALO = 16


def _wide_halo_specs(tm, L, width):
    hb = tm // HALO
    last = L // HALO - 1
    return [
        pl.BlockSpec((1, tm, width), lambda b, i: (b, i, 0)),
        pl.BlockSpec((1, HALO, width), lambda b, i: (b, jnp.minimum((i + 1) * hb, last), 0)),
        pl.BlockSpec((1, HALO, width), lambda b, i: (b, jnp.maximum(i * hb - 1, 0), 0)),
    ]


def _circ(t_ref, n_ref, p_ref):
    return jnp.concatenate([t_ref[0], n_ref[0], p_ref[0]], axis=0)


def _mix_ffn_kernel(*refs, tm, nt, odd, mrow, nside):
    n_in = 15 if odd else 9
    x3, rest = refs[:3], refs[3:n_in]
    mod_ref, ng_ref, wo_ref, wu_ref, cw_ref, wd_ref = refs[n_in:n_in + 6]
    side_src = refs[n_in + 6:n_in + 6 + nside]
    o_ref = refs[n_in + 6 + nside]
    side_dst = refs[n_in + 7 + nside:n_in + 7 + 2 * nside]
    hh_scr, act_scr = refs[n_in + 7 + 2 * nside:]
    _cast_refs(side_src, side_dst)
    i = pl.program_id(1)
    if odd:
        att, hf, hb, om = (_circ(*rest[k:k + 3]) for k in range(0, 12, 3))
        lhs = jnp.concatenate([att, ((hf + hb) * _sigmoid(om.astype(F32))).astype(BF16)], axis=-1)
    else:
        yc, yf = _circ(*rest[0:3]), _circ(*rest[3:6])
        lhs = jnp.concatenate([yc, yf.astype(BF16)], axis=-1)
    mv = [_mod_vec(mod_ref, k, mrow) for k in range(6)]
    x1 = _circ(*x3) + mv[2] * jnp.dot(lhs, wo_ref[...], preferred_element_type=F32)
    hh_scr[...] = _norm_mod(x1, ng_ref[...], mv[3], mv[4]).astype(BF16)
    row = lax.broadcasted_iota(jnp.int32, (tm + 2 * HALO, 1), 0)
    valid = (row < tm) | ((row < tm + HALO) & (i < nt - 1)) | ((row >= tm + HALO) & (i > 0))
    for c in range(N_FF_CHUNKS):
        lo = c * FF_CHUNK
        g = jnp.dot(hh_scr[...], wu_ref[:, lo:lo + FF_CHUNK], preferred_element_type=F32)
        g = jnp.where(valid, g, 0.0)
        cv = _conv3(g, cw_ref[:, lo:lo + FF_CHUNK], tm)
        val = jnp.dot(hh_scr[:tm, :], wu_ref[:, D_FF + lo:D_FF + lo + FF_CHUNK], preferred_element_type=F32)
        act_scr[:, lo:lo + FF_CHUNK] = (cv * _sigmoid(cv) * val).astype(BF16)
    y = jnp.dot(act_scr[...], wd_ref[...], preferred_element_type=F32)
    o_ref[0] = x1[:tm] + mv[5] * y


def _mix_ffn(x, mixed, mod, layer, mrow, ng, w_out, w_up, cw, w_down, tm, odd, side=()):
    B, L, _ = x.shape
    nt = L // tm
    specs = _wide_halo_specs(tm, L, D_MODEL)
    args = [x, x, x]
    for a in mixed:
        specs += _wide_halo_specs(tm, L, a.shape[-1])
        args += [a, a, a]
    side_in, side_out, side_shape = [], [], []
    for w, wl, wd, per in side:
        r = w.shape[-2]
        rb = r * per // (B * nt)
        blk = lambda b, i, per=per: (b * nt + i) // per
        side_in.append(_layer_rows_spec(w, wl, rb, blk))
        side_out.append(pl.BlockSpec((rb, wd), lambda b, i, blk=blk: (blk(b, i), 0)))
        side_shape.append(jax.ShapeDtypeStruct((r, wd), BF16))
    out = pl.pallas_call(
        functools.partial(_mix_ffn_kernel, tm=tm, nt=nt, odd=odd, mrow=mrow, nside=len(side)),
        out_shape=(jax.ShapeDtypeStruct(x.shape, F32), *side_shape),
        grid=(B, nt),
        in_specs=specs + [
            _mod_spec(layer),
            _const_spec((1, D_MODEL)),
            _resident_spec((D_MODEL, D_MODEL)),
            _weight_spec(w_up, (D_MODEL, 2 * D_FF), layer),
            _layer_spec((3, D_FF), layer),
            _weight_spec(w_down, (D_FF, D_MODEL), layer),
        ] + side_in,
        out_specs=(pl.BlockSpec((1, tm, D_MODEL), lambda b, i: (b, i, 0)), *side_out),
        scratch_shapes=[pltpu.VMEM((tm + 2 * HALO, D_MODEL), BF16),
                        pltpu.VMEM((tm, D_FF), BF16)],
        compiler_params=_cparams(("parallel", "arbitrary")),
        name="odd_mix_ffn" if odd else "even_mix_ffn",
    )(*args, mod, ng, w_out, w_up, cw, w_down, *[e[0] for e in side])
    return out if side else out[0]


def _split3(x):
    parts = []
    r = x
    for _ in range(3):
        p = r.astype(BF16)
        parts.append(p)
        r = r - p.astype(F32)
    return parts


def _odd_in_kernel(x_ref, mod_ref, ng_ref, w_ref, pm_ref, gain_ref, cos_ref, sin_ref, gb_ref, tl_ref, tu_ref,
                   qt_ref, kv_ref, vt_ref, qm_ref, kmt_ref, vm_ref, om_ref, gc_ref, gr_ref, *, mrow, tm, nsub):
    for s in range(nsub):
        rs = slice(s * tm, (s + 1) * tm)
        h = _norm_mod(x_ref[0, rs], ng_ref[...], _mod_vec(mod_ref, 0, mrow), _mod_vec(mod_ref, 1, mrow))
        u = jnp.dot(h.astype(BF16), w_ref[...], preferred_element_type=F32)

        uqk = u[:, OQ:OV]
        ms = jnp.dot((uqk * uqk).astype(BF16), pm_ref[...].astype(BF16), preferred_element_type=F32)
        rn = uqk * lax.rsqrt(ms + EPS) * gain_ref[...]
        lane = lax.broadcasted_iota(jnp.int32, (1, LANES), 1)
        first = (lane % 32) < 16
        cos = cos_ref[rs, :]
        sin = sin_ref[rs, :]
        roped = []
        for t in range((OV - OQ) // LANES):
            xt = rn[:, t * LANES:(t + 1) * LANES]
            sw = jnp.where(first, pltpu.roll(xt, LANES - 16, 1), pltpu.roll(xt, 16, 1))
            roped.append(xt * cos + sw * sin)
        for t in range(ATT_Q // LANES):
            qt_ref[0, t * LANES:(t + 1) * LANES, rs] = (roped[t] * (ATT_SCALE * LOG2E)).T.astype(BF16)
        k = roped[ATT_Q // LANES]
        v = u[:, OV:OQM]
        half = LANES // 2
        kv_ref[0, rs, 0:LANES] = k.astype(BF16)
        kv_ref[0, rs, LANES:2 * LANES] = pltpu.roll(k, half, 1).astype(BF16)
        vt_ref[0, :, rs] = v.T.astype(BF16)

        qm_ref[0, rs] = u[:, OQM:OKM].astype(BF16)
        for p in range(ML_QK // LANES):
            km = u[:, OKM + p * LANES:OKM + (p + 1) * LANES] * (ML_QK_DIM ** -0.5)
            kmt_ref[0, p * LANES:(p + 1) * LANES, rs] = km.T.astype(BF16)
        vm_ref[0, rs] = u[:, OVM:OOM].astype(BF16)
        om_ref[0, rs] = u[:, OOM:OG].astype(BF16)

        gt = (u[:, OG:ODD_COLS] + gb_ref[...]).T[:2 * SUBLANES, :]
        row = lax.broadcasted_iota(jnp.int32, (2 * SUBLANES, 1), 0)
        logsig = jnp.minimum(gt, 0.0) - jnp.log(1.0 + jnp.exp(-jnp.abs(gt)))
        parts = _split3(logsig)
        tl, tu = tl_ref[...].astype(BF16), tu_ref[...].astype(BF16)
        cum_f = sum(jnp.dot(p, tu, preferred_element_type=F32) for p in parts)
        cum_b = sum(jnp.dot(p, tl, preferred_element_type=F32) for p in parts)
        sel = row % 8
        gr = jnp.where(sel < 4, gt, jnp.where(row < SUBLANES, cum_f, cum_b)) * LOG2E
        gr_ref[0, :, rs] = gr
        gc_ref[0, rs] = jnp.concatenate([gr, jnp.zeros((LANES - 2 * SUBLANES, tm), F32)], axis=0).T


def _chunk_tri(tm):
    i = np.arange(tm)
    same = (i[:, None] // ML_CHUNK) == (i[None, :] // ML_CHUNK)
    tl = same & (i[None, :] <= i[:, None])
    tu = same & (i[None, :] >= i[:, None])
    return jnp.asarray(tl, F32), jnp.asarray(tu, F32)


def _odd_in(x, mod, layer, mrow, ng, w, pm, gain, cos, sin, gb, tm):
    B, L, _ = x.shape
    nsub = 2 if L % (2 * tm) == 0 else 1
    bm = nsub * tm

    def rows(c, dt=BF16):
        return jax.ShapeDtypeStruct((B, L, c), dt), pl.BlockSpec((1, bm, c), lambda b, i: (b, i, 0))

    def cols(c, dt=BF16):
        return jax.ShapeDtypeStruct((B, c, L), dt), pl.BlockSpec((1, c, bm), lambda b, i: (b, 0, i))

    outs = [cols(ATT_Q), rows(2 * LANES), cols(LANES), rows(ML_QK), cols(ML_QK), rows(ML_V), rows(ML_V),
            rows(LANES, F32), cols(2 * SUBLANES, F32)]
    nqk = OV - OQ
    tl, tu = _chunk_tri(tm)
    return pl.pallas_call(
        functools.partial(_odd_in_kernel, mrow=mrow, tm=tm, nsub=nsub),
        out_shape=tuple(o[0] for o in outs),
        grid=(B, L // bm),
        in_specs=[
            pl.BlockSpec((1, bm, D_MODEL), lambda b, i: (b, i, 0)),
            _mod_spec(layer),
            _const_spec((1, D_MODEL)),
            _const_spec((D_MODEL, ODD_COLS)),
            _const_spec((nqk, nqk)),
            _const_spec((1, nqk)),
            pl.BlockSpec((bm, LANES), lambda b, i: (i, 0)),
            pl.BlockSpec((bm, LANES), lambda b, i: (i, 0)),
            _const_spec((1, LANES)),
            _const_spec((tm, tm)),
            _const_spec((tm, tm)),
        ],
        out_specs=tuple(o[1] for o in outs),
        compiler_params=_cparams(("parallel", "arbitrary")),
        name="odd_in",
    )(x, mod, ng, w, pm, gain, cos, sin, gb, tl, tu)


def _rope_tables(L):
    rows = L // GRID_W
    pos = np.stack([np.repeat(np.arange(rows), GRID_W), np.tile(np.arange(GRID_W), rows)]).astype(np.float64)
    axis_dim = HEAD_DIM // 2
    inv_freq = ROPE_THETA ** (-np.arange(0, axis_dim, 2, dtype=np.float64) / axis_dim)
    ang = pos[:, :, None] * inv_freq
    c, sn = np.cos(ang), np.sin(ang)
    cos = np.concatenate([c[0], c[0], c[1], c[1]], axis=-1)
    sin = np.concatenate([-sn[0], sn[0], -sn[1], sn[1]], axis=-1)
    return jnp.asarray(np.tile(cos, (1, 2)), F32), jnp.asarray(np.tile(sin, (1, 2)), F32)


def _att_steps(n, sink_ref, qt_ref, kvp_ref, kvc_ref, kvn_ref, kvx_ref, vtp_ref, vtc_ref, vtn_ref, vtx_ref,
               o_ref, *, nb):
    half = LANES // 2
    lo = lax.broadcasted_iota(jnp.int32, (1, LANES), 1) < half
    zero = jnp.zeros((), BF16)
    cx = kvx_ref.shape[1]
    klocal = ([kvp_ref[0]] + [kvc_ref[0, i * BLOCK:(i + 1) * BLOCK] for i in range(ATT_QB)] + [kvn_ref[0]])
    vlocal = ([vtp_ref[0]] + [vtc_ref[0, :, i * BLOCK:(i + 1) * BLOCK] for i in range(ATT_QB)] + [vtn_ref[0]])
    kctx = [kvx_ref[0, i:i + BLOCK] for i in range(0, cx, BLOCK)]
    vctx = [vtx_ref[0, :, i:i + BLOCK] for i in range(0, cx, BLOCK)]
    ones = jnp.ones((half, BLOCK), BF16)

    kj = lax.broadcasted_iota(jnp.int32, (BLOCK, BLOCK), 0)
    qi = lax.broadcasted_iota(jnp.int32, (BLOCK, BLOCK), 1)

    def twice(x):
        return jnp.concatenate([x, x], axis=1)

    left = lax.broadcasted_iota(jnp.int32, (1, 2 * BLOCK), 1) < BLOCK
    group = ATT_HEADS // ATT_KV_HEADS

    def block(qb):
        blk_id = n * ATT_QB + qb
        kblocks = klocal[qb:qb + 3] + kctx
        vblocks = vlocal[qb:qb + 3] + vctx
        bias = [twice(jnp.where((kj >= qi) & (blk_id > 0), 0.0, NEG_BIG)).astype(BF16), None,
                twice(jnp.where((kj <= qi) & (blk_id < nb - 1), 0.0, NEG_BIG)).astype(BF16)] + [None] * len(kctx)
        for kvh in range(ATT_KV_HEADS):
            t0 = kvh * group // 2
            cols = slice(qb * BLOCK, (qb + 1) * BLOCK)
            qt2 = jnp.concatenate([qt_ref[0, t0 * LANES:(t0 + 1) * LANES, cols],
                                   qt_ref[0, (t0 + 1) * LANES:(t0 + 2) * LANES, cols]], axis=1)
            outs = []
            for par in range(2):
                sk = jnp.where(left, sink_ref[2 * t0 + par], sink_ref[2 * t0 + 2 + par]) * LOG2E
                m = sk.astype(BF16).astype(F32)
                acc = jnp.zeros((LANES, 2 * BLOCK), F32)
                for blk in range(len(kblocks)):
                    k, ks = kblocks[blk][:, :LANES], kblocks[blk][:, LANES:]
                    kh = ((jnp.where(lo, k, zero), jnp.where(lo, zero, ks)),
                          (jnp.where(lo, ks, zero), jnp.where(lo, zero, k)))[kvh][par]
                    s = jnp.dot(kh, qt2, preferred_element_type=F32)
                    s = s.astype(BF16)
                    if bias[blk] is not None:
                        s = s + bias[blk]
                    m_new = jnp.maximum(m, jnp.max(s, axis=0, keepdims=True).astype(F32))
                    p = jnp.exp2(s - m_new.astype(BF16))
                    vh = jnp.concatenate([vblocks[blk][kvh * half:(kvh + 1) * half], ones], axis=0)
                    acc = acc * jnp.exp2(m - m_new) + jnp.dot(vh, p, preferred_element_type=F32)
                    m = m_new
                l = acc[half:half + 1, :] + jnp.exp2(sk - m)
                outs.append(acc[:half] * (1.0 / l))
            ot = jnp.concatenate(outs, axis=0)
            rows = slice(qb * BLOCK, (qb + 1) * BLOCK)
            o_ref[0, rows, t0 * LANES:(t0 + 1) * LANES] = ot[:, :BLOCK].T.astype(BF16)
            o_ref[0, rows, (t0 + 1) * LANES:(t0 + 2) * LANES] = ot[:, BLOCK:].T.astype(BF16)

    return [functools.partial(block, qb) for qb in range(ATT_QB)]


ML_GROUP = 8


def _mlstm_steps(j, qf_ref, ktf_ref, vf_ref, gcf_ref, grf_ref, qb_ref, ktb_ref, vb_ref, gcb_ref, grb_ref,
                 s0_ref, m0_ref, *rest, T, nb, with_output):
    if with_output:
        hf_ref, hb_ref, s_ref, m_ref = rest
    else:
        s_ref, m_ref = rest

    @pl.when(j == 0)
    def _():
        s_ref[...] = s0_ref[...]
        m_ref[...] = m0_ref[...]

    ti = lax.broadcasted_iota(jnp.int32, (T, T), 0)
    si = lax.broadcasted_iota(jnp.int32, (T, T), 1)
    masks = (si <= ti, si >= ti)
    top = lax.broadcasted_iota(jnp.int32, (LANES, 1), 0) < (LANES // 2)
    zero = jnp.zeros((), BF16)
    ones = jnp.ones((T, ML_V_DIM), BF16)
    zpad = jnp.zeros((ML_QK_DIM, 2 * ML_V_DIM), BF16)
    fwd = (qf_ref, ktf_ref, vf_ref, gcf_ref, grf_ref)
    bwd = (qb_ref, ktb_ref, vb_ref, gcb_ref, grb_ref)
    chains = [(b, d, h) for b in range(nb) for d in range(2) for h in range(ML_HEADS)]

    def group(g0):
        grp = chains[g0:g0 + ML_GROUP]
        st = {}
        for (b, d, h) in grp:
            q_ref, kt_ref, v_ref, gc_ref, gr_ref = fwd if d == 0 else bwd
            r = (b * 2 + d) * ML_HEADS + h
            li, lf = 8 * d + h, 8 * d + 4 + h
            gr = gr_ref[b]
            u_row = gr[li:li + 1, :] - gr[lf:lf + 1, :]
            m_prev = m_ref[r:r + 1, 0:1]
            e = dict(r=r, u_row=u_row, m_prev=m_prev,
                     b_end=gr[lf:lf + 1, (T - 1 if d == 0 else 0):(T if d == 0 else 1)],
                     c_end=jnp.maximum(jnp.max(u_row, axis=1, keepdims=True), m_prev))
            if with_output:
                e["mu"] = jnp.where(masks[d], u_row, NEG_BIG)
                c_col = jnp.maximum(jnp.max(e["mu"], axis=1, keepdims=True), m_prev)
                e["cb"] = jnp.broadcast_to(c_col, (T, LANES))
                e["bb"] = jnp.broadcast_to(gc_ref[b, :, lf:lf + 1], (T, LANES))
            st[(b, d, h)] = e
        for (b, d, h) in grp:
            q_ref, kt_ref, v_ref, gc_ref, gr_ref = fwd if d == 0 else bwd
            e = st[(b, d, h)]
            pair = h // 2
            ktp = kt_ref[b, pair * LANES:(pair + 1) * LANES, :]
            e["kth"] = ktp[(h % 2) * ML_QK_DIM:(h % 2 + 1) * ML_QK_DIM, :]
            e["vext"] = jnp.concatenate([v_ref[b, :, h * ML_V_DIM:(h + 1) * ML_V_DIM], ones], axis=1)
            if with_output:
                qp = q_ref[b, :, pair * LANES:(pair + 1) * LANES]
                kpad = jnp.where(top, ktp, zero) if h % 2 == 0 else jnp.where(top, zero, ktp)
                qk = jnp.dot(qp, kpad, preferred_element_type=F32) * jnp.exp2(e["mu"] - e["cb"])
                qs = qp.astype(F32) * jnp.exp2(e["m_prev"] - e["cb"])
                e["lhs"] = jnp.concatenate([qk.astype(BF16), qs.astype(BF16)], axis=1)
        for (b, d, h) in grp:
            e = st[(b, d, h)]
            e["s_prev"] = s_ref[e["r"]]
            if with_output:
                sb = e["s_prev"].astype(BF16)
                rhs = jnp.concatenate([e["vext"]] + ([sb, zpad] if h % 2 == 0 else [zpad, sb]), axis=0)
                tot = jnp.dot(e["lhs"], rhs, preferred_element_type=F32)
                floor = jnp.exp2(-(e["bb"] + e["cb"]))
                hout = tot[:, :ML_V_DIM] / jnp.maximum(jnp.abs(tot[:, ML_V_DIM:]), floor)
                o_ref = hf_ref if d == 0 else hb_ref
                o_ref[b, :, h * ML_V_DIM:(h + 1) * ML_V_DIM] = hout
        for (b, d, h) in grp:
            e = st[(b, d, h)]
            kw = (e["kth"].astype(F32) * jnp.exp2(e["u_row"] - e["c_end"])).astype(BF16)
            upd = jnp.dot(kw, e["vext"], preferred_element_type=F32)
            s_ref[e["r"]] = jnp.exp2(e["m_prev"] - e["c_end"]) * e["s_prev"] + upd
            m_ref[e["r"]:e["r"] + 1, :] = jnp.broadcast_to(e["b_end"] + e["c_end"], (1, LANES))

    return [functools.partial(group, g0) for g0 in range(0, len(chains), ML_GROUP)]


def _mlstm_kernel(*refs, T, nb, with_output):
    for step in _mlstm_steps(pl.program_id(0), *refs, T=T, nb=nb, with_output=with_output):
        step()


def _att_mlstm_kernel(*refs, T, nb, nblk, ns):
    j = pl.program_id(0)
    sink_ref, att_in, ml_in = refs[0], refs[1:10], refs[10:22]
    o_ref, ml_out = refs[22], refs[23:]
    att = _att_steps(lax.rem(j, ns), sink_ref, *att_in, o_ref, nb=nblk)
    ml = _mlstm_steps(j, *ml_in, *ml_out, T=T, nb=nb, with_output=True)
    for k in range(max(len(att), len(ml))):
        if k < len(att):
            att[k]()
        if k < len(ml):
            ml[k]()


def _mlstm_specs(B, T, nc):
    up = lambda j: j
    down = lambda j: nc - 1 - j

    def specs(o):
        return [pl.BlockSpec((B, T, ML_QK), lambda j: (0, o(j), 0)),
                pl.BlockSpec((B, ML_QK, T), lambda j: (0, 0, o(j))),
                pl.BlockSpec((B, T, ML_V), lambda j: (0, o(j), 0)),
                pl.BlockSpec((B, T, LANES), lambda j: (0, o(j), 0)),
                pl.BlockSpec((B, 2 * SUBLANES, T), lambda j: (0, 0, o(j)))]

    nchains = B * 2 * ML_HEADS
    s_spec = _const_spec((nchains, ML_QK_DIM, 2 * ML_V_DIM))
    m_spec = _const_spec((nchains, LANES))
    state_shape = [jax.ShapeDtypeStruct((nchains, ML_QK_DIM, 2 * ML_V_DIM), F32),
                   jax.ShapeDtypeStruct((nchains, LANES), F32)]
    h_specs = [pl.BlockSpec((B, T, ML_V), lambda j: (0, up(j), 0)),
               pl.BlockSpec((B, T, ML_V), lambda j: (0, down(j), 0))]
    return specs(up) + specs(down) + [s_spec, m_spec], h_specs, [s_spec, m_spec], state_shape


def _att_mlstm(qt, kv, kvx, vt, vtx, sink, qm, kmt, vm, gc, gr, s0, m0):
    B, _, L = qt.shape
    T = ML_CHUNK
    nc = L // T
    nblk = L // BLOCK
    ns = nblk // ATT_QB
    assert nc == B * ns
    cx = kvx.shape[1]
    kw = kv.shape[2]
    vw = vt.shape[1]
    wide = ATT_QB * BLOCK
    smp = lambda j: j // ns
    cur = lambda j: j % ns
    prev = lambda j: jnp.maximum(cur(j) * ATT_QB - 1, 0)
    nxt = lambda j: jnp.minimum((cur(j) + 1) * ATT_QB, nblk - 1)
    att_specs = [
        pl.BlockSpec(memory_space=pltpu.SMEM),
        pl.BlockSpec((1, ATT_Q, wide), lambda j: (smp(j), 0, cur(j))),
        pl.BlockSpec((1, BLOCK, kw), lambda j: (smp(j), prev(j), 0)),
        pl.BlockSpec((1, wide, kw), lambda j: (smp(j), cur(j), 0)),
        pl.BlockSpec((1, BLOCK, kw), lambda j: (smp(j), nxt(j), 0)),
        pl.BlockSpec((1, cx, kw), lambda j: (smp(j), 0, 0)),
        pl.BlockSpec((1, vw, BLOCK), lambda j: (smp(j), 0, prev(j))),
        pl.BlockSpec((1, vw, wide), lambda j: (smp(j), 0, cur(j))),
        pl.BlockSpec((1, vw, BLOCK), lambda j: (smp(j), 0, nxt(j))),
        pl.BlockSpec((1, vw, cx), lambda j: (smp(j), 0, 0)),
    ]
    ml_in, h_specs, st_specs, st_shape = _mlstm_specs(B, T, nc)
    h_shape = jax.ShapeDtypeStruct((B, L, ML_V), F32)
    att, hf, hb, _, _ = pl.pallas_call(
        functools.partial(_att_mlstm_kernel, T=T, nb=B, nblk=nblk, ns=ns),
        out_shape=(jax.ShapeDtypeStruct((B, L, ATT_Q), BF16), h_shape, h_shape, *st_shape),
        grid=(nc,),
        in_specs=att_specs + ml_in,
        out_specs=(pl.BlockSpec((1, wide, ATT_Q), lambda j: (smp(j), cur(j), 0)), *h_specs, *st_specs),
        compiler_params=_cparams(("arbitrary",)),
        name="attention_mlstm",
    )(sink, qt, kv, kv, kv, kvx, vt, vt, vt, vtx, qm, kmt, vm, gc, gr, qm, kmt, vm, gc, gr, s0, m0)
    return att, hf, hb


def _mlstm(qm, kmt, vm, gc, gr, s0, m0, with_output):
    B, L, _ = qm.shape
    T = ML_CHUNK
    nc = L // T
    ml_in, h_specs, st_specs, st_shape = _mlstm_specs(B, T, nc)
    out_shape, out_specs = st_shape, st_specs
    if with_output:
        out_shape = [jax.ShapeDtypeStruct((B, L, ML_V), F32)] * 2 + out_shape
        out_specs = h_specs + out_specs
    return pl.pallas_call(
        functools.partial(_mlstm_kernel, T=T, nb=B, with_output=with_output),
        out_shape=tuple(out_shape),
        grid=(nc,),
        in_specs=ml_in,
        out_specs=tuple(out_specs),
        compiler_params=_cparams(("arbitrary",)),
        name="mlstm_scan" if with_output else "mlstm_context_state",
    )(qm, kmt, vm, gc, gr, qm, kmt, vm, gc, gr, s0, m0)


def _odd_tables(gate_b, q_g, k_g):
    assert sum([ATT_Q, ATT_KV, ATT_KV, ML_QK, ML_QK, ML_V, ML_V]) == OG
    nqk = OV - OQ
    head = np.arange(nqk) // HEAD_DIM
    pm = jnp.asarray((head[:, None] == head[None, :]) / HEAD_DIM, F32)
    gain = jnp.concatenate([jnp.tile(q_g, ATT_HEADS), jnp.tile(k_g, ATT_KV_HEADS)])[None, :]
    gb = jnp.pad(gate_b.reshape(1, -1), ((0, 0), (0, LANES - gate_b.size)))
    return pm, gain, gb


def kernel(x, c, ctx, c_ctx, ada_w, ada_b, norm_g, even_w_in, even_conv, even_w_out, odd_w_in, odd_gate_b,
           odd_q_g, odd_k_g, odd_sink, odd_w_out, ffn_w_up, ffn_conv, ffn_w_down):
    B, L, _ = x.shape
    C = ctx.shape[1]
    depth = ada_w.shape[0]
    assert depth == 2 and L % (DFT_N1 * SUBLANES) == 0 and C % ML_CHUNK == 0

    cv = jnp.concatenate([c, c_ctx[None, :], jnp.zeros((SUBLANES - B - 1, D_MODEL), F32)], axis=0)
    mod = _modulation(cv, ada_w, ada_b)
    lat, cx = None, B

    tm = min(512, L)
    tc = _channel_dft_table()
    w_up0, w_down0, w_in0, w_out0 = _cast_weights(
        (ffn_w_up, ffn_w_down, even_w_in, even_w_out), 0, (2 * D_FF, D_MODEL, EVEN_IN, D_MODEL))
    later = ((ffn_w_up, 1, 2 * D_FF, 1), (ffn_w_down, 1, D_MODEL, 2), (odd_w_in, 0, ODD_COLS, 1),
             (odd_w_out, 0, D_MODEL, 1))

    ng00, ng01 = norm_g[0, 0][None, :], norm_g[0, 1][None, :]

    def even_layer(xs, mrow, tile, n2, side=()):
        yc, zr, zi = _even_in(xs, mod, 0, mrow, ng00, w_in0, even_conv[0], tc, tile, n2)
        yf = _dense_seq_dft(zr, zi) if n2 is None else _seq_dft(zr, zi, tile)
        return _mix_ffn(xs, (yc, yf), mod, 0, mrow, ng01, w_out0, w_up0, ffn_conv, w_down0, tile, odd=False,
                        side=side)

    xl, w_up1, w_down1, w_in1, w_out1 = even_layer(x, lat, tm, L // DFT_N1, later)
    xc = even_layer(ctx, cx, C, None)

    pm, gain, gb = _odd_tables(odd_gate_b[0], odd_q_g[0], odd_k_g[0])
    ng10, ng11 = norm_g[1, 0][None, :], norm_g[1, 1][None, :]
    cos, sin = _rope_tables(L)
    one, nil = jnp.ones((C, LANES), F32), jnp.zeros((C, LANES), F32)
    qt, kv, vt, qm, kmt, vm, om, gc, gr = _odd_in(xl, mod, 1, lat, ng10, w_in1, pm, gain, cos, sin, gb, tm)
    _, kvx, vtx, qmx, kmtx, vmx, _, gcx, grx = _odd_in(xc, mod, 1, cx, ng10, w_in1, pm, gain, one, nil, gb, C)

    nchains = B * 2 * ML_HEADS
    s0 = jnp.zeros((nchains, ML_QK_DIM, 2 * ML_V_DIM), F32)
    m0 = jnp.zeros((nchains, LANES), F32)
    s1, m1 = _mlstm(qmx, kmtx, vmx, gcx, grx, s0, m0, with_output=False)
    att, hf, hb = _att_mlstm(qt, kv, kvx, vt, vtx, odd_sink[0], qm, kmt, vm, gc, gr, s1, m1)
    return _mix_ffn(xl, (att, hf, hb, om), mod, 1, lat, ng11, w_out1, w_up1, ffn_conv, w_down1, tm, odd=True)
```

```python
import functools

import numpy as np
import jax
import jax.numpy as jnp
from jax import lax
from jax.experimental import pallas as pl
from jax.experimental.pallas import tpu as pltpu

F32 = jnp.float32
BF16 = jnp.bfloat16

D_MODEL = 1024
GRID_W = 64
EPS = 1e-6
SC_CH = 512
FT_CH = 512
FT_GROUPS = 4
FT_GROUP_CH = FT_CH // FT_GROUPS
EVEN_IN = 3 * SC_CH + FT_CH
ATT_HEADS = 8
ATT_KV_HEADS = 2
HEAD_DIM = 64
ATT_SCALE = HEAD_DIM ** -0.5
WINDOW = 128
BLOCK = 128
ROPE_THETA = 10000.0
ML_HEADS = 4
ML_QK_DIM = 64
ML_V_DIM = 128
ATT_Q = ATT_HEADS * HEAD_DIM
ATT_KV = ATT_KV_HEADS * HEAD_DIM
ML_QK = ML_HEADS * ML_QK_DIM
ML_V = ML_HEADS * ML_V_DIM
D_FF = 2816

LANES = 128
SUBLANES = 8
VMEM_LIMIT_BYTES = 56 * 1024 * 1024

DFT_N1 = 128
FF_CHUNK = 256
N_FF_CHUNKS = D_FF // FF_CHUNK
ML_CHUNK = 128
ATT_QB = 2
NEG_BIG = -1e30
LOG2E = 1.4426950408889634

OQ, OK_, OV, OQM, OKM, OVM, OOM, OG = 0, 512, 640, 768, 1024, 1280, 1792, 2304
ODD_COLS = OG + LANES


def _cparams(sem):
    return pltpu.CompilerParams(dimension_semantics=sem, vmem_limit_bytes=VMEM_LIMIT_BYTES)


def _sigmoid(x):
    return 1.0 / (1.0 + jnp.exp(-x))


def _norm_mod(x, g, shift, scale):
    y = x * lax.rsqrt(jnp.mean(x * x, axis=-1, keepdims=True) + EPS)
    return y * g * (1.0 + scale) + shift


def _mod_vec(mod_ref, k, mrow):
    r = pl.program_id(0) if mrow is None else mrow
    return mod_ref[k, pl.ds(r, 1), :]


def _mod_spec(layer):
    return pl.BlockSpec((None, 6, SUBLANES, D_MODEL), lambda *_: (layer, 0, 0, 0))


def _halo_rows(x, xn, xp, shift, scale, ng_ref):
    g = ng_ref[...]
    parts = [_norm_mod(r, g, shift, scale) for r in (x, xn, xp)]
    return jnp.concatenate(parts, axis=0).astype(BF16)


def _halo_valid(tm, i, nt):
    row = lax.broadcasted_iota(jnp.int32, (tm + 2 * SUBLANES, 1), 0)
    return ((row < tm) | ((row < tm + SUBLANES) & (i < nt - 1)) | ((row >= tm + SUBLANES) & (i > 0)))


def _conv3(v, cw, tm):
    n = v.shape[0]
    vp = pltpu.roll(v, 1, 0)[:tm]
    vn = pltpu.roll(v, n - 1, 0)[:tm]
    return vp * cw[0:1] + v[:tm] * cw[1:2] + vn * cw[2:3]


def _halo_specs(tm, L):
    hb = tm // SUBLANES
    last = L // SUBLANES - 1
    return [
        pl.BlockSpec((1, tm, D_MODEL), lambda b, i: (b, i, 0)),
        pl.BlockSpec((1, SUBLANES, D_MODEL), lambda b, i: (b, jnp.minimum((i + 1) * hb, last), 0)),
        pl.BlockSpec((1, SUBLANES, D_MODEL), lambda b, i: (b, jnp.maximum(i * hb - 1, 0), 0)),
    ]


def _const_spec(shape):
    nd = len(shape)
    return pl.BlockSpec(shape, lambda *_: (0,) * nd)


def _resident_spec(shape):
    nd = len(shape)
    return pl.BlockSpec(shape, lambda *_: (0,) * nd, pipeline_mode=pl.Buffered(1))


def _weight_spec(w, shape, layer):
    return _resident_spec(shape) if w.ndim == len(shape) else _layer_spec(shape, layer)


def _layer_spec(shape, layer):
    nd = len(shape)
    return pl.BlockSpec((None,) + tuple(shape), lambda *_: (layer,) + (0,) * nd, pipeline_mode=pl.Buffered(1))


def _split_dot(x, p):
    hi = x.astype(BF16)
    lo = (x - hi.astype(F32)).astype(BF16)
    return (jnp.dot(hi, p, preferred_element_type=F32) + jnp.dot(lo, p, preferred_element_type=F32))


def _mod_kernel(cv_ref, w_ref, b_ref, o_ref):
    cv = cv_ref[...]
    o_ref[0, 0] = _split_dot(cv * _sigmoid(cv), w_ref[0].astype(BF16)) + b_ref[0, 0]


def _modulation(cv, ada_w, ada_b):
    depth, _, n = ada_w.shape
    nv = n // D_MODEL
    return pl.pallas_call(
        _mod_kernel,
        out_shape=jax.ShapeDtypeStruct((depth, nv, SUBLANES, D_MODEL), F32),
        grid=(depth, nv),
        in_specs=[
            pl.BlockSpec((SUBLANES, D_MODEL), lambda l, j: (0, 0)),
            pl.BlockSpec((1, D_MODEL, D_MODEL), lambda l, j: (l, 0, j)),
            pl.BlockSpec((1, 1, 1, D_MODEL), lambda l, j: (l, j, 0, 0)),
        ],
        out_specs=pl.BlockSpec((1, 1, SUBLANES, D_MODEL), lambda l, j: (l, j, 0, 0)),
        compiler_params=_cparams(("arbitrary", "arbitrary")),
        name="modulation",
    )(cv, ada_w, ada_b.reshape(depth, nv, 1, D_MODEL))


CAST_STEPS = 8


def _cast_refs(srcs, dsts):
    for src, dst in zip(srcs, dsts):
        w = src[...].astype(BF16)
        pad = dst.shape[-1] - src.shape[-1]
        if pad:
            w = jnp.concatenate([w, jnp.zeros(w.shape[:-1] + (pad,), BF16)], axis=-1)
        dst[...] = w


def _cast_kernel(*refs):
    n = len(refs) // 2
    _cast_refs(refs[:n], refs[n:])


def _layer_rows_spec(w, layer, rb, imap):
    if w.ndim == 2:
        return pl.BlockSpec((rb, w.shape[1]), lambda *a: (imap(*a), 0))
    return pl.BlockSpec((None, rb, w.shape[2]), lambda *a: (layer, imap(*a), 0))


def _side_cast_specs(side, steps, step_of):
    ins, outs, shapes = [], [], []
    for w, wl, wd, per in side:
        r = w.shape[-2]
        rb = r * per // steps
        blk = lambda *a, per=per: step_of(*a) // per
        ins.append(_layer_rows_spec(w, wl, rb, blk))
        outs.append(pl.BlockSpec((rb, wd), lambda *a, blk=blk: (blk(*a), 0)))
        shapes.append(jax.ShapeDtypeStruct((r, wd), BF16))
    return ins, outs, shapes


def _cast_weights(ws, layer, widths):
    in_specs, out_specs, out_shape = [], [], []
    for w, wd in zip(ws, widths):
        r = w.shape[-2]
        in_specs.append(_layer_rows_spec(w, layer, r // CAST_STEPS, lambda i: i))
        out_specs.append(pl.BlockSpec((r // CAST_STEPS, wd), lambda i: (i, 0)))
        out_shape.append(jax.ShapeDtypeStruct((r, wd), BF16))
    return pl.pallas_call(
        _cast_kernel,
        out_shape=tuple(out_shape),
        grid=(CAST_STEPS,),
        in_specs=in_specs,
        out_specs=tuple(out_specs),
        compiler_params=_cparams(("arbitrary",)),
        name="cast_weights",
    )(*ws)


def _even_in_kernel(x_ref, xn_ref, xp_ref, mod_ref, ng_ref, w_ref, cw_ref, tc_ref, *rest, tm, nt, n2, mrow, nsub):
    nside = (len(rest) - 3) // 2
    yc_ref, zr_ref, zi_ref = rest[nside:nside + 3]
    _cast_refs(rest[:nside], rest[nside + 3:])
    i = pl.program_id(1)
    shift, scale = _mod_vec(mod_ref, 0, mrow), _mod_vec(mod_ref, 1, mrow)
    tc = tc_ref[...].astype(BF16)
    for s in range(nsub):
        lo = s * tm
        x = x_ref[0, lo:lo + tm]
        xn = xn_ref[0] if s == nsub - 1 else x_ref[0, lo + tm:lo + tm + SUBLANES]
        xp = xp_ref[0] if s == 0 else x_ref[0, lo - SUBLANES:lo]
        hh = _halo_rows(x, xn, xp, shift, scale, ng_ref)
        u = jnp.dot(hh, w_ref[...], preferred_element_type=F32)
        v = u[:, SC_CH:2 * SC_CH] * u[:, 2 * SC_CH:3 * SC_CH]
        v = jnp.where(_halo_valid(tm, i * nsub + s, nt), v, 0.0)
        yc = u[:tm, :SC_CH] * _conv3(v, cw_ref[...], tm)
        yc_ref[0, lo:lo + tm] = yc.astype(BF16)
        uf = u[:tm, 3 * SC_CH:].astype(BF16)
        for g in range(FT_GROUPS):
            sl = slice(g * FT_GROUP_CH, (g + 1) * FT_GROUP_CH)
            ab = jnp.dot(uf[:, sl], tc, preferred_element_type=F32)
            if n2 is None:
                zr_ref[g, 0, lo:lo + tm] = ab[:, :FT_GROUP_CH]
                zi_ref[g, 0, lo:lo + tm] = ab[:, FT_GROUP_CH:]
            else:
                for a in range(tm // n2):
                    dst = pl.ds(lo + a, n2, stride=tm // n2)
                    zr_ref[g, 0, dst, :] = ab[n2 * a:n2 * (a + 1), :FT_GROUP_CH]
                    zi_ref[g, 0, dst, :] = ab[n2 * a:n2 * (a + 1), FT_GROUP_CH:]


def _even_in(x, mod, layer, mrow, ng, w_in, cw, tc, tm, n2, side=()):
    B, L, _ = x.shape
    nt = L // tm
    nsub = 2 if nt % 2 == 0 else 1
    bm = nsub * tm
    out = jax.ShapeDtypeStruct((B, L, FT_CH), BF16)
    zout = jax.ShapeDtypeStruct((FT_GROUPS, B, L, FT_GROUP_CH), F32)
    ospec = pl.BlockSpec((1, bm, FT_CH), lambda b, i: (b, i, 0))
    zspec = pl.BlockSpec((FT_GROUPS, 1, bm, FT_GROUP_CH), lambda b, i: (0, b, i, 0))
    ns = nt // nsub
    side_in, side_out, side_shape = _side_cast_specs(side, B * ns, lambda b, i: b * ns + i)
    return pl.pallas_call(
        functools.partial(_even_in_kernel, tm=tm, nt=nt, n2=n2, mrow=mrow, nsub=nsub),
        out_shape=(out, zout, zout, *side_shape),
        grid=(B, ns),
        in_specs=_halo_specs(bm, L) + [
            _mod_spec(layer),
            _const_spec((1, D_MODEL)),
            _const_spec((D_MODEL, EVEN_IN)),
            _const_spec((3, SC_CH)),
            _const_spec((FT_GROUP_CH, 2 * FT_GROUP_CH)),
        ] + side_in,
        out_specs=(ospec, zspec, zspec, *side_out),
        compiler_params=_cparams(("parallel", "arbitrary")),
        name="even_in",
    )(x, x, x, mod, ng, w_in, cw, tc, *[e[0] for e in side])


def _seq_dft_kernel(zr_ref, zi_ref, m_ref, g_ref, y_ref, o_scr, *, n2, tm):
    m1 = m_ref[...].astype(BF16)
    chunk = tm // n2
    ntile = DFT_N1 // chunk

    def rows(ref, j):
        return [ref[0, 0, t * tm + j * chunk:t * tm + (j + 1) * chunk, :] for t in range(ntile)]

    for j in range(n2):
        z = jnp.concatenate(rows(zr_ref, j) + rows(zi_ref, j), axis=0)
        o_scr[2 * DFT_N1 * j:2 * DFT_N1 * (j + 1), :] = jnp.dot(m1, z.astype(BF16), preferred_element_type=F32)
    for k1 in range(DFT_N1):
        o = jnp.concatenate([o_scr[pl.ds(k1, n2, stride=2 * DFT_N1), :],
                             o_scr[pl.ds(DFT_N1 + k1, n2, stride=2 * DFT_N1), :]], axis=0)
        y_ref[0, pl.ds(k1, n2, stride=DFT_N1), :] = jnp.dot(g_ref[k1].astype(BF16), o.astype(BF16), preferred_element_type=F32)


def _dft_tables(L):
    n2 = L // DFT_N1
    k = np.arange(DFT_N1)
    a = 2.0 * np.pi * ((k[:, None] * k[None, :]) % DFT_N1) / DFT_N1
    er, ei = np.cos(a) / np.sqrt(DFT_N1), -np.sin(a) / np.sqrt(DFT_N1)
    m1 = np.block([[er, -ei], [ei, er]])
    k1 = np.arange(DFT_N1)[:, None, None]
    k2 = np.arange(n2)[None, :, None]
    nn = np.arange(n2)[None, None, :]
    th = 2.0 * np.pi * ((nn * (k1 + DFT_N1 * k2)) % L) / L
    g = np.concatenate([np.cos(th), np.sin(th)], axis=-1) / np.sqrt(n2)
    return jnp.asarray(m1, F32), jnp.asarray(g, F32)


def _channel_dft_table():
    k = np.arange(FT_GROUP_CH)
    a = 2.0 * np.pi * ((k[:, None] * k[None, :]) % FT_GROUP_CH) / FT_GROUP_CH
    t = np.concatenate([np.cos(a), -np.sin(a)], axis=1) / np.sqrt(FT_GROUP_CH)
    return jnp.asarray(t, F32)


def _seq_dft(zr, zi, tm):
    G, B, L, C = zr.shape
    n2 = L // DFT_N1
    m1, g = _dft_tables(L)
    zspec = pl.BlockSpec((1, 1, L, C), lambda b, j: (j, b, 0, 0))
    return pl.pallas_call(
        functools.partial(_seq_dft_kernel, n2=n2, tm=tm),
        out_shape=jax.ShapeDtypeStruct((B, L, G * C), F32),
        grid=(B, G),
        in_specs=[zspec, zspec, _const_spec((2 * DFT_N1, 2 * DFT_N1)), _const_spec((DFT_N1, n2, 2 * n2))],
        out_specs=pl.BlockSpec((1, L, C), lambda b, j: (b, 0, j)),
        scratch_shapes=[pltpu.VMEM((2 * DFT_N1 * n2, LANES), F32)],
        compiler_params=_cparams(("parallel", "arbitrary")),
        name="seq_dft",
    )(zr, zi, m1, g)


def _dense_dft_kernel(zr_ref, zi_ref, t_ref, y_ref):
    z = jnp.concatenate([zr_ref[0, 0], zi_ref[0, 0]], axis=0).astype(BF16)
    y_ref[0] = jnp.dot(t_ref[...].astype(BF16), z, preferred_element_type=F32)


def _dense_seq_dft(zr, zi):
    G, B, L, C = zr.shape
    k = np.arange(L)
    a = 2.0 * np.pi * ((k[:, None] * k[None, :]) % L) / L
    t = jnp.asarray(np.concatenate([np.cos(a), np.sin(a)], axis=1) / np.sqrt(L), F32)
    zspec = pl.BlockSpec((1, 1, L, C), lambda b, j: (j, b, 0, 0))
    return pl.pallas_call(
        _dense_dft_kernel,
        out_shape=jax.ShapeDtypeStruct((B, L, G * C), F32),
        grid=(B, G),
        in_specs=[zspec, zspec, _const_spec((L, 2 * L))],
        out_specs=pl.BlockSpec((1, L, C), lambda b, j: (b, 0, j)),
        compiler_params=_cparams(("arbitrary", "arbitrary")),
        name="dense_seq_dft",
    )(zr, zi, t)


HALO = 16


def _wide_halo_specs(tm, L, width):
    hb = tm // HALO
    last = L // HALO - 1
    return [
        pl.BlockSpec((1, tm, width), lambda b, i: (b, i, 0)),
        pl.BlockSpec((1, HALO, width), lambda b, i: (b, jnp.minimum((i + 1) * hb, last), 0)),
        pl.BlockSpec((1, HALO, width), lambda b, i: (b, jnp.maximum(i * hb - 1, 0), 0)),
    ]


def _circ(t_ref, n_ref, p_ref):
    return jnp.concatenate([t_ref[0], n_ref[0], p_ref[0]], axis=0)


def _mix_ffn_kernel(*refs, tm, nt, odd, mrow, nside):
    n_in = 15 if odd else 9
    x3, rest = refs[:3], refs[3:n_in]
    mod_ref, ng_ref, wo_ref, wu_ref, cw_ref, wd_ref = refs[n_in:n_in + 6]
    side_src = refs[n_in + 6:n_in + 6 + nside]
    o_ref = refs[n_in + 6 + nside]
    side_dst = refs[n_in + 7 + nside:n_in + 7 + 2 * nside]
    hh_scr, act_scr = refs[n_in + 7 + 2 * nside:]
    _cast_refs(side_src, side_dst)
    i = pl.program_id(1)
    if odd:
        att, hf, hb, om = (_circ(*rest[k:k + 3]) for k in range(0, 12, 3))
        lhs = jnp.concatenate([att, ((hf + hb) * _sigmoid(om.astype(F32))).astype(BF16)], axis=-1)
    else:
        yc, yf = _circ(*rest[0:3]), _circ(*rest[3:6])
        lhs = jnp.concatenate([yc, yf.astype(BF16)], axis=-1)
    mv = [_mod_vec(mod_ref, k, mrow) for k in range(6)]
    x1 = _circ(*x3) + mv[2] * jnp.dot(lhs, wo_ref[...], preferred_element_type=F32)
    hh_scr[...] = _norm_mod(x1, ng_ref[...], mv[3], mv[4]).astype(BF16)
    row = lax.broadcasted_iota(jnp.int32, (tm + 2 * HALO, 1), 0)
    valid = (row < tm) | ((row < tm + HALO) & (i < nt - 1)) | ((row >= tm + HALO) & (i > 0))
    for c in range(N_FF_CHUNKS):
        lo = c * FF_CHUNK
        g = jnp.dot(hh_scr[...], wu_ref[:, lo:lo + FF_CHUNK], preferred_element_type=F32)
        g = jnp.where(valid, g, 0.0)
        cv = _conv3(g, cw_ref[:, lo:lo + FF_CHUNK], tm)
        val = jnp.dot(hh_scr[:tm, :], wu_ref[:, D_FF + lo:D_FF + lo + FF_CHUNK], preferred_element_type=F32)
        act_scr[:, lo:lo + FF_CHUNK] = (cv * _sigmoid(cv) * val).astype(BF16)
    y = jnp.dot(act_scr[...], wd_ref[...], preferred_element_type=F32)
    o_ref[0] = x1[:tm] + mv[5] * y


def _mix_ffn(x, mixed, mod, layer, mrow, ng, w_out, w_up, cw, w_down, tm, odd, side=()):
    B, L, _ = x.shape
    nt = L // tm
    specs = _wide_halo_specs(tm, L, D_MODEL)
    args = [x, x, x]
    for a in mixed:
        specs += _wide_halo_specs(tm, L, a.shape[-1])
        args += [a, a, a]
    side_in, side_out, side_shape = _side_cast_specs(side, B * nt, lambda b, i: b * nt + i)
    out = pl.pallas_call(
        functools.partial(_mix_ffn_kernel, tm=tm, nt=nt, odd=odd, mrow=mrow, nside=len(side)),
        out_shape=(jax.ShapeDtypeStruct(x.shape, F32), *side_shape),
        grid=(B, nt),
        in_specs=specs + [
            _mod_spec(layer),
            _const_spec((1, D_MODEL)),
            _resident_spec((D_MODEL, D_MODEL)),
            _weight_spec(w_up, (D_MODEL, 2 * D_FF), layer),
            _layer_spec((3, D_FF), layer),
            _weight_spec(w_down, (D_FF, D_MODEL), layer),
        ] + side_in,
        out_specs=(pl.BlockSpec((1, tm, D_MODEL), lambda b, i: (b, i, 0)), *side_out),
        scratch_shapes=[pltpu.VMEM((tm + 2 * HALO, D_MODEL), BF16),
                        pltpu.VMEM((tm, D_FF), BF16)],
        compiler_params=_cparams(("parallel", "arbitrary")),
        name="odd_mix_ffn" if odd else "even_mix_ffn",
    )(*args, mod, ng, w_out, w_up, cw, w_down, *[e[0] for e in side])
    return out if side else out[0]


def _split3(x):
    parts = []
    r = x
    for _ in range(3):
        p = r.astype(BF16)
        parts.append(p)
        r = r - p.astype(F32)
    return parts


def _odd_in_kernel(x_ref, mod_ref, ng_ref, w_ref, pm_ref, gain_ref, cos_ref, sin_ref, gb_ref, tl_ref, tu_ref,
                   qt_ref, kv_ref, vt_ref, qm_ref, kmt_ref, vm_ref, om_ref, gc_ref, gr_ref, *, mrow, tm, nsub):
    for s in range(nsub):
        rs = slice(s * tm, (s + 1) * tm)
        h = _norm_mod(x_ref[0, rs], ng_ref[...], _mod_vec(mod_ref, 0, mrow), _mod_vec(mod_ref, 1, mrow))
        u = jnp.dot(h.astype(BF16), w_ref[...], preferred_element_type=F32)

        uqk = u[:, OQ:OV]
        ms = jnp.dot((uqk * uqk).astype(BF16), pm_ref[...].astype(BF16), preferred_element_type=F32)
        rn = uqk * lax.rsqrt(ms + EPS) * gain_ref[...]
        lane = lax.broadcasted_iota(jnp.int32, (1, LANES), 1)
        first = (lane % 32) < 16
        cos = cos_ref[rs, :]
        sin = sin_ref[rs, :]
        roped = []
        for t in range((OV - OQ) // LANES):
            xt = rn[:, t * LANES:(t + 1) * LANES]
            sw = jnp.where(first, pltpu.roll(xt, LANES - 16, 1), pltpu.roll(xt, 16, 1))
            roped.append(xt * cos + sw * sin)
        for t in range(ATT_Q // LANES):
            qt_ref[0, t * LANES:(t + 1) * LANES, rs] = (roped[t] * (ATT_SCALE * LOG2E)).T.astype(BF16)
        k = roped[ATT_Q // LANES]
        v = u[:, OV:OQM]
        half = LANES // 2
        kv_ref[0, rs, 0:LANES] = k.astype(BF16)
        kv_ref[0, rs, LANES:2 * LANES] = pltpu.roll(k, half, 1).astype(BF16)
        vt_ref[0, :, rs] = v.T.astype(BF16)

        qm_ref[0, rs] = u[:, OQM:OKM].astype(BF16)
        for p in range(ML_QK // LANES):
            km = u[:, OKM + p * LANES:OKM + (p + 1) * LANES] * (ML_QK_DIM ** -0.5)
            kmt_ref[0, p * LANES:(p + 1) * LANES, rs] = km.T.astype(BF16)
        vm_ref[0, rs] = u[:, OVM:OOM].astype(BF16)
        om_ref[0, rs] = u[:, OOM:OG].astype(BF16)

        gt = (u[:, OG:ODD_COLS] + gb_ref[...]).T[:2 * SUBLANES, :]
        row = lax.broadcasted_iota(jnp.int32, (2 * SUBLANES, 1), 0)
        logsig = jnp.minimum(gt, 0.0) - jnp.log(1.0 + jnp.exp(-jnp.abs(gt)))
        parts = _split3(logsig)
        tl, tu = tl_ref[...].astype(BF16), tu_ref[...].astype(BF16)
        cum_f = sum(jnp.dot(p, tu, preferred_element_type=F32) for p in parts)
        cum_b = sum(jnp.dot(p, tl, preferred_element_type=F32) for p in parts)
        sel = row % 8
        gr = jnp.where(sel < 4, gt, jnp.where(row < SUBLANES, cum_f, cum_b)) * LOG2E
        gr_ref[0, :, rs] = gr
        gc_ref[0, rs] = jnp.concatenate([gr, jnp.zeros((LANES - 2 * SUBLANES, tm), F32)], axis=0).T


def _chunk_tri(tm):
    i = np.arange(tm)
    same = (i[:, None] // ML_CHUNK) == (i[None, :] // ML_CHUNK)
    tl = same & (i[None, :] <= i[:, None])
    tu = same & (i[None, :] >= i[:, None])
    return jnp.asarray(tl, F32), jnp.asarray(tu, F32)


def _odd_in(x, mod, layer, mrow, ng, w, pm, gain, cos, sin, gb, tm):
    B, L, _ = x.shape
    nsub = 2 if L % (2 * tm) == 0 else 1
    bm = nsub * tm

    def rows(c, dt=BF16):
        return jax.ShapeDtypeStruct((B, L, c), dt), pl.BlockSpec((1, bm, c), lambda b, i: (b, i, 0))

    def cols(c, dt=BF16):
        return jax.ShapeDtypeStruct((B, c, L), dt), pl.BlockSpec((1, c, bm), lambda b, i: (b, 0, i))

    outs = [cols(ATT_Q), rows(2 * LANES), cols(LANES), rows(ML_QK), cols(ML_QK), rows(ML_V), rows(ML_V),
            rows(LANES, F32), cols(2 * SUBLANES, F32)]
    nqk = OV - OQ
    tl, tu = _chunk_tri(tm)
    return pl.pallas_call(
        functools.partial(_odd_in_kernel, mrow=mrow, tm=tm, nsub=nsub),
        out_shape=tuple(o[0] for o in outs),
        grid=(B, L // bm),
        in_specs=[
            pl.BlockSpec((1, bm, D_MODEL), lambda b, i: (b, i, 0)),
            _mod_spec(layer),
            _const_spec((1, D_MODEL)),
            _const_spec((D_MODEL, ODD_COLS)),
            _const_spec((nqk, nqk)),
            _const_spec((1, nqk)),
            pl.BlockSpec((bm, LANES), lambda b, i: (i, 0)),
            pl.BlockSpec((bm, LANES), lambda b, i: (i, 0)),
            _const_spec((1, LANES)),
            _const_spec((tm, tm)),
            _const_spec((tm, tm)),
        ],
        out_specs=tuple(o[1] for o in outs),
        compiler_params=_cparams(("parallel", "arbitrary")),
        name="odd_in",
    )(x, mod, ng, w, pm, gain, cos, sin, gb, tl, tu)


def _rope_tables(L):
    rows = L // GRID_W
    pos = np.stack([np.repeat(np.arange(rows), GRID_W), np.tile(np.arange(GRID_W), rows)]).astype(np.float64)
    axis_dim = HEAD_DIM // 2
    inv_freq = ROPE_THETA ** (-np.arange(0, axis_dim, 2, dtype=np.float64) / axis_dim)
    ang = pos[:, :, None] * inv_freq
    c, sn = np.cos(ang), np.sin(ang)
    cos = np.concatenate([c[0], c[0], c[1], c[1]], axis=-1)
    sin = np.concatenate([-sn[0], sn[0], -sn[1], sn[1]], axis=-1)
    return jnp.asarray(np.tile(cos, (1, 2)), F32), jnp.asarray(np.tile(sin, (1, 2)), F32)


def _att_steps(n, sink_ref, qt_ref, kvp_ref, kvc_ref, kvn_ref, kvx_ref, vtp_ref, vtc_ref, vtn_ref, vtx_ref,
               o_ref, *, nb):
    half = LANES // 2
    lo = lax.broadcasted_iota(jnp.int32, (1, LANES), 1) < half
    zero = jnp.zeros((), BF16)
    cx = kvx_ref.shape[1]
    klocal = ([kvp_ref[0]] + [kvc_ref[0, i * BLOCK:(i + 1) * BLOCK] for i in range(ATT_QB)] + [kvn_ref[0]])
    vlocal = ([vtp_ref[0]] + [vtc_ref[0, :, i * BLOCK:(i + 1) * BLOCK] for i in range(ATT_QB)] + [vtn_ref[0]])
    kctx = [kvx_ref[0, i:i + BLOCK] for i in range(0, cx, BLOCK)]
    vctx = [vtx_ref[0, :, i:i + BLOCK] for i in range(0, cx, BLOCK)]
    ones = jnp.ones((half, BLOCK), BF16)

    kj = lax.broadcasted_iota(jnp.int32, (BLOCK, BLOCK), 0)
    qi = lax.broadcasted_iota(jnp.int32, (BLOCK, BLOCK), 1)

    def twice(x):
        return jnp.concatenate([x, x], axis=1)

    left = lax.broadcasted_iota(jnp.int32, (1, 2 * BLOCK), 1) < BLOCK
    group = ATT_HEADS // ATT_KV_HEADS

    def block(qb):
        blk_id = n * ATT_QB + qb
        kblocks = klocal[qb:qb + 3] + kctx
        vblocks = vlocal[qb:qb + 3] + vctx
        bias = [twice(jnp.where((kj >= qi) & (blk_id > 0), 0.0, NEG_BIG)).astype(BF16), None,
                twice(jnp.where((kj <= qi) & (blk_id < nb - 1), 0.0, NEG_BIG)).astype(BF16)] + [None] * len(kctx)
        for kvh in range(ATT_KV_HEADS):
            t0 = kvh * group // 2
            cols = slice(qb * BLOCK, (qb + 1) * BLOCK)
            qt2 = jnp.concatenate([qt_ref[0, t0 * LANES:(t0 + 1) * LANES, cols],
                                   qt_ref[0, (t0 + 1) * LANES:(t0 + 2) * LANES, cols]], axis=1)
            outs = []
            for par in range(2):
                sk = jnp.where(left, sink_ref[2 * t0 + par], sink_ref[2 * t0 + 2 + par]) * LOG2E
                m = sk.astype(BF16).astype(F32)
                acc = jnp.zeros((LANES, 2 * BLOCK), F32)
                for blk in range(len(kblocks)):
                    k, ks = kblocks[blk][:, :LANES], kblocks[blk][:, LANES:]
                    kh = ((jnp.where(lo, k, zero), jnp.where(lo, zero, ks)),
                          (jnp.where(lo, ks, zero), jnp.where(lo, zero, k)))[kvh][par]
                    s = jnp.dot(kh, qt2, preferred_element_type=F32)
                    s = s.astype(BF16)
                    if bias[blk] is not None:
                        s = s + bias[blk]
                    m_new = jnp.maximum(m, jnp.max(s, axis=0, keepdims=True).astype(F32))
                    p = jnp.exp2(s - m_new.astype(BF16))
                    vh = jnp.concatenate([vblocks[blk][kvh * half:(kvh + 1) * half], ones], axis=0)
                    acc = acc * jnp.exp2(m - m_new) + jnp.dot(vh, p, preferred_element_type=F32)
                    m = m_new
                l = acc[half:half + 1, :] + jnp.exp2(sk - m)
                outs.append(acc[:half] * (1.0 / l))
            ot = jnp.concatenate(outs, axis=0)
            rows = slice(qb * BLOCK, (qb + 1) * BLOCK)
            o_ref[0, rows, t0 * LANES:(t0 + 1) * LANES] = ot[:, :BLOCK].T.astype(BF16)
            o_ref[0, rows, (t0 + 1) * LANES:(t0 + 2) * LANES] = ot[:, BLOCK:].T.astype(BF16)

    return [functools.partial(block, qb) for qb in range(ATT_QB)]


ML_GROUP = 8


def _mlstm_steps(j, qf_ref, ktf_ref, vf_ref, gcf_ref, grf_ref, qb_ref, ktb_ref, vb_ref, gcb_ref, grb_ref,
                 s0_ref, m0_ref, *rest, T, nb, with_output):
    if with_output:
        hf_ref, hb_ref, s_ref, m_ref = rest
    else:
        s_ref, m_ref = rest

    @pl.when(j == 0)
    def _():
        s_ref[...] = s0_ref[...]
        m_ref[...] = m0_ref[...]

    ti = lax.broadcasted_iota(jnp.int32, (T, T), 0)
    si = lax.broadcasted_iota(jnp.int32, (T, T), 1)
    masks = (si <= ti, si >= ti)
    top = lax.broadcasted_iota(jnp.int32, (LANES, 1), 0) < (LANES // 2)
    zero = jnp.zeros((), BF16)
    ones = jnp.ones((T, ML_V_DIM), BF16)
    zpad = jnp.zeros((ML_QK_DIM, 2 * ML_V_DIM), BF16)
    fwd = (qf_ref, ktf_ref, vf_ref, gcf_ref, grf_ref)
    bwd = (qb_ref, ktb_ref, vb_ref, gcb_ref, grb_ref)
    chains = [(b, d, h) for b in range(nb) for d in range(2) for h in range(ML_HEADS)]

    def group(g0):
        grp = chains[g0:g0 + ML_GROUP]
        st = {}
        for (b, d, h) in grp:
            q_ref, kt_ref, v_ref, gc_ref, gr_ref = fwd if d == 0 else bwd
            r = (b * 2 + d) * ML_HEADS + h
            li, lf = 8 * d + h, 8 * d + 4 + h
            gr = gr_ref[b]
            u_row = gr[li:li + 1, :] - gr[lf:lf + 1, :]
            m_prev = m_ref[r:r + 1, 0:1]
            e = dict(r=r, u_row=u_row, m_prev=m_prev,
                     b_end=gr[lf:lf + 1, (T - 1 if d == 0 else 0):(T if d == 0 else 1)],
                     c_end=jnp.maximum(jnp.max(u_row, axis=1, keepdims=True), m_prev))
            if with_output:
                e["mu"] = jnp.where(masks[d], u_row, NEG_BIG)
                c_col = jnp.maximum(jnp.max(e["mu"], axis=1, keepdims=True), m_prev)
                e["cb"] = jnp.broadcast_to(c_col, (T, LANES))
                e["bb"] = jnp.broadcast_to(gc_ref[b, :, lf:lf + 1], (T, LANES))
            st[(b, d, h)] = e
        for (b, d, h) in grp:
            q_ref, kt_ref, v_ref, gc_ref, gr_ref = fwd if d == 0 else bwd
            e = st[(b, d, h)]
            pair = h // 2
            ktp = kt_ref[b, pair * LANES:(pair + 1) * LANES, :]
            e["kth"] = ktp[(h % 2) * ML_QK_DIM:(h % 2 + 1) * ML_QK_DIM, :]
            e["vext"] = jnp.concatenate([v_ref[b, :, h * ML_V_DIM:(h + 1) * ML_V_DIM], ones], axis=1)
            if with_output:
                qp = q_ref[b, :, pair * LANES:(pair + 1) * LANES]
                kpad = jnp.where(top, ktp, zero) if h % 2 == 0 else jnp.where(top, zero, ktp)
                qk = jnp.dot(qp, kpad, preferred_element_type=F32) * jnp.exp2(e["mu"] - e["cb"])
                qs = qp.astype(F32) * jnp.exp2(e["m_prev"] - e["cb"])
                e["lhs"] = jnp.concatenate([qk.astype(BF16), qs.astype(BF16)], axis=1)
        for (b, d, h) in grp:
            e = st[(b, d, h)]
            e["s_prev"] = s_ref[e["r"]]
            if with_output:
                sb = e["s_prev"].astype(BF16)
                rhs = jnp.concatenate([e["vext"]] + ([sb, zpad] if h % 2 == 0 else [zpad, sb]), axis=0)
                tot = jnp.dot(e["lhs"], rhs, preferred_element_type=F32)
                floor = jnp.exp2(-(e["bb"] + e["cb"]))
                hout = tot[:, :ML_V_DIM] / jnp.maximum(jnp.abs(tot[:, ML_V_DIM:]), floor)
                o_ref = hf_ref if d == 0 else hb_ref
                o_ref[b, :, h * ML_V_DIM:(h + 1) * ML_V_DIM] = hout
        for (b, d, h) in grp:
            e = st[(b, d, h)]
            kw = (e["kth"].astype(F32) * jnp.exp2(e["u_row"] - e["c_end"])).astype(BF16)
            upd = jnp.dot(kw, e["vext"], preferred_element_type=F32)
            s_ref[e["r"]] = jnp.exp2(e["m_prev"] - e["c_end"]) * e["s_prev"] + upd
            m_ref[e["r"]:e["r"] + 1, :] = jnp.broadcast_to(e["b_end"] + e["c_end"], (1, LANES))

    return [functools.partial(group, g0) for g0 in range(0, len(chains), ML_GROUP)]


def _mlstm_kernel(*refs, T, nb, with_output):
    for step in _mlstm_steps(pl.program_id(0), *refs, T=T, nb=nb, with_output=with_output):
        step()


def _att_mlstm_kernel(*refs, T, nb, nblk, ns):
    j = pl.program_id(0)
    sink_ref, att_in, ml_in = refs[0], refs[1:10], refs[10:22]
    o_ref, ml_out = refs[22], refs[23:]
    att = _att_steps(lax.rem(j, ns), sink_ref, *att_in, o_ref, nb=nblk)
    ml = _mlstm_steps(j, *ml_in, *ml_out, T=T, nb=nb, with_output=True)
    for k in range(max(len(att), len(ml))):
        if k < len(att):
            att[k]()
        if k < len(ml):
            ml[k]()


def _mlstm_specs(B, T, nc):
    up = lambda j: j
    down = lambda j: nc - 1 - j

    def specs(o):
        return [pl.BlockSpec((B, T, ML_QK), lambda j: (0, o(j), 0)),
                pl.BlockSpec((B, ML_QK, T), lambda j: (0, 0, o(j))),
                pl.BlockSpec((B, T, ML_V), lambda j: (0, o(j), 0)),
                pl.BlockSpec((B, T, LANES), lambda j: (0, o(j), 0)),
                pl.BlockSpec((B, 2 * SUBLANES, T), lambda j: (0, 0, o(j)))]

    nchains = B * 2 * ML_HEADS
    s_spec = _const_spec((nchains, ML_QK_DIM, 2 * ML_V_DIM))
    m_spec = _const_spec((nchains, LANES))
    state_shape = [jax.ShapeDtypeStruct((nchains, ML_QK_DIM, 2 * ML_V_DIM), F32),
                   jax.ShapeDtypeStruct((nchains, LANES), F32)]
    h_specs = [pl.BlockSpec((B, T, ML_V), lambda j: (0, up(j), 0)),
               pl.BlockSpec((B, T, ML_V), lambda j: (0, down(j), 0))]
    return specs(up) + specs(down) + [s_spec, m_spec], h_specs, [s_spec, m_spec], state_shape


def _att_mlstm(qt, kv, kvx, vt, vtx, sink, qm, kmt, vm, gc, gr, s0, m0):
    B, _, L = qt.shape
    T = ML_CHUNK
    nc = L // T
    nblk = L // BLOCK
    ns = nblk // ATT_QB
    assert nc == B * ns
    cx = kvx.shape[1]
    kw = kv.shape[2]
    vw = vt.shape[1]
    wide = ATT_QB * BLOCK
    smp = lambda j: j // ns
    cur = lambda j: j % ns
    prev = lambda j: jnp.maximum(cur(j) * ATT_QB - 1, 0)
    nxt = lambda j: jnp.minimum((cur(j) + 1) * ATT_QB, nblk - 1)
    att_specs = [
        pl.BlockSpec(memory_space=pltpu.SMEM),
        pl.BlockSpec((1, ATT_Q, wide), lambda j: (smp(j), 0, cur(j))),
        pl.BlockSpec((1, BLOCK, kw), lambda j: (smp(j), prev(j), 0)),
        pl.BlockSpec((1, wide, kw), lambda j: (smp(j), cur(j), 0)),
        pl.BlockSpec((1, BLOCK, kw), lambda j: (smp(j), nxt(j), 0)),
        pl.BlockSpec((1, cx, kw), lambda j: (smp(j), 0, 0)),
        pl.BlockSpec((1, vw, BLOCK), lambda j: (smp(j), 0, prev(j))),
        pl.BlockSpec((1, vw, wide), lambda j: (smp(j), 0, cur(j))),
        pl.BlockSpec((1, vw, BLOCK), lambda j: (smp(j), 0, nxt(j))),
        pl.BlockSpec((1, vw, cx), lambda j: (smp(j), 0, 0)),
    ]
    ml_in, h_specs, st_specs, st_shape = _mlstm_specs(B, T, nc)
    h_shape = jax.ShapeDtypeStruct((B, L, ML_V), F32)
    att, hf, hb, _, _ = pl.pallas_call(
        functools.partial(_att_mlstm_kernel, T=T, nb=B, nblk=nblk, ns=ns),
        out_shape=(jax.ShapeDtypeStruct((B, L, ATT_Q), BF16), h_shape, h_shape, *st_shape),
        grid=(nc,),
        in_specs=att_specs + ml_in,
        out_specs=(pl.BlockSpec((1, wide, ATT_Q), lambda j: (smp(j), cur(j), 0)), *h_specs, *st_specs),
        compiler_params=_cparams(("arbitrary",)),
        name="attention_mlstm",
    )(sink, qt, kv, kv, kv, kvx, vt, vt, vt, vtx, qm, kmt, vm, gc, gr, qm, kmt, vm, gc, gr, s0, m0)
    return att, hf, hb


def _mlstm(qm, kmt, vm, gc, gr, s0, m0, with_output):
    B, L, _ = qm.shape
    T = ML_CHUNK
    nc = L // T
    ml_in, h_specs, st_specs, st_shape = _mlstm_specs(B, T, nc)
    out_shape, out_specs = st_shape, st_specs
    if with_output:
        out_shape = [jax.ShapeDtypeStruct((B, L, ML_V), F32)] * 2 + out_shape
        out_specs = h_specs + out_specs
    return pl.pallas_call(
        functools.partial(_mlstm_kernel, T=T, nb=B, with_output=with_output),
        out_shape=tuple(out_shape),
        grid=(nc,),
        in_specs=ml_in,
        out_specs=tuple(out_specs),
        compiler_params=_cparams(("arbitrary",)),
        name="mlstm_scan" if with_output else "mlstm_context_state",
    )(qm, kmt, vm, gc, gr, qm, kmt, vm, gc, gr, s0, m0)


def _odd_tables(gate_b, q_g, k_g):
    assert sum([ATT_Q, ATT_KV, ATT_KV, ML_QK, ML_QK, ML_V, ML_V]) == OG
    nqk = OV - OQ
    head = np.arange(nqk) // HEAD_DIM
    pm = jnp.asarray((head[:, None] == head[None, :]) / HEAD_DIM, F32)
    gain = jnp.concatenate([jnp.tile(q_g, ATT_HEADS), jnp.tile(k_g, ATT_KV_HEADS)])[None, :]
    gb = jnp.pad(gate_b.reshape(1, -1), ((0, 0), (0, LANES - gate_b.size)))
    return pm, gain, gb


def kernel(x, c, ctx, c_ctx, ada_w, ada_b, norm_g, even_w_in, even_conv, even_w_out, odd_w_in, odd_gate_b,
           odd_q_g, odd_k_g, odd_sink, odd_w_out, ffn_w_up, ffn_conv, ffn_w_down):
    B, L, _ = x.shape
    C = ctx.shape[1]
    depth = ada_w.shape[0]
    assert depth == 2 and L % (DFT_N1 * SUBLANES) == 0 and C % ML_CHUNK == 0

    cv = jnp.concatenate([c, c_ctx[None, :], jnp.zeros((SUBLANES - B - 1, D_MODEL), F32)], axis=0)
    mod = _modulation(cv, ada_w, ada_b)
    lat, cx = None, B

    tm = min(512, L)
    tc = _channel_dft_table()
    w_in0, w_out0 = _cast_weights((even_w_in, even_w_out), 0, (EVEN_IN, D_MODEL))
    first = ((ffn_w_up, 0, 2 * D_FF, 1), (ffn_w_down, 0, D_MODEL, 1))
    later = ((ffn_w_up, 1, 2 * D_FF, 1), (ffn_w_down, 1, D_MODEL, 2), (odd_w_in, 0, ODD_COLS, 1),
             (odd_w_out, 0, D_MODEL, 1))

    ng00, ng01 = norm_g[0, 0][None, :], norm_g[0, 1][None, :]

    yc, zr, zi, w_up0, w_down0 = _even_in(x, mod, 0, lat, ng00, w_in0, even_conv[0], tc, tm, L // DFT_N1, first)
    xl, w_up1, w_down1, w_in1, w_out1 = _mix_ffn(x, (yc, _seq_dft(zr, zi, tm)), mod, 0, lat, ng01, w_out0, w_up0,
                                                 ffn_conv, w_down0, tm, odd=False, side=later)
    yc, zr, zi = _even_in(ctx, mod, 0, cx, ng00, w_in0, even_conv[0], tc, C, None)
    xc = _mix_ffn(ctx, (yc, _dense_seq_dft(zr, zi)), mod, 0, cx, ng01, w_out0, w_up0, ffn_conv, w_down0, C, odd=False)

    pm, gain, gb = _odd_tables(odd_gate_b[0], odd_q_g[0], odd_k_g[0])
    ng10, ng11 = norm_g[1, 0][None, :], norm_g[1, 1][None, :]
    cos, sin = _rope_tables(L)
    one, nil = jnp.ones((C, LANES), F32), jnp.zeros((C, LANES), F32)
    qt, kv, vt, qm, kmt, vm, om, gc, gr = _odd_in(xl, mod, 1, lat, ng10, w_in1, pm, gain, cos, sin, gb, tm)
    _, kvx, vtx, qmx, kmtx, vmx, _, gcx, grx = _odd_in(xc, mod, 1, cx, ng10, w_in1, pm, gain, one, nil, gb, C)

    nchains = B * 2 * ML_HEADS
    s0 = jnp.zeros((nchains, ML_QK_DIM, 2 * ML_V_DIM), F32)
    m0 = jnp.zeros((nchains, LANES), F32)
    s1, m1 = _mlstm(qmx, kmtx, vmx, gcx, grx, s0, m0, with_output=False)
    att, hf, hb = _att_mlstm(qt, kv, kvx, vt, vtx, odd_sink[0], qm, kmt, vm, gc, gr, s1, m1)
    return _mix_ffn(xl, (att, hf, hb, om), mod, 1, lat, ng11, w_out1, w_up1, ffn_conv, w_down1, tm, odd=True)
```

```python
import functools

import numpy as np
import jax
import jax.numpy as jnp
from jax import lax
from jax.experimental import pallas as pl
from jax.experimental.pallas import tpu as pltpu

F32 = jnp.float32
BF16 = jnp.bfloat16

D_MODEL = 1024
GRID_W = 64
EPS = 1e-6
SC_CH = 512
FT_CH = 512
FT_GROUPS = 4
FT_GROUP_CH = FT_CH // FT_GROUPS
EVEN_IN = 3 * SC_CH + FT_CH
ATT_HEADS = 8
ATT_KV_HEADS = 2
HEAD_DIM = 64
ATT_SCALE = HEAD_DIM ** -0.5
WINDOW = 128
BLOCK = 128
ROPE_THETA = 10000.0
ML_HEADS = 4
ML_QK_DIM = 64
ML_V_DIM = 128
ATT_Q = ATT_HEADS * HEAD_DIM
ATT_KV = ATT_KV_HEADS * HEAD_DIM
ML_QK = ML_HEADS * ML_QK_DIM
ML_V = ML_HEADS * ML_V_DIM
D_FF = 2816

LANES = 128
SUBLANES = 8
VMEM_LIMIT_BYTES = 56 * 1024 * 1024

DFT_N1 = 128
FF_CHUNK = 256
N_FF_CHUNKS = D_FF // FF_CHUNK
ML_CHUNK = 128
ATT_QB = 2
NEG_BIG = -1e30
LOG2E = 1.4426950408889634

OQ, OK_, OV, OQM, OKM, OVM, OOM, OG = 0, 512, 640, 768, 1024, 1280, 1792, 2304
ODD_COLS = OG + LANES


def _cparams(sem):
    return pltpu.CompilerParams(dimension_semantics=sem, vmem_limit_bytes=VMEM_LIMIT_BYTES)


def _sigmoid(x):
    return 1.0 / (1.0 + jnp.exp(-x))


def _norm_mod(x, g, shift, scale):
    y = x * lax.rsqrt(jnp.mean(x * x, axis=-1, keepdims=True) + EPS)
    return y * g * (1.0 + scale) + shift


def _mod_vec(mod_ref, k, mrow):
    r = pl.program_id(0) if mrow is None else mrow
    return mod_ref[k, pl.ds(r, 1), :]


def _mod_spec(layer):
    return pl.BlockSpec((None, 6, SUBLANES, D_MODEL), lambda *_: (layer, 0, 0, 0))


def _halo_rows(x, xn, xp, shift, scale, ng_ref):
    g = ng_ref[...]
    parts = [_norm_mod(r, g, shift, scale) for r in (x, xn, xp)]
    return jnp.concatenate(parts, axis=0).astype(BF16)


def _halo_valid(tm, i, nt):
    row = lax.broadcasted_iota(jnp.int32, (tm + 2 * SUBLANES, 1), 0)
    return ((row < tm) | ((row < tm + SUBLANES) & (i < nt - 1)) | ((row >= tm + SUBLANES) & (i > 0)))


def _conv3(v, cw, tm):
    n = v.shape[0]
    vp = pltpu.roll(v, 1, 0)[:tm]
    vn = pltpu.roll(v, n - 1, 0)[:tm]
    return vp * cw[0:1] + v[:tm] * cw[1:2] + vn * cw[2:3]


def _halo_specs(tm, L):
    hb = tm // SUBLANES
    last = L // SUBLANES - 1
    return [
        pl.BlockSpec((1, tm, D_MODEL), lambda b, i: (b, i, 0)),
        pl.BlockSpec((1, SUBLANES, D_MODEL), lambda b, i: (b, jnp.minimum((i + 1) * hb, last), 0)),
        pl.BlockSpec((1, SUBLANES, D_MODEL), lambda b, i: (b, jnp.maximum(i * hb - 1, 0), 0)),
    ]


def _const_spec(shape):
    nd = len(shape)
    return pl.BlockSpec(shape, lambda *_: (0,) * nd)


def _resident_spec(shape):
    nd = len(shape)
    return pl.BlockSpec(shape, lambda *_: (0,) * nd, pipeline_mode=pl.Buffered(1))


def _weight_spec(w, shape, layer):
    return _resident_spec(shape) if w.ndim == len(shape) else _layer_spec(shape, layer)


def _layer_spec(shape, layer):
    nd = len(shape)
    return pl.BlockSpec((None,) + tuple(shape), lambda *_: (layer,) + (0,) * nd, pipeline_mode=pl.Buffered(1))


def _split_dot(x, p):
    hi = x.astype(BF16)
    lo = (x - hi.astype(F32)).astype(BF16)
    return (jnp.dot(hi, p, preferred_element_type=F32) + jnp.dot(lo, p, preferred_element_type=F32))


def _mod_kernel(cv_ref, w_ref, b_ref, o_ref):
    cv = cv_ref[...]
    o_ref[0, 0] = _split_dot(cv * _sigmoid(cv), w_ref[0].astype(BF16)) + b_ref[0, 0]


def _modulation(cv, ada_w, ada_b):
    depth, _, n = ada_w.shape
    nv = n // D_MODEL
    return pl.pallas_call(
        _mod_kernel,
        out_shape=jax.ShapeDtypeStruct((depth, nv, SUBLANES, D_MODEL), F32),
        grid=(depth, nv),
        in_specs=[
            pl.BlockSpec((SUBLANES, D_MODEL), lambda l, j: (0, 0)),
            pl.BlockSpec((1, D_MODEL, D_MODEL), lambda l, j: (l, 0, j)),
            pl.BlockSpec((1, 1, 1, D_MODEL), lambda l, j: (l, j, 0, 0)),
        ],
        out_specs=pl.BlockSpec((1, 1, SUBLANES, D_MODEL), lambda l, j: (l, j, 0, 0)),
        compiler_params=_cparams(("arbitrary", "arbitrary")),
        name="modulation",
    )(cv, ada_w, ada_b.reshape(depth, nv, 1, D_MODEL))


CAST_STEPS = 8


def _cast_refs(srcs, dsts):
    for src, dst in zip(srcs, dsts):
        w = src[...].astype(BF16)
        pad = dst.shape[-1] - src.shape[-1]
        if pad:
            w = jnp.concatenate([w, jnp.zeros(w.shape[:-1] + (pad,), BF16)], axis=-1)
        dst[...] = w


def _cast_kernel(*refs):
    n = len(refs) // 2
    _cast_refs(refs[:n], refs[n:])


def _layer_rows_spec(w, layer, rb, imap):
    if w.ndim == 2:
        return pl.BlockSpec((rb, w.shape[1]), lambda *a: (imap(*a), 0))
    return pl.BlockSpec((None, rb, w.shape[2]), lambda *a: (layer, imap(*a), 0))


def _side_cast_specs(side, steps, step_of):
    ins, outs, shapes = [], [], []
    for w, wl, wd, per in side:
        r = w.shape[-2]
        rb = r * per // steps
        blk = lambda *a, per=per: step_of(*a) // per
        ins.append(_layer_rows_spec(w, wl, rb, blk))
        outs.append(pl.BlockSpec((rb, wd), lambda *a, blk=blk: (blk(*a), 0)))
        shapes.append(jax.ShapeDtypeStruct((r, wd), BF16))
    return ins, outs, shapes


def _cast_weights(ws, layer, widths):
    in_specs, out_specs, out_shape = [], [], []
    for w, wd in zip(ws, widths):
        r = w.shape[-2]
        in_specs.append(_layer_rows_spec(w, layer, r // CAST_STEPS, lambda i: i))
        out_specs.append(pl.BlockSpec((r // CAST_STEPS, wd), lambda i: (i, 0)))
        out_shape.append(jax.ShapeDtypeStruct((r, wd), BF16))
    return pl.pallas_call(
        _cast_kernel,
        out_shape=tuple(out_shape),
        grid=(CAST_STEPS,),
        in_specs=in_specs,
        out_specs=tuple(out_specs),
        compiler_params=_cparams(("arbitrary",)),
        name="cast_weights",
    )(*ws)


def _even_in_kernel(x_ref, xn_ref, xp_ref, mod_ref, ng_ref, w_ref, cw_ref, tc_ref, *rest, tm, nt, n2, mrow, nsub):
    nside = (len(rest) - 3) // 2
    yc_ref, zr_ref, zi_ref = rest[nside:nside + 3]
    _cast_refs(rest[:nside], rest[nside + 3:])
    i = pl.program_id(1)
    shift, scale = _mod_vec(mod_ref, 0, mrow), _mod_vec(mod_ref, 1, mrow)
    tc = tc_ref[...].astype(BF16)
    for s in range(nsub):
        lo = s * tm
        x = x_ref[0, lo:lo + tm]
        xn = xn_ref[0] if s == nsub - 1 else x_ref[0, lo + tm:lo + tm + SUBLANES]
        xp = xp_ref[0] if s == 0 else x_ref[0, lo - SUBLANES:lo]
        hh = _halo_rows(x, xn, xp, shift, scale, ng_ref)
        u = jnp.dot(hh, w_ref[...], preferred_element_type=F32)
        v = u[:, SC_CH:2 * SC_CH] * u[:, 2 * SC_CH:3 * SC_CH]
        v = jnp.where(_halo_valid(tm, i * nsub + s, nt), v, 0.0)
        yc = u[:tm, :SC_CH] * _conv3(v, cw_ref[...], tm)
        yc_ref[0, lo:lo + tm] = yc.astype(BF16)
        uf = u[:tm, 3 * SC_CH:].astype(BF16)
        for g in range(FT_GROUPS):
            sl = slice(g * FT_GROUP_CH, (g + 1) * FT_GROUP_CH)
            ab = jnp.dot(uf[:, sl], tc, preferred_element_type=F32)
            if n2 is None:
                zr_ref[g, 0, lo:lo + tm] = ab[:, :FT_GROUP_CH]
                zi_ref[g, 0, lo:lo + tm] = ab[:, FT_GROUP_CH:]
            else:
                for a in range(tm // n2):
                    dst = pl.ds(lo + a, n2, stride=tm // n2)
                    zr_ref[g, 0, dst, :] = ab[n2 * a:n2 * (a + 1), :FT_GROUP_CH]
                    zi_ref[g, 0, dst, :] = ab[n2 * a:n2 * (a + 1), FT_GROUP_CH:]


def _even_in(x, mod, layer, mrow, ng, w_in, cw, tc, tm, n2, side=()):
    B, L, _ = x.shape
    nt = L // tm
    nsub = 2 if nt % 2 == 0 else 1
    bm = nsub * tm
    out = jax.ShapeDtypeStruct((B, L, FT_CH), BF16)
    zout = jax.ShapeDtypeStruct((FT_GROUPS, B, L, FT_GROUP_CH), F32)
    ospec = pl.BlockSpec((1, bm, FT_CH), lambda b, i: (b, i, 0))
    zspec = pl.BlockSpec((FT_GROUPS, 1, bm, FT_GROUP_CH), lambda b, i: (0, b, i, 0))
    ns = nt // nsub
    side_in, side_out, side_shape = _side_cast_specs(side, B * ns, lambda b, i: b * ns + i)
    return pl.pallas_call(
        functools.partial(_even_in_kernel, tm=tm, nt=nt, n2=n2, mrow=mrow, nsub=nsub),
        out_shape=(out, zout, zout, *side_shape),
        grid=(B, ns),
        in_specs=_halo_specs(bm, L) + [
            _mod_spec(layer),
            _const_spec((1, D_MODEL)),
            _const_spec((D_MODEL, EVEN_IN)),
            _const_spec((3, SC_CH)),
            _const_spec((FT_GROUP_CH, 2 * FT_GROUP_CH)),
        ] + side_in,
        out_specs=(ospec, zspec, zspec, *side_out),
        compiler_params=_cparams(("parallel", "arbitrary")),
        name="even_in",
    )(x, x, x, mod, ng, w_in, cw, tc, *[e[0] for e in side])


def _seq_dft_kernel(zr_ref, zi_ref, m_ref, g_ref, y_ref, o_scr, *, n2, tm):
    m1 = m_ref[...].astype(BF16)
    chunk = tm // n2
    ntile = DFT_N1 // chunk

    def rows(ref, j):
        return [ref[0, 0, t * tm + j * chunk:t * tm + (j + 1) * chunk, :] for t in range(ntile)]

    for j in range(n2):
        z = jnp.concatenate(rows(zr_ref, j) + rows(zi_ref, j), axis=0)
        o_scr[2 * DFT_N1 * j:2 * DFT_N1 * (j + 1), :] = jnp.dot(m1, z.astype(BF16), preferred_element_type=F32)
    for k1 in range(DFT_N1):
        o = jnp.concatenate([o_scr[pl.ds(k1, n2, stride=2 * DFT_N1), :],
                             o_scr[pl.ds(DFT_N1 + k1, n2, stride=2 * DFT_N1), :]], axis=0)
        y_ref[0, pl.ds(k1, n2, stride=DFT_N1), :] = jnp.dot(g_ref[k1].astype(BF16), o.astype(BF16), preferred_element_type=F32)


def _dft_tables(L):
    n2 = L // DFT_N1
    k = np.arange(DFT_N1)
    a = 2.0 * np.pi * ((k[:, None] * k[None, :]) % DFT_N1) / DFT_N1
    er, ei = np.cos(a) / np.sqrt(DFT_N1), -np.sin(a) / np.sqrt(DFT_N1)
    m1 = np.block([[er, -ei], [ei, er]])
    k1 = np.arange(DFT_N1)[:, None, None]
    k2 = np.arange(n2)[None, :, None]
    nn = np.arange(n2)[None, None, :]
    th = 2.0 * np.pi * ((nn * (k1 + DFT_N1 * k2)) % L) / L
    g = np.concatenate([np.cos(th), np.sin(th)], axis=-1) / np.sqrt(n2)
    return jnp.asarray(m1, F32), jnp.asarray(g, F32)


def _channel_dft_table():
    k = np.arange(FT_GROUP_CH)
    a = 2.0 * np.pi * ((k[:, None] * k[None, :]) % FT_GROUP_CH) / FT_GROUP_CH
    t = np.concatenate([np.cos(a), -np.sin(a)], axis=1) / np.sqrt(FT_GROUP_CH)
    return jnp.asarray(t, F32)


def _seq_dft(zr, zi, tm):
    G, B, L, C = zr.shape
    n2 = L // DFT_N1
    m1, g = _dft_tables(L)
    zspec = pl.BlockSpec((1, 1, L, C), lambda b, j: (j, b, 0, 0))
    return pl.pallas_call(
        functools.partial(_seq_dft_kernel, n2=n2, tm=tm),
        out_shape=jax.ShapeDtypeStruct((B, L, G * C), F32),
        grid=(B, G),
        in_specs=[zspec, zspec, _const_spec((2 * DFT_N1, 2 * DFT_N1)), _const_spec((DFT_N1, n2, 2 * n2))],
        out_specs=pl.BlockSpec((1, L, C), lambda b, j: (b, 0, j)),
        scratch_shapes=[pltpu.VMEM((2 * DFT_N1 * n2, LANES), F32)],
        compiler_params=_cparams(("parallel", "arbitrary")),
        name="seq_dft",
    )(zr, zi, m1, g)


def _dense_dft_kernel(zr_ref, zi_ref, t_ref, y_ref):
    z = jnp.concatenate([zr_ref[0, 0], zi_ref[0, 0]], axis=0).astype(BF16)
    y_ref[0] = jnp.dot(t_ref[...].astype(BF16), z, preferred_element_type=F32)


def _dense_seq_dft(zr, zi):
    G, B, L, C = zr.shape
    k = np.arange(L)
    a = 2.0 * np.pi * ((k[:, None] * k[None, :]) % L) / L
    t = jnp.asarray(np.concatenate([np.cos(a), np.sin(a)], axis=1) / np.sqrt(L), F32)
    zspec = pl.BlockSpec((1, 1, L, C), lambda b, j: (j, b, 0, 0))
    return pl.pallas_call(
        _dense_dft_kernel,
        out_shape=jax.ShapeDtypeStruct((B, L, G * C), F32),
        grid=(B, G),
        in_specs=[zspec, zspec, _const_spec((L, 2 * L))],
        out_specs=pl.BlockSpec((1, L, C), lambda b, j: (b, 0, j)),
        compiler_params=_cparams(("arbitrary", "arbitrary")),
        name="dense_seq_dft",
    )(zr, zi, t)


HALO = 16


def _wide_halo_specs(tm, L, width):
    hb = tm // HALO
    last = L // HALO - 1
    return [
        pl.BlockSpec((1, tm, width), lambda b, i: (b, i, 0)),
        pl.BlockSpec((1, HALO, width), lambda b, i: (b, jnp.minimum((i + 1) * hb, last), 0)),
        pl.BlockSpec((1, HALO, width), lambda b, i: (b, jnp.maximum(i * hb - 1, 0), 0)),
    ]


def _circ(t_ref, n_ref, p_ref):
    return jnp.concatenate([t_ref[0], n_ref[0], p_ref[0]], axis=0)


def _mix_ffn_kernel(*refs, tm, nt, odd, mrow, nside):
    n_in = 15 if odd else 9
    x3, rest = refs[:3], refs[3:n_in]
    mod_ref, ng_ref, wo_ref, wu_ref, cw_ref, wd_ref = refs[n_in:n_in + 6]
    side_src = refs[n_in + 6:n_in + 6 + nside]
    o_ref = refs[n_in + 6 + nside]
    side_dst = refs[n_in + 7 + nside:n_in + 7 + 2 * nside]
    hh_scr, act_scr = refs[n_in + 7 + 2 * nside:]
    _cast_refs(side_src, side_dst)
    i = pl.program_id(1)
    if odd:
        att, hf, hb, om = (_circ(*rest[k:k + 3]) for k in range(0, 12, 3))
        lhs = jnp.concatenate([att, ((hf + hb) * _sigmoid(om.astype(F32))).astype(BF16)], axis=-1)
    else:
        yc, yf = _circ(*rest[0:3]), _circ(*rest[3:6])
        lhs = jnp.concatenate([yc, yf.astype(BF16)], axis=-1)
    mv = [_mod_vec(mod_ref, k, mrow) for k in range(6)]
    x1 = _circ(*x3) + mv[2] * jnp.dot(lhs, wo_ref[...], preferred_element_type=F32)
    hh_scr[...] = _norm_mod(x1, ng_ref[...], mv[3], mv[4]).astype(BF16)
    row = lax.broadcasted_iota(jnp.int32, (tm + 2 * HALO, 1), 0)
    valid = (row < tm) | ((row < tm + HALO) & (i < nt - 1)) | ((row >= tm + HALO) & (i > 0))
    for c in range(N_FF_CHUNKS):
        lo = c * FF_CHUNK
        g = jnp.dot(hh_scr[...], wu_ref[:, lo:lo + FF_CHUNK], preferred_element_type=F32)
        g = jnp.where(valid, g, 0.0)
        cv = _conv3(g, cw_ref[:, lo:lo + FF_CHUNK], tm)
        val = jnp.dot(hh_scr[:tm, :], wu_ref[:, D_FF + lo:D_FF + lo + FF_CHUNK], preferred_element_type=F32)
        act_scr[:, lo:lo + FF_CHUNK] = (cv * _sigmoid(cv) * val).astype(BF16)
    y = jnp.dot(act_scr[...], wd_ref[...], preferred_element_type=F32)
    o_ref[0] = x1[:tm] + mv[5] * y


def _mix_ffn(x, mixed, mod, layer, mrow, ng, w_out, w_up, cw, w_down, tm, odd, side=()):
    B, L, _ = x.shape
    nt = L // tm
    specs = _wide_halo_specs(tm, L, D_MODEL)
    args = [x, x, x]
    for a in mixed:
        specs += _wide_halo_specs(tm, L, a.shape[-1])
        args += [a, a, a]
    side_in, side_out, side_shape = _side_cast_specs(side, B * nt, lambda b, i: b * nt + i)
    out = pl.pallas_call(
        functools.partial(_mix_ffn_kernel, tm=tm, nt=nt, odd=odd, mrow=mrow, nside=len(side)),
        out_shape=(jax.ShapeDtypeStruct(x.shape, F32), *side_shape),
        grid=(B, nt),
        in_specs=specs + [
            _mod_spec(layer),
            _const_spec((1, D_MODEL)),
            _resident_spec((D_MODEL, D_MODEL)),
            _weight_spec(w_up, (D_MODEL, 2 * D_FF), layer),
            _layer_spec((3, D_FF), layer),
            _weight_spec(w_down, (D_FF, D_MODEL), layer),
        ] + side_in,
        out_specs=(pl.BlockSpec((1, tm, D_MODEL), lambda b, i: (b, i, 0)), *side_out),
        scratch_shapes=[pltpu.VMEM((tm + 2 * HALO, D_MODEL), BF16),
                        pltpu.VMEM((tm, D_FF), BF16)],
        compiler_params=_cparams(("parallel", "arbitrary")),
        name="odd_mix_ffn" if odd else "even_mix_ffn",
    )(*args, mod, ng, w_out, w_up, cw, w_down, *[e[0] for e in side])
    return out if side else out[0]


def _split3(x):
    parts = []
    r = x
    for _ in range(3):
        p = r.astype(BF16)
        parts.append(p)
        r = r - p.astype(F32)
    return parts


def _odd_in_kernel(x_ref, mod_ref, ng_ref, w_ref, pm_ref, gain_ref, cos_ref, sin_ref, gb_ref, tl_ref, tu_ref,
                   qt_ref, kv_ref, vt_ref, qm_ref, kmt_ref, vm_ref, om_ref, gc_ref, gr_ref, *, mrow, tm, nsub):
    for s in range(nsub):
        rs = slice(s * tm, (s + 1) * tm)
        h = _norm_mod(x_ref[0, rs], ng_ref[...], _mod_vec(mod_ref, 0, mrow), _mod_vec(mod_ref, 1, mrow))
        u = jnp.dot(h.astype(BF16), w_ref[...], preferred_element_type=F32)

        uqk = u[:, OQ:OV]
        sq = (uqk * uqk).astype(BF16)
        pm = pm_ref[...].astype(BF16)
        ms = jnp.concatenate([jnp.dot(sq[:, t * LANES:(t + 1) * LANES], pm, preferred_element_type=F32)
                              for t in range((OV - OQ) // LANES)], axis=1)
        rn = uqk * lax.rsqrt(ms + EPS) * gain_ref[...]
        lane = lax.broadcasted_iota(jnp.int32, (1, LANES), 1)
        first = (lane % 32) < 16
        cos = cos_ref[rs, :]
        sin = sin_ref[rs, :]
        roped = []
        for t in range((OV - OQ) // LANES):
            xt = rn[:, t * LANES:(t + 1) * LANES]
            sw = jnp.where(first, pltpu.roll(xt, LANES - 16, 1), pltpu.roll(xt, 16, 1))
            roped.append(xt * cos + sw * sin)
        for t in range(ATT_Q // LANES):
            qt_ref[0, t * LANES:(t + 1) * LANES, rs] = (roped[t] * (ATT_SCALE * LOG2E)).T.astype(BF16)
        k = roped[ATT_Q // LANES]
        v = u[:, OV:OQM]
        half = LANES // 2
        kv_ref[0, rs, 0:LANES] = k.astype(BF16)
        kv_ref[0, rs, LANES:2 * LANES] = pltpu.roll(k, half, 1).astype(BF16)
        vt_ref[0, :, rs] = v.T.astype(BF16)

        qm_ref[0, rs] = u[:, OQM:OKM].astype(BF16)
        for p in range(ML_QK // LANES):
            km = u[:, OKM + p * LANES:OKM + (p + 1) * LANES] * (ML_QK_DIM ** -0.5)
            kmt_ref[0, p * LANES:(p + 1) * LANES, rs] = km.T.astype(BF16)
        vm_ref[0, rs] = u[:, OVM:OOM].astype(BF16)
        om_ref[0, rs] = u[:, OOM:OG].astype(BF16)

        gt = (u[:, OG:ODD_COLS] + gb_ref[...]).T[:2 * SUBLANES, :]
        row = lax.broadcasted_iota(jnp.int32, (2 * SUBLANES, 1), 0)
        logsig = jnp.minimum(gt, 0.0) - jnp.log(1.0 + jnp.exp(-jnp.abs(gt)))
        parts = jnp.concatenate(_split3(logsig), axis=0)
        tl, tu = tl_ref[...].astype(BF16), tu_ref[...].astype(BF16)

        def chunk_cumsum(tri):
            c = jnp.concatenate([jnp.dot(parts[:, k:k + ML_CHUNK], tri, preferred_element_type=F32)
                                 for k in range(0, tm, ML_CHUNK)], axis=1)
            n = 2 * SUBLANES
            return c[0:n] + c[n:2 * n] + c[2 * n:3 * n]

        cum_f, cum_b = chunk_cumsum(tu), chunk_cumsum(tl)
        sel = row % 8
        gr = jnp.where(sel < 4, gt, jnp.where(row < SUBLANES, cum_f, cum_b)) * LOG2E
        gr_ref[0, :, rs] = gr
        gc_ref[0, rs] = jnp.concatenate([gr, jnp.zeros((LANES - 2 * SUBLANES, tm), F32)], axis=0).T


def _chunk_tri():
    i = np.arange(ML_CHUNK)
    return jnp.asarray(i[None, :] <= i[:, None], F32), jnp.asarray(i[None, :] >= i[:, None], F32)


def _odd_in(x, mod, layer, mrow, ng, w, pm, gain, cos, sin, gb, tm):
    B, L, _ = x.shape
    nsub = 2 if L % (2 * tm) == 0 else 1
    bm = nsub * tm

    def rows(c, dt=BF16):
        return jax.ShapeDtypeStruct((B, L, c), dt), pl.BlockSpec((1, bm, c), lambda b, i: (b, i, 0))

    def cols(c, dt=BF16):
        return jax.ShapeDtypeStruct((B, c, L), dt), pl.BlockSpec((1, c, bm), lambda b, i: (b, 0, i))

    outs = [cols(ATT_Q), rows(2 * LANES), cols(LANES), rows(ML_QK), cols(ML_QK), rows(ML_V), rows(ML_V),
            rows(LANES, F32), cols(2 * SUBLANES, F32)]
    nqk = OV - OQ
    tl, tu = _chunk_tri()
    return pl.pallas_call(
        functools.partial(_odd_in_kernel, mrow=mrow, tm=tm, nsub=nsub),
        out_shape=tuple(o[0] for o in outs),
        grid=(B, L // bm),
        in_specs=[
            pl.BlockSpec((1, bm, D_MODEL), lambda b, i: (b, i, 0)),
            _mod_spec(layer),
            _const_spec((1, D_MODEL)),
            _const_spec((D_MODEL, ODD_COLS)),
            _const_spec((LANES, LANES)),
            _const_spec((1, nqk)),
            pl.BlockSpec((bm, LANES), lambda b, i: (i, 0)),
            pl.BlockSpec((bm, LANES), lambda b, i: (i, 0)),
            _const_spec((1, LANES)),
            _const_spec((ML_CHUNK, ML_CHUNK)),
            _const_spec((ML_CHUNK, ML_CHUNK)),
        ],
        out_specs=tuple(o[1] for o in outs),
        compiler_params=_cparams(("parallel", "arbitrary")),
        name="odd_in",
    )(x, mod, ng, w, pm, gain, cos, sin, gb, tl, tu)


def _rope_tables(L):
    rows = L // GRID_W
    pos = np.stack([np.repeat(np.arange(rows), GRID_W), np.tile(np.arange(GRID_W), rows)]).astype(np.float64)
    axis_dim = HEAD_DIM // 2
    inv_freq = ROPE_THETA ** (-np.arange(0, axis_dim, 2, dtype=np.float64) / axis_dim)
    ang = pos[:, :, None] * inv_freq
    c, sn = np.cos(ang), np.sin(ang)
    cos = np.concatenate([c[0], c[0], c[1], c[1]], axis=-1)
    sin = np.concatenate([-sn[0], sn[0], -sn[1], sn[1]], axis=-1)
    return jnp.asarray(np.tile(cos, (1, 2)), F32), jnp.asarray(np.tile(sin, (1, 2)), F32)


def _att_steps(n, sink_ref, qt_ref, kvp_ref, kvc_ref, kvn_ref, kvx_ref, vtp_ref, vtc_ref, vtn_ref, vtx_ref,
               o_ref, *, nb):
    half = LANES // 2
    lo = lax.broadcasted_iota(jnp.int32, (1, LANES), 1) < half
    zero = jnp.zeros((), BF16)
    cx = kvx_ref.shape[1]
    klocal = ([kvp_ref[0]] + [kvc_ref[0, i * BLOCK:(i + 1) * BLOCK] for i in range(ATT_QB)] + [kvn_ref[0]])
    vlocal = ([vtp_ref[0]] + [vtc_ref[0, :, i * BLOCK:(i + 1) * BLOCK] for i in range(ATT_QB)] + [vtn_ref[0]])
    kctx = [kvx_ref[0, i:i + BLOCK] for i in range(0, cx, BLOCK)]
    vctx = [vtx_ref[0, :, i:i + BLOCK] for i in range(0, cx, BLOCK)]
    ones = jnp.ones((half, BLOCK), BF16)

    kj = lax.broadcasted_iota(jnp.int32, (BLOCK, BLOCK), 0)
    qi = lax.broadcasted_iota(jnp.int32, (BLOCK, BLOCK), 1)

    def twice(x):
        return jnp.concatenate([x, x], axis=1)

    left = lax.broadcasted_iota(jnp.int32, (1, 2 * BLOCK), 1) < BLOCK
    group = ATT_HEADS // ATT_KV_HEADS

    def block(qb):
        blk_id = n * ATT_QB + qb
        kblocks = klocal[qb:qb + 3] + kctx
        vblocks = vlocal[qb:qb + 3] + vctx
        bias = [twice(jnp.where((kj >= qi) & (blk_id > 0), 0.0, NEG_BIG)).astype(BF16), None,
                twice(jnp.where((kj <= qi) & (blk_id < nb - 1), 0.0, NEG_BIG)).astype(BF16)] + [None] * len(kctx)
        for kvh in range(ATT_KV_HEADS):
            t0 = kvh * group // 2
            cols = slice(qb * BLOCK, (qb + 1) * BLOCK)
            qt2 = jnp.concatenate([qt_ref[0, t0 * LANES:(t0 + 1) * LANES, cols],
                                   qt_ref[0, (t0 + 1) * LANES:(t0 + 2) * LANES, cols]], axis=1)
            outs = []
            for par in range(2):
                sk = jnp.where(left, sink_ref[2 * t0 + par], sink_ref[2 * t0 + 2 + par]) * LOG2E
                m = sk.astype(BF16).astype(F32)
                acc = jnp.zeros((LANES, 2 * BLOCK), F32)
                for blk in range(len(kblocks)):
                    k, ks = kblocks[blk][:, :LANES], kblocks[blk][:, LANES:]
                    kh = ((jnp.where(lo, k, zero), jnp.where(lo, zero, ks)),
                          (jnp.where(lo, ks, zero), jnp.where(lo, zero, k)))[kvh][par]
                    s = jnp.dot(kh, qt2, preferred_element_type=F32)
                    s = s.astype(BF16)
                    if bias[blk] is not None:
                        s = s + bias[blk]
                    m_new = jnp.maximum(m, jnp.max(s, axis=0, keepdims=True).astype(F32))
                    p = jnp.exp2(s - m_new.astype(BF16))
                    vh = jnp.concatenate([vblocks[blk][kvh * half:(kvh + 1) * half], ones], axis=0)
                    acc = acc * jnp.exp2(m - m_new) + jnp.dot(vh, p, preferred_element_type=F32)
                    m = m_new
                l = acc[half:half + 1, :] + jnp.exp2(sk - m)
                outs.append(acc[:half] * (1.0 / l))
            ot = jnp.concatenate(outs, axis=0)
            rows = slice(qb * BLOCK, (qb + 1) * BLOCK)
            o_ref[0, rows, t0 * LANES:(t0 + 1) * LANES] = ot[:, :BLOCK].T.astype(BF16)
            o_ref[0, rows, (t0 + 1) * LANES:(t0 + 2) * LANES] = ot[:, BLOCK:].T.astype(BF16)

    return [functools.partial(block, qb) for qb in range(ATT_QB)]


ML_GROUP = 8


def _mlstm_steps(j, qf_ref, ktf_ref, vf_ref, gcf_ref, grf_ref, qb_ref, ktb_ref, vb_ref, gcb_ref, grb_ref,
                 s0_ref, m0_ref, *rest, T, nb, with_output):
    if with_output:
        hf_ref, hb_ref, s_ref, m_ref = rest
    else:
        s_ref, m_ref = rest

    @pl.when(j == 0)
    def _():
        s_ref[...] = s0_ref[...]
        m_ref[...] = m0_ref[...]

    ti = lax.broadcasted_iota(jnp.int32, (T, T), 0)
    si = lax.broadcasted_iota(jnp.int32, (T, T), 1)
    masks = (si <= ti, si >= ti)
    top = lax.broadcasted_iota(jnp.int32, (LANES, 1), 0) < (LANES // 2)
    zero = jnp.zeros((), BF16)
    ones = jnp.ones((T, ML_V_DIM), BF16)
    zpad = jnp.zeros((ML_QK_DIM, 2 * ML_V_DIM), BF16)
    fwd = (qf_ref, ktf_ref, vf_ref, gcf_ref, grf_ref)
    bwd = (qb_ref, ktb_ref, vb_ref, gcb_ref, grb_ref)
    chains = [(b, d, h) for b in range(nb) for d in range(2) for h in range(ML_HEADS)]

    def group(g0):
        grp = chains[g0:g0 + ML_GROUP]
        st = {}
        for (b, d, h) in grp:
            q_ref, kt_ref, v_ref, gc_ref, gr_ref = fwd if d == 0 else bwd
            r = (b * 2 + d) * ML_HEADS + h
            li, lf = 8 * d + h, 8 * d + 4 + h
            gr = gr_ref[b]
            u_row = gr[li:li + 1, :] - gr[lf:lf + 1, :]
            m_prev = m_ref[r:r + 1, 0:1]
            e = dict(r=r, u_row=u_row, m_prev=m_prev,
                     b_end=gr[lf:lf + 1, (T - 1 if d == 0 else 0):(T if d == 0 else 1)],
                     c_end=jnp.maximum(jnp.max(u_row, axis=1, keepdims=True), m_prev))
            if with_output:
                e["mu"] = jnp.where(masks[d], u_row, NEG_BIG)
                c_col = jnp.maximum(jnp.max(e["mu"], axis=1, keepdims=True), m_prev)
                e["cb"] = jnp.broadcast_to(c_col, (T, LANES))
                e["bb"] = jnp.broadcast_to(gc_ref[b, :, lf:lf + 1], (T, LANES))
            st[(b, d, h)] = e
        for (b, d, h) in grp:
            q_ref, kt_ref, v_ref, gc_ref, gr_ref = fwd if d == 0 else bwd
            e = st[(b, d, h)]
            pair = h // 2
            ktp = kt_ref[b, pair * LANES:(pair + 1) * LANES, :]
            e["kth"] = ktp[(h % 2) * ML_QK_DIM:(h % 2 + 1) * ML_QK_DIM, :]
            e["vext"] = jnp.concatenate([v_ref[b, :, h * ML_V_DIM:(h + 1) * ML_V_DIM], ones], axis=1)
            if with_output:
                qp = q_ref[b, :, pair * LANES:(pair + 1) * LANES]
                kpad = jnp.where(top, ktp, zero) if h % 2 == 0 else jnp.where(top, zero, ktp)
                qk = jnp.dot(qp, kpad, preferred_element_type=F32) * jnp.exp2(e["mu"] - e["cb"])
                qs = qp.astype(F32) * jnp.exp2(e["m_prev"] - e["cb"])
                e["lhs"] = jnp.concatenate([qk.astype(BF16), qs.astype(BF16)], axis=1)
        for (b, d, h) in grp:
            e = st[(b, d, h)]
            e["s_prev"] = s_ref[e["r"]]
            if with_output:
                sb = e["s_prev"].astype(BF16)
                rhs = jnp.concatenate([e["vext"]] + ([sb, zpad] if h % 2 == 0 else [zpad, sb]), axis=0)
                tot = jnp.dot(e["lhs"], rhs, preferred_element_type=F32)
                floor = jnp.exp2(-(e["bb"] + e["cb"]))
                hout = tot[:, :ML_V_DIM] / jnp.maximum(jnp.abs(tot[:, ML_V_DIM:]), floor)
                o_ref = hf_ref if d == 0 else hb_ref
                o_ref[b, :, h * ML_V_DIM:(h + 1) * ML_V_DIM] = hout
        for (b, d, h) in grp:
            e = st[(b, d, h)]
            kw = (e["kth"].astype(F32) * jnp.exp2(e["u_row"] - e["c_end"])).astype(BF16)
            upd = jnp.dot(kw, e["vext"], preferred_element_type=F32)
            s_ref[e["r"]] = jnp.exp2(e["m_prev"] - e["c_end"]) * e["s_prev"] + upd
            m_ref[e["r"]:e["r"] + 1, :] = jnp.broadcast_to(e["b_end"] + e["c_end"], (1, LANES))

    return [functools.partial(group, g0) for g0 in range(0, len(chains), ML_GROUP)]


def _mlstm_kernel(*refs, T, nb, with_output):
    for step in _mlstm_steps(pl.program_id(0), *refs, T=T, nb=nb, with_output=with_output):
        step()


def _att_mlstm_kernel(*refs, T, nb, nblk, ns):
    j = pl.program_id(0)
    sink_ref, att_in, ml_in = refs[0], refs[1:10], refs[10:22]
    o_ref, ml_out = refs[22], refs[23:]
    att = _att_steps(lax.rem(j, ns), sink_ref, *att_in, o_ref, nb=nblk)
    ml = _mlstm_steps(j, *ml_in, *ml_out, T=T, nb=nb, with_output=True)
    for k in range(max(len(att), len(ml))):
        if k < len(att):
            att[k]()
        if k < len(ml):
            ml[k]()


def _mlstm_specs(B, T, nc):
    up = lambda j: j
    down = lambda j: nc - 1 - j

    def specs(o):
        return [pl.BlockSpec((B, T, ML_QK), lambda j: (0, o(j), 0)),
                pl.BlockSpec((B, ML_QK, T), lambda j: (0, 0, o(j))),
                pl.BlockSpec((B, T, ML_V), lambda j: (0, o(j), 0)),
                pl.BlockSpec((B, T, LANES), lambda j: (0, o(j), 0)),
                pl.BlockSpec((B, 2 * SUBLANES, T), lambda j: (0, 0, o(j)))]

    nchains = B * 2 * ML_HEADS
    s_spec = _const_spec((nchains, ML_QK_DIM, 2 * ML_V_DIM))
    m_spec = _const_spec((nchains, LANES))
    state_shape = [jax.ShapeDtypeStruct((nchains, ML_QK_DIM, 2 * ML_V_DIM), F32),
                   jax.ShapeDtypeStruct((nchains, LANES), F32)]
    h_specs = [pl.BlockSpec((B, T, ML_V), lambda j: (0, up(j), 0)),
               pl.BlockSpec((B, T, ML_V), lambda j: (0, down(j), 0))]
    return specs(up) + specs(down) + [s_spec, m_spec], h_specs, [s_spec, m_spec], state_shape


def _att_mlstm(qt, kv, kvx, vt, vtx, sink, qm, kmt, vm, gc, gr, s0, m0):
    B, _, L = qt.shape
    T = ML_CHUNK
    nc = L // T
    nblk = L // BLOCK
    ns = nblk // ATT_QB
    assert nc == B * ns
    cx = kvx.shape[1]
    kw = kv.shape[2]
    vw = vt.shape[1]
    wide = ATT_QB * BLOCK
    smp = lambda j: j // ns
    cur = lambda j: j % ns
    prev = lambda j: jnp.maximum(cur(j) * ATT_QB - 1, 0)
    nxt = lambda j: jnp.minimum((cur(j) + 1) * ATT_QB, nblk - 1)
    att_specs = [
        pl.BlockSpec(memory_space=pltpu.SMEM),
        pl.BlockSpec((1, ATT_Q, wide), lambda j: (smp(j), 0, cur(j))),
        pl.BlockSpec((1, BLOCK, kw), lambda j: (smp(j), prev(j), 0)),
        pl.BlockSpec((1, wide, kw), lambda j: (smp(j), cur(j), 0)),
        pl.BlockSpec((1, BLOCK, kw), lambda j: (smp(j), nxt(j), 0)),
        pl.BlockSpec((1, cx, kw), lambda j: (smp(j), 0, 0)),
        pl.BlockSpec((1, vw, BLOCK), lambda j: (smp(j), 0, prev(j))),
        pl.BlockSpec((1, vw, wide), lambda j: (smp(j), 0, cur(j))),
        pl.BlockSpec((1, vw, BLOCK), lambda j: (smp(j), 0, nxt(j))),
        pl.BlockSpec((1, vw, cx), lambda j: (smp(j), 0, 0)),
    ]
    ml_in, h_specs, st_specs, st_shape = _mlstm_specs(B, T, nc)
    h_shape = jax.ShapeDtypeStruct((B, L, ML_V), F32)
    att, hf, hb, _, _ = pl.pallas_call(
        functools.partial(_att_mlstm_kernel, T=T, nb=B, nblk=nblk, ns=ns),
        out_shape=(jax.ShapeDtypeStruct((B, L, ATT_Q), BF16), h_shape, h_shape, *st_shape),
        grid=(nc,),
        in_specs=att_specs + ml_in,
        out_specs=(pl.BlockSpec((1, wide, ATT_Q), lambda j: (smp(j), cur(j), 0)), *h_specs, *st_specs),
        compiler_params=_cparams(("arbitrary",)),
        name="attention_mlstm",
    )(sink, qt, kv, kv, kv, kvx, vt, vt, vt, vtx, qm, kmt, vm, gc, gr, qm, kmt, vm, gc, gr, s0, m0)
    return att, hf, hb


def _mlstm(qm, kmt, vm, gc, gr, s0, m0, with_output):
    B, L, _ = qm.shape
    T = ML_CHUNK
    nc = L // T
    ml_in, h_specs, st_specs, st_shape = _mlstm_specs(B, T, nc)
    out_shape, out_specs = st_shape, st_specs
    if with_output:
        out_shape = [jax.ShapeDtypeStruct((B, L, ML_V), F32)] * 2 + out_shape
        out_specs = h_specs + out_specs
    return pl.pallas_call(
        functools.partial(_mlstm_kernel, T=T, nb=B, with_output=with_output),
        out_shape=tuple(out_shape),
        grid=(nc,),
        in_specs=ml_in,
        out_specs=tuple(out_specs),
        compiler_params=_cparams(("arbitrary",)),
        name="mlstm_scan" if with_output else "mlstm_context_state",
    )(qm, kmt, vm, gc, gr, qm, kmt, vm, gc, gr, s0, m0)


def _odd_tables(gate_b, q_g, k_g):
    assert sum([ATT_Q, ATT_KV, ATT_KV, ML_QK, ML_QK, ML_V, ML_V]) == OG
    head = np.arange(LANES) // HEAD_DIM
    pm = jnp.asarray((head[:, None] == head[None, :]) / HEAD_DIM, F32)
    gain = jnp.concatenate([jnp.tile(q_g, ATT_HEADS), jnp.tile(k_g, ATT_KV_HEADS)])[None, :]
    gb = jnp.pad(gate_b.reshape(1, -1), ((0, 0), (0, LANES - gate_b.size)))
    return pm, gain, gb


def kernel(x, c, ctx, c_ctx, ada_w, ada_b, norm_g, even_w_in, even_conv, even_w_out, odd_w_in, odd_gate_b,
           odd_q_g, odd_k_g, odd_sink, odd_w_out, ffn_w_up, ffn_conv, ffn_w_down):
    B, L, _ = x.shape
    C = ctx.shape[1]
    depth = ada_w.shape[0]
    assert depth == 2 and L % (DFT_N1 * SUBLANES) == 0 and C % ML_CHUNK == 0

    cv = jnp.concatenate([c, c_ctx[None, :], jnp.zeros((SUBLANES - B - 1, D_MODEL), F32)], axis=0)
    mod = _modulation(cv, ada_w, ada_b)
    lat, cx = None, B

    tm = min(512, L)
    tc = _channel_dft_table()
    w_in0, w_out0 = _cast_weights((even_w_in, even_w_out), 0, (EVEN_IN, D_MODEL))
    first = ((ffn_w_up, 0, 2 * D_FF, 1), (ffn_w_down, 0, D_MODEL, 1))
    later = ((ffn_w_up, 1, 2 * D_FF, 1), (ffn_w_down, 1, D_MODEL, 2), (odd_w_in, 0, ODD_COLS, 1),
             (odd_w_out, 0, D_MODEL, 1))

    ng00, ng01 = norm_g[0, 0][None, :], norm_g[0, 1][None, :]

    yc, zr, zi, w_up0, w_down0 = _even_in(x, mod, 0, lat, ng00, w_in0, even_conv[0], tc, tm, L // DFT_N1, first)
    xl, w_up1, w_down1, w_in1, w_out1 = _mix_ffn(x, (yc, _seq_dft(zr, zi, tm)), mod, 0, lat, ng01, w_out0, w_up0,
                                                 ffn_conv, w_down0, tm, odd=False, side=later)
    yc, zr, zi = _even_in(ctx, mod, 0, cx, ng00, w_in0, even_conv[0], tc, C, None)
    xc = _mix_ffn(ctx, (yc, _dense_seq_dft(zr, zi)), mod, 0, cx, ng01, w_out0, w_up0, ffn_conv, w_down0, C, odd=False)

    pm, gain, gb = _odd_tables(odd_gate_b[0], odd_q_g[0], odd_k_g[0])
    ng10, ng11 = norm_g[1, 0][None, :], norm_g[1, 1][None, :]
    cos, sin = _rope_tables(L)
    one, nil = jnp.ones((C, LANES), F32), jnp.zeros((C, LANES), F32)
    qt, kv, vt, qm, kmt, vm, om, gc, gr = _odd_in(xl, mod, 1, lat, ng10, w_in1, pm, gain, cos, sin, gb, tm)
    _, kvx, vtx, qmx, kmtx, vmx, _, gcx, grx = _odd_in(xc, mod, 1, cx, ng10, w_in1, pm, gain, one, nil, gb, C)

    nchains = B * 2 * ML_HEADS
    s0 = jnp.zeros((nchains, ML_QK_DIM, 2 * ML_V_DIM), F32)
    m0 = jnp.zeros((nchains, LANES), F32)
    s1, m1 = _mlstm(qmx, kmtx, vmx, gcx, grx, s0, m0, with_output=False)
    att, hf, hb = _att_mlstm(qt, kv, kvx, vt, vtx, odd_sink[0], qm, kmt, vm, gc, gr, s1, m1)
    return _mix_ffn(xl, (att, hf, hb, om), mod, 1, lat, ng11, w_out1, w_up1, ffn_conv, w_down1, tm, odd=True)
```

```python
import functools

import numpy as np
import jax
import jax.numpy as jnp
from jax import lax
from jax.experimental import pallas as pl
from jax.experimental.pallas import tpu as pltpu

F32 = jnp.float32
BF16 = jnp.bfloat16

D_MODEL = 1024
GRID_W = 64
EPS = 1e-6
SC_CH = 512
FT_CH = 512
FT_GROUPS = 4
FT_GROUP_CH = FT_CH // FT_GROUPS
EVEN_IN = 3 * SC_CH + FT_CH
ATT_HEADS = 8
ATT_KV_HEADS = 2
HEAD_DIM = 64
ATT_SCALE = HEAD_DIM ** -0.5
WINDOW = 128
BLOCK = 128
ROPE_THETA = 10000.0
ML_HEADS = 4
ML_QK_DIM = 64
ML_V_DIM = 128
ATT_Q = ATT_HEADS * HEAD_DIM
ATT_KV = ATT_KV_HEADS * HEAD_DIM
ML_QK = ML_HEADS * ML_QK_DIM
ML_V = ML_HEADS * ML_V_DIM
D_FF = 2816

LANES = 128
SUBLANES = 8
VMEM_LIMIT_BYTES = 56 * 1024 * 1024

DFT_N1 = 128
FF_CHUNK = 256
N_FF_CHUNKS = D_FF // FF_CHUNK
ML_CHUNK = 128
ATT_QB = 2
NEG_BIG = -1e30
LOG2E = 1.4426950408889634

OQ, OK_, OV, OQM, OKM, OVM, OOM, OG = 0, 512, 640, 768, 1024, 1280, 1792, 2304
ODD_COLS = OG + LANES


def _cparams(sem):
    return pltpu.CompilerParams(dimension_semantics=sem, vmem_limit_bytes=VMEM_LIMIT_BYTES)


def _sigmoid(x):
    return 1.0 / (1.0 + jnp.exp(-x))


def _norm_mod(x, g, shift, scale):
    y = x * lax.rsqrt(jnp.mean(x * x, axis=-1, keepdims=True) + EPS)
    return y * g * (1.0 + scale) + shift


def _mod_vec(mod_ref, k, mrow):
    r = pl.program_id(0) if mrow is None else mrow
    return mod_ref[k, pl.ds(r, 1), :]


def _mod_spec(layer):
    return pl.BlockSpec((None, 6, SUBLANES, D_MODEL), lambda *_: (layer, 0, 0, 0))


def _halo_rows(x, xn, xp, shift, scale, ng_ref):
    g = ng_ref[...]
    parts = [_norm_mod(r, g, shift, scale) for r in (x, xn, xp)]
    return jnp.concatenate(parts, axis=0).astype(BF16)


def _halo_valid(tm, i, nt):
    row = lax.broadcasted_iota(jnp.int32, (tm + 2 * SUBLANES, 1), 0)
    return ((row < tm) | ((row < tm + SUBLANES) & (i < nt - 1)) | ((row >= tm + SUBLANES) & (i > 0)))


def _conv3(v, cw, tm):
    n = v.shape[0]
    vp = pltpu.roll(v, 1, 0)[:tm]
    vn = pltpu.roll(v, n - 1, 0)[:tm]
    return vp * cw[0:1] + v[:tm] * cw[1:2] + vn * cw[2:3]


def _halo_specs(tm, L):
    hb = tm // SUBLANES
    last = L // SUBLANES - 1
    return [
        pl.BlockSpec((1, tm, D_MODEL), lambda b, i: (b, i, 0)),
        pl.BlockSpec((1, SUBLANES, D_MODEL), lambda b, i: (b, jnp.minimum((i + 1) * hb, last), 0)),
        pl.BlockSpec((1, SUBLANES, D_MODEL), lambda b, i: (b, jnp.maximum(i * hb - 1, 0), 0)),
    ]


def _const_spec(shape):
    nd = len(shape)
    return pl.BlockSpec(shape, lambda *_: (0,) * nd)


def _resident_spec(shape):
    nd = len(shape)
    return pl.BlockSpec(shape, lambda *_: (0,) * nd, pipeline_mode=pl.Buffered(1))


def _weight_spec(w, shape, layer):
    return _resident_spec(shape) if w.ndim == len(shape) else _layer_spec(shape, layer)


def _layer_spec(shape, layer):
    nd = len(shape)
    return pl.BlockSpec((None,) + tuple(shape), lambda *_: (layer,) + (0,) * nd, pipeline_mode=pl.Buffered(1))


def _split_dot(x, p):
    hi = x.astype(BF16).astype(F32)
    y = jnp.dot(jnp.concatenate([hi, x - hi], axis=0).astype(BF16), p, preferred_element_type=F32)
    return y[:x.shape[0]] + y[x.shape[0]:]


def _mod_kernel(cv_ref, w_ref, b_ref, o_ref):
    cv = cv_ref[...]
    o_ref[0, 0] = _split_dot(cv * _sigmoid(cv), w_ref[0].astype(BF16)) + b_ref[0, 0]


def _modulation(cv, ada_w, ada_b):
    depth, _, n = ada_w.shape
    nv = n // D_MODEL
    return pl.pallas_call(
        _mod_kernel,
        out_shape=jax.ShapeDtypeStruct((depth, nv, SUBLANES, D_MODEL), F32),
        grid=(depth, nv),
        in_specs=[
            pl.BlockSpec((SUBLANES, D_MODEL), lambda l, j: (0, 0)),
            pl.BlockSpec((1, D_MODEL, D_MODEL), lambda l, j: (l, 0, j)),
            pl.BlockSpec((1, 1, 1, D_MODEL), lambda l, j: (l, j, 0, 0)),
        ],
        out_specs=pl.BlockSpec((1, 1, SUBLANES, D_MODEL), lambda l, j: (l, j, 0, 0)),
        compiler_params=_cparams(("arbitrary", "arbitrary")),
        name="modulation",
    )(cv, ada_w, ada_b.reshape(depth, nv, 1, D_MODEL))


CAST_STEPS = 8


def _cast_refs(srcs, dsts):
    for src, dst in zip(srcs, dsts):
        w = src[...].astype(BF16)
        pad = dst.shape[-1] - src.shape[-1]
        if pad:
            w = jnp.concatenate([w, jnp.zeros(w.shape[:-1] + (pad,), BF16)], axis=-1)
        dst[...] = w


def _cast_kernel(*refs):
    n = len(refs) // 2
    _cast_refs(refs[:n], refs[n:])


def _layer_rows_spec(w, layer, rb, imap):
    if w.ndim == 2:
        return pl.BlockSpec((rb, w.shape[1]), lambda *a: (imap(*a), 0))
    return pl.BlockSpec((None, rb, w.shape[2]), lambda *a: (layer, imap(*a), 0))


def _side_cast_specs(side, steps, step_of):
    ins, outs, shapes = [], [], []
    for w, wl, wd, per in side:
        r = w.shape[-2]
        rb = r * per // steps
        blk = lambda *a, per=per: step_of(*a) // per
        ins.append(_layer_rows_spec(w, wl, rb, blk))
        outs.append(pl.BlockSpec((rb, wd), lambda *a, blk=blk: (blk(*a), 0)))
        shapes.append(jax.ShapeDtypeStruct((r, wd), BF16))
    return ins, outs, shapes


def _cast_weights(ws, layer, widths):
    in_specs, out_specs, out_shape = [], [], []
    for w, wd in zip(ws, widths):
        r = w.shape[-2]
        in_specs.append(_layer_rows_spec(w, layer, r // CAST_STEPS, lambda i: i))
        out_specs.append(pl.BlockSpec((r // CAST_STEPS, wd), lambda i: (i, 0)))
        out_shape.append(jax.ShapeDtypeStruct((r, wd), BF16))
    return pl.pallas_call(
        _cast_kernel,
        out_shape=tuple(out_shape),
        grid=(CAST_STEPS,),
        in_specs=in_specs,
        out_specs=tuple(out_specs),
        compiler_params=_cparams(("arbitrary",)),
        name="cast_weights",
    )(*ws)


def _even_in_kernel(x_ref, xn_ref, xp_ref, mod_ref, ng_ref, w_ref, cw_ref, tc_ref, *rest, tm, nt, n2, mrow, nsub):
    nside = (len(rest) - 3) // 2
    yc_ref, zr_ref, zi_ref = rest[nside:nside + 3]
    _cast_refs(rest[:nside], rest[nside + 3:])
    i = pl.program_id(1)
    shift, scale = _mod_vec(mod_ref, 0, mrow), _mod_vec(mod_ref, 1, mrow)
    tc = tc_ref[...].astype(BF16)
    for s in range(nsub):
        lo = s * tm
        x = x_ref[0, lo:lo + tm]
        xn = xn_ref[0] if s == nsub - 1 else x_ref[0, lo + tm:lo + tm + SUBLANES]
        xp = xp_ref[0] if s == 0 else x_ref[0, lo - SUBLANES:lo]
        hh = _halo_rows(x, xn, xp, shift, scale, ng_ref)
        u = jnp.dot(hh, w_ref[...], preferred_element_type=F32)
        v = u[:, SC_CH:2 * SC_CH] * u[:, 2 * SC_CH:3 * SC_CH]
        v = jnp.where(_halo_valid(tm, i * nsub + s, nt), v, 0.0)
        yc = u[:tm, :SC_CH] * _conv3(v, cw_ref[...], tm)
        yc_ref[0, lo:lo + tm] = yc.astype(BF16)
        uf = u[:tm, 3 * SC_CH:].astype(BF16)
        for g in range(FT_GROUPS):
            sl = slice(g * FT_GROUP_CH, (g + 1) * FT_GROUP_CH)
            ab = jnp.dot(uf[:, sl], tc, preferred_element_type=F32)
            if n2 is None:
                zr_ref[g, 0, lo:lo + tm] = ab[:, :FT_GROUP_CH]
                zi_ref[g, 0, lo:lo + tm] = ab[:, FT_GROUP_CH:]
            else:
                for a in range(tm // n2):
                    dst = pl.ds(lo + a, n2, stride=tm // n2)
                    zr_ref[g, 0, dst, :] = ab[n2 * a:n2 * (a + 1), :FT_GROUP_CH]
                    zi_ref[g, 0, dst, :] = ab[n2 * a:n2 * (a + 1), FT_GROUP_CH:]


def _even_in(x, mod, layer, mrow, ng, w_in, cw, tc, tm, n2, side=()):
    B, L, _ = x.shape
    nt = L // tm
    nsub = 2 if nt % 2 == 0 else 1
    bm = nsub * tm
    out = jax.ShapeDtypeStruct((B, L, FT_CH), BF16)
    zout = jax.ShapeDtypeStruct((FT_GROUPS, B, L, FT_GROUP_CH), F32)
    ospec = pl.BlockSpec((1, bm, FT_CH), lambda b, i: (b, i, 0))
    zspec = pl.BlockSpec((FT_GROUPS, 1, bm, FT_GROUP_CH), lambda b, i: (0, b, i, 0))
    ns = nt // nsub
    side_in, side_out, side_shape = _side_cast_specs(side, B * ns, lambda b, i: b * ns + i)
    return pl.pallas_call(
        functools.partial(_even_in_kernel, tm=tm, nt=nt, n2=n2, mrow=mrow, nsub=nsub),
        out_shape=(out, zout, zout, *side_shape),
        grid=(B, ns),
        in_specs=_halo_specs(bm, L) + [
            _mod_spec(layer),
            _const_spec((1, D_MODEL)),
            _const_spec((D_MODEL, EVEN_IN)),
            _const_spec((3, SC_CH)),
            _const_spec((FT_GROUP_CH, 2 * FT_GROUP_CH)),
        ] + side_in,
        out_specs=(ospec, zspec, zspec, *side_out),
        compiler_params=_cparams(("parallel", "arbitrary")),
        name="even_in",
    )(x, x, x, mod, ng, w_in, cw, tc, *[e[0] for e in side])


def _seq_dft_kernel(zr_ref, zi_ref, m_ref, g_ref, y_ref, o_scr, *, n2, tm):
    m1 = m_ref[...].astype(BF16)
    chunk = tm // n2
    ntile = DFT_N1 // chunk

    def rows(ref, j):
        return [ref[0, 0, t * tm + j * chunk:t * tm + (j + 1) * chunk, :] for t in range(ntile)]

    for j in range(n2):
        z = jnp.concatenate(rows(zr_ref, j) + rows(zi_ref, j), axis=0)
        o_scr[2 * DFT_N1 * j:2 * DFT_N1 * (j + 1), :] = jnp.dot(m1, z.astype(BF16), preferred_element_type=F32)
    for k1 in range(DFT_N1):
        o = jnp.concatenate([o_scr[pl.ds(k1, n2, stride=2 * DFT_N1), :],
                             o_scr[pl.ds(DFT_N1 + k1, n2, stride=2 * DFT_N1), :]], axis=0)
        y_ref[0, pl.ds(k1, n2, stride=DFT_N1), :] = jnp.dot(g_ref[k1].astype(BF16), o.astype(BF16), preferred_element_type=F32)


def _dft_tables(L):
    n2 = L // DFT_N1
    k = np.arange(DFT_N1)
    a = 2.0 * np.pi * ((k[:, None] * k[None, :]) % DFT_N1) / DFT_N1
    er, ei = np.cos(a) / np.sqrt(DFT_N1), -np.sin(a) / np.sqrt(DFT_N1)
    m1 = np.block([[er, -ei], [ei, er]])
    k1 = np.arange(DFT_N1)[:, None, None]
    k2 = np.arange(n2)[None, :, None]
    nn = np.arange(n2)[None, None, :]
    th = 2.0 * np.pi * ((nn * (k1 + DFT_N1 * k2)) % L) / L
    g = np.concatenate([np.cos(th), np.sin(th)], axis=-1) / np.sqrt(n2)
    return jnp.asarray(m1, F32), jnp.asarray(g, F32)


def _channel_dft_table():
    k = np.arange(FT_GROUP_CH)
    a = 2.0 * np.pi * ((k[:, None] * k[None, :]) % FT_GROUP_CH) / FT_GROUP_CH
    t = np.concatenate([np.cos(a), -np.sin(a)], axis=1) / np.sqrt(FT_GROUP_CH)
    return jnp.asarray(t, F32)


def _seq_dft(zr, zi, tm):
    G, B, L, C = zr.shape
    n2 = L // DFT_N1
    m1, g = _dft_tables(L)
    zspec = pl.BlockSpec((1, 1, L, C), lambda b, j: (j, b, 0, 0))
    return pl.pallas_call(
        functools.partial(_seq_dft_kernel, n2=n2, tm=tm),
        out_shape=jax.ShapeDtypeStruct((B, L, G * C), F32),
        grid=(B, G),
        in_specs=[zspec, zspec, _const_spec((2 * DFT_N1, 2 * DFT_N1)), _const_spec((DFT_N1, n2, 2 * n2))],
        out_specs=pl.BlockSpec((1, L, C), lambda b, j: (b, 0, j)),
        scratch_shapes=[pltpu.VMEM((2 * DFT_N1 * n2, LANES), F32)],
        compiler_params=_cparams(("parallel", "arbitrary")),
        name="seq_dft",
    )(zr, zi, m1, g)


def _dense_dft_kernel(zr_ref, zi_ref, t_ref, y_ref):
    z = jnp.concatenate([zr_ref[0, 0], zi_ref[0, 0]], axis=0).astype(BF16)
    y_ref[0] = jnp.dot(t_ref[...].astype(BF16), z, preferred_element_type=F32)


def _dense_seq_dft(zr, zi):
    G, B, L, C = zr.shape
    k = np.arange(L)
    a = 2.0 * np.pi * ((k[:, None] * k[None, :]) % L) / L
    t = jnp.asarray(np.concatenate([np.cos(a), np.sin(a)], axis=1) / np.sqrt(L), F32)
    zspec = pl.BlockSpec((1, 1, L, C), lambda b, j: (j, b, 0, 0))
    return pl.pallas_call(
        _dense_dft_kernel,
        out_shape=jax.ShapeDtypeStruct((B, L, G * C), F32),
        grid=(B, G),
        in_specs=[zspec, zspec, _const_spec((L, 2 * L))],
        out_specs=pl.BlockSpec((1, L, C), lambda b, j: (b, 0, j)),
        compiler_params=_cparams(("arbitrary", "arbitrary")),
        name="dense_seq_dft",
    )(zr, zi, t)


HALO = 16


def _wide_halo_specs(tm, L, width):
    hb = tm // HALO
    last = L // HALO - 1
    return [
        pl.BlockSpec((1, tm, width), lambda b, i: (b, i, 0)),
        pl.BlockSpec((1, HALO, width), lambda b, i: (b, jnp.minimum((i + 1) * hb, last), 0)),
        pl.BlockSpec((1, HALO, width), lambda b, i: (b, jnp.maximum(i * hb - 1, 0), 0)),
    ]


def _circ(t_ref, n_ref, p_ref):
    return jnp.concatenate([t_ref[0], n_ref[0], p_ref[0]], axis=0)


def _mix_ffn_kernel(*refs, tm, nt, odd, mrow, nside):
    n_in = 15 if odd else 9
    x3, rest = refs[:3], refs[3:n_in]
    mod_ref, ng_ref, wo_ref, wu_ref, cw_ref, wd_ref = refs[n_in:n_in + 6]
    side_src = refs[n_in + 6:n_in + 6 + nside]
    o_ref = refs[n_in + 6 + nside]
    side_dst = refs[n_in + 7 + nside:n_in + 7 + 2 * nside]
    hh_scr, act_scr = refs[n_in + 7 + 2 * nside:]
    _cast_refs(side_src, side_dst)
    i = pl.program_id(1)
    if odd:
        att, hf, hb, om = (_circ(*rest[k:k + 3]) for k in range(0, 12, 3))
        lhs = jnp.concatenate([att, ((hf + hb) * _sigmoid(om.astype(F32))).astype(BF16)], axis=-1)
    else:
        yc, yf = _circ(*rest[0:3]), _circ(*rest[3:6])
        lhs = jnp.concatenate([yc, yf.astype(BF16)], axis=-1)
    mv = [_mod_vec(mod_ref, k, mrow) for k in range(6)]
    x1 = _circ(*x3) + mv[2] * jnp.dot(lhs, wo_ref[...], preferred_element_type=F32)
    hh_scr[...] = _norm_mod(x1, ng_ref[...], mv[3], mv[4]).astype(BF16)
    row = lax.broadcasted_iota(jnp.int32, (tm + 2 * HALO, 1), 0)
    valid = (row < tm) | ((row < tm + HALO) & (i < nt - 1)) | ((row >= tm + HALO) & (i > 0))
    for c in range(N_FF_CHUNKS):
        lo = c * FF_CHUNK
        g = jnp.dot(hh_scr[...], wu_ref[:, lo:lo + FF_CHUNK], preferred_element_type=F32)
        g = jnp.where(valid, g, 0.0)
        cv = _conv3(g, cw_ref[:, lo:lo + FF_CHUNK], tm)
        val = jnp.dot(hh_scr[:tm, :], wu_ref[:, D_FF + lo:D_FF + lo + FF_CHUNK], preferred_element_type=F32)
        act_scr[:, lo:lo + FF_CHUNK] = (cv * _sigmoid(cv) * val).astype(BF16)
    y = jnp.dot(act_scr[...], wd_ref[...], preferred_element_type=F32)
    o_ref[0] = x1[:tm] + mv[5] * y


def _mix_ffn(x, mixed, mod, layer, mrow, ng, w_out, w_up, cw, w_down, tm, odd, side=()):
    B, L, _ = x.shape
    nt = L // tm
    specs = _wide_halo_specs(tm, L, D_MODEL)
    args = [x, x, x]
    for a in mixed:
        specs += _wide_halo_specs(tm, L, a.shape[-1])
        args += [a, a, a]
    side_in, side_out, side_shape = _side_cast_specs(side, B * nt, lambda b, i: b * nt + i)
    out = pl.pallas_call(
        functools.partial(_mix_ffn_kernel, tm=tm, nt=nt, odd=odd, mrow=mrow, nside=len(side)),
        out_shape=(jax.ShapeDtypeStruct(x.shape, F32), *side_shape),
        grid=(B, nt),
        in_specs=specs + [
            _mod_spec(layer),
            _const_spec((1, D_MODEL)),
            _resident_spec((D_MODEL, D_MODEL)),
            _weight_spec(w_up, (D_MODEL, 2 * D_FF), layer),
            _layer_spec((3, D_FF), layer),
            _weight_spec(w_down, (D_FF, D_MODEL), layer),
        ] + side_in,
        out_specs=(pl.BlockSpec((1, tm, D_MODEL), lambda b, i: (b, i, 0)), *side_out),
        scratch_shapes=[pltpu.VMEM((tm + 2 * HALO, D_MODEL), BF16),
                        pltpu.VMEM((tm, D_FF), BF16)],
        compiler_params=_cparams(("parallel", "arbitrary")),
        name="odd_mix_ffn" if odd else "even_mix_ffn",
    )(*args, mod, ng, w_out, w_up, cw, w_down, *[e[0] for e in side])
    return out if side else out[0]


def _split3(x):
    parts = []
    r = x
    for _ in range(3):
        p = r.astype(BF16)
        parts.append(p)
        r = r - p.astype(F32)
    return parts


def _odd_in_kernel(x_ref, mod_ref, ng_ref, w_ref, pm_ref, gain_ref, cos_ref, sin_ref, gb_ref, tl_ref, tu_ref,
                   qt_ref, kv_ref, vt_ref, qm_ref, kmt_ref, vm_ref, om_ref, gc_ref, gr_ref, *, mrow, tm, nsub):
    for s in range(nsub):
        rs = slice(s * tm, (s + 1) * tm)
        h = _norm_mod(x_ref[0, rs], ng_ref[...], _mod_vec(mod_ref, 0, mrow), _mod_vec(mod_ref, 1, mrow))
        u = jnp.dot(h.astype(BF16), w_ref[...], preferred_element_type=F32)

        uqk = u[:, OQ:OV]
        sq = (uqk * uqk).astype(BF16)
        pm = pm_ref[...].astype(BF16)
        ms = jnp.concatenate([jnp.dot(sq[:, t * LANES:(t + 1) * LANES], pm, preferred_element_type=F32)
                              for t in range((OV - OQ) // LANES)], axis=1)
        rn = uqk * lax.rsqrt(ms + EPS) * gain_ref[...]
        lane = lax.broadcasted_iota(jnp.int32, (1, LANES), 1)
        first = (lane % 32) < 16
        cos = cos_ref[rs, :]
        sin = sin_ref[rs, :]
        roped = []
        for t in range((OV - OQ) // LANES):
            xt = rn[:, t * LANES:(t + 1) * LANES]
            sw = jnp.where(first, pltpu.roll(xt, LANES - 16, 1), pltpu.roll(xt, 16, 1))
            roped.append(xt * cos + sw * sin)
        for t in range(ATT_Q // LANES):
            qt_ref[0, t * LANES:(t + 1) * LANES, rs] = (roped[t] * (ATT_SCALE * LOG2E)).T.astype(BF16)
        k = roped[ATT_Q // LANES]
        v = u[:, OV:OQM]
        half = LANES // 2
        kv_ref[0, rs, 0:LANES] = k.astype(BF16)
        kv_ref[0, rs, LANES:2 * LANES] = pltpu.roll(k, half, 1).astype(BF16)
        vt_ref[0, :, rs] = v.T.astype(BF16)

        qm_ref[0, rs] = u[:, OQM:OKM].astype(BF16)
        for p in range(ML_QK // LANES):
            km = u[:, OKM + p * LANES:OKM + (p + 1) * LANES] * (ML_QK_DIM ** -0.5)
            kmt_ref[0, p * LANES:(p + 1) * LANES, rs] = km.T.astype(BF16)
        vm_ref[0, rs] = u[:, OVM:OOM].astype(BF16)
        om_ref[0, rs] = u[:, OOM:OG].astype(BF16)

        gt = (u[:, OG:ODD_COLS] + gb_ref[...]).T[:2 * SUBLANES, :]
        row = lax.broadcasted_iota(jnp.int32, (2 * SUBLANES, 1), 0)
        logsig = jnp.minimum(gt, 0.0) - jnp.log(1.0 + jnp.exp(-jnp.abs(gt)))
        parts = jnp.concatenate(_split3(logsig), axis=0)
        tl, tu = tl_ref[...].astype(BF16), tu_ref[...].astype(BF16)

        def chunk_cumsum(tri):
            c = jnp.concatenate([jnp.dot(parts[:, k:k + ML_CHUNK], tri, preferred_element_type=F32)
                                 for k in range(0, tm, ML_CHUNK)], axis=1)
            n = 2 * SUBLANES
            return c[0:n] + c[n:2 * n] + c[2 * n:3 * n]

        cum_f, cum_b = chunk_cumsum(tu), chunk_cumsum(tl)
        sel = row % 8
        gr = jnp.where(sel < 4, gt, jnp.where(row < SUBLANES, cum_f, cum_b)) * LOG2E
        gr_ref[0, :, rs] = gr
        gc_ref[0, rs] = jnp.concatenate([gr, jnp.zeros((LANES - 2 * SUBLANES, tm), F32)], axis=0).T


def _chunk_tri():
    i = np.arange(ML_CHUNK)
    return jnp.asarray(i[None, :] <= i[:, None], F32), jnp.asarray(i[None, :] >= i[:, None], F32)


def _odd_in(x, mod, layer, mrow, ng, w, pm, gain, cos, sin, gb, tm):
    B, L, _ = x.shape
    nsub = 2 if L % (2 * tm) == 0 else 1
    bm = nsub * tm

    def rows(c, dt=BF16):
        return jax.ShapeDtypeStruct((B, L, c), dt), pl.BlockSpec((1, bm, c), lambda b, i: (b, i, 0))

    def cols(c, dt=BF16):
        return jax.ShapeDtypeStruct((B, c, L), dt), pl.BlockSpec((1, c, bm), lambda b, i: (b, 0, i))

    outs = [cols(ATT_Q), rows(2 * LANES), cols(LANES), rows(ML_QK), cols(ML_QK), rows(ML_V), rows(ML_V),
            rows(LANES, F32), cols(2 * SUBLANES, F32)]
    nqk = OV - OQ
    tl, tu = _chunk_tri()
    return pl.pallas_call(
        functools.partial(_odd_in_kernel, mrow=mrow, tm=tm, nsub=nsub),
        out_shape=tuple(o[0] for o in outs),
        grid=(B, L // bm),
        in_specs=[
            pl.BlockSpec((1, bm, D_MODEL), lambda b, i: (b, i, 0)),
            _mod_spec(layer),
            _const_spec((1, D_MODEL)),
            _const_spec((D_MODEL, ODD_COLS)),
            _const_spec((LANES, LANES)),
            _const_spec((1, nqk)),
            pl.BlockSpec((bm, LANES), lambda b, i: (i, 0)),
            pl.BlockSpec((bm, LANES), lambda b, i: (i, 0)),
            _const_spec((1, LANES)),
            _const_spec((ML_CHUNK, ML_CHUNK)),
            _const_spec((ML_CHUNK, ML_CHUNK)),
        ],
        out_specs=tuple(o[1] for o in outs),
        compiler_params=_cparams(("parallel", "arbitrary")),
        name="odd_in",
    )(x, mod, ng, w, pm, gain, cos, sin, gb, tl, tu)


def _rope_tables(L):
    rows = L // GRID_W
    pos = np.stack([np.repeat(np.arange(rows), GRID_W), np.tile(np.arange(GRID_W), rows)]).astype(np.float64)
    axis_dim = HEAD_DIM // 2
    inv_freq = ROPE_THETA ** (-np.arange(0, axis_dim, 2, dtype=np.float64) / axis_dim)
    ang = pos[:, :, None] * inv_freq
    c, sn = np.cos(ang), np.sin(ang)
    cos = np.concatenate([c[0], c[0], c[1], c[1]], axis=-1)
    sin = np.concatenate([-sn[0], sn[0], -sn[1], sn[1]], axis=-1)
    return jnp.asarray(np.tile(cos, (1, 2)), F32), jnp.asarray(np.tile(sin, (1, 2)), F32)


def _att_steps(n, sink_ref, qt_ref, kvp_ref, kvc_ref, kvn_ref, kvx_ref, vtp_ref, vtc_ref, vtn_ref, vtx_ref,
               o_ref, *, nb):
    half = LANES // 2
    lo = lax.broadcasted_iota(jnp.int32, (1, LANES), 1) < half
    zero = jnp.zeros((), BF16)
    cx = kvx_ref.shape[1]
    klocal = ([kvp_ref[0]] + [kvc_ref[0, i * BLOCK:(i + 1) * BLOCK] for i in range(ATT_QB)] + [kvn_ref[0]])
    vlocal = ([vtp_ref[0]] + [vtc_ref[0, :, i * BLOCK:(i + 1) * BLOCK] for i in range(ATT_QB)] + [vtn_ref[0]])
    kctx = [kvx_ref[0, i:i + BLOCK] for i in range(0, cx, BLOCK)]
    vctx = [vtx_ref[0, :, i:i + BLOCK] for i in range(0, cx, BLOCK)]
    ones = jnp.ones((half, BLOCK), BF16)

    kj = lax.broadcasted_iota(jnp.int32, (BLOCK, BLOCK), 0)
    qi = lax.broadcasted_iota(jnp.int32, (BLOCK, BLOCK), 1)

    def twice(x):
        return jnp.concatenate([x, x], axis=1)

    left = lax.broadcasted_iota(jnp.int32, (1, 2 * BLOCK), 1) < BLOCK
    group = ATT_HEADS // ATT_KV_HEADS

    def block(qb):
        blk_id = n * ATT_QB + qb
        kblocks = klocal[qb:qb + 3] + kctx
        vblocks = vlocal[qb:qb + 3] + vctx
        bias = [twice(jnp.where((kj >= qi) & (blk_id > 0), 0.0, NEG_BIG)).astype(BF16), None,
                twice(jnp.where((kj <= qi) & (blk_id < nb - 1), 0.0, NEG_BIG)).astype(BF16)] + [None] * len(kctx)
        for kvh in range(ATT_KV_HEADS):
            t0 = kvh * group // 2
            cols = slice(qb * BLOCK, (qb + 1) * BLOCK)
            qt2 = jnp.concatenate([qt_ref[0, t0 * LANES:(t0 + 1) * LANES, cols],
                                   qt_ref[0, (t0 + 1) * LANES:(t0 + 2) * LANES, cols]], axis=1)
            outs = []
            for par in range(2):
                sk = jnp.where(left, sink_ref[2 * t0 + par], sink_ref[2 * t0 + 2 + par]) * LOG2E
                m = sk.astype(BF16).astype(F32)
                acc = jnp.zeros((LANES, 2 * BLOCK), F32)
                for blk in range(len(kblocks)):
                    k, ks = kblocks[blk][:, :LANES], kblocks[blk][:, LANES:]
                    kh = ((jnp.where(lo, k, zero), jnp.where(lo, zero, ks)),
                          (jnp.where(lo, ks, zero), jnp.where(lo, zero, k)))[kvh][par]
                    s = jnp.dot(kh, qt2, preferred_element_type=F32)
                    s = s.astype(BF16)
                    if bias[blk] is not None:
                        s = s + bias[blk]
                    m_new = jnp.maximum(m, jnp.max(s, axis=0, keepdims=True).astype(F32))
                    p = jnp.exp2(s - m_new.astype(BF16))
                    vh = jnp.concatenate([vblocks[blk][kvh * half:(kvh + 1) * half], ones], axis=0)
                    acc = acc * jnp.exp2(m - m_new) + jnp.dot(vh, p, preferred_element_type=F32)
                    m = m_new
                l = acc[half:half + 1, :] + jnp.exp2(sk - m)
                outs.append(acc[:half] * (1.0 / l))
            ot = jnp.concatenate(outs, axis=0)
            rows = slice(qb * BLOCK, (qb + 1) * BLOCK)
            o_ref[0, rows, t0 * LANES:(t0 + 1) * LANES] = ot[:, :BLOCK].T.astype(BF16)
            o_ref[0, rows, (t0 + 1) * LANES:(t0 + 2) * LANES] = ot[:, BLOCK:].T.astype(BF16)

    return [functools.partial(block, qb) for qb in range(ATT_QB)]


ML_GROUP = 8


def _mlstm_steps(j, qf_ref, ktf_ref, vf_ref, gcf_ref, grf_ref, qb_ref, ktb_ref, vb_ref, gcb_ref, grb_ref,
                 s0_ref, m0_ref, *rest, T, nb, with_output):
    if with_output:
        hf_ref, hb_ref, s_ref, m_ref = rest
    else:
        s_ref, m_ref = rest

    @pl.when(j == 0)
    def _():
        s_ref[...] = s0_ref[...]
        m_ref[...] = m0_ref[...]

    ti = lax.broadcasted_iota(jnp.int32, (T, T), 0)
    si = lax.broadcasted_iota(jnp.int32, (T, T), 1)
    masks = (si <= ti, si >= ti)
    top = lax.broadcasted_iota(jnp.int32, (LANES, 1), 0) < (LANES // 2)
    zero = jnp.zeros((), BF16)
    ones = jnp.ones((T, ML_V_DIM), BF16)
    zpad = jnp.zeros((ML_QK_DIM, 2 * ML_V_DIM), BF16)
    fwd = (qf_ref, ktf_ref, vf_ref, gcf_ref, grf_ref)
    bwd = (qb_ref, ktb_ref, vb_ref, gcb_ref, grb_ref)
    chains = [(b, d, h) for b in range(nb) for d in range(2) for h in range(ML_HEADS)]

    def group(g0):
        grp = chains[g0:g0 + ML_GROUP]
        st = {}
        for (b, d, h) in grp:
            q_ref, kt_ref, v_ref, gc_ref, gr_ref = fwd if d == 0 else bwd
            r = (b * 2 + d) * ML_HEADS + h
            li, lf = 8 * d + h, 8 * d + 4 + h
            gr = gr_ref[b]
            u_row = gr[li:li + 1, :] - gr[lf:lf + 1, :]
            m_prev = m_ref[r:r + 1, 0:1]
            e = dict(r=r, u_row=u_row, m_prev=m_prev,
                     b_end=gr[lf:lf + 1, (T - 1 if d == 0 else 0):(T if d == 0 else 1)],
                     c_end=jnp.maximum(jnp.max(u_row, axis=1, keepdims=True), m_prev))
            if with_output:
                e["mu"] = jnp.where(masks[d], u_row, NEG_BIG)
                c_col = jnp.maximum(jnp.max(e["mu"], axis=1, keepdims=True), m_prev)
                e["cb"] = jnp.broadcast_to(c_col, (T, LANES))
                e["bb"] = jnp.broadcast_to(gc_ref[b, :, lf:lf + 1], (T, LANES))
            st[(b, d, h)] = e
        for (b, d, h) in grp:
            q_ref, kt_ref, v_ref, gc_ref, gr_ref = fwd if d == 0 else bwd
            e = st[(b, d, h)]
            pair = h // 2
            ktp = kt_ref[b, pair * LANES:(pair + 1) * LANES, :]
            e["kth"] = ktp[(h % 2) * ML_QK_DIM:(h % 2 + 1) * ML_QK_DIM, :]
            e["vext"] = jnp.concatenate([v_ref[b, :, h * ML_V_DIM:(h + 1) * ML_V_DIM], ones], axis=1)
            if with_output:
                qp = q_ref[b, :, pair * LANES:(pair + 1) * LANES]
                kpad = jnp.where(top, ktp, zero) if h % 2 == 0 else jnp.where(top, zero, ktp)
                qk = jnp.dot(qp, kpad, preferred_element_type=F32) * jnp.exp2(e["mu"] - e["cb"])
                qs = qp.astype(F32) * jnp.exp2(e["m_prev"] - e["cb"])
                e["lhs"] = jnp.concatenate([qk.astype(BF16), qs.astype(BF16)], axis=1)
        for (b, d, h) in grp:
            e = st[(b, d, h)]
            e["s_prev"] = s_ref[e["r"]]
            if with_output:
                sb = e["s_prev"].astype(BF16)
                rhs = jnp.concatenate([e["vext"]] + ([sb, zpad] if h % 2 == 0 else [zpad, sb]), axis=0)
                tot = jnp.dot(e["lhs"], rhs, preferred_element_type=F32)
                floor = jnp.exp2(-(e["bb"] + e["cb"]))
                hout = tot[:, :ML_V_DIM] / jnp.maximum(jnp.abs(tot[:, ML_V_DIM:]), floor)
                o_ref = hf_ref if d == 0 else hb_ref
                o_ref[b, :, h * ML_V_DIM:(h + 1) * ML_V_DIM] = hout
        for (b, d, h) in grp:
            e = st[(b, d, h)]
            kw = (e["kth"].astype(F32) * jnp.exp2(e["u_row"] - e["c_end"])).astype(BF16)
            upd = jnp.dot(kw, e["vext"], preferred_element_type=F32)
            s_ref[e["r"]] = jnp.exp2(e["m_prev"] - e["c_end"]) * e["s_prev"] + upd
            m_ref[e["r"]:e["r"] + 1, :] = jnp.broadcast_to(e["b_end"] + e["c_end"], (1, LANES))

    return [functools.partial(group, g0) for g0 in range(0, len(chains), ML_GROUP)]


def _mlstm_kernel(*refs, T, nb, with_output):
    for step in _mlstm_steps(pl.program_id(0), *refs, T=T, nb=nb, with_output=with_output):
        step()


def _att_mlstm_kernel(*refs, T, nb, nblk, ns):
    j = pl.program_id(0)
    sink_ref, att_in, ml_in = refs[0], refs[1:10], refs[10:22]
    o_ref, ml_out = refs[22], refs[23:]
    att = _att_steps(lax.rem(j, ns), sink_ref, *att_in, o_ref, nb=nblk)
    ml = _mlstm_steps(j, *ml_in, *ml_out, T=T, nb=nb, with_output=True)
    for k in range(max(len(att), len(ml))):
        if k < len(att):
            att[k]()
        if k < len(ml):
            ml[k]()


def _mlstm_specs(B, T, nc):
    up = lambda j: j
    down = lambda j: nc - 1 - j

    def specs(o):
        return [pl.BlockSpec((B, T, ML_QK), lambda j: (0, o(j), 0)),
                pl.BlockSpec((B, ML_QK, T), lambda j: (0, 0, o(j))),
                pl.BlockSpec((B, T, ML_V), lambda j: (0, o(j), 0)),
                pl.BlockSpec((B, T, LANES), lambda j: (0, o(j), 0)),
                pl.BlockSpec((B, 2 * SUBLANES, T), lambda j: (0, 0, o(j)))]

    nchains = B * 2 * ML_HEADS
    s_spec = _const_spec((nchains, ML_QK_DIM, 2 * ML_V_DIM))
    m_spec = _const_spec((nchains, LANES))
    state_shape = [jax.ShapeDtypeStruct((nchains, ML_QK_DIM, 2 * ML_V_DIM), F32),
                   jax.ShapeDtypeStruct((nchains, LANES), F32)]
    h_specs = [pl.BlockSpec((B, T, ML_V), lambda j: (0, up(j), 0)),
               pl.BlockSpec((B, T, ML_V), lambda j: (0, down(j), 0))]
    return specs(up) + specs(down) + [s_spec, m_spec], h_specs, [s_spec, m_spec], state_shape


def _att_mlstm(qt, kv, kvx, vt, vtx, sink, qm, kmt, vm, gc, gr, s0, m0):
    B, _, L = qt.shape
    T = ML_CHUNK
    nc = L // T
    nblk = L // BLOCK
    ns = nblk // ATT_QB
    assert nc == B * ns
    cx = kvx.shape[1]
    kw = kv.shape[2]
    vw = vt.shape[1]
    wide = ATT_QB * BLOCK
    smp = lambda j: j // ns
    cur = lambda j: j % ns
    prev = lambda j: jnp.maximum(cur(j) * ATT_QB - 1, 0)
    nxt = lambda j: jnp.minimum((cur(j) + 1) * ATT_QB, nblk - 1)
    att_specs = [
        pl.BlockSpec(memory_space=pltpu.SMEM),
        pl.BlockSpec((1, ATT_Q, wide), lambda j: (smp(j), 0, cur(j))),
        pl.BlockSpec((1, BLOCK, kw), lambda j: (smp(j), prev(j), 0)),
        pl.BlockSpec((1, wide, kw), lambda j: (smp(j), cur(j), 0)),
        pl.BlockSpec((1, BLOCK, kw), lambda j: (smp(j), nxt(j), 0)),
        pl.BlockSpec((1, cx, kw), lambda j: (smp(j), 0, 0)),
        pl.BlockSpec((1, vw, BLOCK), lambda j: (smp(j), 0, prev(j))),
        pl.BlockSpec((1, vw, wide), lambda j: (smp(j), 0, cur(j))),
        pl.BlockSpec((1, vw, BLOCK), lambda j: (smp(j), 0, nxt(j))),
        pl.BlockSpec((1, vw, cx), lambda j: (smp(j), 0, 0)),
    ]
    ml_in, h_specs, st_specs, st_shape = _mlstm_specs(B, T, nc)
    h_shape = jax.ShapeDtypeStruct((B, L, ML_V), F32)
    att, hf, hb, _, _ = pl.pallas_call(
        functools.partial(_att_mlstm_kernel, T=T, nb=B, nblk=nblk, ns=ns),
        out_shape=(jax.ShapeDtypeStruct((B, L, ATT_Q), BF16), h_shape, h_shape, *st_shape),
        grid=(nc,),
        in_specs=att_specs + ml_in,
        out_specs=(pl.BlockSpec((1, wide, ATT_Q), lambda j: (smp(j), cur(j), 0)), *h_specs, *st_specs),
        compiler_params=_cparams(("arbitrary",)),
        name="attention_mlstm",
    )(sink, qt, kv, kv, kv, kvx, vt, vt, vt, vtx, qm, kmt, vm, gc, gr, qm, kmt, vm, gc, gr, s0, m0)
    return att, hf, hb


def _mlstm(qm, kmt, vm, gc, gr, s0, m0, with_output):
    B, L, _ = qm.shape
    T = ML_CHUNK
    nc = L // T
    ml_in, h_specs, st_specs, st_shape = _mlstm_specs(B, T, nc)
    out_shape, out_specs = st_shape, st_specs
    if with_output:
        out_shape = [jax.ShapeDtypeStruct((B, L, ML_V), F32)] * 2 + out_shape
        out_specs = h_specs + out_specs
    return pl.pallas_call(
        functools.partial(_mlstm_kernel, T=T, nb=B, with_output=with_output),
        out_shape=tuple(out_shape),
        grid=(nc,),
        in_specs=ml_in,
        out_specs=tuple(out_specs),
        compiler_params=_cparams(("arbitrary",)),
        name="mlstm_scan" if with_output else "mlstm_context_state",
    )(qm, kmt, vm, gc, gr, qm, kmt, vm, gc, gr, s0, m0)


def _odd_tables(gate_b, q_g, k_g):
    assert sum([ATT_Q, ATT_KV, ATT_KV, ML_QK, ML_QK, ML_V, ML_V]) == OG
    head = np.arange(LANES) // HEAD_DIM
    pm = jnp.asarray((head[:, None] == head[None, :]) / HEAD_DIM, F32)
    gain = jnp.concatenate([jnp.tile(q_g, ATT_HEADS), jnp.tile(k_g, ATT_KV_HEADS)])[None, :]
    gb = jnp.pad(gate_b.reshape(1, -1), ((0, 0), (0, LANES - gate_b.size)))
    return pm, gain, gb


def kernel(x, c, ctx, c_ctx, ada_w, ada_b, norm_g, even_w_in, even_conv, even_w_out, odd_w_in, odd_gate_b,
           odd_q_g, odd_k_g, odd_sink, odd_w_out, ffn_w_up, ffn_conv, ffn_w_down):
    B, L, _ = x.shape
    C = ctx.shape[1]
    depth = ada_w.shape[0]
    assert depth == 2 and L % (DFT_N1 * SUBLANES) == 0 and C % ML_CHUNK == 0

    cv = jnp.concatenate([c, c_ctx[None, :], jnp.zeros((SUBLANES - B - 1, D_MODEL), F32)], axis=0)
    mod = _modulation(cv, ada_w, ada_b)
    lat, cx = None, B

    tm = min(512, L)
    tc = _channel_dft_table()
    w_in0, w_out0 = _cast_weights((even_w_in, even_w_out), 0, (EVEN_IN, D_MODEL))
    first = ((ffn_w_up, 0, 2 * D_FF, 1), (ffn_w_down, 0, D_MODEL, 1))
    later = ((ffn_w_up, 1, 2 * D_FF, 1), (ffn_w_down, 1, D_MODEL, 2), (odd_w_in, 0, ODD_COLS, 1),
             (odd_w_out, 0, D_MODEL, 1))

    ng00, ng01 = norm_g[0, 0][None, :], norm_g[0, 1][None, :]

    yc, zr, zi, w_up0, w_down0 = _even_in(x, mod, 0, lat, ng00, w_in0, even_conv[0], tc, tm, L // DFT_N1, first)
    xl, w_up1, w_down1, w_in1, w_out1 = _mix_ffn(x, (yc, _seq_dft(zr, zi, tm)), mod, 0, lat, ng01, w_out0, w_up0,
                                                 ffn_conv, w_down0, tm, odd=False, side=later)
    yc, zr, zi = _even_in(ctx, mod, 0, cx, ng00, w_in0, even_conv[0], tc, C, None)
    xc = _mix_ffn(ctx, (yc, _dense_seq_dft(zr, zi)), mod, 0, cx, ng01, w_out0, w_up0, ffn_conv, w_down0, C, odd=False)

    pm, gain, gb = _odd_tables(odd_gate_b[0], odd_q_g[0], odd_k_g[0])
    ng10, ng11 = norm_g[1, 0][None, :], norm_g[1, 1][None, :]
    cos, sin = _rope_tables(L)
    one, nil = jnp.ones((C, LANES), F32), jnp.zeros((C, LANES), F32)
    qt, kv, vt, qm, kmt, vm, om, gc, gr = _odd_in(xl, mod, 1, lat, ng10, w_in1, pm, gain, cos, sin, gb, tm)
    _, kvx, vtx, qmx, kmtx, vmx, _, gcx, grx = _odd_in(xc, mod, 1, cx, ng10, w_in1, pm, gain, one, nil, gb, C)

    nchains = B * 2 * ML_HEADS
    s0 = jnp.zeros((nchains, ML_QK_DIM, 2 * ML_V_DIM), F32)
    m0 = jnp.zeros((nchains, LANES), F32)
    s1, m1 = _mlstm(qmx, kmtx, vmx, gcx, grx, s0, m0, with_output=False)
    att, hf, hb = _att_mlstm(qt, kv, kvx, vt, vtx, odd_sink[0], qm, kmt, vm, gc, gr, s1, m1)
    return _mix_ffn(xl, (att, hf, hb, om), mod, 1, lat, ng11, w_out1, w_up1, ffn_conv, w_down1, tm, odd=True)
```

```python
import functools

import numpy as np
import jax
import jax.numpy as jnp
from jax import lax
from jax.experimental import pallas as pl
from jax.experimental.pallas import tpu as pltpu

F32 = jnp.float32
BF16 = jnp.bfloat16

D_MODEL = 1024
GRID_W = 64
EPS = 1e-6
SC_CH = 512
FT_CH = 512
FT_GROUPS = 4
FT_GROUP_CH = FT_CH // FT_GROUPS
EVEN_IN = 3 * SC_CH + FT_CH
ATT_HEADS = 8
ATT_KV_HEADS = 2
HEAD_DIM = 64
ATT_SCALE = HEAD_DIM ** -0.5
WINDOW = 128
BLOCK = 128
ROPE_THETA = 10000.0
ML_HEADS = 4
ML_QK_DIM = 64
ML_V_DIM = 128
ATT_Q = ATT_HEADS * HEAD_DIM
ATT_KV = ATT_KV_HEADS * HEAD_DIM
ML_QK = ML_HEADS * ML_QK_DIM
ML_V = ML_HEADS * ML_V_DIM
D_FF = 2816

LANES = 128
SUBLANES = 8
VMEM_LIMIT_BYTES = 56 * 1024 * 1024

DFT_N1 = 128
FF_CHUNK = 256
N_FF_CHUNKS = D_FF // FF_CHUNK
ML_CHUNK = 128
ATT_QB = 2
NEG_BIG = -1e30
LOG2E = 1.4426950408889634

OQ, OK_, OV, OQM, OKM, OVM, OOM, OG = 0, 512, 640, 768, 1024, 1280, 1792, 2304
ODD_COLS = OG + LANES


def _cparams(sem):
    return pltpu.CompilerParams(dimension_semantics=sem, vmem_limit_bytes=VMEM_LIMIT_BYTES)


def _sigmoid(x):
    return 1.0 / (1.0 + jnp.exp(-x))


def _norm_mod(x, g, shift, scale):
    y = x * lax.rsqrt(jnp.mean(x * x, axis=-1, keepdims=True) + EPS)
    return y * g * (1.0 + scale) + shift


def _mod_vec(mod_ref, k, mrow):
    r = pl.program_id(0) if mrow is None else mrow
    return mod_ref[k, pl.ds(r, 1), :]


def _mod_spec(layer):
    return pl.BlockSpec((None, 6, SUBLANES, D_MODEL), lambda *_: (layer, 0, 0, 0))


def _halo_rows(x, xn, xp, shift, scale, ng_ref):
    g = ng_ref[...]
    parts = [_norm_mod(r, g, shift, scale) for r in (x, xn, xp)]
    return jnp.concatenate(parts, axis=0).astype(BF16)


def _halo_valid(tm, i, nt):
    row = lax.broadcasted_iota(jnp.int32, (tm + 2 * SUBLANES, 1), 0)
    return ((row < tm) | ((row < tm + SUBLANES) & (i < nt - 1)) | ((row >= tm + SUBLANES) & (i > 0)))


def _conv3(v, cw, tm):
    n = v.shape[0]
    vp = pltpu.roll(v, 1, 0)[:tm]
    vn = pltpu.roll(v, n - 1, 0)[:tm]
    return vp * cw[0:1] + v[:tm] * cw[1:2] + vn * cw[2:3]


def _halo_specs(tm, L):
    hb = tm // SUBLANES
    last = L // SUBLANES - 1
    return [
        pl.BlockSpec((1, tm, D_MODEL), lambda b, i: (b, i, 0)),
        pl.BlockSpec((1, SUBLANES, D_MODEL), lambda b, i: (b, jnp.minimum((i + 1) * hb, last), 0)),
        pl.BlockSpec((1, SUBLANES, D_MODEL), lambda b, i: (b, jnp.maximum(i * hb - 1, 0), 0)),
    ]


def _const_spec(shape):
    nd = len(shape)
    return pl.BlockSpec(shape, lambda *_: (0,) * nd)


def _resident_spec(shape):
    nd = len(shape)
    return pl.BlockSpec(shape, lambda *_: (0,) * nd, pipeline_mode=pl.Buffered(1))


def _weight_spec(w, shape, layer):
    return _resident_spec(shape) if w.ndim == len(shape) else _layer_spec(shape, layer)


def _layer_spec(shape, layer):
    nd = len(shape)
    return pl.BlockSpec((None,) + tuple(shape), lambda *_: (layer,) + (0,) * nd, pipeline_mode=pl.Buffered(1))


def _split_dot(x, p):
    hi = x.astype(BF16).astype(F32)
    y = jnp.dot(jnp.concatenate([hi, x - hi], axis=0).astype(BF16), p, preferred_element_type=F32)
    return y[:x.shape[0]] + y[x.shape[0]:]


def _mod_kernel(cv_ref, w_ref, b_ref, o_ref):
    cv = cv_ref[...]
    o_ref[0, 0] = _split_dot(cv * _sigmoid(cv), w_ref[0].astype(BF16)) + b_ref[0, 0]


def _modulation(cv, ada_w, ada_b):
    depth, _, n = ada_w.shape
    nv = n // D_MODEL
    return pl.pallas_call(
        _mod_kernel,
        out_shape=jax.ShapeDtypeStruct((depth, nv, SUBLANES, D_MODEL), F32),
        grid=(depth, nv),
        in_specs=[
            pl.BlockSpec((SUBLANES, D_MODEL), lambda l, j: (0, 0)),
            pl.BlockSpec((1, D_MODEL, D_MODEL), lambda l, j: (l, 0, j)),
            pl.BlockSpec((1, 1, 1, D_MODEL), lambda l, j: (l, j, 0, 0)),
        ],
        out_specs=pl.BlockSpec((1, 1, SUBLANES, D_MODEL), lambda l, j: (l, j, 0, 0)),
        compiler_params=_cparams(("arbitrary", "arbitrary")),
        name="modulation",
    )(cv, ada_w, ada_b.reshape(depth, nv, 1, D_MODEL))


CAST_STEPS = 8


def _cast_refs(srcs, dsts):
    for src, dst in zip(srcs, dsts):
        w = src[...].astype(BF16)
        pad = dst.shape[-1] - src.shape[-1]
        if pad:
            w = jnp.concatenate([w, jnp.zeros(w.shape[:-1] + (pad,), BF16)], axis=-1)
        dst[...] = w


def _cast_kernel(*refs):
    n = len(refs) // 2
    _cast_refs(refs[:n], refs[n:])


def _layer_rows_spec(w, layer, rb, imap):
    if w.ndim == 2:
        return pl.BlockSpec((rb, w.shape[1]), lambda *a: (imap(*a), 0))
    return pl.BlockSpec((None, rb, w.shape[2]), lambda *a: (layer, imap(*a), 0))


def _side_cast_specs(side, steps, step_of):
    ins, outs, shapes = [], [], []
    for w, wl, wd, per in side:
        r = w.shape[-2]
        rb = r * per // steps
        blk = lambda *a, per=per: step_of(*a) // per
        ins.append(_layer_rows_spec(w, wl, rb, blk))
        outs.append(pl.BlockSpec((rb, wd), lambda *a, blk=blk: (blk(*a), 0)))
        shapes.append(jax.ShapeDtypeStruct((r, wd), BF16))
    return ins, outs, shapes


def _cast_weights(ws, layer, widths):
    in_specs, out_specs, out_shape = [], [], []
    for w, wd in zip(ws, widths):
        r = w.shape[-2]
        in_specs.append(_layer_rows_spec(w, layer, r // CAST_STEPS, lambda i: i))
        out_specs.append(pl.BlockSpec((r // CAST_STEPS, wd), lambda i: (i, 0)))
        out_shape.append(jax.ShapeDtypeStruct((r, wd), BF16))
    return pl.pallas_call(
        _cast_kernel,
        out_shape=tuple(out_shape),
        grid=(CAST_STEPS,),
        in_specs=in_specs,
        out_specs=tuple(out_specs),
        compiler_params=_cparams(("arbitrary",)),
        name="cast_weights",
    )(*ws)


def _even_in_kernel(x_ref, xn_ref, xp_ref, mod_ref, ng_ref, w_ref, cw_ref, tc_ref, *rest, tm, nt, n2, mrow, nsub):
    nside = (len(rest) - 3) // 2
    yc_ref, zr_ref, zi_ref = rest[nside:nside + 3]
    _cast_refs(rest[:nside], rest[nside + 3:])
    i = pl.program_id(1)
    shift, scale = _mod_vec(mod_ref, 0, mrow), _mod_vec(mod_ref, 1, mrow)
    tc = tc_ref[...].astype(BF16)
    for s in range(nsub):
        lo = s * tm
        x = x_ref[0, lo:lo + tm]
        xn = xn_ref[0] if s == nsub - 1 else x_ref[0, lo + tm:lo + tm + SUBLANES]
        xp = xp_ref[0] if s == 0 else x_ref[0, lo - SUBLANES:lo]
        hh = _halo_rows(x, xn, xp, shift, scale, ng_ref)
        u = jnp.dot(hh, w_ref[...], preferred_element_type=F32)
        v = u[:, SC_CH:2 * SC_CH] * u[:, 2 * SC_CH:3 * SC_CH]
        v = jnp.where(_halo_valid(tm, i * nsub + s, nt), v, 0.0)
        yc = u[:tm, :SC_CH] * _conv3(v, cw_ref[...], tm)
        yc_ref[0, lo:lo + tm] = yc.astype(BF16)
        uf = u[:tm, 3 * SC_CH:].astype(BF16)
        for g in range(FT_GROUPS):
            sl = slice(g * FT_GROUP_CH, (g + 1) * FT_GROUP_CH)
            ab = jnp.dot(uf[:, sl], tc, preferred_element_type=F32)
            if n2 is None:
                zr_ref[g, 0, lo:lo + tm] = ab[:, :FT_GROUP_CH]
                zi_ref[g, 0, lo:lo + tm] = ab[:, FT_GROUP_CH:]
            else:
                for a in range(tm // n2):
                    dst = pl.ds(lo + a, n2, stride=tm // n2)
                    zr_ref[g, 0, dst, :] = ab[n2 * a:n2 * (a + 1), :FT_GROUP_CH]
                    zi_ref[g, 0, dst, :] = ab[n2 * a:n2 * (a + 1), FT_GROUP_CH:]


def _even_in(x, mod, layer, mrow, ng, w_in, cw, tc, tm, n2, side=()):
    B, L, _ = x.shape
    nt = L // tm
    nsub = 2 if nt % 2 == 0 else 1
    bm = nsub * tm
    out = jax.ShapeDtypeStruct((B, L, FT_CH), BF16)
    zout = jax.ShapeDtypeStruct((FT_GROUPS, B, L, FT_GROUP_CH), F32)
    ospec = pl.BlockSpec((1, bm, FT_CH), lambda b, i: (b, i, 0))
    zspec = pl.BlockSpec((FT_GROUPS, 1, bm, FT_GROUP_CH), lambda b, i: (0, b, i, 0))
    ns = nt // nsub
    side_in, side_out, side_shape = _side_cast_specs(side, B * ns, lambda b, i: b * ns + i)
    return pl.pallas_call(
        functools.partial(_even_in_kernel, tm=tm, nt=nt, n2=n2, mrow=mrow, nsub=nsub),
        out_shape=(out, zout, zout, *side_shape),
        grid=(B, ns),
        in_specs=_halo_specs(bm, L) + [
            _mod_spec(layer),
            _const_spec((1, D_MODEL)),
            _const_spec((D_MODEL, EVEN_IN)),
            _const_spec((3, SC_CH)),
            _const_spec((FT_GROUP_CH, 2 * FT_GROUP_CH)),
        ] + side_in,
        out_specs=(ospec, zspec, zspec, *side_out),
        compiler_params=_cparams(("parallel", "arbitrary")),
        name="even_in",
    )(x, x, x, mod, ng, w_in, cw, tc, *[e[0] for e in side])


def _seq_dft_kernel(zr_ref, zi_ref, m_ref, g_ref, y_ref, o_scr, *, n2, tm):
    m1 = m_ref[...].astype(BF16)
    chunk = tm // n2
    ntile = DFT_N1 // chunk

    def rows(ref, j):
        return [ref[0, 0, t * tm + j * chunk:t * tm + (j + 1) * chunk, :] for t in range(ntile)]

    for j in range(n2):
        z = jnp.concatenate(rows(zr_ref, j) + rows(zi_ref, j), axis=0)
        o_scr[2 * DFT_N1 * j:2 * DFT_N1 * (j + 1), :] = jnp.dot(m1, z.astype(BF16), preferred_element_type=F32)
    for a in range(DFT_N1 // SUBLANES):
        def gather(base):
            x = jnp.concatenate([o_scr[2 * DFT_N1 * j + base + SUBLANES * a:2 * DFT_N1 * j + base + SUBLANES * (a + 1), :]
                                 for j in range(n2)], axis=0)
            return jnp.swapaxes(x.reshape(n2, SUBLANES, LANES), 0, 1)
        xr, xi = gather(0), gather(DFT_N1)
        ys = []
        for r in range(SUBLANES):
            o = jnp.concatenate([xr[r], xi[r]], axis=0).astype(BF16)
            ys.append(jnp.dot(g_ref[SUBLANES * a + r].astype(BF16), o, preferred_element_type=F32))
        y8 = jnp.swapaxes(jnp.stack(ys, axis=0), 0, 1)
        for k2 in range(n2):
            y_ref[0, DFT_N1 * k2 + SUBLANES * a:DFT_N1 * k2 + SUBLANES * (a + 1), :] = y8[k2]


def _dft_tables(L):
    n2 = L // DFT_N1
    k = np.arange(DFT_N1)
    a = 2.0 * np.pi * ((k[:, None] * k[None, :]) % DFT_N1) / DFT_N1
    er, ei = np.cos(a) / np.sqrt(DFT_N1), -np.sin(a) / np.sqrt(DFT_N1)
    m1 = np.block([[er, -ei], [ei, er]])
    k1 = np.arange(DFT_N1)[:, None, None]
    k2 = np.arange(n2)[None, :, None]
    nn = np.arange(n2)[None, None, :]
    th = 2.0 * np.pi * ((nn * (k1 + DFT_N1 * k2)) % L) / L
    g = np.concatenate([np.cos(th), np.sin(th)], axis=-1) / np.sqrt(n2)
    return jnp.asarray(m1, F32), jnp.asarray(g, F32)


def _channel_dft_table():
    k = np.arange(FT_GROUP_CH)
    a = 2.0 * np.pi * ((k[:, None] * k[None, :]) % FT_GROUP_CH) / FT_GROUP_CH
    t = np.concatenate([np.cos(a), -np.sin(a)], axis=1) / np.sqrt(FT_GROUP_CH)
    return jnp.asarray(t, F32)


def _seq_dft(zr, zi, tm):
    G, B, L, C = zr.shape
    n2 = L // DFT_N1
    m1, g = _dft_tables(L)
    zspec = pl.BlockSpec((1, 1, L, C), lambda b, j: (j, b, 0, 0))
    return pl.pallas_call(
        functools.partial(_seq_dft_kernel, n2=n2, tm=tm),
        out_shape=jax.ShapeDtypeStruct((B, L, G * C), F32),
        grid=(B, G),
        in_specs=[zspec, zspec, _const_spec((2 * DFT_N1, 2 * DFT_N1)), _const_spec((DFT_N1, n2, 2 * n2))],
        out_specs=pl.BlockSpec((1, L, C), lambda b, j: (b, 0, j)),
        scratch_shapes=[pltpu.VMEM((2 * DFT_N1 * n2, LANES), F32)],
        compiler_params=_cparams(("parallel", "arbitrary")),
        name="seq_dft",
    )(zr, zi, m1, g)


def _dense_dft_kernel(zr_ref, zi_ref, t_ref, y_ref):
    z = jnp.concatenate([zr_ref[0, 0], zi_ref[0, 0]], axis=0).astype(BF16)
    y_ref[0] = jnp.dot(t_ref[...].astype(BF16), z, preferred_element_type=F32)


def _dense_seq_dft(zr, zi):
    G, B, L, C = zr.shape
    k = np.arange(L)
    a = 2.0 * np.pi * ((k[:, None] * k[None, :]) % L) / L
    t = jnp.asarray(np.concatenate([np.cos(a), np.sin(a)], axis=1) / np.sqrt(L), F32)
    zspec = pl.BlockSpec((1, 1, L, C), lambda b, j: (j, b, 0, 0))
    return pl.pallas_call(
        _dense_dft_kernel,
        out_shape=jax.ShapeDtypeStruct((B, L, G * C), F32),
        grid=(B, G),
        in_specs=[zspec, zspec, _const_spec((L, 2 * L))],
        out_specs=pl.BlockSpec((1, L, C), lambda b, j: (b, 0, j)),
        compiler_params=_cparams(("arbitrary", "arbitrary")),
        name="dense_seq_dft",
    )(zr, zi, t)


HALO = 16


def _wide_halo_specs(tm, L, width):
    hb = tm // HALO
    last = L // HALO - 1
    return [
        pl.BlockSpec((1, tm, width), lambda b, i: (b, i, 0)),
        pl.BlockSpec((1, HALO, width), lambda b, i: (b, jnp.minimum((i + 1) * hb, last), 0)),
        pl.BlockSpec((1, HALO, width), lambda b, i: (b, jnp.maximum(i * hb - 1, 0), 0)),
    ]


def _circ(t_ref, n_ref, p_ref):
    return jnp.concatenate([t_ref[0], n_ref[0], p_ref[0]], axis=0)


def _mix_ffn_kernel(*refs, tm, nt, odd, mrow, nside):
    n_in = 15 if odd else 9
    x3, rest = refs[:3], refs[3:n_in]
    mod_ref, ng_ref, wo_ref, wu_ref, cw_ref, wd_ref = refs[n_in:n_in + 6]
    side_src = refs[n_in + 6:n_in + 6 + nside]
    o_ref = refs[n_in + 6 + nside]
    side_dst = refs[n_in + 7 + nside:n_in + 7 + 2 * nside]
    hh_scr, act_scr = refs[n_in + 7 + 2 * nside:]
    _cast_refs(side_src, side_dst)
    i = pl.program_id(1)
    if odd:
        att, hf, hb, om = (_circ(*rest[k:k + 3]) for k in range(0, 12, 3))
        lhs = jnp.concatenate([att, ((hf + hb) * _sigmoid(om.astype(F32))).astype(BF16)], axis=-1)
    else:
        yc, yf = _circ(*rest[0:3]), _circ(*rest[3:6])
        lhs = jnp.concatenate([yc, yf.astype(BF16)], axis=-1)
    mv = [_mod_vec(mod_ref, k, mrow) for k in range(6)]
    x1 = _circ(*x3) + mv[2] * jnp.dot(lhs, wo_ref[...], preferred_element_type=F32)
    hh_scr[...] = _norm_mod(x1, ng_ref[...], mv[3], mv[4]).astype(BF16)
    row = lax.broadcasted_iota(jnp.int32, (tm + 2 * HALO, 1), 0)
    valid = (row < tm) | ((row < tm + HALO) & (i < nt - 1)) | ((row >= tm + HALO) & (i > 0))
    for c in range(N_FF_CHUNKS):
        lo = c * FF_CHUNK
        g = jnp.dot(hh_scr[...], wu_ref[:, lo:lo + FF_CHUNK], preferred_element_type=F32)
        g = jnp.where(valid, g, 0.0)
        cv = _conv3(g, cw_ref[:, lo:lo + FF_CHUNK], tm)
        val = jnp.dot(hh_scr[:tm, :], wu_ref[:, D_FF + lo:D_FF + lo + FF_CHUNK], preferred_element_type=F32)
        act_scr[:, lo:lo + FF_CHUNK] = (cv * _sigmoid(cv) * val).astype(BF16)
    y = jnp.dot(act_scr[...], wd_ref[...], preferred_element_type=F32)
    o_ref[0] = x1[:tm] + mv[5] * y


def _mix_ffn(x, mixed, mod, layer, mrow, ng, w_out, w_up, cw, w_down, tm, odd, side=()):
    B, L, _ = x.shape
    nt = L // tm
    specs = _wide_halo_specs(tm, L, D_MODEL)
    args = [x, x, x]
    for a in mixed:
        specs += _wide_halo_specs(tm, L, a.shape[-1])
        args += [a, a, a]
    side_in, side_out, side_shape = _side_cast_specs(side, B * nt, lambda b, i: b * nt + i)
    out = pl.pallas_call(
        functools.partial(_mix_ffn_kernel, tm=tm, nt=nt, odd=odd, mrow=mrow, nside=len(side)),
        out_shape=(jax.ShapeDtypeStruct(x.shape, F32), *side_shape),
        grid=(B, nt),
        in_specs=specs + [
            _mod_spec(layer),
            _const_spec((1, D_MODEL)),
            _resident_spec((D_MODEL, D_MODEL)),
            _weight_spec(w_up, (D_MODEL, 2 * D_FF), layer),
            _layer_spec((3, D_FF), layer),
            _weight_spec(w_down, (D_FF, D_MODEL), layer),
        ] + side_in,
        out_specs=(pl.BlockSpec((1, tm, D_MODEL), lambda b, i: (b, i, 0)), *side_out),
        scratch_shapes=[pltpu.VMEM((tm + 2 * HALO, D_MODEL), BF16),
                        pltpu.VMEM((tm, D_FF), BF16)],
        compiler_params=_cparams(("parallel", "arbitrary")),
        name="odd_mix_ffn" if odd else "even_mix_ffn",
    )(*args, mod, ng, w_out, w_up, cw, w_down, *[e[0] for e in side])
    return out if side else out[0]


def _split3(x):
    parts = []
    r = x
    for _ in range(3):
        p = r.astype(BF16)
        parts.append(p)
        r = r - p.astype(F32)
    return parts


def _odd_in_kernel(x_ref, mod_ref, ng_ref, w_ref, pm_ref, gain_ref, cos_ref, sin_ref, gb_ref, tl_ref, tu_ref,
                   qt_ref, kv_ref, vt_ref, qm_ref, kmt_ref, vm_ref, om_ref, gc_ref, gr_ref, *, mrow, tm, nsub):
    for s in range(nsub):
        rs = slice(s * tm, (s + 1) * tm)
        h = _norm_mod(x_ref[0, rs], ng_ref[...], _mod_vec(mod_ref, 0, mrow), _mod_vec(mod_ref, 1, mrow))
        u = jnp.dot(h.astype(BF16), w_ref[...], preferred_element_type=F32)

        uqk = u[:, OQ:OV]
        sq = (uqk * uqk).astype(BF16)
        pm = pm_ref[...].astype(BF16)
        ms = jnp.concatenate([jnp.dot(sq[:, t * LANES:(t + 1) * LANES], pm, preferred_element_type=F32)
                              for t in range((OV - OQ) // LANES)], axis=1)
        rn = uqk * lax.rsqrt(ms + EPS) * gain_ref[...]
        lane = lax.broadcasted_iota(jnp.int32, (1, LANES), 1)
        first = (lane % 32) < 16
        cos = cos_ref[rs, :]
        sin = sin_ref[rs, :]
        roped = []
        for t in range((OV - OQ) // LANES):
            xt = rn[:, t * LANES:(t + 1) * LANES]
            sw = jnp.where(first, pltpu.roll(xt, LANES - 16, 1), pltpu.roll(xt, 16, 1))
            roped.append(xt * cos + sw * sin)
        for t in range(ATT_Q // LANES):
            qt_ref[0, t * LANES:(t + 1) * LANES, rs] = (roped[t] * (ATT_SCALE * LOG2E)).T.astype(BF16)
        k = roped[ATT_Q // LANES]
        v = u[:, OV:OQM]
        half = LANES // 2
        kv_ref[0, rs, 0:LANES] = k.astype(BF16)
        kv_ref[0, rs, LANES:2 * LANES] = pltpu.roll(k, half, 1).astype(BF16)
        vt_ref[0, :, rs] = v.T.astype(BF16)

        qm_ref[0, rs] = u[:, OQM:OKM].astype(BF16)
        for p in range(ML_QK // LANES):
            km = u[:, OKM + p * LANES:OKM + (p + 1) * LANES] * (ML_QK_DIM ** -0.5)
            kmt_ref[0, p * LANES:(p + 1) * LANES, rs] = km.T.astype(BF16)
        vm_ref[0, rs] = u[:, OVM:OOM].astype(BF16)
        om_ref[0, rs] = u[:, OOM:OG].astype(BF16)

        gt = (u[:, OG:ODD_COLS] + gb_ref[...]).T[:2 * SUBLANES, :]
        row = lax.broadcasted_iota(jnp.int32, (2 * SUBLANES, 1), 0)
        logsig = jnp.minimum(gt, 0.0) - jnp.log(1.0 + jnp.exp(-jnp.abs(gt)))
        parts = jnp.concatenate(_split3(logsig), axis=0)
        tl, tu = tl_ref[...].astype(BF16), tu_ref[...].astype(BF16)

        def chunk_cumsum(tri):
            c = jnp.concatenate([jnp.dot(parts[:, k:k + ML_CHUNK], tri, preferred_element_type=F32)
                                 for k in range(0, tm, ML_CHUNK)], axis=1)
            n = 2 * SUBLANES
            return c[0:n] + c[n:2 * n] + c[2 * n:3 * n]

        cum_f, cum_b = chunk_cumsum(tu), chunk_cumsum(tl)
        sel = row % 8
        gr = jnp.where(sel < 4, gt, jnp.where(row < SUBLANES, cum_f, cum_b)) * LOG2E
        gr_ref[0, :, rs] = gr
        gc_ref[0, rs] = jnp.concatenate([gr, jnp.zeros((LANES - 2 * SUBLANES, tm), F32)], axis=0).T


def _chunk_tri():
    i = np.arange(ML_CHUNK)
    return jnp.asarray(i[None, :] <= i[:, None], F32), jnp.asarray(i[None, :] >= i[:, None], F32)


def _odd_in(x, mod, layer, mrow, ng, w, pm, gain, cos, sin, gb, tm):
    B, L, _ = x.shape
    nsub = 2 if L % (2 * tm) == 0 else 1
    bm = nsub * tm

    def rows(c, dt=BF16):
        return jax.ShapeDtypeStruct((B, L, c), dt), pl.BlockSpec((1, bm, c), lambda b, i: (b, i, 0))

    def cols(c, dt=BF16):
        return jax.ShapeDtypeStruct((B, c, L), dt), pl.BlockSpec((1, c, bm), lambda b, i: (b, 0, i))

    outs = [cols(ATT_Q), rows(2 * LANES), cols(LANES), rows(ML_QK), cols(ML_QK), rows(ML_V), rows(ML_V),
            rows(LANES, F32), cols(2 * SUBLANES, F32)]
    nqk = OV - OQ
    tl, tu = _chunk_tri()
    return pl.pallas_call(
        functools.partial(_odd_in_kernel, mrow=mrow, tm=tm, nsub=nsub),
        out_shape=tuple(o[0] for o in outs),
        grid=(B, L // bm),
        in_specs=[
            pl.BlockSpec((1, bm, D_MODEL), lambda b, i: (b, i, 0)),
            _mod_spec(layer),
            _const_spec((1, D_MODEL)),
            _const_spec((D_MODEL, ODD_COLS)),
            _const_spec((LANES, LANES)),
            _const_spec((1, nqk)),
            pl.BlockSpec((bm, LANES), lambda b, i: (i, 0)),
            pl.BlockSpec((bm, LANES), lambda b, i: (i, 0)),
            _const_spec((1, LANES)),
            _const_spec((ML_CHUNK, ML_CHUNK)),
            _const_spec((ML_CHUNK, ML_CHUNK)),
        ],
        out_specs=tuple(o[1] for o in outs),
        compiler_params=_cparams(("parallel", "arbitrary")),
        name="odd_in",
    )(x, mod, ng, w, pm, gain, cos, sin, gb, tl, tu)


def _rope_tables(L):
    rows = L // GRID_W
    pos = np.stack([np.repeat(np.arange(rows), GRID_W), np.tile(np.arange(GRID_W), rows)]).astype(np.float64)
    axis_dim = HEAD_DIM // 2
    inv_freq = ROPE_THETA ** (-np.arange(0, axis_dim, 2, dtype=np.float64) / axis_dim)
    ang = pos[:, :, None] * inv_freq
    c, sn = np.cos(ang), np.sin(ang)
    cos = np.concatenate([c[0], c[0], c[1], c[1]], axis=-1)
    sin = np.concatenate([-sn[0], sn[0], -sn[1], sn[1]], axis=-1)
    return jnp.asarray(np.tile(cos, (1, 2)), F32), jnp.asarray(np.tile(sin, (1, 2)), F32)


def _att_steps(n, sink_ref, qt_ref, kvp_ref, kvc_ref, kvn_ref, kvx_ref, vtp_ref, vtc_ref, vtn_ref, vtx_ref,
               o_ref, *, nb):
    half = LANES // 2
    lo = lax.broadcasted_iota(jnp.int32, (1, LANES), 1) < half
    zero = jnp.zeros((), BF16)
    cx = kvx_ref.shape[1]
    klocal = ([kvp_ref[0]] + [kvc_ref[0, i * BLOCK:(i + 1) * BLOCK] for i in range(ATT_QB)] + [kvn_ref[0]])
    vlocal = ([vtp_ref[0]] + [vtc_ref[0, :, i * BLOCK:(i + 1) * BLOCK] for i in range(ATT_QB)] + [vtn_ref[0]])
    kctx = [kvx_ref[0, i:i + BLOCK] for i in range(0, cx, BLOCK)]
    vctx = [vtx_ref[0, :, i:i + BLOCK] for i in range(0, cx, BLOCK)]
    ones = jnp.ones((half, BLOCK), BF16)

    kj = lax.broadcasted_iota(jnp.int32, (BLOCK, BLOCK), 0)
    qi = lax.broadcasted_iota(jnp.int32, (BLOCK, BLOCK), 1)

    def twice(x):
        return jnp.concatenate([x, x], axis=1)

    left = lax.broadcasted_iota(jnp.int32, (1, 2 * BLOCK), 1) < BLOCK
    group = ATT_HEADS // ATT_KV_HEADS

    def block(qb):
        blk_id = n * ATT_QB + qb
        kblocks = klocal[qb:qb + 3] + kctx
        vblocks = vlocal[qb:qb + 3] + vctx
        bias = [twice(jnp.where((kj >= qi) & (blk_id > 0), 0.0, NEG_BIG)).astype(BF16), None,
                twice(jnp.where((kj <= qi) & (blk_id < nb - 1), 0.0, NEG_BIG)).astype(BF16)] + [None] * len(kctx)
        for kvh in range(ATT_KV_HEADS):
            t0 = kvh * group // 2
            cols = slice(qb * BLOCK, (qb + 1) * BLOCK)
            qt2 = jnp.concatenate([qt_ref[0, t0 * LANES:(t0 + 1) * LANES, cols],
                                   qt_ref[0, (t0 + 1) * LANES:(t0 + 2) * LANES, cols]], axis=1)
            outs = []
            for par in range(2):
                sk = jnp.where(left, sink_ref[2 * t0 + par], sink_ref[2 * t0 + 2 + par]) * LOG2E
                m = sk.astype(BF16).astype(F32)
                acc = jnp.zeros((LANES, 2 * BLOCK), F32)
                for blk in range(len(kblocks)):
                    k, ks = kblocks[blk][:, :LANES], kblocks[blk][:, LANES:]
                    kh = ((jnp.where(lo, k, zero), jnp.where(lo, zero, ks)),
                          (jnp.where(lo, ks, zero), jnp.where(lo, zero, k)))[kvh][par]
                    s = jnp.dot(kh, qt2, preferred_element_type=F32)
                    s = s.astype(BF16)
                    if bias[blk] is not None:
                        s = s + bias[blk]
                    m_new = jnp.maximum(m, jnp.max(s, axis=0, keepdims=True).astype(F32))
                    p = jnp.exp2(s - m_new.astype(BF16))
                    vh = jnp.concatenate([vblocks[blk][kvh * half:(kvh + 1) * half], ones], axis=0)
                    acc = acc * jnp.exp2(m - m_new) + jnp.dot(vh, p, preferred_element_type=F32)
                    m = m_new
                l = acc[half:half + 1, :] + jnp.exp2(sk - m)
                outs.append(acc[:half] * (1.0 / l))
            ot = jnp.concatenate(outs, axis=0)
            rows = slice(qb * BLOCK, (qb + 1) * BLOCK)
            o_ref[0, rows, t0 * LANES:(t0 + 1) * LANES] = ot[:, :BLOCK].T.astype(BF16)
            o_ref[0, rows, (t0 + 1) * LANES:(t0 + 2) * LANES] = ot[:, BLOCK:].T.astype(BF16)

    return [functools.partial(block, qb) for qb in range(ATT_QB)]


ML_GROUP = 8


def _mlstm_steps(j, qf_ref, ktf_ref, vf_ref, gcf_ref, grf_ref, qb_ref, ktb_ref, vb_ref, gcb_ref, grb_ref,
                 s0_ref, m0_ref, *rest, T, nb, with_output):
    if with_output:
        hf_ref, hb_ref, s_ref, m_ref = rest
    else:
        s_ref, m_ref = rest

    @pl.when(j == 0)
    def _():
        s_ref[...] = s0_ref[...]
        m_ref[...] = m0_ref[...]

    ti = lax.broadcasted_iota(jnp.int32, (T, T), 0)
    si = lax.broadcasted_iota(jnp.int32, (T, T), 1)
    masks = (si <= ti, si >= ti)
    top = lax.broadcasted_iota(jnp.int32, (LANES, 1), 0) < (LANES // 2)
    zero = jnp.zeros((), BF16)
    ones = jnp.ones((T, ML_V_DIM), BF16)
    zpad = jnp.zeros((ML_QK_DIM, 2 * ML_V_DIM), BF16)
    fwd = (qf_ref, ktf_ref, vf_ref, gcf_ref, grf_ref)
    bwd = (qb_ref, ktb_ref, vb_ref, gcb_ref, grb_ref)
    chains = [(b, d, h) for b in range(nb) for d in range(2) for h in range(ML_HEADS)]

    def group(g0):
        grp = chains[g0:g0 + ML_GROUP]
        st = {}
        for (b, d, h) in grp:
            q_ref, kt_ref, v_ref, gc_ref, gr_ref = fwd if d == 0 else bwd
            r = (b * 2 + d) * ML_HEADS + h
            li, lf = 8 * d + h, 8 * d + 4 + h
            gr = gr_ref[b]
            u_row = gr[li:li + 1, :] - gr[lf:lf + 1, :]
            m_prev = m_ref[r:r + 1, 0:1]
            e = dict(r=r, u_row=u_row, m_prev=m_prev,
                     b_end=gr[lf:lf + 1, (T - 1 if d == 0 else 0):(T if d == 0 else 1)],
                     c_end=jnp.maximum(jnp.max(u_row, axis=1, keepdims=True), m_prev))
            if with_output:
                e["mu"] = jnp.where(masks[d], u_row, NEG_BIG)
                c_col = jnp.maximum(jnp.max(e["mu"], axis=1, keepdims=True), m_prev)
                e["cb"] = jnp.broadcast_to(c_col, (T, LANES))
                e["bb"] = jnp.broadcast_to(gc_ref[b, :, lf:lf + 1], (T, LANES))
            st[(b, d, h)] = e
        for (b, d, h) in grp:
            q_ref, kt_ref, v_ref, gc_ref, gr_ref = fwd if d == 0 else bwd
            e = st[(b, d, h)]
            pair = h // 2
            ktp = kt_ref[b, pair * LANES:(pair + 1) * LANES, :]
            e["kth"] = ktp[(h % 2) * ML_QK_DIM:(h % 2 + 1) * ML_QK_DIM, :]
            e["vext"] = jnp.concatenate([v_ref[b, :, h * ML_V_DIM:(h + 1) * ML_V_DIM], ones], axis=1)
            if with_output:
                qp = q_ref[b, :, pair * LANES:(pair + 1) * LANES]
                kpad = jnp.where(top, ktp, zero) if h % 2 == 0 else jnp.where(top, zero, ktp)
                qk = jnp.dot(qp, kpad, preferred_element_type=F32) * jnp.exp2(e["mu"] - e["cb"])
                qs = qp.astype(F32) * jnp.exp2(e["m_prev"] - e["cb"])
                e["lhs"] = jnp.concatenate([qk.astype(BF16), qs.astype(BF16)], axis=1)
        for (b, d, h) in grp:
            e = st[(b, d, h)]
            e["s_prev"] = s_ref[e["r"]]
            if with_output:
                sb = e["s_prev"].astype(BF16)
                rhs = jnp.concatenate([e["vext"]] + ([sb, zpad] if h % 2 == 0 else [zpad, sb]), axis=0)
                tot = jnp.dot(e["lhs"], rhs, preferred_element_type=F32)
                floor = jnp.exp2(-(e["bb"] + e["cb"]))
                hout = tot[:, :ML_V_DIM] / jnp.maximum(jnp.abs(tot[:, ML_V_DIM:]), floor)
                o_ref = hf_ref if d == 0 else hb_ref
                o_ref[b, :, h * ML_V_DIM:(h + 1) * ML_V_DIM] = hout
        for (b, d, h) in grp:
            e = st[(b, d, h)]
            kw = (e["kth"].astype(F32) * jnp.exp2(e["u_row"] - e["c_end"])).astype(BF16)
            upd = jnp.dot(kw, e["vext"], preferred_element_type=F32)
            s_ref[e["r"]] = jnp.exp2(e["m_prev"] - e["c_end"]) * e["s_prev"] + upd
            m_ref[e["r"]:e["r"] + 1, :] = jnp.broadcast_to(e["b_end"] + e["c_end"], (1, LANES))

    return [functools.partial(group, g0) for g0 in range(0, len(chains), ML_GROUP)]


def _mlstm_kernel(*refs, T, nb, with_output):
    for step in _mlstm_steps(pl.program_id(0), *refs, T=T, nb=nb, with_output=with_output):
        step()


def _att_mlstm_kernel(*refs, T, nb, nblk, ns):
    j = pl.program_id(0)
    sink_ref, att_in, ml_in = refs[0], refs[1:10], refs[10:22]
    o_ref, ml_out = refs[22], refs[23:]
    att = _att_steps(lax.rem(j, ns), sink_ref, *att_in, o_ref, nb=nblk)
    ml = _mlstm_steps(j, *ml_in, *ml_out, T=T, nb=nb, with_output=True)
    for k in range(max(len(att), len(ml))):
        if k < len(att):
            att[k]()
        if k < len(ml):
            ml[k]()


def _mlstm_specs(B, T, nc):
    up = lambda j: j
    down = lambda j: nc - 1 - j

    def specs(o):
        return [pl.BlockSpec((B, T, ML_QK), lambda j: (0, o(j), 0)),
                pl.BlockSpec((B, ML_QK, T), lambda j: (0, 0, o(j))),
                pl.BlockSpec((B, T, ML_V), lambda j: (0, o(j), 0)),
                pl.BlockSpec((B, T, LANES), lambda j: (0, o(j), 0)),
                pl.BlockSpec((B, 2 * SUBLANES, T), lambda j: (0, 0, o(j)))]

    nchains = B * 2 * ML_HEADS
    s_spec = _const_spec((nchains, ML_QK_DIM, 2 * ML_V_DIM))
    m_spec = _const_spec((nchains, LANES))
    state_shape = [jax.ShapeDtypeStruct((nchains, ML_QK_DIM, 2 * ML_V_DIM), F32),
                   jax.ShapeDtypeStruct((nchains, LANES), F32)]
    h_specs = [pl.BlockSpec((B, T, ML_V), lambda j: (0, up(j), 0)),
               pl.BlockSpec((B, T, ML_V), lambda j: (0, down(j), 0))]
    return specs(up) + specs(down) + [s_spec, m_spec], h_specs, [s_spec, m_spec], state_shape


def _att_mlstm(qt, kv, kvx, vt, vtx, sink, qm, kmt, vm, gc, gr, s0, m0):
    B, _, L = qt.shape
    T = ML_CHUNK
    nc = L // T
    nblk = L // BLOCK
    ns = nblk // ATT_QB
    assert nc == B * ns
    cx = kvx.shape[1]
    kw = kv.shape[2]
    vw = vt.shape[1]
    wide = ATT_QB * BLOCK
    smp = lambda j: j // ns
    cur = lambda j: j % ns
    prev = lambda j: jnp.maximum(cur(j) * ATT_QB - 1, 0)
    nxt = lambda j: jnp.minimum((cur(j) + 1) * ATT_QB, nblk - 1)
    att_specs = [
        pl.BlockSpec(memory_space=pltpu.SMEM),
        pl.BlockSpec((1, ATT_Q, wide), lambda j: (smp(j), 0, cur(j))),
        pl.BlockSpec((1, BLOCK, kw), lambda j: (smp(j), prev(j), 0)),
        pl.BlockSpec((1, wide, kw), lambda j: (smp(j), cur(j), 0)),
        pl.BlockSpec((1, BLOCK, kw), lambda j: (smp(j), nxt(j), 0)),
        pl.BlockSpec((1, cx, kw), lambda j: (smp(j), 0, 0)),
        pl.BlockSpec((1, vw, BLOCK), lambda j: (smp(j), 0, prev(j))),
        pl.BlockSpec((1, vw, wide), lambda j: (smp(j), 0, cur(j))),
        pl.BlockSpec((1, vw, BLOCK), lambda j: (smp(j), 0, nxt(j))),
        pl.BlockSpec((1, vw, cx), lambda j: (smp(j), 0, 0)),
    ]
    ml_in, h_specs, st_specs, st_shape = _mlstm_specs(B, T, nc)
    h_shape = jax.ShapeDtypeStruct((B, L, ML_V), F32)
    att, hf, hb, _, _ = pl.pallas_call(
        functools.partial(_att_mlstm_kernel, T=T, nb=B, nblk=nblk, ns=ns),
        out_shape=(jax.ShapeDtypeStruct((B, L, ATT_Q), BF16), h_shape, h_shape, *st_shape),
        grid=(nc,),
        in_specs=att_specs + ml_in,
        out_specs=(pl.BlockSpec((1, wide, ATT_Q), lambda j: (smp(j), cur(j), 0)), *h_specs, *st_specs),
        compiler_params=_cparams(("arbitrary",)),
        name="attention_mlstm",
    )(sink, qt, kv, kv, kv, kvx, vt, vt, vt, vtx, qm, kmt, vm, gc, gr, qm, kmt, vm, gc, gr, s0, m0)
    return att, hf, hb


def _mlstm(qm, kmt, vm, gc, gr, s0, m0, with_output):
    B, L, _ = qm.shape
    T = ML_CHUNK
    nc = L // T
    ml_in, h_specs, st_specs, st_shape = _mlstm_specs(B, T, nc)
    out_shape, out_specs = st_shape, st_specs
    if with_output:
        out_shape = [jax.ShapeDtypeStruct((B, L, ML_V), F32)] * 2 + out_shape
        out_specs = h_specs + out_specs
    return pl.pallas_call(
        functools.partial(_mlstm_kernel, T=T, nb=B, with_output=with_output),
        out_shape=tuple(out_shape),
        grid=(nc,),
        in_specs=ml_in,
        out_specs=tuple(out_specs),
        compiler_params=_cparams(("arbitrary",)),
        name="mlstm_scan" if with_output else "mlstm_context_state",
    )(qm, kmt, vm, gc, gr, qm, kmt, vm, gc, gr, s0, m0)


def _odd_tables(gate_b, q_g, k_g):
    assert sum([ATT_Q, ATT_KV, ATT_KV, ML_QK, ML_QK, ML_V, ML_V]) == OG
    head = np.arange(LANES) // HEAD_DIM
    pm = jnp.asarray((head[:, None] == head[None, :]) / HEAD_DIM, F32)
    gain = jnp.concatenate([jnp.tile(q_g, ATT_HEADS), jnp.tile(k_g, ATT_KV_HEADS)])[None, :]
    gb = jnp.pad(gate_b.reshape(1, -1), ((0, 0), (0, LANES - gate_b.size)))
    return pm, gain, gb


def kernel(x, c, ctx, c_ctx, ada_w, ada_b, norm_g, even_w_in, even_conv, even_w_out, odd_w_in, odd_gate_b,
           odd_q_g, odd_k_g, odd_sink, odd_w_out, ffn_w_up, ffn_conv, ffn_w_down):
    B, L, _ = x.shape
    C = ctx.shape[1]
    depth = ada_w.shape[0]
    assert depth == 2 and L % (DFT_N1 * SUBLANES) == 0 and C % ML_CHUNK == 0

    cv = jnp.concatenate([c, c_ctx[None, :], jnp.zeros((SUBLANES - B - 1, D_MODEL), F32)], axis=0)
    mod = _modulation(cv, ada_w, ada_b)
    lat, cx = None, B

    tm = min(512, L)
    tc = _channel_dft_table()
    w_in0, w_out0 = _cast_weights((even_w_in, even_w_out), 0, (EVEN_IN, D_MODEL))
    first = ((ffn_w_up, 0, 2 * D_FF, 1), (ffn_w_down, 0, D_MODEL, 1))
    later = ((ffn_w_up, 1, 2 * D_FF, 1), (ffn_w_down, 1, D_MODEL, 2), (odd_w_in, 0, ODD_COLS, 1),
             (odd_w_out, 0, D_MODEL, 1))

    ng00, ng01 = norm_g[0, 0][None, :], norm_g[0, 1][None, :]

    yc, zr, zi, w_up0, w_down0 = _even_in(x, mod, 0, lat, ng00, w_in0, even_conv[0], tc, tm, L // DFT_N1, first)
    xl, w_up1, w_down1, w_in1, w_out1 = _mix_ffn(x, (yc, _seq_dft(zr, zi, tm)), mod, 0, lat, ng01, w_out0, w_up0,
                                                 ffn_conv, w_down0, tm, odd=False, side=later)
    yc, zr, zi = _even_in(ctx, mod, 0, cx, ng00, w_in0, even_conv[0], tc, C, None)
    xc = _mix_ffn(ctx, (yc, _dense_seq_dft(zr, zi)), mod, 0, cx, ng01, w_out0, w_up0, ffn_conv, w_down0, C, odd=False)

    pm, gain, gb = _odd_tables(odd_gate_b[0], odd_q_g[0], odd_k_g[0])
    ng10, ng11 = norm_g[1, 0][None, :], norm_g[1, 1][None, :]
    cos, sin = _rope_tables(L)
    one, nil = jnp.ones((C, LANES), F32), jnp.zeros((C, LANES), F32)
    qt, kv, vt, qm, kmt, vm, om, gc, gr = _odd_in(xl, mod, 1, lat, ng10, w_in1, pm, gain, cos, sin, gb, tm)
    _, kvx, vtx, qmx, kmtx, vmx, _, gcx, grx = _odd_in(xc, mod, 1, cx, ng10, w_in1, pm, gain, one, nil, gb, C)

    nchains = B * 2 * ML_HEADS
    s0 = jnp.zeros((nchains, ML_QK_DIM, 2 * ML_V_DIM), F32)
    m0 = jnp.zeros((nchains, LANES), F32)
    s1, m1 = _mlstm(qmx, kmtx, vmx, gcx, grx, s0, m0, with_output=False)
    att, hf, hb = _att_mlstm(qt, kv, kvx, vt, vtx, odd_sink[0], qm, kmt, vm, gc, gr, s1, m1)
    return _mix_ffn(xl, (att, hf, hb, om), mod, 1, lat, ng11, w_out1, w_up1, ffn_conv, w_down1, tm, odd=True)
```

```python
import functools

import numpy as np
import jax
import jax.numpy as jnp
from jax import lax
from jax.experimental import pallas as pl
from jax.experimental.pallas import tpu as pltpu

F32 = jnp.float32
BF16 = jnp.bfloat16

D_MODEL = 1024
GRID_W = 64
EPS = 1e-6
SC_CH = 512
FT_CH = 512
FT_GROUPS = 4
FT_GROUP_CH = FT_CH // FT_GROUPS
EVEN_IN = 3 * SC_CH + FT_CH
ATT_HEADS = 8
ATT_KV_HEADS = 2
HEAD_DIM = 64
ATT_SCALE = HEAD_DIM ** -0.5
WINDOW = 128
BLOCK = 128
ROPE_THETA = 10000.0
ML_HEADS = 4
ML_QK_DIM = 64
ML_V_DIM = 128
ATT_Q = ATT_HEADS * HEAD_DIM
ATT_KV = ATT_KV_HEADS * HEAD_DIM
ML_QK = ML_HEADS * ML_QK_DIM
ML_V = ML_HEADS * ML_V_DIM
D_FF = 2816

LANES = 128
SUBLANES = 8
VMEM_LIMIT_BYTES = 56 * 1024 * 1024

DFT_N1 = 128
FF_CHUNK = 256
N_FF_CHUNKS = D_FF // FF_CHUNK
ML_CHUNK = 128
ATT_QB = 2
NEG_BIG = -1e30
LOG2E = 1.4426950408889634

OQ = 0
OK_ = OQ + ATT_Q
OV = OK_ + ATT_KV
OQM = OV + ATT_KV
OKM = OQM + ML_QK
OVM = OKM + ML_QK
OOM = OVM + ML_V
OG = OOM + ML_V
ODD_COLS = OG + LANES
assert WINDOW == BLOCK and ATT_KV == LANES and OG % LANES == 0


def _cparams(sem):
    return pltpu.CompilerParams(dimension_semantics=sem, vmem_limit_bytes=VMEM_LIMIT_BYTES)


def _sigmoid(x):
    return 1.0 / (1.0 + jnp.exp(-x))


def _norm_mod(x, g, shift, scale):
    y = x * lax.rsqrt(jnp.mean(x * x, axis=-1, keepdims=True) + EPS)
    return y * g * (1.0 + scale) + shift


def _mod_vec(mod_ref, k, mrow):
    r = pl.program_id(0) if mrow is None else mrow
    return mod_ref[k, pl.ds(r, 1), :]


def _mod_spec(layer):
    return pl.BlockSpec((None, 6, SUBLANES, D_MODEL), lambda *_: (layer, 0, 0, 0))


def _halo_rows(x, xn, xp, shift, scale, ng_ref):
    g = ng_ref[...]
    parts = [_norm_mod(r, g, shift, scale) for r in (x, xn, xp)]
    return jnp.concatenate(parts, axis=0).astype(BF16)


def _halo_valid(tm, i, nt):
    row = lax.broadcasted_iota(jnp.int32, (tm + 2 * SUBLANES, 1), 0)
    return ((row < tm) | ((row < tm + SUBLANES) & (i < nt - 1)) | ((row >= tm + SUBLANES) & (i > 0)))


def _conv3(v, cw, tm):
    n = v.shape[0]
    vp = pltpu.roll(v, 1, 0)[:tm]
    vn = pltpu.roll(v, n - 1, 0)[:tm]
    return vp * cw[0:1] + v[:tm] * cw[1:2] + vn * cw[2:3]


def _halo_specs(tm, L):
    hb = tm // SUBLANES
    last = L // SUBLANES - 1
    return [
        pl.BlockSpec((1, tm, D_MODEL), lambda b, i: (b, i, 0)),
        pl.BlockSpec((1, SUBLANES, D_MODEL), lambda b, i: (b, jnp.minimum((i + 1) * hb, last), 0)),
        pl.BlockSpec((1, SUBLANES, D_MODEL), lambda b, i: (b, jnp.maximum(i * hb - 1, 0), 0)),
    ]


def _const_spec(shape):
    nd = len(shape)
    return pl.BlockSpec(shape, lambda *_: (0,) * nd)


def _resident_spec(shape):
    nd = len(shape)
    return pl.BlockSpec(shape, lambda *_: (0,) * nd, pipeline_mode=pl.Buffered(1))


def _weight_spec(w, shape, layer):
    return _resident_spec(shape) if w.ndim == len(shape) else _layer_spec(shape, layer)


def _layer_spec(shape, layer):
    nd = len(shape)
    return pl.BlockSpec((None,) + tuple(shape), lambda *_: (layer,) + (0,) * nd, pipeline_mode=pl.Buffered(1))


def _cast_refs(srcs, dsts):
    for src, dst in zip(srcs, dsts):
        w = src[...].astype(BF16)
        pad = dst.shape[-1] - src.shape[-1]
        if pad:
            w = jnp.concatenate([w, jnp.zeros(w.shape[:-1] + (pad,), BF16)], axis=-1)
        dst[...] = w


def _layer_rows_spec(w, layer, rb, imap):
    if w.ndim == 2:
        return pl.BlockSpec((rb, w.shape[1]), lambda *a: (imap(*a), 0))
    return pl.BlockSpec((None, rb, w.shape[2]), lambda *a: (layer, imap(*a), 0))


def _side_cast_specs(side, steps, step_of):
    ins, outs, shapes = [], [], []
    for w, wl, wd, per in side:
        r = w.shape[-2]
        rb = r * per // steps
        blk = lambda *a, per=per: step_of(*a) // per
        ins.append(_layer_rows_spec(w, wl, rb, blk))
        outs.append(pl.BlockSpec((rb, wd), lambda *a, blk=blk: (blk(*a), 0)))
        shapes.append(jax.ShapeDtypeStruct((r, wd), BF16))
    return ins, outs, shapes


def _split_dot(x, p):
    hi = x.astype(BF16).astype(F32)
    y = jnp.dot(jnp.concatenate([hi, x - hi], axis=0).astype(BF16), p, preferred_element_type=F32)
    return y[:x.shape[0]] + y[x.shape[0]:]


def _mod_kernel(cv_ref, w_ref, b_ref, *rest):
    nside = (len(rest) - 1) // 2
    o_ref = rest[nside]
    _cast_refs(rest[:nside], rest[nside + 1:])
    cv = cv_ref[...]
    o_ref[0, 0] = _split_dot(cv * _sigmoid(cv), w_ref[0].astype(BF16)) + b_ref[0, 0]


def _modulation(cv, ada_w, ada_b, side=()):
    depth, _, n = ada_w.shape
    nv = n // D_MODEL
    side_in, side_out, side_shape = _side_cast_specs(side, depth * nv, lambda l, j: l * nv + j)
    return pl.pallas_call(
        _mod_kernel,
        out_shape=(jax.ShapeDtypeStruct((depth, nv, SUBLANES, D_MODEL), F32), *side_shape),
        grid=(depth, nv),
        in_specs=[
            pl.BlockSpec((SUBLANES, D_MODEL), lambda l, j: (0, 0)),
            pl.BlockSpec((1, D_MODEL, D_MODEL), lambda l, j: (l, 0, j)),
            pl.BlockSpec((1, 1, 1, D_MODEL), lambda l, j: (l, j, 0, 0)),
        ] + side_in,
        out_specs=(pl.BlockSpec((1, 1, SUBLANES, D_MODEL), lambda l, j: (l, j, 0, 0)), *side_out),
        compiler_params=_cparams(("arbitrary", "arbitrary")),
        name="modulation",
    )(cv, ada_w, ada_b.reshape(depth, nv, 1, D_MODEL), *[e[0] for e in side])


def _even_in_kernel(x_ref, xn_ref, xp_ref, mod_ref, ng_ref, w_ref, cw_ref, tc_ref, *rest, tm, nt, n2, mrow, nsub):
    nside = (len(rest) - 3) // 2
    yc_ref, zr_ref, zi_ref = rest[nside:nside + 3]
    _cast_refs(rest[:nside], rest[nside + 3:])
    i = pl.program_id(1)
    shift, scale = _mod_vec(mod_ref, 0, mrow), _mod_vec(mod_ref, 1, mrow)
    tc = tc_ref[...].astype(BF16)
    for s in range(nsub):
        lo = s * tm
        x = x_ref[0, lo:lo + tm]
        xn = xn_ref[0] if s == nsub - 1 else x_ref[0, lo + tm:lo + tm + SUBLANES]
        xp = xp_ref[0] if s == 0 else x_ref[0, lo - SUBLANES:lo]
        hh = _halo_rows(x, xn, xp, shift, scale, ng_ref)
        u = jnp.dot(hh, w_ref[...], preferred_element_type=F32)
        v = u[:, SC_CH:2 * SC_CH] * u[:, 2 * SC_CH:3 * SC_CH]
        v = jnp.where(_halo_valid(tm, i * nsub + s, nt), v, 0.0)
        yc = u[:tm, :SC_CH] * _conv3(v, cw_ref[...], tm)
        yc_ref[0, lo:lo + tm] = yc.astype(BF16)
        uf = u[:tm, 3 * SC_CH:].astype(BF16)
        for g in range(FT_GROUPS):
            sl = slice(g * FT_GROUP_CH, (g + 1) * FT_GROUP_CH)
            ab = jnp.dot(uf[:, sl], tc, preferred_element_type=F32)
            if n2 is None:
                zr_ref[g, 0, lo:lo + tm] = ab[:, :FT_GROUP_CH]
                zi_ref[g, 0, lo:lo + tm] = ab[:, FT_GROUP_CH:]
            else:
                for a in range(tm // n2):
                    dst = pl.ds(lo + a, n2, stride=tm // n2)
                    zr_ref[g, 0, dst, :] = ab[n2 * a:n2 * (a + 1), :FT_GROUP_CH]
                    zi_ref[g, 0, dst, :] = ab[n2 * a:n2 * (a + 1), FT_GROUP_CH:]


def _even_in(x, mod, layer, mrow, ng, w_in, cw, tc, tm, n2, side=()):
    B, L, _ = x.shape
    nt = L // tm
    nsub = 2 if nt % 2 == 0 else 1
    bm = nsub * tm
    out = jax.ShapeDtypeStruct((B, L, FT_CH), BF16)
    zout = jax.ShapeDtypeStruct((FT_GROUPS, B, L, FT_GROUP_CH), F32)
    ospec = pl.BlockSpec((1, bm, FT_CH), lambda b, i: (b, i, 0))
    zspec = pl.BlockSpec((FT_GROUPS, 1, bm, FT_GROUP_CH), lambda b, i: (0, b, i, 0))
    ns = nt // nsub
    side_in, side_out, side_shape = _side_cast_specs(side, B * ns, lambda b, i: b * ns + i)
    return pl.pallas_call(
        functools.partial(_even_in_kernel, tm=tm, nt=nt, n2=n2, mrow=mrow, nsub=nsub),
        out_shape=(out, zout, zout, *side_shape),
        grid=(B, ns),
        in_specs=_halo_specs(bm, L) + [
            _mod_spec(layer),
            _const_spec((1, D_MODEL)),
            _const_spec((D_MODEL, EVEN_IN)),
            _const_spec((3, SC_CH)),
            _const_spec((FT_GROUP_CH, 2 * FT_GROUP_CH)),
        ] + side_in,
        out_specs=(ospec, zspec, zspec, *side_out),
        compiler_params=_cparams(("parallel", "arbitrary")),
        name="even_in",
    )(x, x, x, mod, ng, w_in, cw, tc, *[e[0] for e in side])


def _seq_dft_kernel(zr_ref, zi_ref, m_ref, g_ref, y_ref, o_scr, *, n2, tm):
    m1 = m_ref[...].astype(BF16)
    chunk = tm // n2
    ntile = DFT_N1 // chunk

    def rows(ref, j):
        return [ref[0, 0, t * tm + j * chunk:t * tm + (j + 1) * chunk, :] for t in range(ntile)]

    for j in range(n2):
        z = jnp.concatenate(rows(zr_ref, j) + rows(zi_ref, j), axis=0)
        o_scr[2 * DFT_N1 * j:2 * DFT_N1 * (j + 1), :] = jnp.dot(m1, z.astype(BF16), preferred_element_type=F32)
    for a in range(DFT_N1 // SUBLANES):
        def gather(base):
            x = jnp.concatenate([o_scr[2 * DFT_N1 * j + base + SUBLANES * a:2 * DFT_N1 * j + base + SUBLANES * (a + 1), :]
                                 for j in range(n2)], axis=0)
            return jnp.swapaxes(x.reshape(n2, SUBLANES, LANES), 0, 1)
        xr, xi = gather(0), gather(DFT_N1)
        ys = []
        for r in range(SUBLANES):
            o = jnp.concatenate([xr[r], xi[r]], axis=0).astype(BF16)
            ys.append(jnp.dot(g_ref[SUBLANES * a + r].astype(BF16), o, preferred_element_type=F32))
        y8 = jnp.swapaxes(jnp.stack(ys, axis=0), 0, 1)
        for k2 in range(n2):
            y_ref[0, DFT_N1 * k2 + SUBLANES * a:DFT_N1 * k2 + SUBLANES * (a + 1), :] = y8[k2]


def _dft_tables(L):
    n2 = L // DFT_N1
    k = np.arange(DFT_N1)
    a = 2.0 * np.pi * ((k[:, None] * k[None, :]) % DFT_N1) / DFT_N1
    er, ei = np.cos(a) / np.sqrt(DFT_N1), -np.sin(a) / np.sqrt(DFT_N1)
    m1 = np.block([[er, -ei], [ei, er]])
    k1 = np.arange(DFT_N1)[:, None, None]
    k2 = np.arange(n2)[None, :, None]
    nn = np.arange(n2)[None, None, :]
    th = 2.0 * np.pi * ((nn * (k1 + DFT_N1 * k2)) % L) / L
    g = np.concatenate([np.cos(th), np.sin(th)], axis=-1) / np.sqrt(n2)
    return jnp.asarray(m1, F32), jnp.asarray(g, F32)


def _channel_dft_table():
    k = np.arange(FT_GROUP_CH)
    a = 2.0 * np.pi * ((k[:, None] * k[None, :]) % FT_GROUP_CH) / FT_GROUP_CH
    t = np.concatenate([np.cos(a), -np.sin(a)], axis=1) / np.sqrt(FT_GROUP_CH)
    return jnp.asarray(t, F32)


def _seq_dft(zr, zi, tm):
    G, B, L, C = zr.shape
    n2 = L // DFT_N1
    m1, g = _dft_tables(L)
    zspec = pl.BlockSpec((1, 1, L, C), lambda b, j: (j, b, 0, 0))
    return pl.pallas_call(
        functools.partial(_seq_dft_kernel, n2=n2, tm=tm),
        out_shape=jax.ShapeDtypeStruct((B, L, G * C), F32),
        grid=(B, G),
        in_specs=[zspec, zspec, _const_spec((2 * DFT_N1, 2 * DFT_N1)), _const_spec((DFT_N1, n2, 2 * n2))],
        out_specs=pl.BlockSpec((1, L, C), lambda b, j: (b, 0, j)),
        scratch_shapes=[pltpu.VMEM((2 * DFT_N1 * n2, LANES), F32)],
        compiler_params=_cparams(("parallel", "arbitrary")),
        name="seq_dft",
    )(zr, zi, m1, g)


def _dense_dft_kernel(zr_ref, zi_ref, t_ref, y_ref):
    z = jnp.concatenate([zr_ref[0, 0], zi_ref[0, 0]], axis=0).astype(BF16)
    y_ref[0] = jnp.dot(t_ref[...].astype(BF16), z, preferred_element_type=F32)


def _dense_seq_dft(zr, zi):
    G, B, L, C = zr.shape
    k = np.arange(L)
    a = 2.0 * np.pi * ((k[:, None] * k[None, :]) % L) / L
    t = jnp.asarray(np.concatenate([np.cos(a), np.sin(a)], axis=1) / np.sqrt(L), F32)
    zspec = pl.BlockSpec((1, 1, L, C), lambda b, j: (j, b, 0, 0))
    return pl.pallas_call(
        _dense_dft_kernel,
        out_shape=jax.ShapeDtypeStruct((B, L, G * C), F32),
        grid=(B, G),
        in_specs=[zspec, zspec, _const_spec((L, 2 * L))],
        out_specs=pl.BlockSpec((1, L, C), lambda b, j: (b, 0, j)),
        compiler_params=_cparams(("arbitrary", "arbitrary")),
        name="dense_seq_dft",
    )(zr, zi, t)


HALO = 16


def _wide_halo_specs(tm, L, width):
    hb = tm // HALO
    last = L // HALO - 1
    return [
        pl.BlockSpec((1, tm, width), lambda b, i: (b, i, 0)),
        pl.BlockSpec((1, HALO, width), lambda b, i: (b, jnp.minimum((i + 1) * hb, last), 0)),
        pl.BlockSpec((1, HALO, width), lambda b, i: (b, jnp.maximum(i * hb - 1, 0), 0)),
    ]


def _circ(t_ref, n_ref, p_ref):
    return jnp.concatenate([t_ref[0], n_ref[0], p_ref[0]], axis=0)


def _mix_ffn_kernel(*refs, tm, nt, odd, mrow, nside):
    n_in = 15 if odd else 9
    x3, rest = refs[:3], refs[3:n_in]
    mod_ref, ng_ref, wo_ref, wu_ref, cw_ref, wd_ref = refs[n_in:n_in + 6]
    side_src = refs[n_in + 6:n_in + 6 + nside]
    o_ref = refs[n_in + 6 + nside]
    side_dst = refs[n_in + 7 + nside:n_in + 7 + 2 * nside]
    hh_scr, act_scr = refs[n_in + 7 + 2 * nside:]
    _cast_refs(side_src, side_dst)
    i = pl.program_id(1)
    if odd:
        att, hf, hb, om = (_circ(*rest[k:k + 3]) for k in range(0, 12, 3))
        lhs = jnp.concatenate([att, ((hf + hb) * _sigmoid(om.astype(F32))).astype(BF16)], axis=-1)
    else:
        yc, yf = _circ(*rest[0:3]), _circ(*rest[3:6])
        lhs = jnp.concatenate([yc, yf.astype(BF16)], axis=-1)
    mv = [_mod_vec(mod_ref, k, mrow) for k in range(6)]
    x1 = _circ(*x3) + mv[2] * jnp.dot(lhs, wo_ref[...], preferred_element_type=F32)
    hh_scr[...] = _norm_mod(x1, ng_ref[...], mv[3], mv[4]).astype(BF16)
    row = lax.broadcasted_iota(jnp.int32, (tm + 2 * HALO, 1), 0)
    valid = (row < tm) | ((row < tm + HALO) & (i < nt - 1)) | ((row >= tm + HALO) & (i > 0))
    for c in range(N_FF_CHUNKS):
        lo = c * FF_CHUNK
        g = jnp.dot(hh_scr[...], wu_ref[:, lo:lo + FF_CHUNK], preferred_element_type=F32)
        g = jnp.where(valid, g, 0.0)
        cv = _conv3(g, cw_ref[:, lo:lo + FF_CHUNK], tm)
        val = jnp.dot(hh_scr[:tm, :], wu_ref[:, D_FF + lo:D_FF + lo + FF_CHUNK], preferred_element_type=F32)
        act_scr[:, lo:lo + FF_CHUNK] = (cv * _sigmoid(cv) * val).astype(BF16)
    y = jnp.dot(act_scr[...], wd_ref[...], preferred_element_type=F32)
    o_ref[0] = x1[:tm] + mv[5] * y


def _mix_ffn(x, mixed, mod, layer, mrow, ng, w_out, w_up, cw, w_down, tm, odd, side=()):
    B, L, _ = x.shape
    nt = L // tm
    specs = _wide_halo_specs(tm, L, D_MODEL)
    args = [x, x, x]
    for a in mixed:
        specs += _wide_halo_specs(tm, L, a.shape[-1])
        args += [a, a, a]
    side_in, side_out, side_shape = _side_cast_specs(side, B * nt, lambda b, i: b * nt + i)
    out = pl.pallas_call(
        functools.partial(_mix_ffn_kernel, tm=tm, nt=nt, odd=odd, mrow=mrow, nside=len(side)),
        out_shape=(jax.ShapeDtypeStruct(x.shape, F32), *side_shape),
        grid=(B, nt),
        in_specs=specs + [
            _mod_spec(layer),
            _const_spec((1, D_MODEL)),
            _resident_spec((D_MODEL, D_MODEL)),
            _weight_spec(w_up, (D_MODEL, 2 * D_FF), layer),
            _layer_spec((3, D_FF), layer),
            _weight_spec(w_down, (D_FF, D_MODEL), layer),
        ] + side_in,
        out_specs=(pl.BlockSpec((1, tm, D_MODEL), lambda b, i: (b, i, 0)), *side_out),
        scratch_shapes=[pltpu.VMEM((tm + 2 * HALO, D_MODEL), BF16),
                        pltpu.VMEM((tm, D_FF), BF16)],
        compiler_params=_cparams(("parallel", "arbitrary")),
        name="odd_mix_ffn" if odd else "even_mix_ffn",
    )(*args, mod, ng, w_out, w_up, cw, w_down, *[e[0] for e in side])
    return out if side else out[0]


def _split3(x):
    parts = []
    r = x
    for _ in range(3):
        p = r.astype(BF16)
        parts.append(p)
        r = r - p.astype(F32)
    return parts


def _odd_in_kernel(x_ref, mod_ref, ng_ref, w_ref, pm_ref, gain_ref, cos_ref, sin_ref, gb_ref, tl_ref, tu_ref,
                   qt_ref, kv_ref, vt_ref, qm_ref, kmt_ref, vm_ref, om_ref, gc_ref, gr_ref, *, mrow, tm, nsub):
    for s in range(nsub):
        rs = slice(s * tm, (s + 1) * tm)
        h = _norm_mod(x_ref[0, rs], ng_ref[...], _mod_vec(mod_ref, 0, mrow), _mod_vec(mod_ref, 1, mrow))
        u = jnp.dot(h.astype(BF16), w_ref[...], preferred_element_type=F32)

        uqk = u[:, OQ:OV]
        sq = (uqk * uqk).astype(BF16)
        pm = pm_ref[...].astype(BF16)
        ms = jnp.concatenate([jnp.dot(sq[:, t * LANES:(t + 1) * LANES], pm, preferred_element_type=F32)
                              for t in range((OV - OQ) // LANES)], axis=1)
        rn = uqk * lax.rsqrt(ms + EPS) * gain_ref[...]
        lane = lax.broadcasted_iota(jnp.int32, (1, LANES), 1)
        first = (lane % 32) < 16
        cos = cos_ref[rs, :]
        sin = sin_ref[rs, :]
        roped = []
        for t in range((OV - OQ) // LANES):
            xt = rn[:, t * LANES:(t + 1) * LANES]
            sw = jnp.where(first, pltpu.roll(xt, LANES - 16, 1), pltpu.roll(xt, 16, 1))
            roped.append(xt * cos + sw * sin)
        for t in range(ATT_Q // LANES):
            qt_ref[0, t * LANES:(t + 1) * LANES, rs] = (roped[t] * (ATT_SCALE * LOG2E)).T.astype(BF16)
        k = roped[ATT_Q // LANES]
        v = u[:, OV:OQM]
        half = LANES // 2
        kv_ref[0, rs, 0:LANES] = k.astype(BF16)
        kv_ref[0, rs, LANES:2 * LANES] = pltpu.roll(k, half, 1).astype(BF16)
        vt_ref[0, :, rs] = v.T.astype(BF16)

        qm_ref[0, rs] = u[:, OQM:OKM].astype(BF16)
        for p in range(ML_QK // LANES):
            km = u[:, OKM + p * LANES:OKM + (p + 1) * LANES] * (ML_QK_DIM ** -0.5)
            kmt_ref[0, p * LANES:(p + 1) * LANES, rs] = km.T.astype(BF16)
        vm_ref[0, rs] = u[:, OVM:OOM].astype(BF16)
        om_ref[0, rs] = u[:, OOM:OG].astype(BF16)

        gt = (u[:, OG:ODD_COLS] + gb_ref[...]).T[:2 * SUBLANES, :]
        row = lax.broadcasted_iota(jnp.int32, (2 * SUBLANES, 1), 0)
        logsig = jnp.minimum(gt, 0.0) - jnp.log(1.0 + jnp.exp(-jnp.abs(gt)))
        parts = jnp.concatenate(_split3(logsig), axis=0)
        tl, tu = tl_ref[...].astype(BF16), tu_ref[...].astype(BF16)

        def chunk_cumsum(tri):
            c = jnp.concatenate([jnp.dot(parts[:, k:k + ML_CHUNK], tri, preferred_element_type=F32)
                                 for k in range(0, tm, ML_CHUNK)], axis=1)
            n = 2 * SUBLANES
            return c[0:n] + c[n:2 * n] + c[2 * n:3 * n]

        cum_f, cum_b = chunk_cumsum(tu), chunk_cumsum(tl)
        sel = row % 8
        gr = jnp.where(sel < 4, gt, jnp.where(row < SUBLANES, cum_f, cum_b)) * LOG2E
        gr_ref[0, :, rs] = gr
        gc_ref[0, rs] = jnp.concatenate([gr, jnp.zeros((LANES - 2 * SUBLANES, tm), F32)], axis=0).T


def _chunk_tri():
    i = np.arange(ML_CHUNK)
    return jnp.asarray(i[None, :] <= i[:, None], F32), jnp.asarray(i[None, :] >= i[:, None], F32)


def _odd_in(x, mod, layer, mrow, ng, w, pm, gain, cos, sin, gb, tm):
    B, L, _ = x.shape
    nsub = 2 if L % (2 * tm) == 0 else 1
    bm = nsub * tm

    def rows(c, dt=BF16):
        return jax.ShapeDtypeStruct((B, L, c), dt), pl.BlockSpec((1, bm, c), lambda b, i: (b, i, 0))

    def cols(c, dt=BF16):
        return jax.ShapeDtypeStruct((B, c, L), dt), pl.BlockSpec((1, c, bm), lambda b, i: (b, 0, i))

    outs = [cols(ATT_Q), rows(2 * LANES), cols(LANES), rows(ML_QK), cols(ML_QK), rows(ML_V), rows(ML_V),
            rows(LANES, F32), cols(2 * SUBLANES, F32)]
    nqk = OV - OQ
    tl, tu = _chunk_tri()
    return pl.pallas_call(
        functools.partial(_odd_in_kernel, mrow=mrow, tm=tm, nsub=nsub),
        out_shape=tuple(o[0] for o in outs),
        grid=(B, L // bm),
        in_specs=[
            pl.BlockSpec((1, bm, D_MODEL), lambda b, i: (b, i, 0)),
            _mod_spec(layer),
            _const_spec((1, D_MODEL)),
            _const_spec((D_MODEL, ODD_COLS)),
            _const_spec((LANES, LANES)),
            _const_spec((1, nqk)),
            pl.BlockSpec((bm, LANES), lambda b, i: (i, 0)),
            pl.BlockSpec((bm, LANES), lambda b, i: (i, 0)),
            _const_spec((1, LANES)),
            _const_spec((ML_CHUNK, ML_CHUNK)),
            _const_spec((ML_CHUNK, ML_CHUNK)),
        ],
        out_specs=tuple(o[1] for o in outs),
        compiler_params=_cparams(("parallel", "arbitrary")),
        name="odd_in",
    )(x, mod, ng, w, pm, gain, cos, sin, gb, tl, tu)


def _rope_tables(L):
    rows = L // GRID_W
    pos = np.stack([np.repeat(np.arange(rows), GRID_W), np.tile(np.arange(GRID_W), rows)]).astype(np.float64)
    axis_dim = HEAD_DIM // 2
    inv_freq = ROPE_THETA ** (-np.arange(0, axis_dim, 2, dtype=np.float64) / axis_dim)
    ang = pos[:, :, None] * inv_freq
    c, sn = np.cos(ang), np.sin(ang)
    cos = np.concatenate([c[0], c[0], c[1], c[1]], axis=-1)
    sin = np.concatenate([-sn[0], sn[0], -sn[1], sn[1]], axis=-1)
    return jnp.asarray(np.tile(cos, (1, 2)), F32), jnp.asarray(np.tile(sin, (1, 2)), F32)


def _att_steps(n, sink_ref, qt_ref, kvp_ref, kvc_ref, kvn_ref, kvx_ref, vtp_ref, vtc_ref, vtn_ref, vtx_ref,
               o_ref, *, nb):
    half = LANES // 2
    lo = lax.broadcasted_iota(jnp.int32, (1, LANES), 1) < half
    zero = jnp.zeros((), BF16)
    cx = kvx_ref.shape[1]
    klocal = ([kvp_ref[0]] + [kvc_ref[0, i * BLOCK:(i + 1) * BLOCK] for i in range(ATT_QB)] + [kvn_ref[0]])
    vlocal = ([vtp_ref[0]] + [vtc_ref[0, :, i * BLOCK:(i + 1) * BLOCK] for i in range(ATT_QB)] + [vtn_ref[0]])
    kctx = [kvx_ref[0, i:i + BLOCK] for i in range(0, cx, BLOCK)]
    vctx = [vtx_ref[0, :, i:i + BLOCK] for i in range(0, cx, BLOCK)]
    ones = jnp.ones((half, BLOCK), BF16)

    kj = lax.broadcasted_iota(jnp.int32, (BLOCK, BLOCK), 0)
    qi = lax.broadcasted_iota(jnp.int32, (BLOCK, BLOCK), 1)

    def twice(x):
        return jnp.concatenate([x, x], axis=1)

    left = lax.broadcasted_iota(jnp.int32, (1, 2 * BLOCK), 1) < BLOCK
    group = ATT_HEADS // ATT_KV_HEADS

    def block(qb):
        blk_id = n * ATT_QB + qb
        kblocks = klocal[qb:qb + 3] + kctx
        vblocks = vlocal[qb:qb + 3] + vctx
        bias = [twice(jnp.where((kj >= qi) & (blk_id > 0), 0.0, NEG_BIG)).astype(BF16), None,
                twice(jnp.where((kj <= qi) & (blk_id < nb - 1), 0.0, NEG_BIG)).astype(BF16)] + [None] * len(kctx)
        for kvh in range(ATT_KV_HEADS):
            t0 = kvh * group // 2
            cols = slice(qb * BLOCK, (qb + 1) * BLOCK)
            qt2 = jnp.concatenate([qt_ref[0, t0 * LANES:(t0 + 1) * LANES, cols],
                                   qt_ref[0, (t0 + 1) * LANES:(t0 + 2) * LANES, cols]], axis=1)
            outs = []
            for par in range(2):
                sk = jnp.where(left, sink_ref[2 * t0 + par], sink_ref[2 * t0 + 2 + par]) * LOG2E
                m = sk.astype(BF16).astype(F32)
                acc = jnp.zeros((LANES, 2 * BLOCK), F32)
                for blk in range(len(kblocks)):
                    k, ks = kblocks[blk][:, :LANES], kblocks[blk][:, LANES:]
                    kh = ((jnp.where(lo, k, zero), jnp.where(lo, zero, ks)),
                          (jnp.where(lo, ks, zero), jnp.where(lo, zero, k)))[kvh][par]
                    s = jnp.dot(kh, qt2, preferred_element_type=F32)
                    s = s.astype(BF16)
                    if bias[blk] is not None:
                        s = s + bias[blk]
                    m_new = jnp.maximum(m, jnp.max(s, axis=0, keepdims=True).astype(F32))
                    p = jnp.exp2(s - m_new.astype(BF16))
                    vh = jnp.concatenate([vblocks[blk][kvh * half:(kvh + 1) * half], ones], axis=0)
                    acc = acc * jnp.exp2(m - m_new) + jnp.dot(vh, p, preferred_element_type=F32)
                    m = m_new
                l = acc[half:half + 1, :] + jnp.exp2(sk - m)
                outs.append(acc[:half] * (1.0 / l))
            ot = jnp.concatenate(outs, axis=0)
            rows = slice(qb * BLOCK, (qb + 1) * BLOCK)
            o_ref[0, rows, t0 * LANES:(t0 + 1) * LANES] = ot[:, :BLOCK].T.astype(BF16)
            o_ref[0, rows, (t0 + 1) * LANES:(t0 + 2) * LANES] = ot[:, BLOCK:].T.astype(BF16)

    return [functools.partial(block, qb) for qb in range(ATT_QB)]


ML_GROUP = 8


def _mlstm_steps(j, qf_ref, ktf_ref, vf_ref, gcf_ref, grf_ref, qb_ref, ktb_ref, vb_ref, gcb_ref, grb_ref,
                 s0_ref, m0_ref, *rest, T, nb, with_output):
    if with_output:
        hf_ref, hb_ref, s_ref, m_ref = rest
    else:
        s_ref, m_ref = rest

    @pl.when(j == 0)
    def _():
        s_ref[...] = s0_ref[...]
        m_ref[...] = m0_ref[...]

    ti = lax.broadcasted_iota(jnp.int32, (T, T), 0)
    si = lax.broadcasted_iota(jnp.int32, (T, T), 1)
    masks = (si <= ti, si >= ti)
    top = lax.broadcasted_iota(jnp.int32, (LANES, 1), 0) < (LANES // 2)
    zero = jnp.zeros((), BF16)
    ones = jnp.ones((T, ML_V_DIM), BF16)
    zpad = jnp.zeros((ML_QK_DIM, 2 * ML_V_DIM), BF16)
    fwd = (qf_ref, ktf_ref, vf_ref, gcf_ref, grf_ref)
    bwd = (qb_ref, ktb_ref, vb_ref, gcb_ref, grb_ref)
    chains = [(b, d, h) for b in range(nb) for d in range(2) for h in range(ML_HEADS)]

    def group(g0):
        grp = chains[g0:g0 + ML_GROUP]
        st = {}
        for (b, d, h) in grp:
            q_ref, kt_ref, v_ref, gc_ref, gr_ref = fwd if d == 0 else bwd
            r = (b * 2 + d) * ML_HEADS + h
            li, lf = 8 * d + h, 8 * d + 4 + h
            gr = gr_ref[b]
            u_row = gr[li:li + 1, :] - gr[lf:lf + 1, :]
            m_prev = m_ref[r:r + 1, 0:1]
            e = dict(r=r, u_row=u_row, m_prev=m_prev,
                     b_end=gr[lf:lf + 1, (T - 1 if d == 0 else 0):(T if d == 0 else 1)],
                     c_end=jnp.maximum(jnp.max(u_row, axis=1, keepdims=True), m_prev))
            if with_output:
                e["mu"] = jnp.where(masks[d], u_row, NEG_BIG)
                c_col = jnp.maximum(jnp.max(e["mu"], axis=1, keepdims=True), m_prev)
                e["cb"] = jnp.broadcast_to(c_col, (T, LANES))
                e["bb"] = jnp.broadcast_to(gc_ref[b, :, lf:lf + 1], (T, LANES))
            st[(b, d, h)] = e
        for (b, d, h) in grp:
            q_ref, kt_ref, v_ref, gc_ref, gr_ref = fwd if d == 0 else bwd
            e = st[(b, d, h)]
            pair = h // 2
            ktp = kt_ref[b, pair * LANES:(pair + 1) * LANES, :]
            e["kth"] = ktp[(h % 2) * ML_QK_DIM:(h % 2 + 1) * ML_QK_DIM, :]
            e["vext"] = jnp.concatenate([v_ref[b, :, h * ML_V_DIM:(h + 1) * ML_V_DIM], ones], axis=1)
            if with_output:
                qp = q_ref[b, :, pair * LANES:(pair + 1) * LANES]
                kpad = jnp.where(top, ktp, zero) if h % 2 == 0 else jnp.where(top, zero, ktp)
                qk = jnp.dot(qp, kpad, preferred_element_type=F32) * jnp.exp2(e["mu"] - e["cb"])
                qs = qp.astype(F32) * jnp.exp2(e["m_prev"] - e["cb"])
                e["lhs"] = jnp.concatenate([qk.astype(BF16), qs.astype(BF16)], axis=1)
        for (b, d, h) in grp:
            e = st[(b, d, h)]
            e["s_prev"] = s_ref[e["r"]]
            if with_output:
                sb = e["s_prev"].astype(BF16)
                rhs = jnp.concatenate([e["vext"]] + ([sb, zpad] if h % 2 == 0 else [zpad, sb]), axis=0)
                tot = jnp.dot(e["lhs"], rhs, preferred_element_type=F32)
                floor = jnp.exp2(-(e["bb"] + e["cb"]))
                hout = tot[:, :ML_V_DIM] / jnp.maximum(jnp.abs(tot[:, ML_V_DIM:]), floor)
                o_ref = hf_ref if d == 0 else hb_ref
                o_ref[b, :, h * ML_V_DIM:(h + 1) * ML_V_DIM] = hout
        for (b, d, h) in grp:
            e = st[(b, d, h)]
            kw = (e["kth"].astype(F32) * jnp.exp2(e["u_row"] - e["c_end"])).astype(BF16)
            upd = jnp.dot(kw, e["vext"], preferred_element_type=F32)
            s_ref[e["r"]] = jnp.exp2(e["m_prev"] - e["c_end"]) * e["s_prev"] + upd
            m_ref[e["r"]:e["r"] + 1, :] = jnp.broadcast_to(e["b_end"] + e["c_end"], (1, LANES))

    return [functools.partial(group, g0) for g0 in range(0, len(chains), ML_GROUP)]


def _mlstm_kernel(*refs, T, nb, with_output):
    for step in _mlstm_steps(pl.program_id(0), *refs, T=T, nb=nb, with_output=with_output):
        step()


def _att_mlstm_kernel(*refs, T, nb, nblk, ns):
    j = pl.program_id(0)
    sink_ref, att_in, ml_in = refs[0], refs[1:10], refs[10:22]
    o_ref, ml_out = refs[22], refs[23:]
    att = _att_steps(lax.rem(j, ns), sink_ref, *att_in, o_ref, nb=nblk)
    ml = _mlstm_steps(j, *ml_in, *ml_out, T=T, nb=nb, with_output=True)
    for k in range(max(len(att), len(ml))):
        if k < len(att):
            att[k]()
        if k < len(ml):
            ml[k]()


def _mlstm_specs(B, T, nc):
    up = lambda j: j
    down = lambda j: nc - 1 - j

    def specs(o):
        return [pl.BlockSpec((B, T, ML_QK), lambda j: (0, o(j), 0)),
                pl.BlockSpec((B, ML_QK, T), lambda j: (0, 0, o(j))),
                pl.BlockSpec((B, T, ML_V), lambda j: (0, o(j), 0)),
                pl.BlockSpec((B, T, LANES), lambda j: (0, o(j), 0)),
                pl.BlockSpec((B, 2 * SUBLANES, T), lambda j: (0, 0, o(j)))]

    nchains = B * 2 * ML_HEADS
    s_spec = _const_spec((nchains, ML_QK_DIM, 2 * ML_V_DIM))
    m_spec = _const_spec((nchains, LANES))
    state_shape = [jax.ShapeDtypeStruct((nchains, ML_QK_DIM, 2 * ML_V_DIM), F32),
                   jax.ShapeDtypeStruct((nchains, LANES), F32)]
    h_specs = [pl.BlockSpec((B, T, ML_V), lambda j: (0, up(j), 0)),
               pl.BlockSpec((B, T, ML_V), lambda j: (0, down(j), 0))]
    return specs(up) + specs(down) + [s_spec, m_spec], h_specs, [s_spec, m_spec], state_shape


def _att_mlstm(qt, kv, kvx, vt, vtx, sink, qm, kmt, vm, gc, gr, s0, m0):
    B, _, L = qt.shape
    T = ML_CHUNK
    nc = L // T
    nblk = L // BLOCK
    ns = nblk // ATT_QB
    assert nc == B * ns
    cx = kvx.shape[1]
    kw = kv.shape[2]
    vw = vt.shape[1]
    wide = ATT_QB * BLOCK
    smp = lambda j: j // ns
    cur = lambda j: j % ns
    prev = lambda j: jnp.maximum(cur(j) * ATT_QB - 1, 0)
    nxt = lambda j: jnp.minimum((cur(j) + 1) * ATT_QB, nblk - 1)
    att_specs = [
        pl.BlockSpec(memory_space=pltpu.SMEM),
        pl.BlockSpec((1, ATT_Q, wide), lambda j: (smp(j), 0, cur(j))),
        pl.BlockSpec((1, BLOCK, kw), lambda j: (smp(j), prev(j), 0)),
        pl.BlockSpec((1, wide, kw), lambda j: (smp(j), cur(j), 0)),
        pl.BlockSpec((1, BLOCK, kw), lambda j: (smp(j), nxt(j), 0)),
        pl.BlockSpec((1, cx, kw), lambda j: (smp(j), 0, 0)),
        pl.BlockSpec((1, vw, BLOCK), lambda j: (smp(j), 0, prev(j))),
        pl.BlockSpec((1, vw, wide), lambda j: (smp(j), 0, cur(j))),
        pl.BlockSpec((1, vw, BLOCK), lambda j: (smp(j), 0, nxt(j))),
        pl.BlockSpec((1, vw, cx), lambda j: (smp(j), 0, 0)),
    ]
    ml_in, h_specs, st_specs, st_shape = _mlstm_specs(B, T, nc)
    h_shape = jax.ShapeDtypeStruct((B, L, ML_V), F32)
    att, hf, hb, _, _ = pl.pallas_call(
        functools.partial(_att_mlstm_kernel, T=T, nb=B, nblk=nblk, ns=ns),
        out_shape=(jax.ShapeDtypeStruct((B, L, ATT_Q), BF16), h_shape, h_shape, *st_shape),
        grid=(nc,),
        in_specs=att_specs + ml_in,
        out_specs=(pl.BlockSpec((1, wide, ATT_Q), lambda j: (smp(j), cur(j), 0)), *h_specs, *st_specs),
        compiler_params=_cparams(("arbitrary",)),
        name="attention_mlstm",
    )(sink, qt, kv, kv, kv, kvx, vt, vt, vt, vtx, qm, kmt, vm, gc, gr, qm, kmt, vm, gc, gr, s0, m0)
    return att, hf, hb


def _mlstm(qm, kmt, vm, gc, gr, s0, m0, with_output):
    B, L, _ = qm.shape
    T = ML_CHUNK
    nc = L // T
    ml_in, h_specs, st_specs, st_shape = _mlstm_specs(B, T, nc)
    out_shape, out_specs = st_shape, st_specs
    if with_output:
        out_shape = [jax.ShapeDtypeStruct((B, L, ML_V), F32)] * 2 + out_shape
        out_specs = h_specs + out_specs
    return pl.pallas_call(
        functools.partial(_mlstm_kernel, T=T, nb=B, with_output=with_output),
        out_shape=tuple(out_shape),
        grid=(nc,),
        in_specs=ml_in,
        out_specs=tuple(out_specs),
        compiler_params=_cparams(("arbitrary",)),
        name="mlstm_scan" if with_output else "mlstm_context_state",
    )(qm, kmt, vm, gc, gr, qm, kmt, vm, gc, gr, s0, m0)


def _odd_tables(gate_b, q_g, k_g):
    assert sum([ATT_Q, ATT_KV, ATT_KV, ML_QK, ML_QK, ML_V, ML_V]) == OG
    head = np.arange(LANES) // HEAD_DIM
    pm = jnp.asarray((head[:, None] == head[None, :]) / HEAD_DIM, F32)
    gain = jnp.concatenate([jnp.tile(q_g, ATT_HEADS), jnp.tile(k_g, ATT_KV_HEADS)])[None, :]
    gb = jnp.pad(gate_b.reshape(1, -1), ((0, 0), (0, LANES - gate_b.size)))
    return pm, gain, gb


def kernel(x, c, ctx, c_ctx, ada_w, ada_b, norm_g, even_w_in, even_conv, even_w_out, odd_w_in, odd_gate_b,
           odd_q_g, odd_k_g, odd_sink, odd_w_out, ffn_w_up, ffn_conv, ffn_w_down):
    B, L, _ = x.shape
    C = ctx.shape[1]
    depth = ada_w.shape[0]
    assert depth == 2 and L % (DFT_N1 * SUBLANES) == 0 and C % ML_CHUNK == 0

    cv = jnp.concatenate([c, c_ctx[None, :], jnp.zeros((SUBLANES - B - 1, D_MODEL), F32)], axis=0)
    mod, w_in0, w_out0 = _modulation(cv, ada_w, ada_b, ((even_w_in, 0, EVEN_IN, 3), (even_w_out, 0, D_MODEL, 3)))
    lat, cx = None, B

    tm = min(512, L)
    tc = _channel_dft_table()
    first = ((ffn_w_up, 0, 2 * D_FF, 1), (ffn_w_down, 0, D_MODEL, 1))
    later = ((ffn_w_up, 1, 2 * D_FF, 1), (ffn_w_down, 1, D_MODEL, 2), (odd_w_in, 0, ODD_COLS, 1),
             (odd_w_out, 0, D_MODEL, 1))

    ng00, ng01 = norm_g[0, 0][None, :], norm_g[0, 1][None, :]

    yc, zr, zi, w_up0, w_down0 = _even_in(x, mod, 0, lat, ng00, w_in0, even_conv[0], tc, tm, L // DFT_N1, first)
    xl, w_up1, w_down1, w_in1, w_out1 = _mix_ffn(x, (yc, _seq_dft(zr, zi, tm)), mod, 0, lat, ng01, w_out0, w_up0,
                                                 ffn_conv, w_down0, tm, odd=False, side=later)
    yc, zr, zi = _even_in(ctx, mod, 0, cx, ng00, w_in0, even_conv[0], tc, C, None)
    xc = _mix_ffn(ctx, (yc, _dense_seq_dft(zr, zi)), mod, 0, cx, ng01, w_out0, w_up0, ffn_conv, w_down0, C, odd=False)

    pm, gain, gb = _odd_tables(odd_gate_b[0], odd_q_g[0], odd_k_g[0])
    ng10, ng11 = norm_g[1, 0][None, :], norm_g[1, 1][None, :]
    cos, sin = _rope_tables(L)
    one, nil = jnp.ones((C, LANES), F32), jnp.zeros((C, LANES), F32)
    qt, kv, vt, qm, kmt, vm, om, gc, gr = _odd_in(xl, mod, 1, lat, ng10, w_in1, pm, gain, cos, sin, gb, tm)
    _, kvx, vtx, qmx, kmtx, vmx, _, gcx, grx = _odd_in(xc, mod, 1, cx, ng10, w_in1, pm, gain, one, nil, gb, C)

    nchains = B * 2 * ML_HEADS
    s0 = jnp.zeros((nchains, ML_QK_DIM, 2 * ML_V_DIM), F32)
    m0 = jnp.zeros((nchains, LANES), F32)
    s1, m1 = _mlstm(qmx, kmtx, vmx, gcx, grx, s0, m0, with_output=False)
    att, hf, hb = _att_mlstm(qt, kv, kvx, vt, vtx, odd_sink[0], qm, kmt, vm, gc, gr, s1, m1)
    return _mix_ffn(xl, (att, hf, hb, om), mod, 1, lat, ng11, w_out1, w_up1, ffn_conv, w_down1, tm, odd=True)
```

```python
import functools

import numpy as np
import jax
import jax.numpy as jnp
from jax import lax
from jax.experimental import pallas as pl
from jax.experimental.pallas import tpu as pltpu

F32 = jnp.float32
BF16 = jnp.bfloat16

D_MODEL = 1024
GRID_W = 64
EPS = 1e-6
SC_CH = 512
FT_CH = 512
FT_GROUPS = 4
FT_GROUP_CH = FT_CH // FT_GROUPS
EVEN_IN = 3 * SC_CH + FT_CH
ATT_HEADS = 8
ATT_KV_HEADS = 2
HEAD_DIM = 64
ATT_SCALE = HEAD_DIM ** -0.5
WINDOW = 128
BLOCK = 128
ROPE_THETA = 10000.0
ML_HEADS = 4
ML_QK_DIM = 64
ML_V_DIM = 128
ATT_Q = ATT_HEADS * HEAD_DIM
ATT_KV = ATT_KV_HEADS * HEAD_DIM
ML_QK = ML_HEADS * ML_QK_DIM
ML_V = ML_HEADS * ML_V_DIM
D_FF = 2816

LANES = 128
SUBLANES = 8
VMEM_LIMIT_BYTES = 56 * 1024 * 1024

DFT_N1 = 128
FF_CHUNK = 256
N_FF_CHUNKS = D_FF // FF_CHUNK
ML_CHUNK = 128
ATT_QB = 2
NEG_BIG = -1e30
LOG2E = 1.4426950408889634

OQ = 0
OK_ = OQ + ATT_Q
OV = OK_ + ATT_KV
OQM = OV + ATT_KV
OKM = OQM + ML_QK
OVM = OKM + ML_QK
OOM = OVM + ML_V
OG = OOM + ML_V
N_GATES = 4 * ML_HEADS
assert WINDOW == BLOCK and ATT_KV == LANES and OG % LANES == 0 and N_GATES == 2 * SUBLANES


def _cparams(sem):
    return pltpu.CompilerParams(dimension_semantics=sem, vmem_limit_bytes=VMEM_LIMIT_BYTES)


def _sigmoid(x):
    return 1.0 / (1.0 + jnp.exp(-x))


def _norm_mod(x, g, shift, scale):
    y = x * lax.rsqrt(jnp.mean(x * x, axis=-1, keepdims=True) + EPS)
    return y * g * (1.0 + scale) + shift


def _mod_vec(mod_ref, k, mrow):
    r = pl.program_id(0) if mrow is None else mrow
    return mod_ref[k, pl.ds(r, 1), :]


def _mod_spec(layer):
    return pl.BlockSpec((None, 6, SUBLANES, D_MODEL), lambda *_: (layer, 0, 0, 0))


def _halo_rows(x, xn, xp, shift, scale, ng_ref):
    g = ng_ref[...]
    parts = [_norm_mod(r, g, shift, scale) for r in (x, xn, xp)]
    return jnp.concatenate(parts, axis=0).astype(BF16)


def _halo_valid(tm, i, nt):
    row = lax.broadcasted_iota(jnp.int32, (tm + 2 * SUBLANES, 1), 0)
    return ((row < tm) | ((row < tm + SUBLANES) & (i < nt - 1)) | ((row >= tm + SUBLANES) & (i > 0)))


def _conv3(v, cw, tm):
    n = v.shape[0]
    vp = pltpu.roll(v, 1, 0)[:tm]
    vn = pltpu.roll(v, n - 1, 0)[:tm]
    return vp * cw[0:1] + v[:tm] * cw[1:2] + vn * cw[2:3]


def _halo_specs(tm, L):
    hb = tm // SUBLANES
    last = L // SUBLANES - 1
    return [
        pl.BlockSpec((1, tm, D_MODEL), lambda b, i: (b, i, 0)),
        pl.BlockSpec((1, SUBLANES, D_MODEL), lambda b, i: (b, jnp.minimum((i + 1) * hb, last), 0)),
        pl.BlockSpec((1, SUBLANES, D_MODEL), lambda b, i: (b, jnp.maximum(i * hb - 1, 0), 0)),
    ]


def _const_spec(shape):
    nd = len(shape)
    return pl.BlockSpec(shape, lambda *_: (0,) * nd)


def _resident_spec(shape):
    nd = len(shape)
    return pl.BlockSpec(shape, lambda *_: (0,) * nd, pipeline_mode=pl.Buffered(1))


def _weight_spec(w, shape, layer):
    return _resident_spec(shape) if w.ndim == len(shape) else _layer_spec(shape, layer)


def _layer_spec(shape, layer):
    nd = len(shape)
    return pl.BlockSpec((None,) + tuple(shape), lambda *_: (layer,) + (0,) * nd, pipeline_mode=pl.Buffered(1))


def _cast_refs(srcs, dsts):
    for src, dst in zip(srcs, dsts):
        w = src[...].astype(BF16)
        dst[...] = w[..., :dst.shape[-1]]


def _layer_rows_spec(w, layer, rb, imap):
    if w.ndim == 2:
        return pl.BlockSpec((rb, w.shape[1]), lambda *a: (imap(*a), 0))
    return pl.BlockSpec((None, rb, w.shape[2]), lambda *a: (layer, imap(*a), 0))


def _side_cast_specs(side, steps, step_of):
    ins, outs, shapes = [], [], []
    for w, wl, wd, per in side:
        r = w.shape[-2]
        rb = r * per // steps
        blk = lambda *a, per=per: step_of(*a) // per
        ins.append(_layer_rows_spec(w, wl, rb, blk))
        outs.append(pl.BlockSpec((rb, wd), lambda *a, blk=blk: (blk(*a), 0)))
        shapes.append(jax.ShapeDtypeStruct((r, wd), BF16))
    return ins, outs, shapes


def _split_dot(x, p):
    hi = x.astype(BF16).astype(F32)
    y = jnp.dot(jnp.concatenate([hi, x - hi], axis=0).astype(BF16), p, preferred_element_type=F32)
    return y[:x.shape[0]] + y[x.shape[0]:]


def _mod_kernel(cv_ref, w_ref, b_ref, *rest):
    nside = (len(rest) - 1) // 2
    o_ref = rest[nside]
    _cast_refs(rest[:nside], rest[nside + 1:])
    cv = cv_ref[...]
    o_ref[0, 0] = _split_dot(cv * _sigmoid(cv), w_ref[0].astype(BF16)) + b_ref[0, 0]


def _modulation(cv, ada_w, ada_b, side=()):
    depth, _, n = ada_w.shape
    nv = n // D_MODEL
    side_in, side_out, side_shape = _side_cast_specs(side, depth * nv, lambda l, j: l * nv + j)
    return pl.pallas_call(
        _mod_kernel,
        out_shape=(jax.ShapeDtypeStruct((depth, nv, SUBLANES, D_MODEL), F32), *side_shape),
        grid=(depth, nv),
        in_specs=[
            pl.BlockSpec((SUBLANES, D_MODEL), lambda l, j: (0, 0)),
            pl.BlockSpec((1, D_MODEL, D_MODEL), lambda l, j: (l, 0, j)),
            pl.BlockSpec((1, 1, 1, D_MODEL), lambda l, j: (l, j, 0, 0)),
        ] + side_in,
        out_specs=(pl.BlockSpec((1, 1, SUBLANES, D_MODEL), lambda l, j: (l, j, 0, 0)), *side_out),
        compiler_params=_cparams(("arbitrary", "arbitrary")),
        name="modulation",
    )(cv, ada_w, ada_b.reshape(depth, nv, 1, D_MODEL), *[e[0] for e in side])


def _even_in_kernel(x_ref, xn_ref, xp_ref, mod_ref, ng_ref, w_ref, cw_ref, tc_ref, *rest, tm, nt, n2, mrow, nsub):
    nside = (len(rest) - 3) // 2
    yc_ref, zr_ref, zi_ref = rest[nside:nside + 3]
    _cast_refs(rest[:nside], rest[nside + 3:])
    i = pl.program_id(1)
    shift, scale = _mod_vec(mod_ref, 0, mrow), _mod_vec(mod_ref, 1, mrow)
    tc = tc_ref[...].astype(BF16)
    for s in range(nsub):
        lo = s * tm
        x = x_ref[0, lo:lo + tm]
        xn = xn_ref[0] if s == nsub - 1 else x_ref[0, lo + tm:lo + tm + SUBLANES]
        xp = xp_ref[0] if s == 0 else x_ref[0, lo - SUBLANES:lo]
        hh = _halo_rows(x, xn, xp, shift, scale, ng_ref)
        u = jnp.dot(hh, w_ref[...], preferred_element_type=F32)
        v = u[:, SC_CH:2 * SC_CH] * u[:, 2 * SC_CH:3 * SC_CH]
        v = jnp.where(_halo_valid(tm, i * nsub + s, nt), v, 0.0)
        yc = u[:tm, :SC_CH] * _conv3(v, cw_ref[...], tm)
        yc_ref[0, lo:lo + tm] = yc.astype(BF16)
        uf = u[:tm, 3 * SC_CH:].astype(BF16)
        for g in range(FT_GROUPS):
            sl = slice(g * FT_GROUP_CH, (g + 1) * FT_GROUP_CH)
            ab = jnp.dot(uf[:, sl], tc, preferred_element_type=F32)
            if n2 is None:
                zr_ref[g, 0, lo:lo + tm] = ab[:, :FT_GROUP_CH]
                zi_ref[g, 0, lo:lo + tm] = ab[:, FT_GROUP_CH:]
            else:
                for a in range(tm // n2):
                    dst = pl.ds(lo + a, n2, stride=tm // n2)
                    zr_ref[g, 0, dst, :] = ab[n2 * a:n2 * (a + 1), :FT_GROUP_CH]
                    zi_ref[g, 0, dst, :] = ab[n2 * a:n2 * (a + 1), FT_GROUP_CH:]


def _even_in(x, mod, layer, mrow, ng, w_in, cw, tc, tm, n2, side=()):
    B, L, _ = x.shape
    nt = L // tm
    nsub = 2 if nt % 2 == 0 else 1
    bm = nsub * tm
    out = jax.ShapeDtypeStruct((B, L, FT_CH), BF16)
    zout = jax.ShapeDtypeStruct((FT_GROUPS, B, L, FT_GROUP_CH), F32)
    ospec = pl.BlockSpec((1, bm, FT_CH), lambda b, i: (b, i, 0))
    zspec = pl.BlockSpec((FT_GROUPS, 1, bm, FT_GROUP_CH), lambda b, i: (0, b, i, 0))
    ns = nt // nsub
    side_in, side_out, side_shape = _side_cast_specs(side, B * ns, lambda b, i: b * ns + i)
    return pl.pallas_call(
        functools.partial(_even_in_kernel, tm=tm, nt=nt, n2=n2, mrow=mrow, nsub=nsub),
        out_shape=(out, zout, zout, *side_shape),
        grid=(B, ns),
        in_specs=_halo_specs(bm, L) + [
            _mod_spec(layer),
            _const_spec((1, D_MODEL)),
            _const_spec((D_MODEL, EVEN_IN)),
            _const_spec((3, SC_CH)),
            _const_spec((FT_GROUP_CH, 2 * FT_GROUP_CH)),
        ] + side_in,
        out_specs=(ospec, zspec, zspec, *side_out),
        compiler_params=_cparams(("parallel", "arbitrary")),
        name="even_in",
    )(x, x, x, mod, ng, w_in, cw, tc, *[e[0] for e in side])


def _seq_dft_kernel(zr_ref, zi_ref, m_ref, g_ref, y_ref, o_scr, *, n2, tm):
    m1 = m_ref[...].astype(BF16)
    chunk = tm // n2
    ntile = DFT_N1 // chunk

    def rows(ref, j):
        return [ref[0, 0, t * tm + j * chunk:t * tm + (j + 1) * chunk, :] for t in range(ntile)]

    for j in range(n2):
        z = jnp.concatenate(rows(zr_ref, j) + rows(zi_ref, j), axis=0)
        o_scr[2 * DFT_N1 * j:2 * DFT_N1 * (j + 1), :] = jnp.dot(m1, z.astype(BF16), preferred_element_type=F32)
    for a in range(DFT_N1 // SUBLANES):
        def gather(base):
            x = jnp.concatenate([o_scr[2 * DFT_N1 * j + base + SUBLANES * a:2 * DFT_N1 * j + base + SUBLANES * (a + 1), :]
                                 for j in range(n2)], axis=0)
            return jnp.swapaxes(x.reshape(n2, SUBLANES, LANES), 0, 1)
        xr, xi = gather(0), gather(DFT_N1)
        ys = []
        for r in range(SUBLANES):
            o = jnp.concatenate([xr[r], xi[r]], axis=0).astype(BF16)
            ys.append(jnp.dot(g_ref[SUBLANES * a + r].astype(BF16), o, preferred_element_type=F32))
        y8 = jnp.swapaxes(jnp.stack(ys, axis=0), 0, 1)
        for k2 in range(n2):
            y_ref[0, DFT_N1 * k2 + SUBLANES * a:DFT_N1 * k2 + SUBLANES * (a + 1), :] = y8[k2]


def _dft_tables(L):
    n2 = L // DFT_N1
    k = np.arange(DFT_N1)
    a = 2.0 * np.pi * ((k[:, None] * k[None, :]) % DFT_N1) / DFT_N1
    er, ei = np.cos(a) / np.sqrt(DFT_N1), -np.sin(a) / np.sqrt(DFT_N1)
    m1 = np.block([[er, -ei], [ei, er]])
    k1 = np.arange(DFT_N1)[:, None, None]
    k2 = np.arange(n2)[None, :, None]
    nn = np.arange(n2)[None, None, :]
    th = 2.0 * np.pi * ((nn * (k1 + DFT_N1 * k2)) % L) / L
    g = np.concatenate([np.cos(th), np.sin(th)], axis=-1) / np.sqrt(n2)
    return jnp.asarray(m1, F32), jnp.asarray(g, F32)


def _channel_dft_table():
    k = np.arange(FT_GROUP_CH)
    a = 2.0 * np.pi * ((k[:, None] * k[None, :]) % FT_GROUP_CH) / FT_GROUP_CH
    t = np.concatenate([np.cos(a), -np.sin(a)], axis=1) / np.sqrt(FT_GROUP_CH)
    return jnp.asarray(t, F32)


def _seq_dft(zr, zi, tm):
    G, B, L, C = zr.shape
    n2 = L // DFT_N1
    m1, g = _dft_tables(L)
    zspec = pl.BlockSpec((1, 1, L, C), lambda b, j: (j, b, 0, 0))
    return pl.pallas_call(
        functools.partial(_seq_dft_kernel, n2=n2, tm=tm),
        out_shape=jax.ShapeDtypeStruct((B, L, G * C), F32),
        grid=(B, G),
        in_specs=[zspec, zspec, _const_spec((2 * DFT_N1, 2 * DFT_N1)), _const_spec((DFT_N1, n2, 2 * n2))],
        out_specs=pl.BlockSpec((1, L, C), lambda b, j: (b, 0, j)),
        scratch_shapes=[pltpu.VMEM((2 * DFT_N1 * n2, LANES), F32)],
        compiler_params=_cparams(("parallel", "arbitrary")),
        name="seq_dft",
    )(zr, zi, m1, g)


def _dense_dft_kernel(zr_ref, zi_ref, t_ref, y_ref):
    z = jnp.concatenate([zr_ref[0, 0], zi_ref[0, 0]], axis=0).astype(BF16)
    y_ref[0] = jnp.dot(t_ref[...].astype(BF16), z, preferred_element_type=F32)


def _dense_seq_dft(zr, zi):
    G, B, L, C = zr.shape
    k = np.arange(L)
    a = 2.0 * np.pi * ((k[:, None] * k[None, :]) % L) / L
    t = jnp.asarray(np.concatenate([np.cos(a), np.sin(a)], axis=1) / np.sqrt(L), F32)
    zspec = pl.BlockSpec((1, 1, L, C), lambda b, j: (j, b, 0, 0))
    return pl.pallas_call(
        _dense_dft_kernel,
        out_shape=jax.ShapeDtypeStruct((B, L, G * C), F32),
        grid=(B, G),
        in_specs=[zspec, zspec, _const_spec((L, 2 * L))],
        out_specs=pl.BlockSpec((1, L, C), lambda b, j: (b, 0, j)),
        compiler_params=_cparams(("arbitrary", "arbitrary")),
        name="dense_seq_dft",
    )(zr, zi, t)


HALO = 16


def _wide_halo_specs(tm, L, width):
    hb = tm // HALO
    last = L // HALO - 1
    return [
        pl.BlockSpec((1, tm, width), lambda b, i: (b, i, 0)),
        pl.BlockSpec((1, HALO, width), lambda b, i: (b, jnp.minimum((i + 1) * hb, last), 0)),
        pl.BlockSpec((1, HALO, width), lambda b, i: (b, jnp.maximum(i * hb - 1, 0), 0)),
    ]


def _circ(t_ref, n_ref, p_ref):
    return jnp.concatenate([t_ref[0], n_ref[0], p_ref[0]], axis=0)


def _mix_ffn_kernel(*refs, tm, nt, odd, mrow, nside):
    n_in = 15 if odd else 9
    x3, rest = refs[:3], refs[3:n_in]
    mod_ref, ng_ref, wo_ref, wu_ref, cw_ref, wd_ref = refs[n_in:n_in + 6]
    side_src = refs[n_in + 6:n_in + 6 + nside]
    o_ref = refs[n_in + 6 + nside]
    side_dst = refs[n_in + 7 + nside:n_in + 7 + 2 * nside]
    hh_scr, act_scr = refs[n_in + 7 + 2 * nside:]
    _cast_refs(side_src, side_dst)
    i = pl.program_id(1)
    if odd:
        att, hf, hb, om = (_circ(*rest[k:k + 3]) for k in range(0, 12, 3))
        lhs = jnp.concatenate([att, ((hf + hb) * _sigmoid(om.astype(F32))).astype(BF16)], axis=-1)
    else:
        yc, yf = _circ(*rest[0:3]), _circ(*rest[3:6])
        lhs = jnp.concatenate([yc, yf.astype(BF16)], axis=-1)
    mv = [_mod_vec(mod_ref, k, mrow) for k in range(6)]
    x1 = _circ(*x3) + mv[2] * jnp.dot(lhs, wo_ref[...], preferred_element_type=F32)
    hh_scr[...] = _norm_mod(x1, ng_ref[...], mv[3], mv[4]).astype(BF16)
    row = lax.broadcasted_iota(jnp.int32, (tm + 2 * HALO, 1), 0)
    valid = (row < tm) | ((row < tm + HALO) & (i < nt - 1)) | ((row >= tm + HALO) & (i > 0))
    for c in range(N_FF_CHUNKS):
        lo = c * FF_CHUNK
        g = jnp.dot(hh_scr[...], wu_ref[:, lo:lo + FF_CHUNK], preferred_element_type=F32)
        g = jnp.where(valid, g, 0.0)
        cv = _conv3(g, cw_ref[:, lo:lo + FF_CHUNK], tm)
        val = jnp.dot(hh_scr[:tm, :], wu_ref[:, D_FF + lo:D_FF + lo + FF_CHUNK], preferred_element_type=F32)
        act_scr[:, lo:lo + FF_CHUNK] = (cv * _sigmoid(cv) * val).astype(BF16)
    y = jnp.dot(act_scr[...], wd_ref[...], preferred_element_type=F32)
    o_ref[0] = x1[:tm] + mv[5] * y


def _mix_ffn(x, mixed, mod, layer, mrow, ng, w_out, w_up, cw, w_down, tm, odd, side=()):
    B, L, _ = x.shape
    nt = L // tm
    specs = _wide_halo_specs(tm, L, D_MODEL)
    args = [x, x, x]
    for a in mixed:
        specs += _wide_halo_specs(tm, L, a.shape[-1])
        args += [a, a, a]
    side_in, side_out, side_shape = _side_cast_specs(side, B * nt, lambda b, i: b * nt + i)
    out = pl.pallas_call(
        functools.partial(_mix_ffn_kernel, tm=tm, nt=nt, odd=odd, mrow=mrow, nside=len(side)),
        out_shape=(jax.ShapeDtypeStruct(x.shape, F32), *side_shape),
        grid=(B, nt),
        in_specs=specs + [
            _mod_spec(layer),
            _const_spec((1, D_MODEL)),
            _resident_spec((D_MODEL, D_MODEL)),
            _weight_spec(w_up, (D_MODEL, 2 * D_FF), layer),
            _layer_spec((3, D_FF), layer),
            _weight_spec(w_down, (D_FF, D_MODEL), layer),
        ] + side_in,
        out_specs=(pl.BlockSpec((1, tm, D_MODEL), lambda b, i: (b, i, 0)), *side_out),
        scratch_shapes=[pltpu.VMEM((tm + 2 * HALO, D_MODEL), BF16),
                        pltpu.VMEM((tm, D_FF), BF16)],
        compiler_params=_cparams(("parallel", "arbitrary")),
        name="odd_mix_ffn" if odd else "even_mix_ffn",
    )(*args, mod, ng, w_out, w_up, cw, w_down, *[e[0] for e in side])
    return out if side else out[0]


def _split3(x):
    parts = []
    r = x
    for _ in range(3):
        p = r.astype(BF16)
        parts.append(p)
        r = r - p.astype(F32)
    return parts


def _odd_in_kernel(x_ref, mod_ref, ng_ref, w_ref, wg_ref, pm_ref, gain_ref, cos_ref, sin_ref, gb_ref, tl_ref, tu_ref,
                   qt_ref, kv_ref, vt_ref, qm_ref, kmt_ref, vm_ref, om_ref, gc_ref, gr_ref, *, mrow, tm, nsub):
    for s in range(nsub):
        rs = slice(s * tm, (s + 1) * tm)
        h = _norm_mod(x_ref[0, rs], ng_ref[...], _mod_vec(mod_ref, 0, mrow), _mod_vec(mod_ref, 1, mrow)).astype(BF16)
        u = jnp.dot(h, w_ref[...], preferred_element_type=F32)

        uqk = u[:, OQ:OV]
        sq = (uqk * uqk).astype(BF16)
        pm = pm_ref[...].astype(BF16)
        ms = jnp.concatenate([jnp.dot(sq[:, t * LANES:(t + 1) * LANES], pm, preferred_element_type=F32)
                              for t in range((OV - OQ) // LANES)], axis=1)
        rn = uqk * lax.rsqrt(ms + EPS) * gain_ref[...]
        lane = lax.broadcasted_iota(jnp.int32, (1, LANES), 1)
        first = (lane % 32) < 16
        cos = cos_ref[rs, :]
        sin = sin_ref[rs, :]
        roped = []
        for t in range((OV - OQ) // LANES):
            xt = rn[:, t * LANES:(t + 1) * LANES]
            sw = jnp.where(first, pltpu.roll(xt, LANES - 16, 1), pltpu.roll(xt, 16, 1))
            roped.append(xt * cos + sw * sin)
        for t in range(ATT_Q // LANES):
            qt_ref[0, t * LANES:(t + 1) * LANES, rs] = (roped[t] * (ATT_SCALE * LOG2E)).T.astype(BF16)
        k = roped[ATT_Q // LANES]
        v = u[:, OV:OQM]
        half = LANES // 2
        kv_ref[0, rs, 0:LANES] = k.astype(BF16)
        kv_ref[0, rs, LANES:2 * LANES] = pltpu.roll(k, half, 1).astype(BF16)
        vt_ref[0, :, rs] = v.T.astype(BF16)

        qm_ref[0, rs] = u[:, OQM:OKM].astype(BF16)
        for p in range(ML_QK // LANES):
            km = u[:, OKM + p * LANES:OKM + (p + 1) * LANES] * (ML_QK_DIM ** -0.5)
            kmt_ref[0, p * LANES:(p + 1) * LANES, rs] = km.T.astype(BF16)
        vm_ref[0, rs] = u[:, OVM:OOM].astype(BF16)
        om_ref[0, rs] = u[:, OOM:OG].astype(BF16)

        gt = lax.dot_general(wg_ref[...].astype(BF16), h, (((1,), (1,)), ((), ())),
                             preferred_element_type=F32) + gb_ref[...]
        row = lax.broadcasted_iota(jnp.int32, (2 * SUBLANES, 1), 0)
        logsig = jnp.minimum(gt, 0.0) - jnp.log(1.0 + jnp.exp(-jnp.abs(gt)))
        parts = jnp.concatenate(_split3(logsig), axis=0)
        tl, tu = tl_ref[...].astype(BF16), tu_ref[...].astype(BF16)

        def chunk_cumsum(tri):
            c = jnp.concatenate([jnp.dot(parts[:, k:k + ML_CHUNK], tri, preferred_element_type=F32)
                                 for k in range(0, tm, ML_CHUNK)], axis=1)
            n = 2 * SUBLANES
            return c[0:n] + c[n:2 * n] + c[2 * n:3 * n]

        cum_f, cum_b = chunk_cumsum(tu), chunk_cumsum(tl)
        sel = row % 8
        gr = jnp.where(sel < 4, gt, jnp.where(row < SUBLANES, cum_f, cum_b)) * LOG2E
        gr_ref[0, :, rs] = gr
        gc_ref[0, rs] = jnp.concatenate([gr, jnp.zeros((LANES - 2 * SUBLANES, tm), F32)], axis=0).T


def _chunk_tri():
    i = np.arange(ML_CHUNK)
    return jnp.asarray(i[None, :] <= i[:, None], F32), jnp.asarray(i[None, :] >= i[:, None], F32)


def _odd_in(x, mod, layer, mrow, ng, w, wg, pm, gain, cos, sin, gb, tm):
    B, L, _ = x.shape
    nsub = 2 if L % (2 * tm) == 0 else 1
    bm = nsub * tm

    def rows(c, dt=BF16):
        return jax.ShapeDtypeStruct((B, L, c), dt), pl.BlockSpec((1, bm, c), lambda b, i: (b, i, 0))

    def cols(c, dt=BF16):
        return jax.ShapeDtypeStruct((B, c, L), dt), pl.BlockSpec((1, c, bm), lambda b, i: (b, 0, i))

    outs = [cols(ATT_Q), rows(2 * LANES), cols(LANES), rows(ML_QK), cols(ML_QK), rows(ML_V), rows(ML_V),
            rows(LANES, F32), cols(2 * SUBLANES, F32)]
    nqk = OV - OQ
    tl, tu = _chunk_tri()
    return pl.pallas_call(
        functools.partial(_odd_in_kernel, mrow=mrow, tm=tm, nsub=nsub),
        out_shape=tuple(o[0] for o in outs),
        grid=(B, L // bm),
        in_specs=[
            pl.BlockSpec((1, bm, D_MODEL), lambda b, i: (b, i, 0)),
            _mod_spec(layer),
            _const_spec((1, D_MODEL)),
            _const_spec((D_MODEL, OG)),
            _const_spec((N_GATES, D_MODEL)),
            _const_spec((LANES, LANES)),
            _const_spec((1, nqk)),
            pl.BlockSpec((bm, LANES), lambda b, i: (i, 0)),
            pl.BlockSpec((bm, LANES), lambda b, i: (i, 0)),
            _const_spec((N_GATES, 1)),
            _const_spec((ML_CHUNK, ML_CHUNK)),
            _const_spec((ML_CHUNK, ML_CHUNK)),
        ],
        out_specs=tuple(o[1] for o in outs),
        compiler_params=_cparams(("parallel", "arbitrary")),
        name="odd_in",
    )(x, mod, ng, w, wg, pm, gain, cos, sin, gb, tl, tu)


def _rope_tables(L):
    rows = L // GRID_W
    pos = np.stack([np.repeat(np.arange(rows), GRID_W), np.tile(np.arange(GRID_W), rows)]).astype(np.float64)
    axis_dim = HEAD_DIM // 2
    inv_freq = ROPE_THETA ** (-np.arange(0, axis_dim, 2, dtype=np.float64) / axis_dim)
    ang = pos[:, :, None] * inv_freq
    c, sn = np.cos(ang), np.sin(ang)
    cos = np.concatenate([c[0], c[0], c[1], c[1]], axis=-1)
    sin = np.concatenate([-sn[0], sn[0], -sn[1], sn[1]], axis=-1)
    return jnp.asarray(np.tile(cos, (1, 2)), F32), jnp.asarray(np.tile(sin, (1, 2)), F32)


def _att_steps(n, sink_ref, qt_ref, kvp_ref, kvc_ref, kvn_ref, kvx_ref, vtp_ref, vtc_ref, vtn_ref, vtx_ref,
               o_ref, *, nb):
    half = LANES // 2
    lo = lax.broadcasted_iota(jnp.int32, (1, LANES), 1) < half
    zero = jnp.zeros((), BF16)
    cx = kvx_ref.shape[1]
    klocal = ([kvp_ref[0]] + [kvc_ref[0, i * BLOCK:(i + 1) * BLOCK] for i in range(ATT_QB)] + [kvn_ref[0]])
    vlocal = ([vtp_ref[0]] + [vtc_ref[0, :, i * BLOCK:(i + 1) * BLOCK] for i in range(ATT_QB)] + [vtn_ref[0]])
    kctx = [kvx_ref[0, i:i + BLOCK] for i in range(0, cx, BLOCK)]
    vctx = [vtx_ref[0, :, i:i + BLOCK] for i in range(0, cx, BLOCK)]
    ones = jnp.ones((half, BLOCK), BF16)

    kj = lax.broadcasted_iota(jnp.int32, (BLOCK, BLOCK), 0)
    qi = lax.broadcasted_iota(jnp.int32, (BLOCK, BLOCK), 1)

    def twice(x):
        return jnp.concatenate([x, x], axis=1)

    left = lax.broadcasted_iota(jnp.int32, (1, 2 * BLOCK), 1) < BLOCK
    group = ATT_HEADS // ATT_KV_HEADS

    def block(qb):
        blk_id = n * ATT_QB + qb
        kblocks = klocal[qb:qb + 3] + kctx
        vblocks = vlocal[qb:qb + 3] + vctx
        bias = [twice(jnp.where((kj >= qi) & (blk_id > 0), 0.0, NEG_BIG)).astype(BF16), None,
                twice(jnp.where((kj <= qi) & (blk_id < nb - 1), 0.0, NEG_BIG)).astype(BF16)] + [None] * len(kctx)
        for kvh in range(ATT_KV_HEADS):
            t0 = kvh * group // 2
            cols = slice(qb * BLOCK, (qb + 1) * BLOCK)
            qt2 = jnp.concatenate([qt_ref[0, t0 * LANES:(t0 + 1) * LANES, cols],
                                   qt_ref[0, (t0 + 1) * LANES:(t0 + 2) * LANES, cols]], axis=1)
            outs = []
            for par in range(2):
                sk = jnp.where(left, sink_ref[2 * t0 + par], sink_ref[2 * t0 + 2 + par]) * LOG2E
                m = sk.astype(BF16).astype(F32)
                acc = jnp.zeros((LANES, 2 * BLOCK), F32)
                for blk in range(len(kblocks)):
                    k, ks = kblocks[blk][:, :LANES], kblocks[blk][:, LANES:]
                    kh = ((jnp.where(lo, k, zero), jnp.where(lo, zero, ks)),
                          (jnp.where(lo, ks, zero), jnp.where(lo, zero, k)))[kvh][par]
                    s = jnp.dot(kh, qt2, preferred_element_type=F32)
                    s = s.astype(BF16)
                    if bias[blk] is not None:
                        s = s + bias[blk]
                    m_new = jnp.maximum(m, jnp.max(s, axis=0, keepdims=True).astype(F32))
                    p = jnp.exp2(s - m_new.astype(BF16))
                    vh = jnp.concatenate([vblocks[blk][kvh * half:(kvh + 1) * half], ones], axis=0)
                    acc = acc * jnp.exp2(m - m_new) + jnp.dot(vh, p, preferred_element_type=F32)
                    m = m_new
                l = acc[half:half + 1, :] + jnp.exp2(sk - m)
                outs.append(acc[:half] * (1.0 / l))
            ot = jnp.concatenate(outs, axis=0)
            rows = slice(qb * BLOCK, (qb + 1) * BLOCK)
            o_ref[0, rows, t0 * LANES:(t0 + 1) * LANES] = ot[:, :BLOCK].T.astype(BF16)
            o_ref[0, rows, (t0 + 1) * LANES:(t0 + 2) * LANES] = ot[:, BLOCK:].T.astype(BF16)

    return [functools.partial(block, qb) for qb in range(ATT_QB)]


ML_GROUP = 8


def _mlstm_steps(j, qf_ref, ktf_ref, vf_ref, gcf_ref, grf_ref, qb_ref, ktb_ref, vb_ref, gcb_ref, grb_ref,
                 s0_ref, m0_ref, *rest, T, nb, with_output):
    if with_output:
        hf_ref, hb_ref, s_ref, m_ref = rest
    else:
        s_ref, m_ref = rest

    @pl.when(j == 0)
    def _():
        s_ref[...] = s0_ref[...]
        m_ref[...] = m0_ref[...]

    ti = lax.broadcasted_iota(jnp.int32, (T, T), 0)
    si = lax.broadcasted_iota(jnp.int32, (T, T), 1)
    masks = (si <= ti, si >= ti)
    top = lax.broadcasted_iota(jnp.int32, (LANES, 1), 0) < (LANES // 2)
    zero = jnp.zeros((), BF16)
    ones = jnp.ones((T, ML_V_DIM), BF16)
    zpad = jnp.zeros((ML_QK_DIM, 2 * ML_V_DIM), BF16)
    fwd = (qf_ref, ktf_ref, vf_ref, gcf_ref, grf_ref)
    bwd = (qb_ref, ktb_ref, vb_ref, gcb_ref, grb_ref)
    chains = [(b, d, h) for b in range(nb) for d in range(2) for h in range(ML_HEADS)]

    def group(g0):
        grp = chains[g0:g0 + ML_GROUP]
        st = {}
        for (b, d, h) in grp:
            q_ref, kt_ref, v_ref, gc_ref, gr_ref = fwd if d == 0 else bwd
            r = (b * 2 + d) * ML_HEADS + h
            li, lf = 8 * d + h, 8 * d + 4 + h
            gr = gr_ref[b]
            u_row = gr[li:li + 1, :] - gr[lf:lf + 1, :]
            m_prev = m_ref[r:r + 1, 0:1]
            e = dict(r=r, u_row=u_row, m_prev=m_prev,
                     b_end=gr[lf:lf + 1, (T - 1 if d == 0 else 0):(T if d == 0 else 1)],
                     c_end=jnp.maximum(jnp.max(u_row, axis=1, keepdims=True), m_prev))
            if with_output:
                e["mu"] = jnp.where(masks[d], u_row, NEG_BIG)
                c_col = jnp.maximum(jnp.max(e["mu"], axis=1, keepdims=True), m_prev)
                e["cb"] = jnp.broadcast_to(c_col, (T, LANES))
                e["bb"] = jnp.broadcast_to(gc_ref[b, :, lf:lf + 1], (T, LANES))
            st[(b, d, h)] = e
        for (b, d, h) in grp:
            q_ref, kt_ref, v_ref, gc_ref, gr_ref = fwd if d == 0 else bwd
            e = st[(b, d, h)]
            pair = h // 2
            ktp = kt_ref[b, pair * LANES:(pair + 1) * LANES, :]
            e["kth"] = ktp[(h % 2) * ML_QK_DIM:(h % 2 + 1) * ML_QK_DIM, :]
            e["vext"] = jnp.concatenate([v_ref[b, :, h * ML_V_DIM:(h + 1) * ML_V_DIM], ones], axis=1)
            if with_output:
                qp = q_ref[b, :, pair * LANES:(pair + 1) * LANES]
                kpad = jnp.where(top, ktp, zero) if h % 2 == 0 else jnp.where(top, zero, ktp)
                qk = jnp.dot(qp, kpad, preferred_element_type=F32) * jnp.exp2(e["mu"] - e["cb"])
                qs = qp.astype(F32) * jnp.exp2(e["m_prev"] - e["cb"])
                e["lhs"] = jnp.concatenate([qk.astype(BF16), qs.astype(BF16)], axis=1)
        for (b, d, h) in grp:
            e = st[(b, d, h)]
            e["s_prev"] = s_ref[e["r"]]
            if with_output:
                sb = e["s_prev"].astype(BF16)
                rhs = jnp.concatenate([e["vext"]] + ([sb, zpad] if h % 2 == 0 else [zpad, sb]), axis=0)
                tot = jnp.dot(e["lhs"], rhs, preferred_element_type=F32)
                floor = jnp.exp2(-(e["bb"] + e["cb"]))
                hout = tot[:, :ML_V_DIM] / jnp.maximum(jnp.abs(tot[:, ML_V_DIM:]), floor)
                o_ref = hf_ref if d == 0 else hb_ref
                o_ref[b, :, h * ML_V_DIM:(h + 1) * ML_V_DIM] = hout
        for (b, d, h) in grp:
            e = st[(b, d, h)]
            kw = (e["kth"].astype(F32) * jnp.exp2(e["u_row"] - e["c_end"])).astype(BF16)
            upd = jnp.dot(kw, e["vext"], preferred_element_type=F32)
            s_ref[e["r"]] = jnp.exp2(e["m_prev"] - e["c_end"]) * e["s_prev"] + upd
            m_ref[e["r"]:e["r"] + 1, :] = jnp.broadcast_to(e["b_end"] + e["c_end"], (1, LANES))

    return [functools.partial(group, g0) for g0 in range(0, len(chains), ML_GROUP)]


def _mlstm_kernel(*refs, T, nb, with_output):
    for step in _mlstm_steps(pl.program_id(0), *refs, T=T, nb=nb, with_output=with_output):
        step()


def _att_mlstm_kernel(*refs, T, nb, nblk, ns):
    j = pl.program_id(0)
    sink_ref, att_in, ml_in = refs[0], refs[1:10], refs[10:22]
    o_ref, ml_out = refs[22], refs[23:]
    att = _att_steps(lax.rem(j, ns), sink_ref, *att_in, o_ref, nb=nblk)
    ml = _mlstm_steps(j, *ml_in, *ml_out, T=T, nb=nb, with_output=True)
    for k in range(max(len(att), len(ml))):
        if k < len(att):
            att[k]()
        if k < len(ml):
            ml[k]()


def _mlstm_specs(B, T, nc):
    up = lambda j: j
    down = lambda j: nc - 1 - j

    def specs(o):
        return [pl.BlockSpec((B, T, ML_QK), lambda j: (0, o(j), 0)),
                pl.BlockSpec((B, ML_QK, T), lambda j: (0, 0, o(j))),
                pl.BlockSpec((B, T, ML_V), lambda j: (0, o(j), 0)),
                pl.BlockSpec((B, T, LANES), lambda j: (0, o(j), 0)),
                pl.BlockSpec((B, 2 * SUBLANES, T), lambda j: (0, 0, o(j)))]

    nchains = B * 2 * ML_HEADS
    s_spec = _const_spec((nchains, ML_QK_DIM, 2 * ML_V_DIM))
    m_spec = _const_spec((nchains, LANES))
    state_shape = [jax.ShapeDtypeStruct((nchains, ML_QK_DIM, 2 * ML_V_DIM), F32),
                   jax.ShapeDtypeStruct((nchains, LANES), F32)]
    h_specs = [pl.BlockSpec((B, T, ML_V), lambda j: (0, up(j), 0)),
               pl.BlockSpec((B, T, ML_V), lambda j: (0, down(j), 0))]
    return specs(up) + specs(down) + [s_spec, m_spec], h_specs, [s_spec, m_spec], state_shape


def _att_mlstm(qt, kv, kvx, vt, vtx, sink, qm, kmt, vm, gc, gr, s0, m0):
    B, _, L = qt.shape
    T = ML_CHUNK
    nc = L // T
    nblk = L // BLOCK
    ns = nblk // ATT_QB
    assert nc == B * ns
    cx = kvx.shape[1]
    kw = kv.shape[2]
    vw = vt.shape[1]
    wide = ATT_QB * BLOCK
    smp = lambda j: j // ns
    cur = lambda j: j % ns
    prev = lambda j: jnp.maximum(cur(j) * ATT_QB - 1, 0)
    nxt = lambda j: jnp.minimum((cur(j) + 1) * ATT_QB, nblk - 1)
    att_specs = [
        pl.BlockSpec(memory_space=pltpu.SMEM),
        pl.BlockSpec((1, ATT_Q, wide), lambda j: (smp(j), 0, cur(j))),
        pl.BlockSpec((1, BLOCK, kw), lambda j: (smp(j), prev(j), 0)),
        pl.BlockSpec((1, wide, kw), lambda j: (smp(j), cur(j), 0)),
        pl.BlockSpec((1, BLOCK, kw), lambda j: (smp(j), nxt(j), 0)),
        pl.BlockSpec((1, cx, kw), lambda j: (smp(j), 0, 0)),
        pl.BlockSpec((1, vw, BLOCK), lambda j: (smp(j), 0, prev(j))),
        pl.BlockSpec((1, vw, wide), lambda j: (smp(j), 0, cur(j))),
        pl.BlockSpec((1, vw, BLOCK), lambda j: (smp(j), 0, nxt(j))),
        pl.BlockSpec((1, vw, cx), lambda j: (smp(j), 0, 0)),
    ]
    ml_in, h_specs, st_specs, st_shape = _mlstm_specs(B, T, nc)
    h_shape = jax.ShapeDtypeStruct((B, L, ML_V), F32)
    att, hf, hb, _, _ = pl.pallas_call(
        functools.partial(_att_mlstm_kernel, T=T, nb=B, nblk=nblk, ns=ns),
        out_shape=(jax.ShapeDtypeStruct((B, L, ATT_Q), BF16), h_shape, h_shape, *st_shape),
        grid=(nc,),
        in_specs=att_specs + ml_in,
        out_specs=(pl.BlockSpec((1, wide, ATT_Q), lambda j: (smp(j), cur(j), 0)), *h_specs, *st_specs),
        compiler_params=_cparams(("arbitrary",)),
        name="attention_mlstm",
    )(sink, qt, kv, kv, kv, kvx, vt, vt, vt, vtx, qm, kmt, vm, gc, gr, qm, kmt, vm, gc, gr, s0, m0)
    return att, hf, hb


def _mlstm(qm, kmt, vm, gc, gr, s0, m0, with_output):
    B, L, _ = qm.shape
    T = ML_CHUNK
    nc = L // T
    ml_in, h_specs, st_specs, st_shape = _mlstm_specs(B, T, nc)
    out_shape, out_specs = st_shape, st_specs
    if with_output:
        out_shape = [jax.ShapeDtypeStruct((B, L, ML_V), F32)] * 2 + out_shape
        out_specs = h_specs + out_specs
    return pl.pallas_call(
        functools.partial(_mlstm_kernel, T=T, nb=B, with_output=with_output),
        out_shape=tuple(out_shape),
        grid=(nc,),
        in_specs=ml_in,
        out_specs=tuple(out_specs),
        compiler_params=_cparams(("arbitrary",)),
        name="mlstm_scan" if with_output else "mlstm_context_state",
    )(qm, kmt, vm, gc, gr, qm, kmt, vm, gc, gr, s0, m0)


def _odd_tables(gate_b, q_g, k_g):
    assert sum([ATT_Q, ATT_KV, ATT_KV, ML_QK, ML_QK, ML_V, ML_V]) == OG
    head = np.arange(LANES) // HEAD_DIM
    pm = jnp.asarray((head[:, None] == head[None, :]) / HEAD_DIM, F32)
    gain = jnp.concatenate([jnp.tile(q_g, ATT_HEADS), jnp.tile(k_g, ATT_KV_HEADS)])[None, :]
    return pm, gain, gate_b.reshape(N_GATES, 1)


def kernel(x, c, ctx, c_ctx, ada_w, ada_b, norm_g, even_w_in, even_conv, even_w_out, odd_w_in, odd_gate_b,
           odd_q_g, odd_k_g, odd_sink, odd_w_out, ffn_w_up, ffn_conv, ffn_w_down):
    B, L, _ = x.shape
    C = ctx.shape[1]
    depth = ada_w.shape[0]
    assert depth == 2 and L % (DFT_N1 * SUBLANES) == 0 and C % ML_CHUNK == 0

    cv = jnp.concatenate([c, c_ctx[None, :], jnp.zeros((SUBLANES - B - 1, D_MODEL), F32)], axis=0)
    mod, w_in0, w_out0 = _modulation(cv, ada_w, ada_b, ((even_w_in, 0, EVEN_IN, 3), (even_w_out, 0, D_MODEL, 3)))
    lat, cx = None, B

    tm = min(512, L)
    tc = _channel_dft_table()
    first = ((ffn_w_up, 0, 2 * D_FF, 1), (ffn_w_down, 0, D_MODEL, 1))
    later = ((ffn_w_up, 1, 2 * D_FF, 1), (ffn_w_down, 1, D_MODEL, 2), (odd_w_in, 0, OG, 1),
             (odd_w_out, 0, D_MODEL, 1))

    ng00, ng01 = norm_g[0, 0][None, :], norm_g[0, 1][None, :]

    yc, zr, zi, w_up0, w_down0 = _even_in(x, mod, 0, lat, ng00, w_in0, even_conv[0], tc, tm, L // DFT_N1, first)
    xl, w_up1, w_down1, w_in1, w_out1 = _mix_ffn(x, (yc, _seq_dft(zr, zi, tm)), mod, 0, lat, ng01, w_out0, w_up0,
                                                 ffn_conv, w_down0, tm, odd=False, side=later)
    yc, zr, zi = _even_in(ctx, mod, 0, cx, ng00, w_in0, even_conv[0], tc, C, None)
    xc = _mix_ffn(ctx, (yc, _dense_seq_dft(zr, zi)), mod, 0, cx, ng01, w_out0, w_up0, ffn_conv, w_down0, C, odd=False)

    pm, gain, gb = _odd_tables(odd_gate_b[0], odd_q_g[0], odd_k_g[0])
    wg = odd_w_in[0][:, OG:].T
    ng10, ng11 = norm_g[1, 0][None, :], norm_g[1, 1][None, :]
    cos, sin = _rope_tables(L)
    one, nil = jnp.ones((C, LANES), F32), jnp.zeros((C, LANES), F32)
    qt, kv, vt, qm, kmt, vm, om, gc, gr = _odd_in(xl, mod, 1, lat, ng10, w_in1, wg, pm, gain, cos, sin, gb, tm)
    _, kvx, vtx, qmx, kmtx, vmx, _, gcx, grx = _odd_in(xc, mod, 1, cx, ng10, w_in1, wg, pm, gain, one, nil, gb, C)

    nchains = B * 2 * ML_HEADS
    s0 = jnp.zeros((nchains, ML_QK_DIM, 2 * ML_V_DIM), F32)
    m0 = jnp.zeros((nchains, LANES), F32)
    s1, m1 = _mlstm(qmx, kmtx, vmx, gcx, grx, s0, m0, with_output=False)
    att, hf, hb = _att_mlstm(qt, kv, kvx, vt, vtx, odd_sink[0], qm, kmt, vm, gc, gr, s1, m1)
    return _mix_ffn(xl, (att, hf, hb, om), mod, 1, lat, ng11, w_out1, w_up1, ffn_conv, w_down1, tm, odd=True)
```

```python
import functools

import numpy as np
import jax
import jax.numpy as jnp
from jax import lax
from jax.experimental import pallas as pl
from jax.experimental.pallas import tpu as pltpu

F32 = jnp.float32
BF16 = jnp.bfloat16

D_MODEL = 1024
GRID_W = 64
EPS = 1e-6
SC_CH = 512
FT_CH = 512
FT_GROUPS = 4
FT_GROUP_CH = FT_CH // FT_GROUPS
EVEN_IN = 3 * SC_CH + FT_CH
ATT_HEADS = 8
ATT_KV_HEADS = 2
HEAD_DIM = 64
ATT_SCALE = HEAD_DIM ** -0.5
WINDOW = 128
BLOCK = 128
ROPE_THETA = 10000.0
ML_HEADS = 4
ML_QK_DIM = 64
ML_V_DIM = 128
ATT_Q = ATT_HEADS * HEAD_DIM
ATT_KV = ATT_KV_HEADS * HEAD_DIM
ML_QK = ML_HEADS * ML_QK_DIM
ML_V = ML_HEADS * ML_V_DIM
D_FF = 2816

LANES = 128
SUBLANES = 8
VMEM_LIMIT_BYTES = 56 * 1024 * 1024

DFT_N1 = 128
FF_CHUNK = 256
N_FF_CHUNKS = D_FF // FF_CHUNK
ML_CHUNK = 128
ATT_QB = 2
NEG_BIG = -1e30
LOG2E = 1.4426950408889634

OQ = 0
OK_ = OQ + ATT_Q
OV = OK_ + ATT_KV
OQM = OV + ATT_KV
OKM = OQM + ML_QK
OVM = OKM + ML_QK
OOM = OVM + ML_V
OG = OOM + ML_V
N_GATES = 4 * ML_HEADS
assert WINDOW == BLOCK and ATT_KV == LANES and OG % LANES == 0 and N_GATES == 2 * SUBLANES


def _cparams(sem):
    return pltpu.CompilerParams(dimension_semantics=sem, vmem_limit_bytes=VMEM_LIMIT_BYTES)


def _sigmoid(x):
    return 1.0 / (1.0 + jnp.exp(-x))


def _norm_mod(x, g, shift, scale):
    y = x * lax.rsqrt(jnp.mean(x * x, axis=-1, keepdims=True) + EPS)
    return y * g * (1.0 + scale) + shift


def _mod_vec(mod_ref, k, mrow):
    r = pl.program_id(0) if mrow is None else mrow
    return mod_ref[k, pl.ds(r, 1), :]


def _mod_spec(layer):
    return pl.BlockSpec((None, 6, SUBLANES, D_MODEL), lambda *_: (layer, 0, 0, 0))


def _halo_rows(x, xn, xp, shift, scale, ng_ref):
    g = ng_ref[...]
    parts = [_norm_mod(r, g, shift, scale) for r in (x, xn, xp)]
    return jnp.concatenate(parts, axis=0).astype(BF16)


def _halo_valid(tm, i, nt):
    row = lax.broadcasted_iota(jnp.int32, (tm + 2 * SUBLANES, 1), 0)
    return ((row < tm) | ((row < tm + SUBLANES) & (i < nt - 1)) | ((row >= tm + SUBLANES) & (i > 0)))


def _conv3(v, cw, tm):
    n = v.shape[0]
    vp = pltpu.roll(v, 1, 0)[:tm]
    vn = pltpu.roll(v, n - 1, 0)[:tm]
    return vp * cw[0:1] + v[:tm] * cw[1:2] + vn * cw[2:3]


def _halo_specs(tm, L):
    hb = tm // SUBLANES
    last = L // SUBLANES - 1
    return [
        pl.BlockSpec((1, tm, D_MODEL), lambda b, i: (b, i, 0)),
        pl.BlockSpec((1, SUBLANES, D_MODEL), lambda b, i: (b, jnp.minimum((i + 1) * hb, last), 0)),
        pl.BlockSpec((1, SUBLANES, D_MODEL), lambda b, i: (b, jnp.maximum(i * hb - 1, 0), 0)),
    ]


def _const_spec(shape):
    nd = len(shape)
    return pl.BlockSpec(shape, lambda *_: (0,) * nd)


def _resident_spec(shape):
    nd = len(shape)
    return pl.BlockSpec(shape, lambda *_: (0,) * nd, pipeline_mode=pl.Buffered(1))


def _weight_spec(w, shape, layer):
    return _resident_spec(shape) if w.ndim == len(shape) else _layer_spec(shape, layer)


def _layer_spec(shape, layer):
    nd = len(shape)
    return pl.BlockSpec((None,) + tuple(shape), lambda *_: (layer,) + (0,) * nd, pipeline_mode=pl.Buffered(1))


def _cast_refs(srcs, dsts):
    for src, dst in zip(srcs, dsts):
        w = src[...].astype(BF16)
        dst[...] = w[..., :dst.shape[-1]]


def _layer_rows_spec(w, layer, rb, imap):
    if w.ndim == 2:
        return pl.BlockSpec((rb, w.shape[1]), lambda *a: (imap(*a), 0))
    return pl.BlockSpec((None, rb, w.shape[2]), lambda *a: (layer, imap(*a), 0))


def _side_cast_specs(side, steps, step_of):
    ins, outs, shapes = [], [], []
    for w, wl, wd, per in side:
        r = w.shape[-2]
        rb = r * per // steps
        blk = lambda *a, per=per: step_of(*a) // per
        ins.append(_layer_rows_spec(w, wl, rb, blk))
        outs.append(pl.BlockSpec((rb, wd), lambda *a, blk=blk: (blk(*a), 0)))
        shapes.append(jax.ShapeDtypeStruct((r, wd), BF16))
    return ins, outs, shapes


def _split_dot(x, p):
    hi = x.astype(BF16).astype(F32)
    y = jnp.dot(jnp.concatenate([hi, x - hi], axis=0).astype(BF16), p, preferred_element_type=F32)
    return y[:x.shape[0]] + y[x.shape[0]:]


def _mod_kernel(cv_ref, w_ref, b_ref, *rest):
    nside = (len(rest) - 1) // 2
    o_ref = rest[nside]
    _cast_refs(rest[:nside], rest[nside + 1:])
    cv = cv_ref[...]
    o_ref[0, 0] = _split_dot(cv * _sigmoid(cv), w_ref[0].astype(BF16)) + b_ref[0, 0]


def _modulation(cv, ada_w, ada_b, side=()):
    depth, _, n = ada_w.shape
    nv = n // D_MODEL
    side_in, side_out, side_shape = _side_cast_specs(side, depth * nv, lambda l, j: l * nv + j)
    return pl.pallas_call(
        _mod_kernel,
        out_shape=(jax.ShapeDtypeStruct((depth, nv, SUBLANES, D_MODEL), F32), *side_shape),
        grid=(depth, nv),
        in_specs=[
            pl.BlockSpec((SUBLANES, D_MODEL), lambda l, j: (0, 0)),
            pl.BlockSpec((1, D_MODEL, D_MODEL), lambda l, j: (l, 0, j)),
            pl.BlockSpec((1, 1, 1, D_MODEL), lambda l, j: (l, j, 0, 0)),
        ] + side_in,
        out_specs=(pl.BlockSpec((1, 1, SUBLANES, D_MODEL), lambda l, j: (l, j, 0, 0)), *side_out),
        compiler_params=_cparams(("arbitrary", "arbitrary")),
        name="modulation",
    )(cv, ada_w, ada_b.reshape(depth, nv, 1, D_MODEL), *[e[0] for e in side])


def _even_in_kernel(x_ref, xn_ref, xp_ref, mod_ref, ng_ref, w_ref, cw_ref, tc_ref, *rest, tm, nt, n2, mrow, nsub):
    nside = (len(rest) - 3) // 2
    yc_ref, zr_ref, zi_ref = rest[nside:nside + 3]
    _cast_refs(rest[:nside], rest[nside + 3:])
    i = pl.program_id(1)
    shift, scale = _mod_vec(mod_ref, 0, mrow), _mod_vec(mod_ref, 1, mrow)
    tc = tc_ref[...].astype(BF16)
    for s in range(nsub):
        lo = s * tm
        x = x_ref[0, lo:lo + tm]
        xn = xn_ref[0] if s == nsub - 1 else x_ref[0, lo + tm:lo + tm + SUBLANES]
        xp = xp_ref[0] if s == 0 else x_ref[0, lo - SUBLANES:lo]
        hh = _halo_rows(x, xn, xp, shift, scale, ng_ref)
        u = jnp.dot(hh, w_ref[...], preferred_element_type=F32)
        v = u[:, SC_CH:2 * SC_CH] * u[:, 2 * SC_CH:3 * SC_CH]
        v = jnp.where(_halo_valid(tm, i * nsub + s, nt), v, 0.0)
        yc = u[:tm, :SC_CH] * _conv3(v, cw_ref[...], tm)
        yc_ref[0, lo:lo + tm] = yc.astype(BF16)
        uf = u[:tm, 3 * SC_CH:].astype(BF16)
        for g in range(FT_GROUPS):
            sl = slice(g * FT_GROUP_CH, (g + 1) * FT_GROUP_CH)
            ab = jnp.dot(uf[:, sl], tc, preferred_element_type=F32)
            if n2 is None:
                zr_ref[g, 0, lo:lo + tm] = ab[:, :FT_GROUP_CH]
                zi_ref[g, 0, lo:lo + tm] = ab[:, FT_GROUP_CH:]
            else:
                def regroup(v):
                    return jnp.swapaxes(v.reshape(tm // n2, n2, FT_GROUP_CH), 0, 1).reshape(tm, FT_GROUP_CH)

                zr_ref[g, 0, lo:lo + tm] = regroup(ab[:, :FT_GROUP_CH])
                zi_ref[g, 0, lo:lo + tm] = regroup(ab[:, FT_GROUP_CH:])


def _even_in(x, mod, layer, mrow, ng, w_in, cw, tc, tm, n2, side=()):
    B, L, _ = x.shape
    nt = L // tm
    nsub = 2 if nt % 2 == 0 else 1
    bm = nsub * tm
    out = jax.ShapeDtypeStruct((B, L, FT_CH), BF16)
    zout = jax.ShapeDtypeStruct((FT_GROUPS, B, L, FT_GROUP_CH), F32)
    ospec = pl.BlockSpec((1, bm, FT_CH), lambda b, i: (b, i, 0))
    zspec = pl.BlockSpec((FT_GROUPS, 1, bm, FT_GROUP_CH), lambda b, i: (0, b, i, 0))
    ns = nt // nsub
    side_in, side_out, side_shape = _side_cast_specs(side, B * ns, lambda b, i: b * ns + i)
    return pl.pallas_call(
        functools.partial(_even_in_kernel, tm=tm, nt=nt, n2=n2, mrow=mrow, nsub=nsub),
        out_shape=(out, zout, zout, *side_shape),
        grid=(B, ns),
        in_specs=_halo_specs(bm, L) + [
            _mod_spec(layer),
            _const_spec((1, D_MODEL)),
            _const_spec((D_MODEL, EVEN_IN)),
            _const_spec((3, SC_CH)),
            _const_spec((FT_GROUP_CH, 2 * FT_GROUP_CH)),
        ] + side_in,
        out_specs=(ospec, zspec, zspec, *side_out),
        compiler_params=_cparams(("parallel", "arbitrary")),
        name="even_in",
    )(x, x, x, mod, ng, w_in, cw, tc, *[e[0] for e in side])


def _seq_dft_kernel(zr_ref, zi_ref, m_ref, g_ref, y_ref, o_scr, *, n2, tm):
    m1 = m_ref[...].astype(BF16)
    chunk = tm // n2
    ntile = DFT_N1 // chunk

    def rows(ref, j):
        return [ref[0, 0, t * tm + j * chunk:t * tm + (j + 1) * chunk, :] for t in range(ntile)]

    for j in range(n2):
        z = jnp.concatenate(rows(zr_ref, j) + rows(zi_ref, j), axis=0)
        o_scr[2 * DFT_N1 * j:2 * DFT_N1 * (j + 1), :] = jnp.dot(m1, z.astype(BF16), preferred_element_type=F32)
    for a in range(DFT_N1 // SUBLANES):
        def gather(base):
            x = jnp.concatenate([o_scr[2 * DFT_N1 * j + base + SUBLANES * a:2 * DFT_N1 * j + base + SUBLANES * (a + 1), :]
                                 for j in range(n2)], axis=0)
            return jnp.swapaxes(x.reshape(n2, SUBLANES, LANES), 0, 1)
        xr, xi = gather(0), gather(DFT_N1)
        ys = []
        for r in range(SUBLANES):
            o = jnp.concatenate([xr[r], xi[r]], axis=0).astype(BF16)
            ys.append(jnp.dot(g_ref[SUBLANES * a + r].astype(BF16), o, preferred_element_type=F32))
        y8 = jnp.swapaxes(jnp.stack(ys, axis=0), 0, 1)
        for k2 in range(n2):
            y_ref[0, DFT_N1 * k2 + SUBLANES * a:DFT_N1 * k2 + SUBLANES * (a + 1), :] = y8[k2]


def _dft_tables(L):
    n2 = L // DFT_N1
    k = np.arange(DFT_N1)
    a = 2.0 * np.pi * ((k[:, None] * k[None, :]) % DFT_N1) / DFT_N1
    er, ei = np.cos(a) / np.sqrt(DFT_N1), -np.sin(a) / np.sqrt(DFT_N1)
    m1 = np.block([[er, -ei], [ei, er]])
    k1 = np.arange(DFT_N1)[:, None, None]
    k2 = np.arange(n2)[None, :, None]
    nn = np.arange(n2)[None, None, :]
    th = 2.0 * np.pi * ((nn * (k1 + DFT_N1 * k2)) % L) / L
    g = np.concatenate([np.cos(th), np.sin(th)], axis=-1) / np.sqrt(n2)
    return jnp.asarray(m1, F32), jnp.asarray(g, F32)


def _channel_dft_table():
    k = np.arange(FT_GROUP_CH)
    a = 2.0 * np.pi * ((k[:, None] * k[None, :]) % FT_GROUP_CH) / FT_GROUP_CH
    t = np.concatenate([np.cos(a), -np.sin(a)], axis=1) / np.sqrt(FT_GROUP_CH)
    return jnp.asarray(t, F32)


def _seq_dft(zr, zi, tm):
    G, B, L, C = zr.shape
    n2 = L // DFT_N1
    m1, g = _dft_tables(L)
    zspec = pl.BlockSpec((1, 1, L, C), lambda b, j: (j, b, 0, 0))
    return pl.pallas_call(
        functools.partial(_seq_dft_kernel, n2=n2, tm=tm),
        out_shape=jax.ShapeDtypeStruct((B, L, G * C), F32),
        grid=(B, G),
        in_specs=[zspec, zspec, _const_spec((2 * DFT_N1, 2 * DFT_N1)), _const_spec((DFT_N1, n2, 2 * n2))],
        out_specs=pl.BlockSpec((1, L, C), lambda b, j: (b, 0, j)),
        scratch_shapes=[pltpu.VMEM((2 * DFT_N1 * n2, LANES), F32)],
        compiler_params=_cparams(("parallel", "arbitrary")),
        name="seq_dft",
    )(zr, zi, m1, g)


def _dense_dft_kernel(zr_ref, zi_ref, t_ref, y_ref):
    z = jnp.concatenate([zr_ref[0, 0], zi_ref[0, 0]], axis=0).astype(BF16)
    y_ref[0] = jnp.dot(t_ref[...].astype(BF16), z, preferred_element_type=F32)


def _dense_seq_dft(zr, zi):
    G, B, L, C = zr.shape
    k = np.arange(L)
    a = 2.0 * np.pi * ((k[:, None] * k[None, :]) % L) / L
    t = jnp.asarray(np.concatenate([np.cos(a), np.sin(a)], axis=1) / np.sqrt(L), F32)
    zspec = pl.BlockSpec((1, 1, L, C), lambda b, j: (j, b, 0, 0))
    return pl.pallas_call(
        _dense_dft_kernel,
        out_shape=jax.ShapeDtypeStruct((B, L, G * C), F32),
        grid=(B, G),
        in_specs=[zspec, zspec, _const_spec((L, 2 * L))],
        out_specs=pl.BlockSpec((1, L, C), lambda b, j: (b, 0, j)),
        compiler_params=_cparams(("arbitrary", "arbitrary")),
        name="dense_seq_dft",
    )(zr, zi, t)


HALO = 16


def _wide_halo_specs(tm, L, width):
    hb = tm // HALO
    last = L // HALO - 1
    return [
        pl.BlockSpec((1, tm, width), lambda b, i: (b, i, 0)),
        pl.BlockSpec((1, HALO, width), lambda b, i: (b, jnp.minimum((i + 1) * hb, last), 0)),
        pl.BlockSpec((1, HALO, width), lambda b, i: (b, jnp.maximum(i * hb - 1, 0), 0)),
    ]


def _circ(t_ref, n_ref, p_ref):
    return jnp.concatenate([t_ref[0], n_ref[0], p_ref[0]], axis=0)


def _mix_ffn_kernel(*refs, tm, nt, odd, mrow, nside):
    n_in = 15 if odd else 9
    x3, rest = refs[:3], refs[3:n_in]
    mod_ref, ng_ref, wo_ref, wu_ref, cw_ref, wd_ref = refs[n_in:n_in + 6]
    side_src = refs[n_in + 6:n_in + 6 + nside]
    o_ref = refs[n_in + 6 + nside]
    side_dst = refs[n_in + 7 + nside:n_in + 7 + 2 * nside]
    hh_scr, act_scr = refs[n_in + 7 + 2 * nside:]
    _cast_refs(side_src, side_dst)
    i = pl.program_id(1)
    if odd:
        att, hf, hb, om = (_circ(*rest[k:k + 3]) for k in range(0, 12, 3))
        lhs = jnp.concatenate([att, ((hf + hb) * _sigmoid(om.astype(F32))).astype(BF16)], axis=-1)
    else:
        yc, yf = _circ(*rest[0:3]), _circ(*rest[3:6])
        lhs = jnp.concatenate([yc, yf.astype(BF16)], axis=-1)
    mv = [_mod_vec(mod_ref, k, mrow) for k in range(6)]
    x1 = _circ(*x3) + mv[2] * jnp.dot(lhs, wo_ref[...], preferred_element_type=F32)
    hh_scr[...] = _norm_mod(x1, ng_ref[...], mv[3], mv[4]).astype(BF16)
    row = lax.broadcasted_iota(jnp.int32, (tm + 2 * HALO, 1), 0)
    valid = (row < tm) | ((row < tm + HALO) & (i < nt - 1)) | ((row >= tm + HALO) & (i > 0))
    for c in range(N_FF_CHUNKS):
        lo = c * FF_CHUNK
        g = jnp.dot(hh_scr[...], wu_ref[:, lo:lo + FF_CHUNK], preferred_element_type=F32)
        g = jnp.where(valid, g, 0.0)
        cv = _conv3(g, cw_ref[:, lo:lo + FF_CHUNK], tm)
        val = jnp.dot(hh_scr[:tm, :], wu_ref[:, D_FF + lo:D_FF + lo + FF_CHUNK], preferred_element_type=F32)
        act_scr[:, lo:lo + FF_CHUNK] = (cv * _sigmoid(cv) * val).astype(BF16)
    y = jnp.dot(act_scr[...], wd_ref[...], preferred_element_type=F32)
    o_ref[0] = x1[:tm] + mv[5] * y


def _mix_ffn(x, mixed, mod, layer, mrow, ng, w_out, w_up, cw, w_down, tm, odd, side=()):
    B, L, _ = x.shape
    nt = L // tm
    specs = _wide_halo_specs(tm, L, D_MODEL)
    args = [x, x, x]
    for a in mixed:
        specs += _wide_halo_specs(tm, L, a.shape[-1])
        args += [a, a, a]
    side_in, side_out, side_shape = _side_cast_specs(side, B * nt, lambda b, i: b * nt + i)
    out = pl.pallas_call(
        functools.partial(_mix_ffn_kernel, tm=tm, nt=nt, odd=odd, mrow=mrow, nside=len(side)),
        out_shape=(jax.ShapeDtypeStruct(x.shape, F32), *side_shape),
        grid=(B, nt),
        in_specs=specs + [
            _mod_spec(layer),
            _const_spec((1, D_MODEL)),
            _resident_spec((D_MODEL, D_MODEL)),
            _weight_spec(w_up, (D_MODEL, 2 * D_FF), layer),
            _layer_spec((3, D_FF), layer),
            _weight_spec(w_down, (D_FF, D_MODEL), layer),
        ] + side_in,
        out_specs=(pl.BlockSpec((1, tm, D_MODEL), lambda b, i: (b, i, 0)), *side_out),
        scratch_shapes=[pltpu.VMEM((tm + 2 * HALO, D_MODEL), BF16),
                        pltpu.VMEM((tm, D_FF), BF16)],
        compiler_params=_cparams(("parallel", "arbitrary")),
        name="odd_mix_ffn" if odd else "even_mix_ffn",
    )(*args, mod, ng, w_out, w_up, cw, w_down, *[e[0] for e in side])
    return out if side else out[0]


def _split3(x):
    parts = []
    r = x
    for _ in range(3):
        p = r.astype(BF16)
        parts.append(p)
        r = r - p.astype(F32)
    return parts


def _odd_in_kernel(x_ref, mod_ref, ng_ref, w_ref, wg_ref, pm_ref, gain_ref, cos_ref, sin_ref, gb_ref, tl_ref, tu_ref,
                   qt_ref, kv_ref, vt_ref, qm_ref, kmt_ref, vm_ref, om_ref, gc_ref, gr_ref, *, mrow, tm, nsub):
    for s in range(nsub):
        rs = slice(s * tm, (s + 1) * tm)
        h = _norm_mod(x_ref[0, rs], ng_ref[...], _mod_vec(mod_ref, 0, mrow), _mod_vec(mod_ref, 1, mrow)).astype(BF16)
        u = jnp.dot(h, w_ref[...], preferred_element_type=F32)

        uqk = u[:, OQ:OV]
        sq = (uqk * uqk).astype(BF16)
        pm = pm_ref[...].astype(BF16)
        ms = jnp.concatenate([jnp.dot(sq[:, t * LANES:(t + 1) * LANES], pm, preferred_element_type=F32)
                              for t in range((OV - OQ) // LANES)], axis=1)
        rn = uqk * lax.rsqrt(ms + EPS) * gain_ref[...]
        lane = lax.broadcasted_iota(jnp.int32, (1, LANES), 1)
        first = (lane % 32) < 16
        cos = cos_ref[rs, :]
        sin = sin_ref[rs, :]
        roped = []
        for t in range((OV - OQ) // LANES):
            xt = rn[:, t * LANES:(t + 1) * LANES]
            sw = jnp.where(first, pltpu.roll(xt, LANES - 16, 1), pltpu.roll(xt, 16, 1))
            roped.append(xt * cos + sw * sin)
        for t in range(ATT_Q // LANES):
            qt_ref[0, t * LANES:(t + 1) * LANES, rs] = (roped[t] * (ATT_SCALE * LOG2E)).T.astype(BF16)
        k = roped[ATT_Q // LANES]
        v = u[:, OV:OQM]
        half = LANES // 2
        kv_ref[0, rs, 0:LANES] = k.astype(BF16)
        kv_ref[0, rs, LANES:2 * LANES] = pltpu.roll(k, half, 1).astype(BF16)
        vt_ref[0, :, rs] = v.T.astype(BF16)

        qm_ref[0, rs] = u[:, OQM:OKM].astype(BF16)
        for p in range(ML_QK // LANES):
            km = u[:, OKM + p * LANES:OKM + (p + 1) * LANES] * (ML_QK_DIM ** -0.5)
            kmt_ref[0, p * LANES:(p + 1) * LANES, rs] = km.T.astype(BF16)
        vm_ref[0, rs] = u[:, OVM:OOM].astype(BF16)
        om_ref[0, rs] = u[:, OOM:OG].astype(BF16)

        gt = lax.dot_general(wg_ref[...].astype(BF16), h, (((1,), (1,)), ((), ())),
                             preferred_element_type=F32) + gb_ref[...]
        row = lax.broadcasted_iota(jnp.int32, (2 * SUBLANES, 1), 0)
        logsig = jnp.minimum(gt, 0.0) - jnp.log(1.0 + jnp.exp(-jnp.abs(gt)))
        parts = jnp.concatenate(_split3(logsig), axis=0)
        tl, tu = tl_ref[...].astype(BF16), tu_ref[...].astype(BF16)

        def chunk_cumsum(tri):
            c = jnp.concatenate([jnp.dot(parts[:, k:k + ML_CHUNK], tri, preferred_element_type=F32)
                                 for k in range(0, tm, ML_CHUNK)], axis=1)
            n = 2 * SUBLANES
            return c[0:n] + c[n:2 * n] + c[2 * n:3 * n]

        cum_f, cum_b = chunk_cumsum(tu), chunk_cumsum(tl)
        sel = row % 8
        gr = jnp.where(sel < 4, gt, jnp.where(row < SUBLANES, cum_f, cum_b)) * LOG2E
        gr_ref[0, :, rs] = gr
        gc_ref[0, rs] = jnp.concatenate([gr, jnp.zeros((LANES - 2 * SUBLANES, tm), F32)], axis=0).T


def _chunk_tri():
    i = np.arange(ML_CHUNK)
    return jnp.asarray(i[None, :] <= i[:, None], F32), jnp.asarray(i[None, :] >= i[:, None], F32)


def _odd_in(x, mod, layer, mrow, ng, w, wg, pm, gain, cos, sin, gb, tm):
    B, L, _ = x.shape
    nsub = 2 if L % (2 * tm) == 0 else 1
    bm = nsub * tm

    def rows(c, dt=BF16):
        return jax.ShapeDtypeStruct((B, L, c), dt), pl.BlockSpec((1, bm, c), lambda b, i: (b, i, 0))

    def cols(c, dt=BF16):
        return jax.ShapeDtypeStruct((B, c, L), dt), pl.BlockSpec((1, c, bm), lambda b, i: (b, 0, i))

    outs = [cols(ATT_Q), rows(2 * LANES), cols(LANES), rows(ML_QK), cols(ML_QK), rows(ML_V), rows(ML_V),
            rows(LANES, F32), cols(2 * SUBLANES, F32)]
    nqk = OV - OQ
    tl, tu = _chunk_tri()
    return pl.pallas_call(
        functools.partial(_odd_in_kernel, mrow=mrow, tm=tm, nsub=nsub),
        out_shape=tuple(o[0] for o in outs),
        grid=(B, L // bm),
        in_specs=[
            pl.BlockSpec((1, bm, D_MODEL), lambda b, i: (b, i, 0)),
            _mod_spec(layer),
            _const_spec((1, D_MODEL)),
            _const_spec((D_MODEL, OG)),
            _const_spec((N_GATES, D_MODEL)),
            _const_spec((LANES, LANES)),
            _const_spec((1, nqk)),
            pl.BlockSpec((bm, LANES), lambda b, i: (i, 0)),
            pl.BlockSpec((bm, LANES), lambda b, i: (i, 0)),
            _const_spec((N_GATES, 1)),
            _const_spec((ML_CHUNK, ML_CHUNK)),
            _const_spec((ML_CHUNK, ML_CHUNK)),
        ],
        out_specs=tuple(o[1] for o in outs),
        compiler_params=_cparams(("parallel", "arbitrary")),
        name="odd_in",
    )(x, mod, ng, w, wg, pm, gain, cos, sin, gb, tl, tu)


def _rope_tables(L):
    rows = L // GRID_W
    pos = np.stack([np.repeat(np.arange(rows), GRID_W), np.tile(np.arange(GRID_W), rows)]).astype(np.float64)
    axis_dim = HEAD_DIM // 2
    inv_freq = ROPE_THETA ** (-np.arange(0, axis_dim, 2, dtype=np.float64) / axis_dim)
    ang = pos[:, :, None] * inv_freq
    c, sn = np.cos(ang), np.sin(ang)
    cos = np.concatenate([c[0], c[0], c[1], c[1]], axis=-1)
    sin = np.concatenate([-sn[0], sn[0], -sn[1], sn[1]], axis=-1)
    return jnp.asarray(np.tile(cos, (1, 2)), F32), jnp.asarray(np.tile(sin, (1, 2)), F32)


def _att_steps(n, sink_ref, qt_ref, kvp_ref, kvc_ref, kvn_ref, kvx_ref, vtp_ref, vtc_ref, vtn_ref, vtx_ref,
               o_ref, *, nb):
    half = LANES // 2
    lo = lax.broadcasted_iota(jnp.int32, (1, LANES), 1) < half
    zero = jnp.zeros((), BF16)
    cx = kvx_ref.shape[1]
    klocal = ([kvp_ref[0]] + [kvc_ref[0, i * BLOCK:(i + 1) * BLOCK] for i in range(ATT_QB)] + [kvn_ref[0]])
    vlocal = ([vtp_ref[0]] + [vtc_ref[0, :, i * BLOCK:(i + 1) * BLOCK] for i in range(ATT_QB)] + [vtn_ref[0]])
    kctx = [kvx_ref[0, i:i + BLOCK] for i in range(0, cx, BLOCK)]
    vctx = [vtx_ref[0, :, i:i + BLOCK] for i in range(0, cx, BLOCK)]
    ones = jnp.ones((half, BLOCK), BF16)

    kj = lax.broadcasted_iota(jnp.int32, (BLOCK, BLOCK), 0)
    qi = lax.broadcasted_iota(jnp.int32, (BLOCK, BLOCK), 1)

    def twice(x):
        return jnp.concatenate([x, x], axis=1)

    left = lax.broadcasted_iota(jnp.int32, (1, 2 * BLOCK), 1) < BLOCK
    group = ATT_HEADS // ATT_KV_HEADS

    def block(qb):
        blk_id = n * ATT_QB + qb
        kblocks = klocal[qb:qb + 3] + kctx
        vblocks = vlocal[qb:qb + 3] + vctx
        bias = [twice(jnp.where((kj >= qi) & (blk_id > 0), 0.0, NEG_BIG)).astype(BF16), None,
                twice(jnp.where((kj <= qi) & (blk_id < nb - 1), 0.0, NEG_BIG)).astype(BF16)] + [None] * len(kctx)
        for kvh in range(ATT_KV_HEADS):
            t0 = kvh * group // 2
            cols = slice(qb * BLOCK, (qb + 1) * BLOCK)
            qt2 = jnp.concatenate([qt_ref[0, t0 * LANES:(t0 + 1) * LANES, cols],
                                   qt_ref[0, (t0 + 1) * LANES:(t0 + 2) * LANES, cols]], axis=1)
            outs = []
            for par in range(2):
                sk = jnp.where(left, sink_ref[2 * t0 + par], sink_ref[2 * t0 + 2 + par]) * LOG2E
                m = sk.astype(BF16).astype(F32)
                acc = jnp.zeros((LANES, 2 * BLOCK), F32)
                for blk in range(len(kblocks)):
                    k, ks = kblocks[blk][:, :LANES], kblocks[blk][:, LANES:]
                    kh = ((jnp.where(lo, k, zero), jnp.where(lo, zero, ks)),
                          (jnp.where(lo, ks, zero), jnp.where(lo, zero, k)))[kvh][par]
                    s = jnp.dot(kh, qt2, preferred_element_type=F32)
                    s = s.astype(BF16)
                    if bias[blk] is not None:
                        s = s + bias[blk]
                    m_new = jnp.maximum(m, jnp.max(s, axis=0, keepdims=True).astype(F32))
                    p = jnp.exp2(s - m_new.astype(BF16))
                    vh = jnp.concatenate([vblocks[blk][kvh * half:(kvh + 1) * half], ones], axis=0)
                    acc = acc * jnp.exp2(m - m_new) + jnp.dot(vh, p, preferred_element_type=F32)
                    m = m_new
                l = acc[half:half + 1, :] + jnp.exp2(sk - m)
                outs.append(acc[:half] * (1.0 / l))
            ot = jnp.concatenate(outs, axis=0)
            rows = slice(qb * BLOCK, (qb + 1) * BLOCK)
            o_ref[0, rows, t0 * LANES:(t0 + 1) * LANES] = ot[:, :BLOCK].T.astype(BF16)
            o_ref[0, rows, (t0 + 1) * LANES:(t0 + 2) * LANES] = ot[:, BLOCK:].T.astype(BF16)

    return [functools.partial(block, qb) for qb in range(ATT_QB)]


ML_GROUP = 8


def _mlstm_steps(j, qf_ref, ktf_ref, vf_ref, gcf_ref, grf_ref, qb_ref, ktb_ref, vb_ref, gcb_ref, grb_ref,
                 s0_ref, m0_ref, *rest, T, nb, with_output):
    if with_output:
        hf_ref, hb_ref, s_ref, m_ref = rest
    else:
        s_ref, m_ref = rest

    @pl.when(j == 0)
    def _():
        s_ref[...] = s0_ref[...]
        m_ref[...] = m0_ref[...]

    ti = lax.broadcasted_iota(jnp.int32, (T, T), 0)
    si = lax.broadcasted_iota(jnp.int32, (T, T), 1)
    masks = (si <= ti, si >= ti)
    top = lax.broadcasted_iota(jnp.int32, (LANES, 1), 0) < (LANES // 2)
    zero = jnp.zeros((), BF16)
    ones = jnp.ones((T, ML_V_DIM), BF16)
    zpad = jnp.zeros((ML_QK_DIM, 2 * ML_V_DIM), BF16)
    fwd = (qf_ref, ktf_ref, vf_ref, gcf_ref, grf_ref)
    bwd = (qb_ref, ktb_ref, vb_ref, gcb_ref, grb_ref)
    chains = [(b, d, h) for b in range(nb) for d in range(2) for h in range(ML_HEADS)]

    def group(g0):
        grp = chains[g0:g0 + ML_GROUP]
        st = {}
        for (b, d, h) in grp:
            q_ref, kt_ref, v_ref, gc_ref, gr_ref = fwd if d == 0 else bwd
            r = (b * 2 + d) * ML_HEADS + h
            li, lf = 8 * d + h, 8 * d + 4 + h
            gr = gr_ref[b]
            u_row = gr[li:li + 1, :] - gr[lf:lf + 1, :]
            m_prev = m_ref[r:r + 1, 0:1]
            e = dict(r=r, u_row=u_row, m_prev=m_prev,
                     b_end=gr[lf:lf + 1, (T - 1 if d == 0 else 0):(T if d == 0 else 1)],
                     c_end=jnp.maximum(jnp.max(u_row, axis=1, keepdims=True), m_prev))
            if with_output:
                e["mu"] = jnp.where(masks[d], u_row, NEG_BIG)
                c_col = jnp.maximum(jnp.max(e["mu"], axis=1, keepdims=True), m_prev)
                e["cb"] = jnp.broadcast_to(c_col, (T, LANES))
                e["bb"] = jnp.broadcast_to(gc_ref[b, :, lf:lf + 1], (T, LANES))
            st[(b, d, h)] = e
        for (b, d, h) in grp:
            q_ref, kt_ref, v_ref, gc_ref, gr_ref = fwd if d == 0 else bwd
            e = st[(b, d, h)]
            pair = h // 2
            ktp = kt_ref[b, pair * LANES:(pair + 1) * LANES, :]
            e["kth"] = ktp[(h % 2) * ML_QK_DIM:(h % 2 + 1) * ML_QK_DIM, :]
            e["vext"] = jnp.concatenate([v_ref[b, :, h * ML_V_DIM:(h + 1) * ML_V_DIM], ones], axis=1)
            if with_output:
                qp = q_ref[b, :, pair * LANES:(pair + 1) * LANES]
                kpad = jnp.where(top, ktp, zero) if h % 2 == 0 else jnp.where(top, zero, ktp)
                qk = jnp.dot(qp, kpad, preferred_element_type=F32) * jnp.exp2(e["mu"] - e["cb"])
                qs = qp.astype(F32) * jnp.exp2(e["m_prev"] - e["cb"])
                e["lhs"] = jnp.concatenate([qk.astype(BF16), qs.astype(BF16)], axis=1)
        for (b, d, h) in grp:
            e = st[(b, d, h)]
            e["s_prev"] = s_ref[e["r"]]
            if with_output:
                sb = e["s_prev"].astype(BF16)
                rhs = jnp.concatenate([e["vext"]] + ([sb, zpad] if h % 2 == 0 else [zpad, sb]), axis=0)
                tot = jnp.dot(e["lhs"], rhs, preferred_element_type=F32)
                floor = jnp.exp2(-(e["bb"] + e["cb"]))
                hout = tot[:, :ML_V_DIM] / jnp.maximum(jnp.abs(tot[:, ML_V_DIM:]), floor)
                o_ref = hf_ref if d == 0 else hb_ref
                o_ref[b, :, h * ML_V_DIM:(h + 1) * ML_V_DIM] = hout
        for (b, d, h) in grp:
            e = st[(b, d, h)]
            kw = (e["kth"].astype(F32) * jnp.exp2(e["u_row"] - e["c_end"])).astype(BF16)
            upd = jnp.dot(kw, e["vext"], preferred_element_type=F32)
            s_ref[e["r"]] = jnp.exp2(e["m_prev"] - e["c_end"]) * e["s_prev"] + upd
            m_ref[e["r"]:e["r"] + 1, :] = jnp.broadcast_to(e["b_end"] + e["c_end"], (1, LANES))

    return [functools.partial(group, g0) for g0 in range(0, len(chains), ML_GROUP)]


def _mlstm_kernel(*refs, T, nb, with_output):
    for step in _mlstm_steps(pl.program_id(0), *refs, T=T, nb=nb, with_output=with_output):
        step()


def _att_mlstm_kernel(*refs, T, nb, nblk, ns):
    j = pl.program_id(0)
    sink_ref, att_in, ml_in = refs[0], refs[1:10], refs[10:22]
    o_ref, ml_out = refs[22], refs[23:]
    att = _att_steps(lax.rem(j, ns), sink_ref, *att_in, o_ref, nb=nblk)
    ml = _mlstm_steps(j, *ml_in, *ml_out, T=T, nb=nb, with_output=True)
    for k in range(max(len(att), len(ml))):
        if k < len(att):
            att[k]()
        if k < len(ml):
            ml[k]()


def _mlstm_specs(B, T, nc):
    up = lambda j: j
    down = lambda j: nc - 1 - j

    def specs(o):
        return [pl.BlockSpec((B, T, ML_QK), lambda j: (0, o(j), 0)),
                pl.BlockSpec((B, ML_QK, T), lambda j: (0, 0, o(j))),
                pl.BlockSpec((B, T, ML_V), lambda j: (0, o(j), 0)),
                pl.BlockSpec((B, T, LANES), lambda j: (0, o(j), 0)),
                pl.BlockSpec((B, 2 * SUBLANES, T), lambda j: (0, 0, o(j)))]

    nchains = B * 2 * ML_HEADS
    s_spec = _const_spec((nchains, ML_QK_DIM, 2 * ML_V_DIM))
    m_spec = _const_spec((nchains, LANES))
    state_shape = [jax.ShapeDtypeStruct((nchains, ML_QK_DIM, 2 * ML_V_DIM), F32),
                   jax.ShapeDtypeStruct((nchains, LANES), F32)]
    h_specs = [pl.BlockSpec((B, T, ML_V), lambda j: (0, up(j), 0)),
               pl.BlockSpec((B, T, ML_V), lambda j: (0, down(j), 0))]
    return specs(up) + specs(down) + [s_spec, m_spec], h_specs, [s_spec, m_spec], state_shape


def _att_mlstm(qt, kv, kvx, vt, vtx, sink, qm, kmt, vm, gc, gr, s0, m0):
    B, _, L = qt.shape
    T = ML_CHUNK
    nc = L // T
    nblk = L // BLOCK
    ns = nblk // ATT_QB
    assert nc == B * ns
    cx = kvx.shape[1]
    kw = kv.shape[2]
    vw = vt.shape[1]
    wide = ATT_QB * BLOCK
    smp = lambda j: j // ns
    cur = lambda j: j % ns
    prev = lambda j: jnp.maximum(cur(j) * ATT_QB - 1, 0)
    nxt = lambda j: jnp.minimum((cur(j) + 1) * ATT_QB, nblk - 1)
    att_specs = [
        pl.BlockSpec(memory_space=pltpu.SMEM),
        pl.BlockSpec((1, ATT_Q, wide), lambda j: (smp(j), 0, cur(j))),
        pl.BlockSpec((1, BLOCK, kw), lambda j: (smp(j), prev(j), 0)),
        pl.BlockSpec((1, wide, kw), lambda j: (smp(j), cur(j), 0)),
        pl.BlockSpec((1, BLOCK, kw), lambda j: (smp(j), nxt(j), 0)),
        pl.BlockSpec((1, cx, kw), lambda j: (smp(j), 0, 0)),
        pl.BlockSpec((1, vw, BLOCK), lambda j: (smp(j), 0, prev(j))),
        pl.BlockSpec((1, vw, wide), lambda j: (smp(j), 0, cur(j))),
        pl.BlockSpec((1, vw, BLOCK), lambda j: (smp(j), 0, nxt(j))),
        pl.BlockSpec((1, vw, cx), lambda j: (smp(j), 0, 0)),
    ]
    ml_in, h_specs, st_specs, st_shape = _mlstm_specs(B, T, nc)
    h_shape = jax.ShapeDtypeStruct((B, L, ML_V), F32)
    att, hf, hb, _, _ = pl.pallas_call(
        functools.partial(_att_mlstm_kernel, T=T, nb=B, nblk=nblk, ns=ns),
        out_shape=(jax.ShapeDtypeStruct((B, L, ATT_Q), BF16), h_shape, h_shape, *st_shape),
        grid=(nc,),
        in_specs=att_specs + ml_in,
        out_specs=(pl.BlockSpec((1, wide, ATT_Q), lambda j: (smp(j), cur(j), 0)), *h_specs, *st_specs),
        compiler_params=_cparams(("arbitrary",)),
        name="attention_mlstm",
    )(sink, qt, kv, kv, kv, kvx, vt, vt, vt, vtx, qm, kmt, vm, gc, gr, qm, kmt, vm, gc, gr, s0, m0)
    return att, hf, hb


def _mlstm(qm, kmt, vm, gc, gr, s0, m0, with_output):
    B, L, _ = qm.shape
    T = ML_CHUNK
    nc = L // T
    ml_in, h_specs, st_specs, st_shape = _mlstm_specs(B, T, nc)
    out_shape, out_specs = st_shape, st_specs
    if with_output:
        out_shape = [jax.ShapeDtypeStruct((B, L, ML_V), F32)] * 2 + out_shape
        out_specs = h_specs + out_specs
    return pl.pallas_call(
        functools.partial(_mlstm_kernel, T=T, nb=B, with_output=with_output),
        out_shape=tuple(out_shape),
        grid=(nc,),
        in_specs=ml_in,
        out_specs=tuple(out_specs),
        compiler_params=_cparams(("arbitrary",)),
        name="mlstm_scan" if with_output else "mlstm_context_state",
    )(qm, kmt, vm, gc, gr, qm, kmt, vm, gc, gr, s0, m0)


def _odd_tables(gate_b, q_g, k_g):
    assert sum([ATT_Q, ATT_KV, ATT_KV, ML_QK, ML_QK, ML_V, ML_V]) == OG
    head = np.arange(LANES) // HEAD_DIM
    pm = jnp.asarray((head[:, None] == head[None, :]) / HEAD_DIM, F32)
    gain = jnp.concatenate([jnp.tile(q_g, ATT_HEADS), jnp.tile(k_g, ATT_KV_HEADS)])[None, :]
    return pm, gain, gate_b.reshape(N_GATES, 1)


def kernel(x, c, ctx, c_ctx, ada_w, ada_b, norm_g, even_w_in, even_conv, even_w_out, odd_w_in, odd_gate_b,
           odd_q_g, odd_k_g, odd_sink, odd_w_out, ffn_w_up, ffn_conv, ffn_w_down):
    B, L, _ = x.shape
    C = ctx.shape[1]
    depth = ada_w.shape[0]
    assert depth == 2 and L % (DFT_N1 * SUBLANES) == 0 and C % ML_CHUNK == 0

    cv = jnp.concatenate([c, c_ctx[None, :], jnp.zeros((SUBLANES - B - 1, D_MODEL), F32)], axis=0)
    mod, w_in0, w_out0 = _modulation(cv, ada_w, ada_b, ((even_w_in, 0, EVEN_IN, 3), (even_w_out, 0, D_MODEL, 3)))
    lat, cx = None, B

    tm = min(512, L)
    tc = _channel_dft_table()
    first = ((ffn_w_up, 0, 2 * D_FF, 1), (ffn_w_down, 0, D_MODEL, 1))
    later = ((ffn_w_up, 1, 2 * D_FF, 1), (ffn_w_down, 1, D_MODEL, 2), (odd_w_in, 0, OG, 1),
             (odd_w_out, 0, D_MODEL, 1))

    ng00, ng01 = norm_g[0, 0][None, :], norm_g[0, 1][None, :]

    yc, zr, zi, w_up0, w_down0 = _even_in(x, mod, 0, lat, ng00, w_in0, even_conv[0], tc, tm, L // DFT_N1, first)
    xl, w_up1, w_down1, w_in1, w_out1 = _mix_ffn(x, (yc, _seq_dft(zr, zi, tm)), mod, 0, lat, ng01, w_out0, w_up0,
                                                 ffn_conv, w_down0, tm, odd=False, side=later)
    yc, zr, zi = _even_in(ctx, mod, 0, cx, ng00, w_in0, even_conv[0], tc, C, None)
    xc = _mix_ffn(ctx, (yc, _dense_seq_dft(zr, zi)), mod, 0, cx, ng01, w_out0, w_up0, ffn_conv, w_down0, C, odd=False)

    pm, gain, gb = _odd_tables(odd_gate_b[0], odd_q_g[0], odd_k_g[0])
    wg = odd_w_in[0][:, OG:].T
    ng10, ng11 = norm_g[1, 0][None, :], norm_g[1, 1][None, :]
    cos, sin = _rope_tables(L)
    one, nil = jnp.ones((C, LANES), F32), jnp.zeros((C, LANES), F32)
    qt, kv, vt, qm, kmt, vm, om, gc, gr = _odd_in(xl, mod, 1, lat, ng10, w_in1, wg, pm, gain, cos, sin, gb, tm)
    _, kvx, vtx, qmx, kmtx, vmx, _, gcx, grx = _odd_in(xc, mod, 1, cx, ng10, w_in1, wg, pm, gain, one, nil, gb, C)

    nchains = B * 2 * ML_HEADS
    s0 = jnp.zeros((nchains, ML_QK_DIM, 2 * ML_V_DIM), F32)
    m0 = jnp.zeros((nchains, LANES), F32)
    s1, m1 = _mlstm(qmx, kmtx, vmx, gcx, grx, s0, m0, with_output=False)
    att, hf, hb = _att_mlstm(qt, kv, kvx, vt, vtx, odd_sink[0], qm, kmt, vm, gc, gr, s1, m1)
    return _mix_ffn(xl, (att, hf, hb, om), mod, 1, lat, ng11, w_out1, w_up1, ffn_conv, w_down1, tm, odd=True)
```

```python
import functools

import numpy as np
import jax
import jax.numpy as jnp
from jax import lax
from jax.experimental import pallas as pl
from jax.experimental.pallas import tpu as pltpu

F32 = jnp.float32
BF16 = jnp.bfloat16

D_MODEL = 1024
GRID_W = 64
EPS = 1e-6
SC_CH = 512
FT_CH = 512
FT_GROUPS = 4
FT_GROUP_CH = FT_CH // FT_GROUPS
EVEN_IN = 3 * SC_CH + FT_CH
ATT_HEADS = 8
ATT_KV_HEADS = 2
HEAD_DIM = 64
ATT_SCALE = HEAD_DIM ** -0.5
WINDOW = 128
BLOCK = 128
ROPE_THETA = 10000.0
ML_HEADS = 4
ML_QK_DIM = 64
ML_V_DIM = 128
ATT_Q = ATT_HEADS * HEAD_DIM
ATT_KV = ATT_KV_HEADS * HEAD_DIM
ML_QK = ML_HEADS * ML_QK_DIM
ML_V = ML_HEADS * ML_V_DIM
D_FF = 2816

LANES = 128
SUBLANES = 8
VMEM_LIMIT_BYTES = 56 * 1024 * 1024

DFT_N1 = 128
FF_CHUNK = 256
N_FF_CHUNKS = D_FF // FF_CHUNK
ML_CHUNK = 128
ATT_QB = 2
NEG_BIG = -1e30
LOG2E = 1.4426950408889634

OQ = 0
OK_ = OQ + ATT_Q
OV = OK_ + ATT_KV
OQM = OV + ATT_KV
OKM = OQM + ML_QK
OVM = OKM + ML_QK
OOM = OVM + ML_V
OG = OOM + ML_V
N_GATES = 4 * ML_HEADS
assert WINDOW == BLOCK and ATT_KV == LANES and OG % LANES == 0 and N_GATES == 2 * SUBLANES


def _cparams(sem):
    return pltpu.CompilerParams(dimension_semantics=sem, vmem_limit_bytes=VMEM_LIMIT_BYTES)


def _sigmoid(x):
    return 1.0 / (1.0 + jnp.exp(-x))


def _norm_mod(x, g, shift, scale):
    y = x * lax.rsqrt(jnp.mean(x * x, axis=-1, keepdims=True) + EPS)
    return y * g * (1.0 + scale) + shift


def _mod_vec(mod_ref, k, mrow):
    r = pl.program_id(0) if mrow is None else mrow
    return mod_ref[k, pl.ds(r, 1), :]


def _mod_spec(layer):
    return pl.BlockSpec((None, 6, SUBLANES, D_MODEL), lambda *_: (layer, 0, 0, 0))


def _halo_rows(x, xn, xp, shift, scale, ng_ref):
    g = ng_ref[...]
    parts = [_norm_mod(r, g, shift, scale) for r in (x, xn, xp)]
    return jnp.concatenate(parts, axis=0).astype(BF16)


def _halo_valid(tm, i, nt):
    row = lax.broadcasted_iota(jnp.int32, (tm + 2 * SUBLANES, 1), 0)
    return ((row < tm) | ((row < tm + SUBLANES) & (i < nt - 1)) | ((row >= tm + SUBLANES) & (i > 0)))


def _conv3(v, cw, tm):
    n = v.shape[0]
    vp = pltpu.roll(v, 1, 0)[:tm]
    vn = pltpu.roll(v, n - 1, 0)[:tm]
    return vp * cw[0:1] + v[:tm] * cw[1:2] + vn * cw[2:3]


def _halo_specs(tm, L):
    hb = tm // SUBLANES
    last = L // SUBLANES - 1
    return [
        pl.BlockSpec((1, tm, D_MODEL), lambda b, i: (b, i, 0)),
        pl.BlockSpec((1, SUBLANES, D_MODEL), lambda b, i: (b, jnp.minimum((i + 1) * hb, last), 0)),
        pl.BlockSpec((1, SUBLANES, D_MODEL), lambda b, i: (b, jnp.maximum(i * hb - 1, 0), 0)),
    ]


def _const_spec(shape):
    nd = len(shape)
    return pl.BlockSpec(shape, lambda *_: (0,) * nd)


def _resident_spec(shape):
    nd = len(shape)
    return pl.BlockSpec(shape, lambda *_: (0,) * nd, pipeline_mode=pl.Buffered(1))


def _weight_spec(w, shape, layer):
    return _resident_spec(shape) if w.ndim == len(shape) else _layer_spec(shape, layer)


def _layer_spec(shape, layer):
    nd = len(shape)
    return pl.BlockSpec((None,) + tuple(shape), lambda *_: (layer,) + (0,) * nd, pipeline_mode=pl.Buffered(1))


def _cast_refs(srcs, dsts):
    for src, dst in zip(srcs, dsts):
        w = src[...].astype(BF16)
        dst[...] = w[..., :dst.shape[-1]]


def _layer_rows_spec(w, layer, rb, imap):
    if w.ndim == 2:
        return pl.BlockSpec((rb, w.shape[1]), lambda *a: (imap(*a), 0))
    return pl.BlockSpec((None, rb, w.shape[2]), lambda *a: (layer, imap(*a), 0))


def _side_cast_specs(side, steps, step_of):
    ins, outs, shapes = [], [], []
    for w, wl, wd, per in side:
        r = w.shape[-2]
        rb = r * per // steps
        blk = lambda *a, per=per: step_of(*a) // per
        ins.append(_layer_rows_spec(w, wl, rb, blk))
        outs.append(pl.BlockSpec((rb, wd), lambda *a, blk=blk: (blk(*a), 0)))
        shapes.append(jax.ShapeDtypeStruct((r, wd), BF16))
    return ins, outs, shapes


def _split_dot(x, p):
    hi = x.astype(BF16).astype(F32)
    y = jnp.dot(jnp.concatenate([hi, x - hi], axis=0).astype(BF16), p, preferred_element_type=F32)
    return y[:x.shape[0]] + y[x.shape[0]:]


def _mod_kernel(cv_ref, w_ref, b_ref, *rest):
    nside = (len(rest) - 1) // 2
    o_ref = rest[nside]
    _cast_refs(rest[:nside], rest[nside + 1:])
    cv = cv_ref[...]
    o_ref[0, 0] = _split_dot(cv * _sigmoid(cv), w_ref[0].astype(BF16)) + b_ref[0, 0]


def _modulation(cv, ada_w, ada_b, side=()):
    depth, _, n = ada_w.shape
    nv = n // D_MODEL
    side_in, side_out, side_shape = _side_cast_specs(side, depth * nv, lambda l, j: l * nv + j)
    return pl.pallas_call(
        _mod_kernel,
        out_shape=(jax.ShapeDtypeStruct((depth, nv, SUBLANES, D_MODEL), F32), *side_shape),
        grid=(depth, nv),
        in_specs=[
            pl.BlockSpec((SUBLANES, D_MODEL), lambda l, j: (0, 0)),
            pl.BlockSpec((1, D_MODEL, D_MODEL), lambda l, j: (l, 0, j)),
            pl.BlockSpec((1, 1, 1, D_MODEL), lambda l, j: (l, j, 0, 0)),
        ] + side_in,
        out_specs=(pl.BlockSpec((1, 1, SUBLANES, D_MODEL), lambda l, j: (l, j, 0, 0)), *side_out),
        compiler_params=_cparams(("arbitrary", "arbitrary")),
        name="modulation",
    )(cv, ada_w, ada_b.reshape(depth, nv, 1, D_MODEL), *[e[0] for e in side])


def _even_in_kernel(x_ref, xn_ref, xp_ref, mod_ref, ng_ref, w_ref, cw_ref, tc_ref, *rest, tm, nt, n2, mrow, nsub):
    nside = (len(rest) - 3) // 2
    yc_ref, zr_ref, zi_ref = rest[nside:nside + 3]
    _cast_refs(rest[:nside], rest[nside + 3:])
    i = pl.program_id(1)
    shift, scale = _mod_vec(mod_ref, 0, mrow), _mod_vec(mod_ref, 1, mrow)
    tc = tc_ref[...].astype(BF16)
    for s in range(nsub):
        lo = s * tm
        x = x_ref[0, lo:lo + tm]
        xn = xn_ref[0] if s == nsub - 1 else x_ref[0, lo + tm:lo + tm + SUBLANES]
        xp = xp_ref[0] if s == 0 else x_ref[0, lo - SUBLANES:lo]
        hh = _halo_rows(x, xn, xp, shift, scale, ng_ref)
        u = jnp.dot(hh, w_ref[...], preferred_element_type=F32)
        v = u[:, SC_CH:2 * SC_CH] * u[:, 2 * SC_CH:3 * SC_CH]
        v = jnp.where(_halo_valid(tm, i * nsub + s, nt), v, 0.0)
        yc = u[:tm, :SC_CH] * _conv3(v, cw_ref[...], tm)
        yc_ref[0, lo:lo + tm] = yc.astype(BF16)
        uf = u[:tm, 3 * SC_CH:].astype(BF16)
        for g in range(FT_GROUPS):
            sl = slice(g * FT_GROUP_CH, (g + 1) * FT_GROUP_CH)
            ab = jnp.dot(uf[:, sl], tc, preferred_element_type=F32)
            if n2 is None:
                zr_ref[g, 0, lo:lo + tm] = ab[:, :FT_GROUP_CH]
                zi_ref[g, 0, lo:lo + tm] = ab[:, FT_GROUP_CH:]
            else:
                for a in range(tm // n2):
                    dst = pl.ds(lo + a, n2, stride=tm // n2)
                    zr_ref[g, 0, dst, :] = ab[n2 * a:n2 * (a + 1), :FT_GROUP_CH]
                    zi_ref[g, 0, dst, :] = ab[n2 * a:n2 * (a + 1), FT_GROUP_CH:]


def _even_in(x, mod, layer, mrow, ng, w_in, cw, tc, tm, n2, side=()):
    B, L, _ = x.shape
    nt = L // tm
    nsub = 2 if nt % 2 == 0 else 1
    bm = nsub * tm
    out = jax.ShapeDtypeStruct((B, L, FT_CH), BF16)
    zout = jax.ShapeDtypeStruct((FT_GROUPS, B, L, FT_GROUP_CH), F32)
    ospec = pl.BlockSpec((1, bm, FT_CH), lambda b, i: (b, i, 0))
    zspec = pl.BlockSpec((FT_GROUPS, 1, bm, FT_GROUP_CH), lambda b, i: (0, b, i, 0))
    ns = nt // nsub
    side_in, side_out, side_shape = _side_cast_specs(side, B * ns, lambda b, i: b * ns + i)
    return pl.pallas_call(
        functools.partial(_even_in_kernel, tm=tm, nt=nt, n2=n2, mrow=mrow, nsub=nsub),
        out_shape=(out, zout, zout, *side_shape),
        grid=(B, ns),
        in_specs=_halo_specs(bm, L) + [
            _mod_spec(layer),
            _const_spec((1, D_MODEL)),
            _const_spec((D_MODEL, EVEN_IN)),
            _const_spec((3, SC_CH)),
            _const_spec((FT_GROUP_CH, 2 * FT_GROUP_CH)),
        ] + side_in,
        out_specs=(ospec, zspec, zspec, *side_out),
        compiler_params=_cparams(("parallel", "arbitrary")),
        name="even_in",
    )(x, x, x, mod, ng, w_in, cw, tc, *[e[0] for e in side])


def _seq_dft_kernel(zr_ref, zi_ref, m_ref, g_ref, y_ref, o_scr, *, n2, tm):
    m1 = m_ref[...].astype(BF16)
    chunk = tm // n2
    ntile = DFT_N1 // chunk

    def rows(ref, j):
        return [ref[0, 0, t * tm + j * chunk:t * tm + (j + 1) * chunk, :] for t in range(ntile)]

    for j in range(n2):
        z = jnp.concatenate(rows(zr_ref, j) + rows(zi_ref, j), axis=0)
        o_scr[2 * DFT_N1 * j:2 * DFT_N1 * (j + 1), :] = jnp.dot(m1, z.astype(BF16), preferred_element_type=F32)
    for a in range(DFT_N1 // SUBLANES):
        def gather(base):
            x = jnp.concatenate([o_scr[2 * DFT_N1 * j + base + SUBLANES * a:2 * DFT_N1 * j + base + SUBLANES * (a + 1), :]
                                 for j in range(n2)], axis=0)
            return jnp.swapaxes(x.reshape(n2, SUBLANES, LANES), 0, 1)
        xr, xi = gather(0), gather(DFT_N1)
        ys = []
        for r in range(SUBLANES):
            o = jnp.concatenate([xr[r], xi[r]], axis=0).astype(BF16)
            ys.append(jnp.dot(g_ref[SUBLANES * a + r].astype(BF16), o, preferred_element_type=F32))
        y8 = jnp.swapaxes(jnp.stack(ys, axis=0), 0, 1)
        for k2 in range(n2):
            y_ref[0, DFT_N1 * k2 + SUBLANES * a:DFT_N1 * k2 + SUBLANES * (a + 1), :] = y8[k2]


def _dft_tables(L):
    n2 = L // DFT_N1
    k = np.arange(DFT_N1)
    a = 2.0 * np.pi * ((k[:, None] * k[None, :]) % DFT_N1) / DFT_N1
    er, ei = np.cos(a) / np.sqrt(DFT_N1), -np.sin(a) / np.sqrt(DFT_N1)
    m1 = np.block([[er, -ei], [ei, er]])
    k1 = np.arange(DFT_N1)[:, None, None]
    k2 = np.arange(n2)[None, :, None]
    nn = np.arange(n2)[None, None, :]
    th = 2.0 * np.pi * ((nn * (k1 + DFT_N1 * k2)) % L) / L
    g = np.concatenate([np.cos(th), np.sin(th)], axis=-1) / np.sqrt(n2)
    return jnp.asarray(m1, F32), jnp.asarray(g, F32)


def _channel_dft_table():
    k = np.arange(FT_GROUP_CH)
    a = 2.0 * np.pi * ((k[:, None] * k[None, :]) % FT_GROUP_CH) / FT_GROUP_CH
    t = np.concatenate([np.cos(a), -np.sin(a)], axis=1) / np.sqrt(FT_GROUP_CH)
    return jnp.asarray(t, F32)


def _seq_dft(zr, zi, tm):
    G, B, L, C = zr.shape
    n2 = L // DFT_N1
    m1, g = _dft_tables(L)
    zspec = pl.BlockSpec((1, 1, L, C), lambda b, j: (j, b, 0, 0))
    return pl.pallas_call(
        functools.partial(_seq_dft_kernel, n2=n2, tm=tm),
        out_shape=jax.ShapeDtypeStruct((B, L, G * C), F32),
        grid=(B, G),
        in_specs=[zspec, zspec, _const_spec((2 * DFT_N1, 2 * DFT_N1)), _const_spec((DFT_N1, n2, 2 * n2))],
        out_specs=pl.BlockSpec((1, L, C), lambda b, j: (b, 0, j)),
        scratch_shapes=[pltpu.VMEM((2 * DFT_N1 * n2, LANES), F32)],
        compiler_params=_cparams(("parallel", "arbitrary")),
        name="seq_dft",
    )(zr, zi, m1, g)


def _dense_dft_kernel(zr_ref, zi_ref, t_ref, y_ref):
    z = jnp.concatenate([zr_ref[0, 0], zi_ref[0, 0]], axis=0).astype(BF16)
    y_ref[0] = jnp.dot(t_ref[...].astype(BF16), z, preferred_element_type=F32)


def _dense_seq_dft(zr, zi):
    G, B, L, C = zr.shape
    k = np.arange(L)
    a = 2.0 * np.pi * ((k[:, None] * k[None, :]) % L) / L
    t = jnp.asarray(np.concatenate([np.cos(a), np.sin(a)], axis=1) / np.sqrt(L), F32)
    zspec = pl.BlockSpec((1, 1, L, C), lambda b, j: (j, b, 0, 0))
    return pl.pallas_call(
        _dense_dft_kernel,
        out_shape=jax.ShapeDtypeStruct((B, L, G * C), F32),
        grid=(B, G),
        in_specs=[zspec, zspec, _const_spec((L, 2 * L))],
        out_specs=pl.BlockSpec((1, L, C), lambda b, j: (b, 0, j)),
        compiler_params=_cparams(("arbitrary", "arbitrary")),
        name="dense_seq_dft",
    )(zr, zi, t)


HALO = 16


def _wide_halo_specs(tm, L, width):
    hb = tm // HALO
    last = L // HALO - 1
    return [
        pl.BlockSpec((1, tm, width), lambda b, i: (b, i, 0)),
        pl.BlockSpec((1, HALO, width), lambda b, i: (b, jnp.minimum((i + 1) * hb, last), 0)),
        pl.BlockSpec((1, HALO, width), lambda b, i: (b, jnp.maximum(i * hb - 1, 0), 0)),
    ]


def _circ(t_ref, n_ref, p_ref):
    return jnp.concatenate([t_ref[0], n_ref[0], p_ref[0]], axis=0)


def _mix_ffn_kernel(*refs, tm, nt, odd, mrow, nside):
    n_in = 15 if odd else 9
    x3, rest = refs[:3], refs[3:n_in]
    mod_ref, ng_ref, wo_ref, wu_ref, cw_ref, wd_ref = refs[n_in:n_in + 6]
    side_src = refs[n_in + 6:n_in + 6 + nside]
    o_ref = refs[n_in + 6 + nside]
    side_dst = refs[n_in + 7 + nside:n_in + 7 + 2 * nside]
    hh_scr, act_scr = refs[n_in + 7 + 2 * nside:]
    _cast_refs(side_src, side_dst)
    i = pl.program_id(1)
    if odd:
        att, hf, hb, om = (_circ(*rest[k:k + 3]) for k in range(0, 12, 3))
        lhs = jnp.concatenate([att, ((hf + hb) * _sigmoid(om.astype(F32))).astype(BF16)], axis=-1)
    else:
        yc, yf = _circ(*rest[0:3]), _circ(*rest[3:6])
        lhs = jnp.concatenate([yc, yf.astype(BF16)], axis=-1)
    mv = [_mod_vec(mod_ref, k, mrow) for k in range(6)]
    x1 = _circ(*x3) + mv[2] * jnp.dot(lhs, wo_ref[...], preferred_element_type=F32)
    hh_scr[...] = _norm_mod(x1, ng_ref[...], mv[3], mv[4]).astype(BF16)
    row = lax.broadcasted_iota(jnp.int32, (tm + 2 * HALO, 1), 0)
    valid = (row < tm) | ((row < tm + HALO) & (i < nt - 1)) | ((row >= tm + HALO) & (i > 0))
    for c in range(N_FF_CHUNKS):
        lo = c * FF_CHUNK
        g = jnp.dot(hh_scr[...], wu_ref[:, lo:lo + FF_CHUNK], preferred_element_type=F32)
        g = jnp.where(valid, g, 0.0)
        cv = _conv3(g, cw_ref[:, lo:lo + FF_CHUNK], tm)
        val = jnp.dot(hh_scr[:tm, :], wu_ref[:, D_FF + lo:D_FF + lo + FF_CHUNK], preferred_element_type=F32)
        act_scr[:, lo:lo + FF_CHUNK] = (cv * _sigmoid(cv) * val).astype(BF16)
    for c0 in range(0, D_MODEL, D_MODEL // 2):
        cs = slice(c0, c0 + D_MODEL // 2)
        y = jnp.dot(act_scr[...], wd_ref[:, cs], preferred_element_type=F32)
        o_ref[0, :, cs] = x1[:tm, cs] + mv[5][:, cs] * y


def _mix_ffn(x, mixed, mod, layer, mrow, ng, w_out, w_up, cw, w_down, tm, odd, side=()):
    B, L, _ = x.shape
    nt = L // tm
    specs = _wide_halo_specs(tm, L, D_MODEL)
    args = [x, x, x]
    for a in mixed:
        specs += _wide_halo_specs(tm, L, a.shape[-1])
        args += [a, a, a]
    side_in, side_out, side_shape = _side_cast_specs(side, B * nt, lambda b, i: b * nt + i)
    out = pl.pallas_call(
        functools.partial(_mix_ffn_kernel, tm=tm, nt=nt, odd=odd, mrow=mrow, nside=len(side)),
        out_shape=(jax.ShapeDtypeStruct(x.shape, F32), *side_shape),
        grid=(B, nt),
        in_specs=specs + [
            _mod_spec(layer),
            _const_spec((1, D_MODEL)),
            _resident_spec((D_MODEL, D_MODEL)),
            _weight_spec(w_up, (D_MODEL, 2 * D_FF), layer),
            _layer_spec((3, D_FF), layer),
            _weight_spec(w_down, (D_FF, D_MODEL), layer),
        ] + side_in,
        out_specs=(pl.BlockSpec((1, tm, D_MODEL), lambda b, i: (b, i, 0)), *side_out),
        scratch_shapes=[pltpu.VMEM((tm + 2 * HALO, D_MODEL), BF16),
                        pltpu.VMEM((tm, D_FF), BF16)],
        compiler_params=_cparams(("parallel", "arbitrary")),
        name="odd_mix_ffn" if odd else "even_mix_ffn",
    )(*args, mod, ng, w_out, w_up, cw, w_down, *[e[0] for e in side])
    return out if side else out[0]


def _split3(x):
    parts = []
    r = x
    for _ in range(3):
        p = r.astype(BF16)
        parts.append(p)
        r = r - p.astype(F32)
    return parts


def _odd_in_kernel(x_ref, mod_ref, ng_ref, w_ref, wg_ref, pm_ref, gain_ref, cos_ref, sin_ref, gb_ref, tl_ref, tu_ref,
                   qt_ref, kv_ref, vt_ref, qm_ref, kmt_ref, vm_ref, om_ref, gc_ref, gr_ref, *, mrow, tm, nsub):
    for s in range(nsub):
        rs = slice(s * tm, (s + 1) * tm)
        h = _norm_mod(x_ref[0, rs], ng_ref[...], _mod_vec(mod_ref, 0, mrow), _mod_vec(mod_ref, 1, mrow)).astype(BF16)
        u = jnp.dot(h, w_ref[...], preferred_element_type=F32)

        uqk = u[:, OQ:OV]
        sq = (uqk * uqk).astype(BF16)
        pm = pm_ref[...].astype(BF16)
        ms = jnp.concatenate([jnp.dot(sq[:, t * LANES:(t + 1) * LANES], pm, preferred_element_type=F32)
                              for t in range((OV - OQ) // LANES)], axis=1)
        rn = uqk * lax.rsqrt(ms + EPS) * gain_ref[...]
        lane = lax.broadcasted_iota(jnp.int32, (1, LANES), 1)
        first = (lane % 32) < 16
        cos = cos_ref[rs, :]
        sin = sin_ref[rs, :]
        roped = []
        for t in range((OV - OQ) // LANES):
            xt = rn[:, t * LANES:(t + 1) * LANES]
            sw = jnp.where(first, pltpu.roll(xt, LANES - 16, 1), pltpu.roll(xt, 16, 1))
            roped.append(xt * cos + sw * sin)
        for t in range(ATT_Q // LANES):
            qt_ref[0, t * LANES:(t + 1) * LANES, rs] = (roped[t] * (ATT_SCALE * LOG2E)).T.astype(BF16)
        k = roped[ATT_Q // LANES]
        v = u[:, OV:OQM]
        half = LANES // 2
        kv_ref[0, rs, 0:LANES] = k.astype(BF16)
        kv_ref[0, rs, LANES:2 * LANES] = pltpu.roll(k, half, 1).astype(BF16)
        vt_ref[0, :, rs] = v.T.astype(BF16)

        qm_ref[0, rs] = u[:, OQM:OKM].astype(BF16)
        for p in range(ML_QK // LANES):
            km = u[:, OKM + p * LANES:OKM + (p + 1) * LANES] * (ML_QK_DIM ** -0.5)
            kmt_ref[0, p * LANES:(p + 1) * LANES, rs] = km.T.astype(BF16)
        vm_ref[0, rs] = u[:, OVM:OOM].astype(BF16)
        om_ref[0, rs] = u[:, OOM:OG].astype(BF16)

        gt = lax.dot_general(wg_ref[...].astype(BF16), h, (((1,), (1,)), ((), ())),
                             preferred_element_type=F32) + gb_ref[...]
        row = lax.broadcasted_iota(jnp.int32, (2 * SUBLANES, 1), 0)
        logsig = jnp.minimum(gt, 0.0) - jnp.log(1.0 + jnp.exp(-jnp.abs(gt)))
        parts = jnp.concatenate(_split3(logsig), axis=0)
        tl, tu = tl_ref[...].astype(BF16), tu_ref[...].astype(BF16)

        def chunk_cumsum(tri):
            c = jnp.concatenate([jnp.dot(parts[:, k:k + ML_CHUNK], tri, preferred_element_type=F32)
                                 for k in range(0, tm, ML_CHUNK)], axis=1)
            n = 2 * SUBLANES
            return c[0:n] + c[n:2 * n] + c[2 * n:3 * n]

        cum_f, cum_b = chunk_cumsum(tu), chunk_cumsum(tl)
        sel = row % 8
        gr = jnp.where(sel < 4, gt, jnp.where(row < SUBLANES, cum_f, cum_b)) * LOG2E
        gr_ref[0, :, rs] = gr
        gc_ref[0, rs] = jnp.concatenate([gr, jnp.zeros((LANES - 2 * SUBLANES, tm), F32)], axis=0).T


def _chunk_tri():
    i = np.arange(ML_CHUNK)
    return jnp.asarray(i[None, :] <= i[:, None], F32), jnp.asarray(i[None, :] >= i[:, None], F32)


def _odd_in(x, mod, layer, mrow, ng, w, wg, pm, gain, cos, sin, gb, tm):
    B, L, _ = x.shape
    nsub = 2 if L % (2 * tm) == 0 else 1
    bm = nsub * tm

    def rows(c, dt=BF16):
        return jax.ShapeDtypeStruct((B, L, c), dt), pl.BlockSpec((1, bm, c), lambda b, i: (b, i, 0))

    def cols(c, dt=BF16):
        return jax.ShapeDtypeStruct((B, c, L), dt), pl.BlockSpec((1, c, bm), lambda b, i: (b, 0, i))

    outs = [cols(ATT_Q), rows(2 * LANES), cols(LANES), rows(ML_QK), cols(ML_QK), rows(ML_V), rows(ML_V),
            rows(LANES, F32), cols(2 * SUBLANES, F32)]
    nqk = OV - OQ
    tl, tu = _chunk_tri()
    return pl.pallas_call(
        functools.partial(_odd_in_kernel, mrow=mrow, tm=tm, nsub=nsub),
        out_shape=tuple(o[0] for o in outs),
        grid=(B, L // bm),
        in_specs=[
            pl.BlockSpec((1, bm, D_MODEL), lambda b, i: (b, i, 0)),
            _mod_spec(layer),
            _const_spec((1, D_MODEL)),
            _const_spec((D_MODEL, OG)),
            _const_spec((N_GATES, D_MODEL)),
            _const_spec((LANES, LANES)),
            _const_spec((1, nqk)),
            pl.BlockSpec((bm, LANES), lambda b, i: (i, 0)),
            pl.BlockSpec((bm, LANES), lambda b, i: (i, 0)),
            _const_spec((N_GATES, 1)),
            _const_spec((ML_CHUNK, ML_CHUNK)),
            _const_spec((ML_CHUNK, ML_CHUNK)),
        ],
        out_specs=tuple(o[1] for o in outs),
        compiler_params=_cparams(("parallel", "arbitrary")),
        name="odd_in",
    )(x, mod, ng, w, wg, pm, gain, cos, sin, gb, tl, tu)


def _rope_tables(L):
    rows = L // GRID_W
    pos = np.stack([np.repeat(np.arange(rows), GRID_W), np.tile(np.arange(GRID_W), rows)]).astype(np.float64)
    axis_dim = HEAD_DIM // 2
    inv_freq = ROPE_THETA ** (-np.arange(0, axis_dim, 2, dtype=np.float64) / axis_dim)
    ang = pos[:, :, None] * inv_freq
    c, sn = np.cos(ang), np.sin(ang)
    cos = np.concatenate([c[0], c[0], c[1], c[1]], axis=-1)
    sin = np.concatenate([-sn[0], sn[0], -sn[1], sn[1]], axis=-1)
    return jnp.asarray(np.tile(cos, (1, 2)), F32), jnp.asarray(np.tile(sin, (1, 2)), F32)


def _att_steps(n, sink_ref, qt_ref, kvp_ref, kvc_ref, kvn_ref, kvx_ref, vtp_ref, vtc_ref, vtn_ref, vtx_ref,
               o_ref, *, nb):
    half = LANES // 2
    lo = lax.broadcasted_iota(jnp.int32, (1, LANES), 1) < half
    zero = jnp.zeros((), BF16)
    cx = kvx_ref.shape[1]
    klocal = ([kvp_ref[0]] + [kvc_ref[0, i * BLOCK:(i + 1) * BLOCK] for i in range(ATT_QB)] + [kvn_ref[0]])
    vlocal = ([vtp_ref[0]] + [vtc_ref[0, :, i * BLOCK:(i + 1) * BLOCK] for i in range(ATT_QB)] + [vtn_ref[0]])
    kctx = [kvx_ref[0, i:i + BLOCK] for i in range(0, cx, BLOCK)]
    vctx = [vtx_ref[0, :, i:i + BLOCK] for i in range(0, cx, BLOCK)]
    ones = jnp.ones((half, BLOCK), BF16)

    kj = lax.broadcasted_iota(jnp.int32, (BLOCK, BLOCK), 0)
    qi = lax.broadcasted_iota(jnp.int32, (BLOCK, BLOCK), 1)

    def twice(x):
        return jnp.concatenate([x, x], axis=1)

    left = lax.broadcasted_iota(jnp.int32, (1, 2 * BLOCK), 1) < BLOCK
    group = ATT_HEADS // ATT_KV_HEADS

    def block(qb):
        blk_id = n * ATT_QB + qb
        kblocks = klocal[qb:qb + 3] + kctx
        vblocks = vlocal[qb:qb + 3] + vctx
        bias = [twice(jnp.where((kj >= qi) & (blk_id > 0), 0.0, NEG_BIG)).astype(BF16), None,
                twice(jnp.where((kj <= qi) & (blk_id < nb - 1), 0.0, NEG_BIG)).astype(BF16)] + [None] * len(kctx)
        for kvh in range(ATT_KV_HEADS):
            t0 = kvh * group // 2
            cols = slice(qb * BLOCK, (qb + 1) * BLOCK)
            qt2 = jnp.concatenate([qt_ref[0, t0 * LANES:(t0 + 1) * LANES, cols],
                                   qt_ref[0, (t0 + 1) * LANES:(t0 + 2) * LANES, cols]], axis=1)
            outs = []
            for par in range(2):
                sk = jnp.where(left, sink_ref[2 * t0 + par], sink_ref[2 * t0 + 2 + par]) * LOG2E
                m = sk.astype(BF16).astype(F32)
                acc = jnp.zeros((LANES, 2 * BLOCK), F32)
                for blk in range(len(kblocks)):
                    k, ks = kblocks[blk][:, :LANES], kblocks[blk][:, LANES:]
                    kh = ((jnp.where(lo, k, zero), jnp.where(lo, zero, ks)),
                          (jnp.where(lo, ks, zero), jnp.where(lo, zero, k)))[kvh][par]
                    s = jnp.dot(kh, qt2, preferred_element_type=F32)
                    s = s.astype(BF16)
                    if bias[blk] is not None:
                        s = s + bias[blk]
                    m_new = jnp.maximum(m, jnp.max(s, axis=0, keepdims=True).astype(F32))
                    p = jnp.exp2(s - m_new.astype(BF16))
                    vh = jnp.concatenate([vblocks[blk][kvh * half:(kvh + 1) * half], ones], axis=0)
                    acc = acc * jnp.exp2(m - m_new) + jnp.dot(vh, p, preferred_element_type=F32)
                    m = m_new
                l = acc[half:half + 1, :] + jnp.exp2(sk - m)
                outs.append(acc[:half] * (1.0 / l))
            ot = jnp.concatenate(outs, axis=0)
            rows = slice(qb * BLOCK, (qb + 1) * BLOCK)
            o_ref[0, rows, t0 * LANES:(t0 + 1) * LANES] = ot[:, :BLOCK].T.astype(BF16)
            o_ref[0, rows, (t0 + 1) * LANES:(t0 + 2) * LANES] = ot[:, BLOCK:].T.astype(BF16)

    return [functools.partial(block, qb) for qb in range(ATT_QB)]


ML_GROUP = 8


def _mlstm_steps(j, qf_ref, ktf_ref, vf_ref, gcf_ref, grf_ref, qb_ref, ktb_ref, vb_ref, gcb_ref, grb_ref,
                 s0_ref, m0_ref, *rest, T, nb, with_output):
    if with_output:
        hf_ref, hb_ref, s_ref, m_ref = rest
    else:
        s_ref, m_ref = rest

    @pl.when(j == 0)
    def _():
        s_ref[...] = s0_ref[...]
        m_ref[...] = m0_ref[...]

    ti = lax.broadcasted_iota(jnp.int32, (T, T), 0)
    si = lax.broadcasted_iota(jnp.int32, (T, T), 1)
    masks = (si <= ti, si >= ti)
    top = lax.broadcasted_iota(jnp.int32, (LANES, 1), 0) < (LANES // 2)
    zero = jnp.zeros((), BF16)
    ones = jnp.ones((T, ML_V_DIM), BF16)
    zpad = jnp.zeros((ML_QK_DIM, 2 * ML_V_DIM), BF16)
    fwd = (qf_ref, ktf_ref, vf_ref, gcf_ref, grf_ref)
    bwd = (qb_ref, ktb_ref, vb_ref, gcb_ref, grb_ref)
    chains = [(b, d, h) for b in range(nb) for d in range(2) for h in range(ML_HEADS)]

    def group(g0):
        grp = chains[g0:g0 + ML_GROUP]
        st = {}
        for (b, d, h) in grp:
            q_ref, kt_ref, v_ref, gc_ref, gr_ref = fwd if d == 0 else bwd
            r = (b * 2 + d) * ML_HEADS + h
            li, lf = 8 * d + h, 8 * d + 4 + h
            gr = gr_ref[b]
            u_row = gr[li:li + 1, :] - gr[lf:lf + 1, :]
            m_prev = m_ref[r:r + 1, 0:1]
            e = dict(r=r, u_row=u_row, m_prev=m_prev,
                     b_end=gr[lf:lf + 1, (T - 1 if d == 0 else 0):(T if d == 0 else 1)],
                     c_end=jnp.maximum(jnp.max(u_row, axis=1, keepdims=True), m_prev))
            if with_output:
                e["mu"] = jnp.where(masks[d], u_row, NEG_BIG)
                c_col = jnp.maximum(jnp.max(e["mu"], axis=1, keepdims=True), m_prev)
                e["cb"] = jnp.broadcast_to(c_col, (T, LANES))
                e["bb"] = jnp.broadcast_to(gc_ref[b, :, lf:lf + 1], (T, LANES))
            st[(b, d, h)] = e
        for (b, d, h) in grp:
            q_ref, kt_ref, v_ref, gc_ref, gr_ref = fwd if d == 0 else bwd
            e = st[(b, d, h)]
            pair = h // 2
            ktp = kt_ref[b, pair * LANES:(pair + 1) * LANES, :]
            e["kth"] = ktp[(h % 2) * ML_QK_DIM:(h % 2 + 1) * ML_QK_DIM, :]
            e["vext"] = jnp.concatenate([v_ref[b, :, h * ML_V_DIM:(h + 1) * ML_V_DIM], ones], axis=1)
            if with_output:
                qp = q_ref[b, :, pair * LANES:(pair + 1) * LANES]
                kpad = jnp.where(top, ktp, zero) if h % 2 == 0 else jnp.where(top, zero, ktp)
                qk = jnp.dot(qp, kpad, preferred_element_type=F32) * jnp.exp2(e["mu"] - e["cb"])
                qs = qp.astype(F32) * jnp.exp2(e["m_prev"] - e["cb"])
                e["lhs"] = jnp.concatenate([qk.astype(BF16), qs.astype(BF16)], axis=1)
        for (b, d, h) in grp:
            e = st[(b, d, h)]
            e["s_prev"] = s_ref[e["r"]]
            if with_output:
                sb = e["s_prev"].astype(BF16)
                rhs = jnp.concatenate([e["vext"]] + ([sb, zpad] if h % 2 == 0 else [zpad, sb]), axis=0)
                tot = jnp.dot(e["lhs"], rhs, preferred_element_type=F32)
                floor = jnp.exp2(-(e["bb"] + e["cb"]))
                hout = tot[:, :ML_V_DIM] / jnp.maximum(jnp.abs(tot[:, ML_V_DIM:]), floor)
                o_ref = hf_ref if d == 0 else hb_ref
                o_ref[b, :, h * ML_V_DIM:(h + 1) * ML_V_DIM] = hout
        for (b, d, h) in grp:
            e = st[(b, d, h)]
            kw = (e["kth"].astype(F32) * jnp.exp2(e["u_row"] - e["c_end"])).astype(BF16)
            upd = jnp.dot(kw, e["vext"], preferred_element_type=F32)
            s_ref[e["r"]] = jnp.exp2(e["m_prev"] - e["c_end"]) * e["s_prev"] + upd
            m_ref[e["r"]:e["r"] + 1, :] = jnp.broadcast_to(e["b_end"] + e["c_end"], (1, LANES))

    return [functools.partial(group, g0) for g0 in range(0, len(chains), ML_GROUP)]


def _mlstm_kernel(*refs, T, nb, with_output):
    for step in _mlstm_steps(pl.program_id(0), *refs, T=T, nb=nb, with_output=with_output):
        step()


def _att_mlstm_kernel(*refs, T, nb, nblk, ns):
    j = pl.program_id(0)
    sink_ref, att_in, ml_in = refs[0], refs[1:10], refs[10:22]
    o_ref, ml_out = refs[22], refs[23:]
    att = _att_steps(lax.rem(j, ns), sink_ref, *att_in, o_ref, nb=nblk)
    ml = _mlstm_steps(j, *ml_in, *ml_out, T=T, nb=nb, with_output=True)
    for step in att + ml:
        step()


def _mlstm_specs(B, T, nc):
    up = lambda j: j
    down = lambda j: nc - 1 - j

    def specs(o):
        return [pl.BlockSpec((B, T, ML_QK), lambda j: (0, o(j), 0)),
                pl.BlockSpec((B, ML_QK, T), lambda j: (0, 0, o(j))),
                pl.BlockSpec((B, T, ML_V), lambda j: (0, o(j), 0)),
                pl.BlockSpec((B, T, LANES), lambda j: (0, o(j), 0)),
                pl.BlockSpec((B, 2 * SUBLANES, T), lambda j: (0, 0, o(j)))]

    nchains = B * 2 * ML_HEADS
    s_spec = _const_spec((nchains, ML_QK_DIM, 2 * ML_V_DIM))
    m_spec = _const_spec((nchains, LANES))
    state_shape = [jax.ShapeDtypeStruct((nchains, ML_QK_DIM, 2 * ML_V_DIM), F32),
                   jax.ShapeDtypeStruct((nchains, LANES), F32)]
    h_specs = [pl.BlockSpec((B, T, ML_V), lambda j: (0, up(j), 0)),
               pl.BlockSpec((B, T, ML_V), lambda j: (0, down(j), 0))]
    return specs(up) + specs(down) + [s_spec, m_spec], h_specs, [s_spec, m_spec], state_shape


def _att_mlstm(qt, kv, kvx, vt, vtx, sink, qm, kmt, vm, gc, gr, s0, m0):
    B, _, L = qt.shape
    T = ML_CHUNK
    nc = L // T
    nblk = L // BLOCK
    ns = nblk // ATT_QB
    assert nc == B * ns
    cx = kvx.shape[1]
    kw = kv.shape[2]
    vw = vt.shape[1]
    wide = ATT_QB * BLOCK
    smp = lambda j: j // ns
    cur = lambda j: j % ns
    prev = lambda j: jnp.maximum(cur(j) * ATT_QB - 1, 0)
    nxt = lambda j: jnp.minimum((cur(j) + 1) * ATT_QB, nblk - 1)
    att_specs = [
        pl.BlockSpec(memory_space=pltpu.SMEM),
        pl.BlockSpec((1, ATT_Q, wide), lambda j: (smp(j), 0, cur(j))),
        pl.BlockSpec((1, BLOCK, kw), lambda j: (smp(j), prev(j), 0)),
        pl.BlockSpec((1, wide, kw), lambda j: (smp(j), cur(j), 0)),
        pl.BlockSpec((1, BLOCK, kw), lambda j: (smp(j), nxt(j), 0)),
        pl.BlockSpec((1, cx, kw), lambda j: (smp(j), 0, 0)),
        pl.BlockSpec((1, vw, BLOCK), lambda j: (smp(j), 0, prev(j))),
        pl.BlockSpec((1, vw, wide), lambda j: (smp(j), 0, cur(j))),
        pl.BlockSpec((1, vw, BLOCK), lambda j: (smp(j), 0, nxt(j))),
        pl.BlockSpec((1, vw, cx), lambda j: (smp(j), 0, 0)),
    ]
    ml_in, h_specs, st_specs, st_shape = _mlstm_specs(B, T, nc)
    h_shape = jax.ShapeDtypeStruct((B, L, ML_V), F32)
    att, hf, hb, _, _ = pl.pallas_call(
        functools.partial(_att_mlstm_kernel, T=T, nb=B, nblk=nblk, ns=ns),
        out_shape=(jax.ShapeDtypeStruct((B, L, ATT_Q), BF16), h_shape, h_shape, *st_shape),
        grid=(nc,),
        in_specs=att_specs + ml_in,
        out_specs=(pl.BlockSpec((1, wide, ATT_Q), lambda j: (smp(j), cur(j), 0)), *h_specs, *st_specs),
        compiler_params=_cparams(("arbitrary",)),
        name="attention_mlstm",
    )(sink, qt, kv, kv, kv, kvx, vt, vt, vt, vtx, qm, kmt, vm, gc, gr, qm, kmt, vm, gc, gr, s0, m0)
    return att, hf, hb


def _mlstm(qm, kmt, vm, gc, gr, s0, m0, with_output):
    B, L, _ = qm.shape
    T = ML_CHUNK
    nc = L // T
    ml_in, h_specs, st_specs, st_shape = _mlstm_specs(B, T, nc)
    out_shape, out_specs = st_shape, st_specs
    if with_output:
        out_shape = [jax.ShapeDtypeStruct((B, L, ML_V), F32)] * 2 + out_shape
        out_specs = h_specs + out_specs
    return pl.pallas_call(
        functools.partial(_mlstm_kernel, T=T, nb=B, with_output=with_output),
        out_shape=tuple(out_shape),
        grid=(nc,),
        in_specs=ml_in,
        out_specs=tuple(out_specs),
        compiler_params=_cparams(("arbitrary",)),
        name="mlstm_scan" if with_output else "mlstm_context_state",
    )(qm, kmt, vm, gc, gr, qm, kmt, vm, gc, gr, s0, m0)


def _odd_tables(gate_b, q_g, k_g):
    assert sum([ATT_Q, ATT_KV, ATT_KV, ML_QK, ML_QK, ML_V, ML_V]) == OG
    head = np.arange(LANES) // HEAD_DIM
    pm = jnp.asarray((head[:, None] == head[None, :]) / HEAD_DIM, F32)
    gain = jnp.concatenate([jnp.tile(q_g, ATT_HEADS), jnp.tile(k_g, ATT_KV_HEADS)])[None, :]
    return pm, gain, gate_b.reshape(N_GATES, 1)


def kernel(x, c, ctx, c_ctx, ada_w, ada_b, norm_g, even_w_in, even_conv, even_w_out, odd_w_in, odd_gate_b,
           odd_q_g, odd_k_g, odd_sink, odd_w_out, ffn_w_up, ffn_conv, ffn_w_down):
    B, L, _ = x.shape
    C = ctx.shape[1]
    depth = ada_w.shape[0]
    assert depth == 2 and L % (DFT_N1 * SUBLANES) == 0 and C % ML_CHUNK == 0

    cv = jnp.concatenate([c, c_ctx[None, :], jnp.zeros((SUBLANES - B - 1, D_MODEL), F32)], axis=0)
    mod, w_in0, w_out0 = _modulation(cv, ada_w, ada_b, ((even_w_in, 0, EVEN_IN, 3), (even_w_out, 0, D_MODEL, 3)))
    lat, cx = None, B

    tm = min(512, L)
    tc = _channel_dft_table()
    first = ((ffn_w_up, 0, 2 * D_FF, 1), (ffn_w_down, 0, D_MODEL, 1))
    later = ((ffn_w_up, 1, 2 * D_FF, 1), (ffn_w_down, 1, D_MODEL, 2), (odd_w_in, 0, OG, 1),
             (odd_w_out, 0, D_MODEL, 1))

    ng00, ng01 = norm_g[0, 0][None, :], norm_g[0, 1][None, :]

    yc, zr, zi, w_up0, w_down0 = _even_in(x, mod, 0, lat, ng00, w_in0, even_conv[0], tc, tm, L // DFT_N1, first)
    xl, w_up1, w_down1, w_in1, w_out1 = _mix_ffn(x, (yc, _seq_dft(zr, zi, tm)), mod, 0, lat, ng01, w_out0, w_up0,
                                                 ffn_conv, w_down0, tm, odd=False, side=later)
    yc, zr, zi = _even_in(ctx, mod, 0, cx, ng00, w_in0, even_conv[0], tc, C, None)
    xc = _mix_ffn(ctx, (yc, _dense_seq_dft(zr, zi)), mod, 0, cx, ng01, w_out0, w_up0, ffn_conv, w_down0, C, odd=False)

    pm, gain, gb = _odd_tables(odd_gate_b[0], odd_q_g[0], odd_k_g[0])
    wg = odd_w_in[0][:, OG:].T
    ng10, ng11 = norm_g[1, 0][None, :], norm_g[1, 1][None, :]
    cos, sin = _rope_tables(L)
    one, nil = jnp.ones((C, LANES), F32), jnp.zeros((C, LANES), F32)
    qt, kv, vt, qm, kmt, vm, om, gc, gr = _odd_in(xl, mod, 1, lat, ng10, w_in1, wg, pm, gain, cos, sin, gb, tm)
    _, kvx, vtx, qmx, kmtx, vmx, _, gcx, grx = _odd_in(xc, mod, 1, cx, ng10, w_in1, wg, pm, gain, one, nil, gb, C)

    nchains = B * 2 * ML_HEADS
    s0 = jnp.zeros((nchains, ML_QK_DIM, 2 * ML_V_DIM), F32)
    m0 = jnp.zeros((nchains, LANES), F32)
    s1, m1 = _mlstm(qmx, kmtx, vmx, gcx, grx, s0, m0, with_output=False)
    att, hf, hb = _att_mlstm(qt, kv, kvx, vt, vtx, odd_sink[0], qm, kmt, vm, gc, gr, s1, m1)
    return _mix_ffn(xl, (att, hf, hb, om), mod, 1, lat, ng11, w_out1, w_up1, ffn_conv, w_down1, tm, odd=True)
```

```python
import functools

import numpy as np
import jax
import jax.numpy as jnp
from jax import lax
from jax.experimental import pallas as pl
from jax.experimental.pallas import tpu as pltpu

F32 = jnp.float32
BF16 = jnp.bfloat16

D_MODEL = 1024
GRID_W = 64
EPS = 1e-6
SC_CH = 512
FT_CH = 512
FT_GROUPS = 4
FT_GROUP_CH = FT_CH // FT_GROUPS
EVEN_IN = 3 * SC_CH + FT_CH
ATT_HEADS = 8
ATT_KV_HEADS = 2
HEAD_DIM = 64
ATT_SCALE = HEAD_DIM ** -0.5
WINDOW = 128
BLOCK = 128
ROPE_THETA = 10000.0
ML_HEADS = 4
ML_QK_DIM = 64
ML_V_DIM = 128
ATT_Q = ATT_HEADS * HEAD_DIM
ATT_KV = ATT_KV_HEADS * HEAD_DIM
ML_QK = ML_HEADS * ML_QK_DIM
ML_V = ML_HEADS * ML_V_DIM
D_FF = 2816

LANES = 128
SUBLANES = 8
VMEM_LIMIT_BYTES = 56 * 1024 * 1024

DFT_N1 = 128
FF_CHUNK = 256
N_FF_CHUNKS = D_FF // FF_CHUNK
ML_CHUNK = 128
ATT_QB = 2
NEG_BIG = -1e30
LOG2E = 1.4426950408889634

OQ = 0
OK_ = OQ + ATT_Q
OV = OK_ + ATT_KV
OQM = OV + ATT_KV
OKM = OQM + ML_QK
OVM = OKM + ML_QK
OOM = OVM + ML_V
OG = OOM + ML_V
N_GATES = 4 * ML_HEADS
assert WINDOW == BLOCK and ATT_KV == LANES and OG % LANES == 0 and N_GATES == 2 * SUBLANES


def _cparams(sem):
    return pltpu.CompilerParams(dimension_semantics=sem, vmem_limit_bytes=VMEM_LIMIT_BYTES)


def _sigmoid(x):
    return 1.0 / (1.0 + jnp.exp(-x))


def _norm_mod(x, g, shift, scale):
    y = x * lax.rsqrt(jnp.mean(x * x, axis=-1, keepdims=True) + EPS)
    return y * g * (1.0 + scale) + shift


def _mod_vec(mod_ref, k, mrow):
    r = pl.program_id(0) if mrow is None else mrow
    return mod_ref[k, pl.ds(r, 1), :]


def _mod_spec(layer):
    return pl.BlockSpec((None, 6, SUBLANES, D_MODEL), lambda *_: (layer, 0, 0, 0))


def _halo_rows(x, xn, xp, shift, scale, ng_ref):
    g = ng_ref[...]
    parts = [_norm_mod(r, g, shift, scale) for r in (x, xn, xp)]
    return jnp.concatenate(parts, axis=0).astype(BF16)


def _halo_valid(tm, i, nt):
    row = lax.broadcasted_iota(jnp.int32, (tm + 2 * SUBLANES, 1), 0)
    return ((row < tm) | ((row < tm + SUBLANES) & (i < nt - 1)) | ((row >= tm + SUBLANES) & (i > 0)))


def _conv3(v, cw, tm):
    n = v.shape[0]
    vp = pltpu.roll(v, 1, 0)[:tm]
    vn = pltpu.roll(v, n - 1, 0)[:tm]
    return vp * cw[0:1] + v[:tm] * cw[1:2] + vn * cw[2:3]


def _halo_specs(tm, L):
    hb = tm // SUBLANES
    last = L // SUBLANES - 1
    return [
        pl.BlockSpec((1, tm, D_MODEL), lambda b, i: (b, i, 0)),
        pl.BlockSpec((1, SUBLANES, D_MODEL), lambda b, i: (b, jnp.minimum((i + 1) * hb, last), 0)),
        pl.BlockSpec((1, SUBLANES, D_MODEL), lambda b, i: (b, jnp.maximum(i * hb - 1, 0), 0)),
    ]


def _const_spec(shape):
    nd = len(shape)
    return pl.BlockSpec(shape, lambda *_: (0,) * nd)


def _resident_spec(shape):
    nd = len(shape)
    return pl.BlockSpec(shape, lambda *_: (0,) * nd, pipeline_mode=pl.Buffered(1))


def _weight_spec(w, shape, layer):
    return _resident_spec(shape) if w.ndim == len(shape) else _layer_spec(shape, layer)


def _layer_spec(shape, layer):
    nd = len(shape)
    return pl.BlockSpec((None,) + tuple(shape), lambda *_: (layer,) + (0,) * nd, pipeline_mode=pl.Buffered(1))


def _cast_refs(srcs, dsts):
    for src, dst in zip(srcs, dsts):
        w = src[...].astype(BF16)
        dst[...] = w[..., :dst.shape[-1]]


def _layer_rows_spec(w, layer, rb, imap):
    if w.ndim == 2:
        return pl.BlockSpec((rb, w.shape[1]), lambda *a: (imap(*a), 0))
    return pl.BlockSpec((None, rb, w.shape[2]), lambda *a: (layer, imap(*a), 0))


def _side_cast_specs(side, steps, step_of):
    ins, outs, shapes = [], [], []
    for w, wl, wd, per in side:
        r = w.shape[-2]
        rb = r * per // steps
        blk = lambda *a, per=per: step_of(*a) // per
        ins.append(_layer_rows_spec(w, wl, rb, blk))
        outs.append(pl.BlockSpec((rb, wd), lambda *a, blk=blk: (blk(*a), 0)))
        shapes.append(jax.ShapeDtypeStruct((r, wd), BF16))
    return ins, outs, shapes


def _split_dot(x, p):
    hi = x.astype(BF16).astype(F32)
    y = jnp.dot(jnp.concatenate([hi, x - hi], axis=0).astype(BF16), p, preferred_element_type=F32)
    return y[:x.shape[0]] + y[x.shape[0]:]


def _mod_kernel(cv_ref, w_ref, b_ref, *rest):
    nside = (len(rest) - 1) // 2
    o_ref = rest[nside]
    _cast_refs(rest[:nside], rest[nside + 1:])
    cv = cv_ref[...]
    o_ref[0, 0] = _split_dot(cv * _sigmoid(cv), w_ref[0].astype(BF16)) + b_ref[0, 0]


def _modulation(cv, ada_w, ada_b, side=()):
    depth, _, n = ada_w.shape
    nv = n // D_MODEL
    side_in, side_out, side_shape = _side_cast_specs(side, depth * nv, lambda l, j: l * nv + j)
    return pl.pallas_call(
        _mod_kernel,
        out_shape=(jax.ShapeDtypeStruct((depth, nv, SUBLANES, D_MODEL), F32), *side_shape),
        grid=(depth, nv),
        in_specs=[
            pl.BlockSpec((SUBLANES, D_MODEL), lambda l, j: (0, 0)),
            pl.BlockSpec((1, D_MODEL, D_MODEL), lambda l, j: (l, 0, j)),
            pl.BlockSpec((1, 1, 1, D_MODEL), lambda l, j: (l, j, 0, 0)),
        ] + side_in,
        out_specs=(pl.BlockSpec((1, 1, SUBLANES, D_MODEL), lambda l, j: (l, j, 0, 0)), *side_out),
        compiler_params=_cparams(("arbitrary", "arbitrary")),
        name="modulation",
    )(cv, ada_w, ada_b.reshape(depth, nv, 1, D_MODEL), *[e[0] for e in side])


def _even_in_kernel(x_ref, xn_ref, xp_ref, mod_ref, ng_ref, w_ref, cw_ref, tc_ref, *rest, tm, nt, n2, mrow, nsub):
    nside = (len(rest) - 3) // 2
    yc_ref, zr_ref, zi_ref = rest[nside:nside + 3]
    _cast_refs(rest[:nside], rest[nside + 3:])
    i = pl.program_id(1)
    shift, scale = _mod_vec(mod_ref, 0, mrow), _mod_vec(mod_ref, 1, mrow)
    tc = tc_ref[...].astype(BF16)
    for s in range(nsub):
        lo = s * tm
        x = x_ref[0, lo:lo + tm]
        xn = xn_ref[0] if s == nsub - 1 else x_ref[0, lo + tm:lo + tm + SUBLANES]
        xp = xp_ref[0] if s == 0 else x_ref[0, lo - SUBLANES:lo]
        hh = _halo_rows(x, xn, xp, shift, scale, ng_ref)
        def proj(k, rows):
            return jnp.dot(hh[:rows], w_ref[:, k * SC_CH:(k + 1) * SC_CH], preferred_element_type=F32)

        v = proj(1, tm + 2 * SUBLANES) * proj(2, tm + 2 * SUBLANES)
        v = jnp.where(_halo_valid(tm, i * nsub + s, nt), v, 0.0)
        yc = proj(0, tm) * _conv3(v, cw_ref[...], tm)
        yc_ref[0, lo:lo + tm] = yc.astype(BF16)
        uf = proj(3, tm).astype(BF16)
        for g in range(FT_GROUPS):
            sl = slice(g * FT_GROUP_CH, (g + 1) * FT_GROUP_CH)
            ab = jnp.dot(uf[:, sl], tc, preferred_element_type=F32)
            if n2 is None:
                zr_ref[g, 0, lo:lo + tm] = ab[:, :FT_GROUP_CH]
                zi_ref[g, 0, lo:lo + tm] = ab[:, FT_GROUP_CH:]
            else:
                for a in range(tm // n2):
                    dst = pl.ds(lo + a, n2, stride=tm // n2)
                    zr_ref[g, 0, dst, :] = ab[n2 * a:n2 * (a + 1), :FT_GROUP_CH]
                    zi_ref[g, 0, dst, :] = ab[n2 * a:n2 * (a + 1), FT_GROUP_CH:]


def _even_in(x, mod, layer, mrow, ng, w_in, cw, tc, tm, n2, side=()):
    B, L, _ = x.shape
    nt = L // tm
    nsub = 2 if nt % 2 == 0 else 1
    bm = nsub * tm
    out = jax.ShapeDtypeStruct((B, L, FT_CH), BF16)
    zout = jax.ShapeDtypeStruct((FT_GROUPS, B, L, FT_GROUP_CH), F32)
    ospec = pl.BlockSpec((1, bm, FT_CH), lambda b, i: (b, i, 0))
    zspec = pl.BlockSpec((FT_GROUPS, 1, bm, FT_GROUP_CH), lambda b, i: (0, b, i, 0))
    ns = nt // nsub
    side_in, side_out, side_shape = _side_cast_specs(side, B * ns, lambda b, i: b * ns + i)
    return pl.pallas_call(
        functools.partial(_even_in_kernel, tm=tm, nt=nt, n2=n2, mrow=mrow, nsub=nsub),
        out_shape=(out, zout, zout, *side_shape),
        grid=(B, ns),
        in_specs=_halo_specs(bm, L) + [
            _mod_spec(layer),
            _const_spec((1, D_MODEL)),
            _const_spec((D_MODEL, EVEN_IN)),
            _const_spec((3, SC_CH)),
            _const_spec((FT_GROUP_CH, 2 * FT_GROUP_CH)),
        ] + side_in,
        out_specs=(ospec, zspec, zspec, *side_out),
        compiler_params=_cparams(("parallel", "arbitrary")),
        name="even_in",
    )(x, x, x, mod, ng, w_in, cw, tc, *[e[0] for e in side])


def _seq_dft_kernel(zr_ref, zi_ref, m_ref, g_ref, y_ref, o_scr, *, n2, tm):
    m1 = m_ref[...].astype(BF16)
    chunk = tm // n2
    ntile = DFT_N1 // chunk

    def rows(ref, j):
        return [ref[0, 0, t * tm + j * chunk:t * tm + (j + 1) * chunk, :] for t in range(ntile)]

    for j in range(n2):
        z = jnp.concatenate(rows(zr_ref, j) + rows(zi_ref, j), axis=0)
        o_scr[2 * DFT_N1 * j:2 * DFT_N1 * (j + 1), :] = jnp.dot(m1, z.astype(BF16), preferred_element_type=F32)
    for a in range(DFT_N1 // SUBLANES):
        def gather(base):
            x = jnp.concatenate([o_scr[2 * DFT_N1 * j + base + SUBLANES * a:2 * DFT_N1 * j + base + SUBLANES * (a + 1), :]
                                 for j in range(n2)], axis=0)
            return jnp.swapaxes(x.reshape(n2, SUBLANES, LANES), 0, 1)
        xr, xi = gather(0), gather(DFT_N1)
        ys = []
        for r in range(SUBLANES):
            o = jnp.concatenate([xr[r], xi[r]], axis=0).astype(BF16)
            ys.append(jnp.dot(g_ref[SUBLANES * a + r].astype(BF16), o, preferred_element_type=F32))
        y8 = jnp.swapaxes(jnp.stack(ys, axis=0), 0, 1)
        for k2 in range(n2):
            y_ref[0, DFT_N1 * k2 + SUBLANES * a:DFT_N1 * k2 + SUBLANES * (a + 1), :] = y8[k2]


def _dft_tables(L):
    n2 = L // DFT_N1
    k = np.arange(DFT_N1)
    a = 2.0 * np.pi * ((k[:, None] * k[None, :]) % DFT_N1) / DFT_N1
    er, ei = np.cos(a) / np.sqrt(DFT_N1), -np.sin(a) / np.sqrt(DFT_N1)
    m1 = np.block([[er, -ei], [ei, er]])
    k1 = np.arange(DFT_N1)[:, None, None]
    k2 = np.arange(n2)[None, :, None]
    nn = np.arange(n2)[None, None, :]
    th = 2.0 * np.pi * ((nn * (k1 + DFT_N1 * k2)) % L) / L
    g = np.concatenate([np.cos(th), np.sin(th)], axis=-1) / np.sqrt(n2)
    return jnp.asarray(m1, F32), jnp.asarray(g, F32)


def _channel_dft_table():
    k = np.arange(FT_GROUP_CH)
    a = 2.0 * np.pi * ((k[:, None] * k[None, :]) % FT_GROUP_CH) / FT_GROUP_CH
    t = np.concatenate([np.cos(a), -np.sin(a)], axis=1) / np.sqrt(FT_GROUP_CH)
    return jnp.asarray(t, F32)


def _seq_dft(zr, zi, tm):
    G, B, L, C = zr.shape
    n2 = L // DFT_N1
    m1, g = _dft_tables(L)
    zspec = pl.BlockSpec((1, 1, L, C), lambda b, j: (j, b, 0, 0))
    return pl.pallas_call(
        functools.partial(_seq_dft_kernel, n2=n2, tm=tm),
        out_shape=jax.ShapeDtypeStruct((B, L, G * C), F32),
        grid=(B, G),
        in_specs=[zspec, zspec, _const_spec((2 * DFT_N1, 2 * DFT_N1)), _const_spec((DFT_N1, n2, 2 * n2))],
        out_specs=pl.BlockSpec((1, L, C), lambda b, j: (b, 0, j)),
        scratch_shapes=[pltpu.VMEM((2 * DFT_N1 * n2, LANES), F32)],
        compiler_params=_cparams(("parallel", "arbitrary")),
        name="seq_dft",
    )(zr, zi, m1, g)


def _dense_dft_kernel(zr_ref, zi_ref, t_ref, y_ref):
    z = jnp.concatenate([zr_ref[0, 0], zi_ref[0, 0]], axis=0).astype(BF16)
    y_ref[0] = jnp.dot(t_ref[...].astype(BF16), z, preferred_element_type=F32)


def _dense_seq_dft(zr, zi):
    G, B, L, C = zr.shape
    k = np.arange(L)
    a = 2.0 * np.pi * ((k[:, None] * k[None, :]) % L) / L
    t = jnp.asarray(np.concatenate([np.cos(a), np.sin(a)], axis=1) / np.sqrt(L), F32)
    zspec = pl.BlockSpec((1, 1, L, C), lambda b, j: (j, b, 0, 0))
    return pl.pallas_call(
        _dense_dft_kernel,
        out_shape=jax.ShapeDtypeStruct((B, L, G * C), F32),
        grid=(B, G),
        in_specs=[zspec, zspec, _const_spec((L, 2 * L))],
        out_specs=pl.BlockSpec((1, L, C), lambda b, j: (b, 0, j)),
        compiler_params=_cparams(("arbitrary", "arbitrary")),
        name="dense_seq_dft",
    )(zr, zi, t)


HALO = 16


def _wide_halo_specs(tm, L, width):
    hb = tm // HALO
    last = L // HALO - 1
    return [
        pl.BlockSpec((1, tm, width), lambda b, i: (b, i, 0)),
        pl.BlockSpec((1, HALO, width), lambda b, i: (b, jnp.minimum((i + 1) * hb, last), 0)),
        pl.BlockSpec((1, HALO, width), lambda b, i: (b, jnp.maximum(i * hb - 1, 0), 0)),
    ]


def _circ(t_ref, n_ref, p_ref):
    return jnp.concatenate([t_ref[0], n_ref[0], p_ref[0]], axis=0)


def _mix_ffn_kernel(*refs, tm, nt, odd, mrow, nside):
    n_in = 15 if odd else 9
    x3, rest = refs[:3], refs[3:n_in]
    mod_ref, ng_ref, wo_ref, wu_ref, cw_ref, wd_ref = refs[n_in:n_in + 6]
    side_src = refs[n_in + 6:n_in + 6 + nside]
    o_ref = refs[n_in + 6 + nside]
    side_dst = refs[n_in + 7 + nside:n_in + 7 + 2 * nside]
    hh_scr, act_scr = refs[n_in + 7 + 2 * nside:]
    _cast_refs(side_src, side_dst)
    i = pl.program_id(1)
    if odd:
        att, hf, hb, om = (_circ(*rest[k:k + 3]) for k in range(0, 12, 3))
        lhs = jnp.concatenate([att, ((hf + hb) * _sigmoid(om.astype(F32))).astype(BF16)], axis=-1)
    else:
        yc, yf = _circ(*rest[0:3]), _circ(*rest[3:6])
        lhs = jnp.concatenate([yc, yf.astype(BF16)], axis=-1)
    mv = [_mod_vec(mod_ref, k, mrow) for k in range(6)]
    x1 = _circ(*x3) + mv[2] * jnp.dot(lhs, wo_ref[...], preferred_element_type=F32)
    hh_scr[...] = _norm_mod(x1, ng_ref[...], mv[3], mv[4]).astype(BF16)
    row = lax.broadcasted_iota(jnp.int32, (tm + 2 * HALO, 1), 0)
    valid = (row < tm) | ((row < tm + HALO) & (i < nt - 1)) | ((row >= tm + HALO) & (i > 0))
    for c in range(N_FF_CHUNKS):
        lo = c * FF_CHUNK
        g = jnp.dot(hh_scr[...], wu_ref[:, lo:lo + FF_CHUNK], preferred_element_type=F32)
        g = jnp.where(valid, g, 0.0)
        cv = _conv3(g, cw_ref[:, lo:lo + FF_CHUNK], tm)
        val = jnp.dot(hh_scr[:tm, :], wu_ref[:, D_FF + lo:D_FF + lo + FF_CHUNK], preferred_element_type=F32)
        act_scr[:, lo:lo + FF_CHUNK] = (cv * _sigmoid(cv) * val).astype(BF16)
    for c0 in range(0, D_MODEL, D_MODEL // 2):
        cs = slice(c0, c0 + D_MODEL // 2)
        y = jnp.dot(act_scr[...], wd_ref[:, cs], preferred_element_type=F32)
        o_ref[0, :, cs] = x1[:tm, cs] + mv[5][:, cs] * y


def _mix_ffn(x, mixed, mod, layer, mrow, ng, w_out, w_up, cw, w_down, tm, odd, side=()):
    B, L, _ = x.shape
    nt = L // tm
    specs = _wide_halo_specs(tm, L, D_MODEL)
    args = [x, x, x]
    for a in mixed:
        specs += _wide_halo_specs(tm, L, a.shape[-1])
        args += [a, a, a]
    side_in, side_out, side_shape = _side_cast_specs(side, B * nt, lambda b, i: b * nt + i)
    out = pl.pallas_call(
        functools.partial(_mix_ffn_kernel, tm=tm, nt=nt, odd=odd, mrow=mrow, nside=len(side)),
        out_shape=(jax.ShapeDtypeStruct(x.shape, F32), *side_shape),
        grid=(B, nt),
        in_specs=specs + [
            _mod_spec(layer),
            _const_spec((1, D_MODEL)),
            _resident_spec((D_MODEL, D_MODEL)),
            _weight_spec(w_up, (D_MODEL, 2 * D_FF), layer),
            _layer_spec((3, D_FF), layer),
            _weight_spec(w_down, (D_FF, D_MODEL), layer),
        ] + side_in,
        out_specs=(pl.BlockSpec((1, tm, D_MODEL), lambda b, i: (b, i, 0)), *side_out),
        scratch_shapes=[pltpu.VMEM((tm + 2 * HALO, D_MODEL), BF16),
                        pltpu.VMEM((tm, D_FF), BF16)],
        compiler_params=_cparams(("parallel", "arbitrary")),
        name="odd_mix_ffn" if odd else "even_mix_ffn",
    )(*args, mod, ng, w_out, w_up, cw, w_down, *[e[0] for e in side])
    return out if side else out[0]


def _split3(x):
    parts = []
    r = x
    for _ in range(3):
        p = r.astype(BF16)
        parts.append(p)
        r = r - p.astype(F32)
    return parts


def _odd_in_kernel(x_ref, mod_ref, ng_ref, w_ref, wg_ref, pm_ref, gain_ref, cos_ref, sin_ref, gb_ref, tl_ref, tu_ref,
                   qt_ref, kv_ref, vt_ref, qm_ref, kmt_ref, vm_ref, om_ref, gc_ref, gr_ref, *, mrow, tm, nsub):
    for s in range(nsub):
        rs = slice(s * tm, (s + 1) * tm)
        h = _norm_mod(x_ref[0, rs], ng_ref[...], _mod_vec(mod_ref, 0, mrow), _mod_vec(mod_ref, 1, mrow)).astype(BF16)
        u = jnp.dot(h, w_ref[...], preferred_element_type=F32)

        uqk = u[:, OQ:OV]
        sq = (uqk * uqk).astype(BF16)
        pm = pm_ref[...].astype(BF16)
        ms = jnp.concatenate([jnp.dot(sq[:, t * LANES:(t + 1) * LANES], pm, preferred_element_type=F32)
                              for t in range((OV - OQ) // LANES)], axis=1)
        rn = uqk * lax.rsqrt(ms + EPS) * gain_ref[...]
        lane = lax.broadcasted_iota(jnp.int32, (1, LANES), 1)
        first = (lane % 32) < 16
        cos = cos_ref[rs, :]
        sin = sin_ref[rs, :]
        roped = []
        for t in range((OV - OQ) // LANES):
            xt = rn[:, t * LANES:(t + 1) * LANES]
            sw = jnp.where(first, pltpu.roll(xt, LANES - 16, 1), pltpu.roll(xt, 16, 1))
            roped.append(xt * cos + sw * sin)
        for t in range(ATT_Q // LANES):
            qt_ref[0, t * LANES:(t + 1) * LANES, rs] = (roped[t] * (ATT_SCALE * LOG2E)).T.astype(BF16)
        k = roped[ATT_Q // LANES]
        v = u[:, OV:OQM]
        half = LANES // 2
        kv_ref[0, rs, 0:LANES] = k.astype(BF16)
        kv_ref[0, rs, LANES:2 * LANES] = pltpu.roll(k, half, 1).astype(BF16)
        vt_ref[0, :, rs] = v.T.astype(BF16)

        qm_ref[0, rs] = u[:, OQM:OKM].astype(BF16)
        for p in range(ML_QK // LANES):
            km = u[:, OKM + p * LANES:OKM + (p + 1) * LANES] * (ML_QK_DIM ** -0.5)
            kmt_ref[0, p * LANES:(p + 1) * LANES, rs] = km.T.astype(BF16)
        vm_ref[0, rs] = u[:, OVM:OOM].astype(BF16)
        om_ref[0, rs] = u[:, OOM:OG].astype(BF16)

        gt = lax.dot_general(wg_ref[...].astype(BF16), h, (((1,), (1,)), ((), ())),
                             preferred_element_type=F32) + gb_ref[...]
        row = lax.broadcasted_iota(jnp.int32, (2 * SUBLANES, 1), 0)
        logsig = jnp.minimum(gt, 0.0) - jnp.log(1.0 + jnp.exp(-jnp.abs(gt)))
        parts = jnp.concatenate(_split3(logsig), axis=0)
        tl, tu = tl_ref[...].astype(BF16), tu_ref[...].astype(BF16)

        def chunk_cumsum(tri):
            c = jnp.concatenate([jnp.dot(parts[:, k:k + ML_CHUNK], tri, preferred_element_type=F32)
                                 for k in range(0, tm, ML_CHUNK)], axis=1)
            n = 2 * SUBLANES
            return c[0:n] + c[n:2 * n] + c[2 * n:3 * n]

        cum_f, cum_b = chunk_cumsum(tu), chunk_cumsum(tl)
        sel = row % 8
        gr = jnp.where(sel < 4, gt, jnp.where(row < SUBLANES, cum_f, cum_b)) * LOG2E
        gr_ref[0, :, rs] = gr
        gc_ref[0, rs] = jnp.concatenate([gr, jnp.zeros((LANES - 2 * SUBLANES, tm), F32)], axis=0).T


def _chunk_tri():
    i = np.arange(ML_CHUNK)
    return jnp.asarray(i[None, :] <= i[:, None], F32), jnp.asarray(i[None, :] >= i[:, None], F32)


def _odd_in(x, mod, layer, mrow, ng, w, wg, pm, gain, cos, sin, gb, tm):
    B, L, _ = x.shape
    nsub = 2 if L % (2 * tm) == 0 else 1
    bm = nsub * tm

    def rows(c, dt=BF16):
        return jax.ShapeDtypeStruct((B, L, c), dt), pl.BlockSpec((1, bm, c), lambda b, i: (b, i, 0))

    def cols(c, dt=BF16):
        return jax.ShapeDtypeStruct((B, c, L), dt), pl.BlockSpec((1, c, bm), lambda b, i: (b, 0, i))

    outs = [cols(ATT_Q), rows(2 * LANES), cols(LANES), rows(ML_QK), cols(ML_QK), rows(ML_V), rows(ML_V),
            rows(LANES, F32), cols(2 * SUBLANES, F32)]
    nqk = OV - OQ
    tl, tu = _chunk_tri()
    return pl.pallas_call(
        functools.partial(_odd_in_kernel, mrow=mrow, tm=tm, nsub=nsub),
        out_shape=tuple(o[0] for o in outs),
        grid=(B, L // bm),
        in_specs=[
            pl.BlockSpec((1, bm, D_MODEL), lambda b, i: (b, i, 0)),
            _mod_spec(layer),
            _const_spec((1, D_MODEL)),
            _const_spec((D_MODEL, OG)),
            _const_spec((N_GATES, D_MODEL)),
            _const_spec((LANES, LANES)),
            _const_spec((1, nqk)),
            pl.BlockSpec((bm, LANES), lambda b, i: (i, 0)),
            pl.BlockSpec((bm, LANES), lambda b, i: (i, 0)),
            _const_spec((N_GATES, 1)),
            _const_spec((ML_CHUNK, ML_CHUNK)),
            _const_spec((ML_CHUNK, ML_CHUNK)),
        ],
        out_specs=tuple(o[1] for o in outs),
        compiler_params=_cparams(("parallel", "arbitrary")),
        name="odd_in",
    )(x, mod, ng, w, wg, pm, gain, cos, sin, gb, tl, tu)


def _rope_tables(L):
    rows = L // GRID_W
    pos = np.stack([np.repeat(np.arange(rows), GRID_W), np.tile(np.arange(GRID_W), rows)]).astype(np.float64)
    axis_dim = HEAD_DIM // 2
    inv_freq = ROPE_THETA ** (-np.arange(0, axis_dim, 2, dtype=np.float64) / axis_dim)
    ang = pos[:, :, None] * inv_freq
    c, sn = np.cos(ang), np.sin(ang)
    cos = np.concatenate([c[0], c[0], c[1], c[1]], axis=-1)
    sin = np.concatenate([-sn[0], sn[0], -sn[1], sn[1]], axis=-1)
    return jnp.asarray(np.tile(cos, (1, 2)), F32), jnp.asarray(np.tile(sin, (1, 2)), F32)


def _att_steps(n, sink_ref, qt_ref, kvp_ref, kvc_ref, kvn_ref, kvx_ref, vtp_ref, vtc_ref, vtn_ref, vtx_ref,
               o_ref, *, nb):
    half = LANES // 2
    lo = lax.broadcasted_iota(jnp.int32, (1, LANES), 1) < half
    zero = jnp.zeros((), BF16)
    cx = kvx_ref.shape[1]
    klocal = ([kvp_ref[0]] + [kvc_ref[0, i * BLOCK:(i + 1) * BLOCK] for i in range(ATT_QB)] + [kvn_ref[0]])
    vlocal = ([vtp_ref[0]] + [vtc_ref[0, :, i * BLOCK:(i + 1) * BLOCK] for i in range(ATT_QB)] + [vtn_ref[0]])
    kctx = [kvx_ref[0, i:i + BLOCK] for i in range(0, cx, BLOCK)]
    vctx = [vtx_ref[0, :, i:i + BLOCK] for i in range(0, cx, BLOCK)]
    ones = jnp.ones((half, BLOCK), BF16)

    kj = lax.broadcasted_iota(jnp.int32, (BLOCK, BLOCK), 0)
    qi = lax.broadcasted_iota(jnp.int32, (BLOCK, BLOCK), 1)

    def twice(x):
        return jnp.concatenate([x, x], axis=1)

    left = lax.broadcasted_iota(jnp.int32, (1, 2 * BLOCK), 1) < BLOCK
    group = ATT_HEADS // ATT_KV_HEADS

    def block(qb):
        blk_id = n * ATT_QB + qb
        kblocks = klocal[qb:qb + 3] + kctx
        vblocks = vlocal[qb:qb + 3] + vctx
        bias = [twice(jnp.where((kj >= qi) & (blk_id > 0), 0.0, NEG_BIG)).astype(BF16), None,
                twice(jnp.where((kj <= qi) & (blk_id < nb - 1), 0.0, NEG_BIG)).astype(BF16)] + [None] * len(kctx)
        for kvh in range(ATT_KV_HEADS):
            t0 = kvh * group // 2
            cols = slice(qb * BLOCK, (qb + 1) * BLOCK)
            qt2 = jnp.concatenate([qt_ref[0, t0 * LANES:(t0 + 1) * LANES, cols],
                                   qt_ref[0, (t0 + 1) * LANES:(t0 + 2) * LANES, cols]], axis=1)
            outs = []
            for par in range(2):
                sk = jnp.where(left, sink_ref[2 * t0 + par], sink_ref[2 * t0 + 2 + par]) * LOG2E
                m = sk.astype(BF16).astype(F32)
                acc = jnp.zeros((LANES, 2 * BLOCK), F32)
                for blk in range(len(kblocks)):
                    k, ks = kblocks[blk][:, :LANES], kblocks[blk][:, LANES:]
                    kh = ((jnp.where(lo, k, zero), jnp.where(lo, zero, ks)),
                          (jnp.where(lo, ks, zero), jnp.where(lo, zero, k)))[kvh][par]
                    s = jnp.dot(kh, qt2, preferred_element_type=F32)
                    s = s.astype(BF16)
                    if bias[blk] is not None:
                        s = s + bias[blk]
                    m_new = jnp.maximum(m, jnp.max(s, axis=0, keepdims=True).astype(F32))
                    p = jnp.exp2(s - m_new.astype(BF16))
                    vh = jnp.concatenate([vblocks[blk][kvh * half:(kvh + 1) * half], ones], axis=0)
                    acc = acc * jnp.exp2(m - m_new) + jnp.dot(vh, p, preferred_element_type=F32)
                    m = m_new
                l = acc[half:half + 1, :] + jnp.exp2(sk - m)
                outs.append(acc[:half] * (1.0 / l))
            ot = jnp.concatenate(outs, axis=0)
            rows = slice(qb * BLOCK, (qb + 1) * BLOCK)
            o_ref[0, rows, t0 * LANES:(t0 + 1) * LANES] = ot[:, :BLOCK].T.astype(BF16)
            o_ref[0, rows, (t0 + 1) * LANES:(t0 + 2) * LANES] = ot[:, BLOCK:].T.astype(BF16)

    return [functools.partial(block, qb) for qb in range(ATT_QB)]


ML_GROUP = 8


def _mlstm_steps(j, qf_ref, ktf_ref, vf_ref, gcf_ref, grf_ref, qb_ref, ktb_ref, vb_ref, gcb_ref, grb_ref,
                 s0_ref, m0_ref, *rest, T, nb, with_output):
    if with_output:
        hf_ref, hb_ref, s_ref, m_ref = rest
    else:
        s_ref, m_ref = rest

    @pl.when(j == 0)
    def _():
        s_ref[...] = s0_ref[...]
        m_ref[...] = m0_ref[...]

    ti = lax.broadcasted_iota(jnp.int32, (T, T), 0)
    si = lax.broadcasted_iota(jnp.int32, (T, T), 1)
    masks = (si <= ti, si >= ti)
    top = lax.broadcasted_iota(jnp.int32, (LANES, 1), 0) < (LANES // 2)
    zero = jnp.zeros((), BF16)
    ones = jnp.ones((T, ML_V_DIM), BF16)
    zpad = jnp.zeros((ML_QK_DIM, 2 * ML_V_DIM), BF16)
    fwd = (qf_ref, ktf_ref, vf_ref, gcf_ref, grf_ref)
    bwd = (qb_ref, ktb_ref, vb_ref, gcb_ref, grb_ref)
    chains = [(b, d, h) for b in range(nb) for d in range(2) for h in range(ML_HEADS)]

    def group(g0):
        grp = chains[g0:g0 + ML_GROUP]
        st = {}
        for (b, d, h) in grp:
            q_ref, kt_ref, v_ref, gc_ref, gr_ref = fwd if d == 0 else bwd
            r = (b * 2 + d) * ML_HEADS + h
            li, lf = 8 * d + h, 8 * d + 4 + h
            gr = gr_ref[b]
            u_row = gr[li:li + 1, :] - gr[lf:lf + 1, :]
            m_prev = m_ref[r:r + 1, 0:1]
            e = dict(r=r, u_row=u_row, m_prev=m_prev,
                     b_end=gr[lf:lf + 1, (T - 1 if d == 0 else 0):(T if d == 0 else 1)],
                     c_end=jnp.maximum(jnp.max(u_row, axis=1, keepdims=True), m_prev))
            if with_output:
                e["mu"] = jnp.where(masks[d], u_row, NEG_BIG)
                c_col = jnp.maximum(jnp.max(e["mu"], axis=1, keepdims=True), m_prev)
                e["cb"] = jnp.broadcast_to(c_col, (T, LANES))
                e["bb"] = jnp.broadcast_to(gc_ref[b, :, lf:lf + 1], (T, LANES))
            st[(b, d, h)] = e
        for (b, d, h) in grp:
            q_ref, kt_ref, v_ref, gc_ref, gr_ref = fwd if d == 0 else bwd
            e = st[(b, d, h)]
            pair = h // 2
            ktp = kt_ref[b, pair * LANES:(pair + 1) * LANES, :]
            e["kth"] = ktp[(h % 2) * ML_QK_DIM:(h % 2 + 1) * ML_QK_DIM, :]
            e["vext"] = jnp.concatenate([v_ref[b, :, h * ML_V_DIM:(h + 1) * ML_V_DIM], ones], axis=1)
            if with_output:
                qp = q_ref[b, :, pair * LANES:(pair + 1) * LANES]
                kpad = jnp.where(top, ktp, zero) if h % 2 == 0 else jnp.where(top, zero, ktp)
                qk = jnp.dot(qp, kpad, preferred_element_type=F32) * jnp.exp2(e["mu"] - e["cb"])
                qs = qp.astype(F32) * jnp.exp2(e["m_prev"] - e["cb"])
                e["lhs"] = jnp.concatenate([qk.astype(BF16), qs.astype(BF16)], axis=1)
        for (b, d, h) in grp:
            e = st[(b, d, h)]
            e["s_prev"] = s_ref[e["r"]]
            if with_output:
                sb = e["s_prev"].astype(BF16)
                rhs = jnp.concatenate([e["vext"]] + ([sb, zpad] if h % 2 == 0 else [zpad, sb]), axis=0)
                tot = jnp.dot(e["lhs"], rhs, preferred_element_type=F32)
                floor = jnp.exp2(-(e["bb"] + e["cb"]))
                hout = tot[:, :ML_V_DIM] / jnp.maximum(jnp.abs(tot[:, ML_V_DIM:]), floor)
                o_ref = hf_ref if d == 0 else hb_ref
                o_ref[b, :, h * ML_V_DIM:(h + 1) * ML_V_DIM] = hout
        for (b, d, h) in grp:
            e = st[(b, d, h)]
            kw = (e["kth"].astype(F32) * jnp.exp2(e["u_row"] - e["c_end"])).astype(BF16)
            upd = jnp.dot(kw, e["vext"], preferred_element_type=F32)
            s_ref[e["r"]] = jnp.exp2(e["m_prev"] - e["c_end"]) * e["s_prev"] + upd
            m_ref[e["r"]:e["r"] + 1, :] = jnp.broadcast_to(e["b_end"] + e["c_end"], (1, LANES))

    return [functools.partial(group, g0) for g0 in range(0, len(chains), ML_GROUP)]


def _mlstm_kernel(*refs, T, nb, with_output):
    for step in _mlstm_steps(pl.program_id(0), *refs, T=T, nb=nb, with_output=with_output):
        step()


def _att_mlstm_kernel(*refs, T, nb, nblk, ns):
    j = pl.program_id(0)
    sink_ref, att_in, ml_in = refs[0], refs[1:10], refs[10:22]
    o_ref, ml_out = refs[22], refs[23:]
    att = _att_steps(lax.rem(j, ns), sink_ref, *att_in, o_ref, nb=nblk)
    ml = _mlstm_steps(j, *ml_in, *ml_out, T=T, nb=nb, with_output=True)
    for step in att + ml:
        step()


def _mlstm_specs(B, T, nc):
    up = lambda j: j
    down = lambda j: nc - 1 - j

    def specs(o):
        return [pl.BlockSpec((B, T, ML_QK), lambda j: (0, o(j), 0)),
                pl.BlockSpec((B, ML_QK, T), lambda j: (0, 0, o(j))),
                pl.BlockSpec((B, T, ML_V), lambda j: (0, o(j), 0)),
                pl.BlockSpec((B, T, LANES), lambda j: (0, o(j), 0)),
                pl.BlockSpec((B, 2 * SUBLANES, T), lambda j: (0, 0, o(j)))]

    nchains = B * 2 * ML_HEADS
    s_spec = _const_spec((nchains, ML_QK_DIM, 2 * ML_V_DIM))
    m_spec = _const_spec((nchains, LANES))
    state_shape = [jax.ShapeDtypeStruct((nchains, ML_QK_DIM, 2 * ML_V_DIM), F32),
                   jax.ShapeDtypeStruct((nchains, LANES), F32)]
    h_specs = [pl.BlockSpec((B, T, ML_V), lambda j: (0, up(j), 0)),
               pl.BlockSpec((B, T, ML_V), lambda j: (0, down(j), 0))]
    return specs(up) + specs(down) + [s_spec, m_spec], h_specs, [s_spec, m_spec], state_shape


def _att_mlstm(qt, kv, kvx, vt, vtx, sink, qm, kmt, vm, gc, gr, s0, m0):
    B, _, L = qt.shape
    T = ML_CHUNK
    nc = L // T
    nblk = L // BLOCK
    ns = nblk // ATT_QB
    assert nc == B * ns
    cx = kvx.shape[1]
    kw = kv.shape[2]
    vw = vt.shape[1]
    wide = ATT_QB * BLOCK
    smp = lambda j: j // ns
    cur = lambda j: j % ns
    prev = lambda j: jnp.maximum(cur(j) * ATT_QB - 1, 0)
    nxt = lambda j: jnp.minimum((cur(j) + 1) * ATT_QB, nblk - 1)
    att_specs = [
        pl.BlockSpec(memory_space=pltpu.SMEM),
        pl.BlockSpec((1, ATT_Q, wide), lambda j: (smp(j), 0, cur(j))),
        pl.BlockSpec((1, BLOCK, kw), lambda j: (smp(j), prev(j), 0)),
        pl.BlockSpec((1, wide, kw), lambda j: (smp(j), cur(j), 0)),
        pl.BlockSpec((1, BLOCK, kw), lambda j: (smp(j), nxt(j), 0)),
        pl.BlockSpec((1, cx, kw), lambda j: (smp(j), 0, 0)),
        pl.BlockSpec((1, vw, BLOCK), lambda j: (smp(j), 0, prev(j))),
        pl.BlockSpec((1, vw, wide), lambda j: (smp(j), 0, cur(j))),
        pl.BlockSpec((1, vw, BLOCK), lambda j: (smp(j), 0, nxt(j))),
        pl.BlockSpec((1, vw, cx), lambda j: (smp(j), 0, 0)),
    ]
    ml_in, h_specs, st_specs, st_shape = _mlstm_specs(B, T, nc)
    h_shape = jax.ShapeDtypeStruct((B, L, ML_V), F32)
    att, hf, hb, _, _ = pl.pallas_call(
        functools.partial(_att_mlstm_kernel, T=T, nb=B, nblk=nblk, ns=ns),
        out_shape=(jax.ShapeDtypeStruct((B, L, ATT_Q), BF16), h_shape, h_shape, *st_shape),
        grid=(nc,),
        in_specs=att_specs + ml_in,
        out_specs=(pl.BlockSpec((1, wide, ATT_Q), lambda j: (smp(j), cur(j), 0)), *h_specs, *st_specs),
        compiler_params=_cparams(("arbitrary",)),
        name="attention_mlstm",
    )(sink, qt, kv, kv, kv, kvx, vt, vt, vt, vtx, qm, kmt, vm, gc, gr, qm, kmt, vm, gc, gr, s0, m0)
    return att, hf, hb


def _mlstm(qm, kmt, vm, gc, gr, s0, m0, with_output):
    B, L, _ = qm.shape
    T = ML_CHUNK
    nc = L // T
    ml_in, h_specs, st_specs, st_shape = _mlstm_specs(B, T, nc)
    out_shape, out_specs = st_shape, st_specs
    if with_output:
        out_shape = [jax.ShapeDtypeStruct((B, L, ML_V), F32)] * 2 + out_shape
        out_specs = h_specs + out_specs
    return pl.pallas_call(
        functools.partial(_mlstm_kernel, T=T, nb=B, with_output=with_output),
        out_shape=tuple(out_shape),
        grid=(nc,),
        in_specs=ml_in,
        out_specs=tuple(out_specs),
        compiler_params=_cparams(("arbitrary",)),
        name="mlstm_scan" if with_output else "mlstm_context_state",
    )(qm, kmt, vm, gc, gr, qm, kmt, vm, gc, gr, s0, m0)


def _odd_tables(gate_b, q_g, k_g):
    assert sum([ATT_Q, ATT_KV, ATT_KV, ML_QK, ML_QK, ML_V, ML_V]) == OG
    head = np.arange(LANES) // HEAD_DIM
    pm = jnp.asarray((head[:, None] == head[None, :]) / HEAD_DIM, F32)
    gain = jnp.concatenate([jnp.tile(q_g, ATT_HEADS), jnp.tile(k_g, ATT_KV_HEADS)])[None, :]
    return pm, gain, gate_b.reshape(N_GATES, 1)


def kernel(x, c, ctx, c_ctx, ada_w, ada_b, norm_g, even_w_in, even_conv, even_w_out, odd_w_in, odd_gate_b,
           odd_q_g, odd_k_g, odd_sink, odd_w_out, ffn_w_up, ffn_conv, ffn_w_down):
    B, L, _ = x.shape
    C = ctx.shape[1]
    depth = ada_w.shape[0]
    assert depth == 2 and L % (DFT_N1 * SUBLANES) == 0 and C % ML_CHUNK == 0

    cv = jnp.concatenate([c, c_ctx[None, :], jnp.zeros((SUBLANES - B - 1, D_MODEL), F32)], axis=0)
    mod, w_in0, w_out0 = _modulation(cv, ada_w, ada_b, ((even_w_in, 0, EVEN_IN, 3), (even_w_out, 0, D_MODEL, 3)))
    lat, cx = None, B

    tm = min(512, L)
    tc = _channel_dft_table()
    first = ((ffn_w_up, 0, 2 * D_FF, 1), (ffn_w_down, 0, D_MODEL, 1))
    later = ((ffn_w_up, 1, 2 * D_FF, 1), (ffn_w_down, 1, D_MODEL, 2), (odd_w_in, 0, OG, 1),
             (odd_w_out, 0, D_MODEL, 1))

    ng00, ng01 = norm_g[0, 0][None, :], norm_g[0, 1][None, :]

    yc, zr, zi, w_up0, w_down0 = _even_in(x, mod, 0, lat, ng00, w_in0, even_conv[0], tc, tm, L // DFT_N1, first)
    xl, w_up1, w_down1, w_in1, w_out1 = _mix_ffn(x, (yc, _seq_dft(zr, zi, tm)), mod, 0, lat, ng01, w_out0, w_up0,
                                                 ffn_conv, w_down0, tm, odd=False, side=later)
    yc, zr, zi = _even_in(ctx, mod, 0, cx, ng00, w_in0, even_conv[0], tc, C, None)
    xc = _mix_ffn(ctx, (yc, _dense_seq_dft(zr, zi)), mod, 0, cx, ng01, w_out0, w_up0, ffn_conv, w_down0, C, odd=False)

    pm, gain, gb = _odd_tables(odd_gate_b[0], odd_q_g[0], odd_k_g[0])
    wg = odd_w_in[0][:, OG:].T
    ng10, ng11 = norm_g[1, 0][None, :], norm_g[1, 1][None, :]
    cos, sin = _rope_tables(L)
    one, nil = jnp.ones((C, LANES), F32), jnp.zeros((C, LANES), F32)
    qt, kv, vt, qm, kmt, vm, om, gc, gr = _odd_in(xl, mod, 1, lat, ng10, w_in1, wg, pm, gain, cos, sin, gb, tm)
    _, kvx, vtx, qmx, kmtx, vmx, _, gcx, grx = _odd_in(xc, mod, 1, cx, ng10, w_in1, wg, pm, gain, one, nil, gb, C)

    nchains = B * 2 * ML_HEADS
    s0 = jnp.zeros((nchains, ML_QK_DIM, 2 * ML_V_DIM), F32)
    m0 = jnp.zeros((nchains, LANES), F32)
    s1, m1 = _mlstm(qmx, kmtx, vmx, gcx, grx, s0, m0, with_output=False)
    att, hf, hb = _att_mlstm(qt, kv, kvx, vt, vtx, odd_sink[0], qm, kmt, vm, gc, gr, s1, m1)
    return _mix_ffn(xl, (att, hf, hb, om), mod, 1, lat, ng11, w_out1, w_up1, ffn_conv, w_down1, tm, odd=True)
```
